```python
import math
import jax, jax.numpy as jnp
from jax import lax
import numpy as np

D_MODEL = 2048
BATCH = 4
SEQ = 4096
DEPTH = 1

CONV_WIDTH = D_MODEL // 2
CONV_K = 3
N_HEADS = 16
N_KV_HEADS = 4
HEAD_DIM = 64
ATTN_WIDTH = N_HEADS * HEAD_DIM
KV_WIDTH = N_KV_HEADS * HEAD_DIM
IDX_HEADS = 16
IDX_DIM = 64
IDX_TOPK_MAX = 256
Q_BLOCK = 128
REL_BUCKETS = 32
REL_MAX_DIST = 128
N_EXPERTS = 64
N_GROUPS = 8
TOPK_GROUPS = 4
TOP_K = 8
D_EXPERT = 512
ROUTED_SCALE = 2.5
EPS = 1e-6
NEG = -1e30

kernel_name = "hybrid_conv_dsa_moe_block"


def rms_norm(x, w):
    xf = x.astype(jnp.float32)
    y = xf * lax.rsqrt(jnp.mean(xf * xf, axis=-1, keepdims=True) + EPS)
    return (y * w.astype(jnp.float32)).astype(x.dtype)


def layer_norm(x, w, b):
    xf = x.astype(jnp.float32)
    mu = jnp.mean(xf, axis=-1, keepdims=True)
    var = jnp.mean(jnp.square(xf - mu), axis=-1, keepdims=True)
    y = (xf - mu) * lax.rsqrt(var + EPS)
    return (y * w.astype(jnp.float32) + b.astype(jnp.float32)).astype(x.dtype)


def t5_bucket(dist):
    n = jnp.maximum(dist, 0)
    max_exact = REL_BUCKETS // 2
    nf = jnp.maximum(n, 1).astype(jnp.float32)
    large = max_exact + (jnp.log(nf / max_exact) / math.log(REL_MAX_DIST / max_exact)
                         * (REL_BUCKETS - max_exact)).astype(jnp.int32)
    large = jnp.minimum(large, REL_BUCKETS - 1)
    return jnp.where(n < max_exact, n, large)


def short_conv_mixer(b_gate, c_gate, u, conv_w):
    v = c_gate * u
    y = lax.conv_general_dilated(
        v, conv_w[:, None, :].astype(v.dtype), window_strides=(1,),
        padding=[(CONV_K - 1, 0)], dimension_numbers=("NWC", "WIO", "NWC"),
        feature_group_count=CONV_WIDTH)
    return b_gate * y


def dsa_attention(q, k, v, qi, ki, wi, rel_bias):
    B, S = q.shape[0], q.shape[1]
    n_sel = min(IDX_TOPK_MAX, S // 4)
    nb = S // Q_BLOCK
    rep = N_HEADS // N_KV_HEADS
    key_pos = jnp.arange(S, dtype=jnp.int32)

    def blockify(a):
        return a.reshape((B, nb, Q_BLOCK) + a.shape[2:]).swapaxes(0, 1)

    def one_block(args):
        blk, qb, qib, wib = args
        t = blk * Q_BLOCK + jnp.arange(Q_BLOCK, dtype=jnp.int32)
        dots = jnp.einsum("bqhd,bsd->bqhs", qib, ki, preferred_element_type=jnp.float32)
        idx_score = jnp.einsum("bqh,bqhs->bqs", wib.astype(jnp.float32),
                               jax.nn.relu(dots)) * (IDX_DIM ** -0.5)
        causal = key_pos[None, :] <= t[:, None]
        idx_score = jnp.where(causal[None], idx_score, NEG)
        _, sel = lax.top_k(idx_score, n_sel)
        valid = sel <= t[None, :, None]
        k_sel = jax.vmap(lambda kk, ii: kk[ii])(k, sel)
        v_sel = jax.vmap(lambda vv, ii: vv[ii])(v, sel)
        qg = qb.reshape(B, Q_BLOCK, N_KV_HEADS, rep, HEAD_DIM)
        logits = jnp.einsum("bqgrd,bqngd->bqgrn", qg, k_sel,
                            preferred_element_type=jnp.float32) * (HEAD_DIM ** -0.5)
        bias = rel_bias[t5_bucket(t[None, :, None] - sel)]
        bias = bias.reshape(B, Q_BLOCK, n_sel, N_KV_HEADS, rep).transpose(0, 1, 3, 4, 2)
        logits = logits + bias.astype(jnp.float32)
        logits = jnp.where(valid[:, :, None, None, :], logits, NEG)
        p = jax.nn.softmax(logits, axis=-1)
        o = jnp.einsum("bqgrn,bqngd->bqgrd", p.astype(v.dtype), v_sel)
        return o.reshape(B, Q_BLOCK, ATTN_WIDTH)

    out = lax.map(one_block, (jnp.arange(nb, dtype=jnp.int32), blockify(q),
                              blockify(qi), blockify(wi)))
    return out.swapaxes(0, 1).reshape(B, S, ATTN_WIDTH)


def swiglu(x, w1, w3, w2):
    return jnp.dot(jax.nn.silu(jnp.dot(x, w1)) * jnp.dot(x, w3), w2)


def moe_ffn(h, w_router, router_bias, w1, w3, w2, ws1, ws3, ws2):
    B, S, D = h.shape
    xt = h.reshape(B * S, D)
    scores = jax.nn.sigmoid(jnp.dot(xt, w_router, preferred_element_type=jnp.float32))
    sel_scores = scores + router_bias.astype(jnp.float32)
    grp = sel_scores.reshape(B * S, N_GROUPS, N_EXPERTS // N_GROUPS)
    grp_score = lax.top_k(grp, 2)[0].sum(-1)
    _, top_g = lax.top_k(grp_score, TOPK_GROUPS)
    gmask = jnp.sum(jax.nn.one_hot(top_g, N_GROUPS, dtype=jnp.float32), axis=1) > 0
    emask = jnp.repeat(gmask, N_EXPERTS // N_GROUPS, axis=1)
    _, top_e = lax.top_k(jnp.where(emask, sel_scores, NEG), TOP_K)
    w_sel = jnp.take_along_axis(scores, top_e, axis=1)
    w_sel = w_sel / jnp.sum(w_sel, axis=-1, keepdims=True) * ROUTED_SCALE
    combine = jnp.einsum("nk,nke->ne", w_sel,
                         jax.nn.one_hot(top_e, N_EXPERTS, dtype=jnp.float32)).astype(xt.dtype)
    out = swiglu(xt, ws1, ws3, ws2)
    for e in range(N_EXPERTS):
        out = out + combine[:, e:e + 1] * swiglu(xt, w1[e], w3[e], w2[e])
    return out.reshape(B, S, D)


def _split_sizes():
    return [CONV_WIDTH, CONV_WIDTH, CONV_WIDTH, ATTN_WIDTH, KV_WIDTH, KV_WIDTH,
            IDX_HEADS * IDX_DIM, IDX_DIM, IDX_HEADS, D_MODEL, D_MODEL]


def setup_inputs(seed: int = 0) -> dict:
    key = jax.random.key(seed)
    ks = jax.random.split(key, 32)
    L, D, E, F = DEPTH, D_MODEL, N_EXPERTS, D_EXPERT
    total_in = sum(_split_sizes())

    def nrm(k, shape, scale):
        return jax.random.normal(k, shape, jnp.float32) * scale

    return {
        "x": nrm(ks[0], (BATCH, SEQ, D), 1.0),
        "c": nrm(ks[1], (BATCH, D), 1.0),
        "rel_bias": nrm(ks[2], (REL_BUCKETS, N_HEADS), 0.5),
        "norm1_w": 1.0 + nrm(ks[3], (L, D), 0.02),
        "norm2_w": 1.0 + nrm(ks[4], (L, D), 0.02),
        "w_ada": nrm(ks[5], (L, D, 6 * D), 0.3 * D ** -0.5),
        "b_ada": nrm(ks[6], (L, 6 * D), 0.02),
        "w_in": nrm(ks[7], (L, D, total_in), D ** -0.5),
        "conv_w": nrm(ks[8], (L, CONV_K, CONV_WIDTH), CONV_K ** -0.5),
        "w_conv_out": nrm(ks[9], (L, CONV_WIDTH, D), CONV_WIDTH ** -0.5),
        "q_norm_w": 1.0 + nrm(ks[10], (L, HEAD_DIM), 0.02),
        "k_norm_w": 1.0 + nrm(ks[11], (L, HEAD_DIM), 0.02),
        "idx_k_norm_w": 1.0 + nrm(ks[12], (L, IDX_DIM), 0.02),
        "idx_k_norm_b": nrm(ks[13], (L, IDX_DIM), 0.02),
        "w_attn_out": nrm(ks[14], (L, ATTN_WIDTH, D), ATTN_WIDTH ** -0.5),
        "w_o": nrm(ks[15], (L, D, D), D ** -0.5),
        "w_router": nrm(ks[16], (L, D, E), D ** -0.5),
        "router_bias": nrm(ks[17], (L, E), 0.01),
        "w1": nrm(ks[18], (L, E, D, F), D ** -0.5),
        "w3": nrm(ks[19], (L, E, D, F), D ** -0.5),
        "w2": nrm(ks[20], (L, E, F, D), F ** -0.5),
        "ws1": nrm(ks[21], (L, D, F), D ** -0.5),
        "ws3": nrm(ks[22], (L, D, F), D ** -0.5),
        "ws2": nrm(ks[23], (L, F, D), F ** -0.5),
    }


def reference(x, c, rel_bias, norm1_w, norm2_w, w_ada, b_ada, w_in, conv_w, w_conv_out,
              q_norm_w, k_norm_w, idx_k_norm_w, idx_k_norm_b, w_attn_out, w_o,
              w_router, router_bias, w1, w3, w2, ws1, ws3, ws2):
    B, S, D = x.shape
    sizes = _split_sizes()
    splits = [int(v) for v in np.cumsum(sizes)[:-1]]
    for l in range(DEPTH):
        mod = jnp.dot(jax.nn.silu(c), w_ada[l]) + b_ada[l]
        sh1, sc1, g1, sh2, sc2, g2 = [m[:, None, :] for m in jnp.split(mod, 6, axis=-1)]

        h = rms_norm(x, norm1_w[l]) * (1 + sc1) + sh1
        proj = jnp.dot(h, w_in[l])
        cb, cc, cu, q, k, v, qi, ki, wi, ga, gb = jnp.split(proj, splits, axis=-1)

        y_conv = jnp.dot(short_conv_mixer(cb, cc, cu, conv_w[l]), w_conv_out[l])

        q = rms_norm(q.reshape(B, S, N_HEADS, HEAD_DIM), q_norm_w[l])
        k = rms_norm(k.reshape(B, S, N_KV_HEADS, HEAD_DIM), k_norm_w[l])
        v = v.reshape(B, S, N_KV_HEADS, HEAD_DIM)
        qi = qi.reshape(B, S, IDX_HEADS, IDX_DIM)
        ki = layer_norm(ki, idx_k_norm_w[l], idx_k_norm_b[l])
        wi = wi * (IDX_HEADS ** -0.5)
        y_attn = jnp.dot(dsa_attention(q, k, v, qi, ki, wi, rel_bias), w_attn_out[l])

        mixed = jax.nn.sigmoid(ga) * y_conv + jax.nn.sigmoid(gb) * y_attn
        x = x + g1 * jnp.dot(mixed, w_o[l])

        h2 = rms_norm(x, norm2_w[l]) * (1 + sc2) + sh2
        x = x + g2 * moe_ffn(h2, w_router[l], router_bias[l], w1[l], w3[l], w2[l],
                             ws1[l], ws3[l], ws2[l])
    return x
```

```python
import functools
import math

import numpy as np
import jax
import jax.numpy as jnp
from jax import lax
from jax.experimental import pallas as pl
from jax.experimental.pallas import tpu as pltpu

F32 = jnp.float32
BF16 = jnp.bfloat16
I32 = jnp.int32

D_MODEL = 2048
CONV_WIDTH = D_MODEL // 2
CONV_K = 3
N_HEADS = 16
N_KV_HEADS = 4
HEAD_DIM = 64
ATTN_WIDTH = N_HEADS * HEAD_DIM
KV_WIDTH = N_KV_HEADS * HEAD_DIM
IDX_HEADS = 16
IDX_DIM = 64
IDX_TOPK_MAX = 256
REL_BUCKETS = 32
REL_MAX_DIST = 128
N_EXPERTS = 64
N_GROUPS = 8
TOPK_GROUPS = 4
TOP_K = 8
D_EXPERT = 512
ROUTED_SCALE = 2.5
EPS = 1e-6
NEG = -1e30

REP = N_HEADS // N_KV_HEADS

LANES = 128
VMEM_LIMIT = 56 * 1024 * 1024

TQ = 128
KB = 128
KC = 4 * KB
PROJ_TM = 1024
PROJ_TN = 896
PREP_TM = 512
MIX_TM = 512
POST_TM = 512
ROUTE_TN = 512
MOE_TM = 256
COMB_TM = 128

_SEG = dict(cb=(0, 1024), cc=(1024, 2048), cu=(2048, 3072), q=(3072, 4096), k=(4096, 4352),
            v=(4352, 4608), qi=(4608, 5632), ki=(5632, 5696), wi=(5696, 5712),
            ga=(5712, 7760), gb=(7760, 9808))
_ORDER = ["ga", "gb", "cb", "cc", "cu", "q", "qi", "k", "v", "ki", "wi"]
PROJ_W = 9856
COL_GA, COL_GB = 0, 1
COL_CB, COL_CC, COL_CU, COL_Q, COL_QI = 4, 5, 6, 7, 8
COL_K, COL_V = 36, 37
COL_KW = 76

INT_MIN = -(2 ** 31)
INT_MAX = 2 ** 31 - 1


def _sortable_key_of(x):
    bits = int(np.float32(x).view(np.int32))
    return bits ^ 0x7FFFFFFF if bits < 0 else bits


NEG_KEY = _sortable_key_of(NEG)


def _sigmoid(x):
    return 1.0 / (1.0 + jnp.exp(-x))


def _params(*sem):
    return pltpu.CompilerParams(dimension_semantics=sem, vmem_limit_bytes=VMEM_LIMIT)


def _mod_kernel(c_ref, w_ref, b_ref, o_ref):
    c = c_ref[...]
    s = (c * _sigmoid(c)).astype(BF16)
    o_ref[...] = jnp.dot(s, w_ref[...].astype(BF16), preferred_element_type=F32) + b_ref[...]


def _mod(c, w_ada, b_ada):
    b = c.shape[0]
    rows = 8
    cp = jnp.pad(c, ((0, rows - b), (0, 0)))
    n = w_ada.shape[1]
    tn = 1024
    out = pl.pallas_call(
        _mod_kernel,
        grid=(n // tn,),
        in_specs=[pl.BlockSpec((rows, D_MODEL), lambda j: (0, 0)),
                  pl.BlockSpec((D_MODEL, tn), lambda j: (0, j)),
                  pl.BlockSpec((1, tn), lambda j: (0, j))],
        out_specs=pl.BlockSpec((rows, tn), lambda j: (0, j)),
        out_shape=jax.ShapeDtypeStruct((rows, n), F32),
        compiler_params=_params("arbitrary"),
        name="mod",
    )(cp, w_ada, b_ada.reshape(1, n))
    return out[:b]


def _proj_kernel(x_ref, nw_ref, sc_ref, sh_ref, w_ref, o_ref, h_ref):
    @pl.when(pl.program_id(2) == 0)
    def _():
        x = x_ref[0]
        ms = jnp.mean(x * x, axis=-1, keepdims=True)
        y = x * lax.rsqrt(ms + EPS) * nw_ref[...]
        h_ref[...] = (y * (1.0 + sc_ref[0]) + sh_ref[0]).astype(BF16)

    o_ref[0] = jnp.dot(h_ref[...], w_ref[...], preferred_element_type=F32).astype(BF16)


def _proj(x, norm_w, sc, sh, w_in_p):
    b, s, d = x.shape
    tm, tn = PROJ_TM, PROJ_TN
    return pl.pallas_call(
        _proj_kernel,
        grid=(b, s // tm, PROJ_W // tn),
        in_specs=[pl.BlockSpec((1, tm, d), lambda bi, i, j: (bi, i, 0)),
                  pl.BlockSpec((1, d), lambda bi, i, j: (0, 0)),
                  pl.BlockSpec((1, 1, d), lambda bi, i, j: (bi, 0, 0)),
                  pl.BlockSpec((1, 1, d), lambda bi, i, j: (bi, 0, 0)),
                  pl.BlockSpec((d, tn), lambda bi, i, j: (0, j))],
        out_specs=pl.BlockSpec((1, tm, tn), lambda bi, i, j: (bi, i, j)),
        out_shape=jax.ShapeDtypeStruct((b, s, PROJ_W), BF16),
        scratch_shapes=[pltpu.VMEM((tm, d), BF16)],
        compiler_params=_params("arbitrary", "arbitrary", "arbitrary"),
        name="proj",
    )(x, norm_w.reshape(1, d), sc, sh, w_in_p)


def _prep_kernel(q_ref, qi_ref, k_ref, v_ref, kw_ref, qnw_ref, knw_ref, inw_ref, inb_ref,
                 qT_ref, qiT_ref, kh_ref, vT_ref, kin_ref, wT_ref):
    tm = q_ref.shape[1]
    nqb = tm // TQ

    q3 = q_ref[0].astype(F32).T.reshape(N_HEADS, HEAD_DIM, tm)
    ms = jnp.mean(q3 * q3, axis=1, keepdims=True)
    qn = q3 * lax.rsqrt(ms + EPS) * (qnw_ref[...] * HEAD_DIM ** -0.5)
    qi3 = qi_ref[0].astype(F32).T.reshape(IDX_HEADS, IDX_DIM, tm)
    for jb in range(nqb):
        for h in range(N_HEADS):
            g, r = divmod(h, REP)
            qT_ref[0, jb, g, :, r * TQ:(r + 1) * TQ] = qn[h, :, jb * TQ:(jb + 1) * TQ].astype(BF16)
        for h in range(IDX_HEADS):
            qiT_ref[0, jb, :, h * TQ:(h + 1) * TQ] = qi3[h, :, jb * TQ:(jb + 1) * TQ].astype(BF16)

    k = k_ref[0].astype(F32)
    for g in range(N_KV_HEADS):
        kg = k[:, g * HEAD_DIM:(g + 1) * HEAD_DIM]
        msk = jnp.mean(kg * kg, axis=-1, keepdims=True)
        kh_ref[0, g] = (kg * lax.rsqrt(msk + EPS) * knw_ref[...]).astype(BF16)

    v3 = v_ref[0].astype(F32).T.reshape(N_KV_HEADS, HEAD_DIM, tm)
    for g in range(N_KV_HEADS):
        for jb in range(tm // KB):
            vT_ref[0, g, jb] = v3[g, :, jb * KB:(jb + 1) * KB].astype(BF16)

    kw = kw_ref[0].astype(F32)
    ki = kw[:, :IDX_DIM]
    mu = jnp.mean(ki, axis=-1, keepdims=True)
    var = jnp.mean(jnp.square(ki - mu), axis=-1, keepdims=True)
    kin_ref[0] = ((ki - mu) * lax.rsqrt(var + EPS) * inw_ref[...] + inb_ref[...]).astype(BF16)
    wiT = kw.T[IDX_DIM:IDX_DIM + IDX_HEADS] * (IDX_HEADS ** -0.5 * IDX_DIM ** -0.5)
    for jb in range(nqb):
        wT_ref[0, jb] = wiT[:, jb * TQ:(jb + 1) * TQ]


def _prep(proj, q_norm_w, k_norm_w, idx_k_norm_w, idx_k_norm_b):
    b, s, _ = proj.shape
    tm = PREP_TM
    nqb = tm // TQ
    nq = s // TQ
    return pl.pallas_call(
        _prep_kernel,
        grid=(b, s // tm),
        in_specs=[pl.BlockSpec((1, tm, ATTN_WIDTH), lambda bi, i: (bi, i, COL_Q)),
                  pl.BlockSpec((1, tm, IDX_HEADS * IDX_DIM), lambda bi, i: (bi, i, COL_QI)),
                  pl.BlockSpec((1, tm, KV_WIDTH), lambda bi, i: (bi, i, COL_K)),
                  pl.BlockSpec((1, tm, KV_WIDTH), lambda bi, i: (bi, i, COL_V)),
                  pl.BlockSpec((1, tm, LANES), lambda bi, i: (bi, i, COL_KW)),
                  pl.BlockSpec((1, HEAD_DIM, 1), lambda bi, i: (0, 0, 0)),
                  pl.BlockSpec((1, HEAD_DIM), lambda bi, i: (0, 0)),
                  pl.BlockSpec((1, IDX_DIM), lambda bi, i: (0, 0)),
                  pl.BlockSpec((1, IDX_DIM), lambda bi, i: (0, 0))],
        out_specs=[pl.BlockSpec((1, nqb, N_KV_HEADS, HEAD_DIM, REP * TQ), lambda bi, i: (bi, i, 0, 0, 0)),
                   pl.BlockSpec((1, nqb, IDX_DIM, IDX_HEADS * TQ), lambda bi, i: (bi, i, 0, 0)),
                   pl.BlockSpec((1, N_KV_HEADS, tm, HEAD_DIM), lambda bi, i: (bi, 0, i, 0)),
                   pl.BlockSpec((1, N_KV_HEADS, tm // KB, HEAD_DIM, KB), lambda bi, i: (bi, 0, i, 0, 0)),
                   pl.BlockSpec((1, tm, IDX_DIM), lambda bi, i: (bi, i, 0)),
                   pl.BlockSpec((1, nqb, IDX_HEADS, TQ), lambda bi, i: (bi, i, 0, 0))],
        out_shape=[jax.ShapeDtypeStruct((b, nq, N_KV_HEADS, HEAD_DIM, REP * TQ), BF16),
                   jax.ShapeDtypeStruct((b, nq, IDX_DIM, IDX_HEADS * TQ), BF16),
                   jax.ShapeDtypeStruct((b, N_KV_HEADS, s, HEAD_DIM), BF16),
                   jax.ShapeDtypeStruct((b, N_KV_HEADS, s // KB, HEAD_DIM, KB), BF16),
                   jax.ShapeDtypeStruct((b, s, IDX_DIM), BF16),
                   jax.ShapeDtypeStruct((b, nq, IDX_HEADS, TQ), F32)],
        compiler_params=_params("arbitrary", "arbitrary"),
        name="prep",
    )(proj, proj, proj, proj, proj,
      q_norm_w.reshape(1, HEAD_DIM, 1), k_norm_w.reshape(1, HEAD_DIM),
      idx_k_norm_w.reshape(1, IDX_DIM), idx_k_norm_b.reshape(1, IDX_DIM))


def _t5_bucket_np(n):
    n = np.maximum(n, 0)
    max_exact = REL_BUCKETS // 2
    nf = np.maximum(n, 1).astype(np.float64)
    large = max_exact + np.floor(np.log(nf / max_exact) / math.log(REL_MAX_DIST / max_exact)
                                 * (REL_BUCKETS - max_exact)).astype(np.int64)
    large = np.minimum(large, REL_BUCKETS - 1)
    return np.where(n < max_exact, n, large).astype(np.int32)


def _bias_kernel(rb_ref, bucket_ref, o_ref):
    h = pl.program_id(0)
    bucket = bucket_ref[...]
    acc = jnp.zeros(bucket.shape, F32)
    for bkt in range(REL_BUCKETS):
        acc = jnp.where(bucket == bkt, rb_ref[bkt, h], acc)
    o_ref[0] = acc


def _bias_strips(rel_bias):
    kk = np.arange(3 * TQ)[:, None]
    qq = np.arange(TQ)[None, :]
    bucket = jnp.asarray(_t5_bucket_np(qq + TQ - kk))
    return pl.pallas_call(
        _bias_kernel,
        grid=(N_HEADS,),
        in_specs=[pl.BlockSpec(memory_space=pltpu.SMEM),
                  pl.BlockSpec((3 * TQ, TQ), lambda h: (0, 0))],
        out_specs=pl.BlockSpec((1, 3 * TQ, TQ), lambda h: (h, 0, 0)),
        out_shape=jax.ShapeDtypeStruct((N_HEADS, 3 * TQ, TQ), F32),
        compiler_params=_params("arbitrary"),
        name="bias",
    )(rel_bias, bucket)


def _attn_kernel(fb_ref, qT_ref, qiT_ref, wT_ref, kh_ref, vT_ref, kin_ref, biasT_ref, o_ref,
                 keys_ref, am_ref, amf_ref, p_ref, *, n_sel):
    i = pl.program_id(1)
    seq = kin_ref.shape[1]
    t0 = i * TQ
    n_chunks = lax.shift_right_logical(i + 4, 2)
    q_pos = t0 + lax.broadcasted_iota(I32, (KB, TQ), 1)
    k_off = lax.broadcasted_iota(I32, (KB, TQ), 0)

    qiT = qiT_ref[0, 0]
    wT = wT_ref[0, 0]

    def score_chunk(c, carry):
        k0 = pl.multiple_of(c * KC, KC)
        d = jnp.dot(kin_ref[0, pl.ds(k0, KC), :], qiT, preferred_element_type=F32)
        acc = jnp.zeros((KC, TQ), F32)
        for h in range(IDX_HEADS):
            acc = acc + wT[h:h + 1, :] * jnp.maximum(d[:, h * TQ:(h + 1) * TQ], 0.0)
        for j in range(KC // KB):
            blk = c * (KC // KB) + j
            sc = jnp.where(blk * KB + k_off <= q_pos, acc[j * KB:(j + 1) * KB], NEG)
            bits = lax.bitcast_convert_type(sc, I32)
            keys_ref[blk] = jnp.where(bits < 0, bits ^ 0x7FFFFFFF, bits)
        return carry

    lax.fori_loop(0, n_chunks, score_chunk, 0)

    n_virtual = (seq - n_chunks * KC).astype(F32)

    def count(pred):
        def body(c, acc):
            for j in range(KC // KB):
                blk = c * (KC // KB) + j
                hit = jnp.where(pred(keys_ref[blk], blk), 1.0, 0.0)
                acc = acc + jnp.sum(hit.reshape(KB // 8, 8, TQ), axis=0)
            return acc
        acc = lax.fori_loop(0, n_chunks, body, jnp.zeros((8, TQ), F32))
        return jnp.sum(acc, axis=0, keepdims=True)

    def bit_body(it, thr):
        cand = thr + lax.shift_left(jnp.int32(1), 31 - it)
        cnt = count(lambda kb, blk: kb >= cand) + jnp.where(NEG_KEY >= cand, n_virtual, 0.0)
        return jnp.where(cnt >= n_sel, cand, thr)

    thr = lax.fori_loop(0, 32, bit_body, jnp.full((1, TQ), INT_MIN, I32))

    cnt_gt = count(lambda kb, blk: kb > thr) + jnp.where(NEG_KEY > thr, n_virtual, 0.0)
    cnt_eq = count(lambda kb, blk: kb == thr) + jnp.where(NEG_KEY == thr, n_virtual, 0.0)
    need = n_sel - cnt_gt
    p_ref[...] = jnp.full((8, TQ), INT_MAX, I32)
    has_tie = jnp.max(jnp.where(cnt_eq > need, 1.0, 0.0)) > 0.0

    @pl.when(has_tie)
    def _():
        idx_bits = int(seq).bit_length()

        def p_body(it, p):
            cand = p | lax.shift_left(jnp.int32(1), idx_bits - 1 - it)
            below = count(lambda kb, blk: (kb == thr) & (blk * KB + k_off < cand))
            return jnp.where(below < need, cand, p)

        p = lax.fori_loop(0, idx_bits, p_body, jnp.zeros((1, TQ), I32))
        p_ref[...] = jnp.broadcast_to(p, (8, TQ))

    p_last = p_ref[0:1, :]

    bw = jnp.maximum(i - 1, 0)
    ws = pl.multiple_of(bw * KB, KB)

    def mask_chunk(c, carry):
        for j in range(KC // KB):
            blk = c * (KC // KB) + j
            kb = keys_ref[blk]
            k_pos = blk * KB + k_off
            sel = (kb > thr) | ((kb == thr) & (k_pos <= p_last))
            v = jnp.where(sel & (k_pos <= q_pos), 0.0, NEG)
            am_ref[blk] = v
            amf_ref[blk] = jnp.where(k_pos < ws, v, NEG)
        return carry

    lax.fori_loop(0, n_chunks, mask_chunk, 0)

    off = pl.multiple_of(TQ - (t0 - ws), TQ)
    n_far = lax.shift_right_logical(bw + 3, 2)
    am_near = jnp.concatenate([am_ref[bw], am_ref[bw + 1]], axis=0)
    outs = []
    for g in range(N_KV_HEADS):
        qg = qT_ref[0, 0, g]
        s = jnp.dot(kh_ref[0, g, pl.ds(ws, 2 * KB), :], qg, preferred_element_type=F32)
        s = jnp.concatenate(
            [s[:, r * TQ:(r + 1) * TQ] + (biasT_ref[REP * g + r, pl.ds(off, 2 * KB), :] + am_near)
             for r in range(REP)], axis=1)
        m = jnp.max(s, axis=0, keepdims=True)
        p = jnp.exp(s - m)
        l = jnp.sum(p, axis=0, keepdims=True)
        pb = p.astype(BF16)
        acc = (jnp.dot(vT_ref[0, g, bw], pb[:KB], preferred_element_type=F32)
               + jnp.dot(vT_ref[0, g, bw + 1], pb[KB:], preferred_element_type=F32))

        def far_body(f, carry, g=g, qg=qg):
            m, l, acc = carry
            k0 = pl.multiple_of(f * KC, KC)
            s = jnp.dot(kh_ref[0, g, pl.ds(k0, KC), :], qg, preferred_element_type=F32)
            amf = jnp.concatenate([amf_ref[f * (KC // KB) + j] for j in range(KC // KB)], axis=0)
            s = jnp.concatenate(
                [s[:, r * TQ:(r + 1) * TQ] + (amf + fb_ref[REP * g + r]) for r in range(REP)], axis=1)
            m_new = jnp.maximum(m, jnp.max(s, axis=0, keepdims=True))
            alpha = jnp.exp(m - m_new)
            p = jnp.exp(s - m_new)
            l = alpha * l + jnp.sum(p, axis=0, keepdims=True)
            vc = jnp.concatenate([vT_ref[0, g, f * (KC // KB) + j] for j in range(KC // KB)], axis=1)
            acc = alpha * acc + jnp.dot(vc, p.astype(BF16), preferred_element_type=F32)
            return m_new, l, acc

        m, l, acc = lax.fori_loop(0, n_far, far_body, (m, l, acc))
        og = acc / l
        outs.extend(og[:, r * TQ:(r + 1) * TQ] for r in range(REP))
    o_ref[0] = jnp.concatenate(outs, axis=0).T.astype(BF16)


def _attention(qT, qiT, wT, kh, vT, kin, bias_strips, far_bias):
    b, nq = qT.shape[0], qT.shape[1]
    s = kin.shape[1]
    n_sel = min(IDX_TOPK_MAX, s // 4)
    nb = s // KB
    grid_spec = pltpu.PrefetchScalarGridSpec(
        num_scalar_prefetch=0,
        grid=(b, nq),
        in_specs=[pl.BlockSpec(memory_space=pltpu.SMEM),
                  pl.BlockSpec((1, 1, N_KV_HEADS, HEAD_DIM, REP * TQ), lambda bi, i: (bi, i, 0, 0, 0)),
                  pl.BlockSpec((1, 1, IDX_DIM, IDX_HEADS * TQ), lambda bi, i: (bi, i, 0, 0)),
                  pl.BlockSpec((1, 1, IDX_HEADS, TQ), lambda bi, i: (bi, i, 0, 0)),
                  pl.BlockSpec((1, N_KV_HEADS, s, HEAD_DIM), lambda bi, i: (bi, 0, 0, 0)),
                  pl.BlockSpec((1, N_KV_HEADS, nb, HEAD_DIM, KB), lambda bi, i: (bi, 0, 0, 0, 0)),
                  pl.BlockSpec((1, s, IDX_DIM), lambda bi, i: (bi, 0, 0)),
                  pl.BlockSpec((N_HEADS, 3 * TQ, TQ), lambda bi, i: (0, 0, 0))],
        out_specs=pl.BlockSpec((1, TQ, ATTN_WIDTH), lambda bi, i: (bi, i, 0)),
        scratch_shapes=[pltpu.VMEM((nb, KB, TQ), I32),
                        pltpu.VMEM((nb, KB, TQ), F32),
                        pltpu.VMEM((nb, KB, TQ), F32),
                        pltpu.VMEM((8, TQ), I32)],
    )
    return pl.pallas_call(
        functools.partial(_attn_kernel, n_sel=n_sel),
        grid_spec=grid_spec,
        out_shape=jax.ShapeDtypeStruct((b, s, ATTN_WIDTH), BF16),
        compiler_params=_params("arbitrary", "arbitrary"),
        name="attn",
    )(far_bias, qT, qiT, wT, kh, vT, kin, bias_strips)


HALO = 16


def _mix_kernel(cb_ref, cc_ref, cu_ref, ccp_ref, cup_ref, at_ref, ga_ref, gb_ref,
                cw_ref, wco_ref, wao_ref, o_ref):
    tm = cb_ref.shape[1]
    v = cc_ref[0].astype(F32) * cu_ref[0].astype(F32)
    first = pl.program_id(1) == 0
    hv = ccp_ref[0].astype(F32) * cup_ref[0].astype(F32)
    hv = jnp.where(first, 0.0, hv)
    row = lax.broadcasted_iota(I32, v.shape, 0)
    v1 = jnp.where(row == 0, hv[HALO - 1:HALO], pltpu.roll(v, 1, 0))
    v2 = pltpu.roll(v, 2, 0)
    v2 = jnp.where(row == 0, hv[HALO - 2:HALO - 1], jnp.where(row == 1, hv[HALO - 1:HALO], v2))
    y = cw_ref[0:1] * v2 + cw_ref[1:2] * v1 + cw_ref[2:3] * v
    yc = (cb_ref[0].astype(F32) * y).astype(BF16)
    y_conv = jnp.dot(yc, wco_ref[...], preferred_element_type=F32)
    y_attn = jnp.dot(at_ref[0], wao_ref[...], preferred_element_type=F32)
    mixed = _sigmoid(ga_ref[0].astype(F32)) * y_conv + _sigmoid(gb_ref[0].astype(F32)) * y_attn
    o_ref[0] = mixed.astype(BF16)


def _mix(proj, attn, conv_w, w_conv_out_b, w_attn_out_b):
    b, s, _ = proj.shape
    tm = MIX_TM
    hb = tm // HALO
    prev = lambda col: (lambda bi, i: (bi, jnp.maximum(i * hb - 1, 0), col))
    return pl.pallas_call(
        _mix_kernel,
        grid=(b, s // tm),
        in_specs=[pl.BlockSpec((1, tm, CONV_WIDTH), lambda bi, i: (bi, i, COL_CB)),
                  pl.BlockSpec((1, tm, CONV_WIDTH), lambda bi, i: (bi, i, COL_CC)),
                  pl.BlockSpec((1, tm, CONV_WIDTH), lambda bi, i: (bi, i, COL_CU)),
                  pl.BlockSpec((1, HALO, CONV_WIDTH), prev(COL_CC)),
                  pl.BlockSpec((1, HALO, CONV_WIDTH), prev(COL_CU)),
                  pl.BlockSpec((1, tm, ATTN_WIDTH), lambda bi, i: (bi, i, 0)),
                  pl.BlockSpec((1, tm, D_MODEL), lambda bi, i: (bi, i, COL_GA)),
                  pl.BlockSpec((1, tm, D_MODEL), lambda bi, i: (bi, i, COL_GB)),
                  pl.BlockSpec((8, CONV_WIDTH), lambda bi, i: (0, 0)),
                  pl.BlockSpec((CONV_WIDTH, D_MODEL), lambda bi, i: (0, 0)),
                  pl.BlockSpec((ATTN_WIDTH, D_MODEL), lambda bi, i: (0, 0))],
        out_specs=pl.BlockSpec((1, tm, D_MODEL), lambda bi, i: (bi, i, 0)),
        out_shape=jax.ShapeDtypeStruct((b, s, D_MODEL), BF16),
        compiler_params=_params("arbitrary", "arbitrary"),
        name="mix",
    )(proj, proj, proj, proj, proj, attn, proj, proj,
      jnp.pad(conv_w, ((0, 8 - CONV_K), (0, 0))), w_conv_out_b, w_attn_out_b)


def _post_kernel(x_ref, mx_ref, g1_ref, nw_ref, sc_ref, sh_ref, g2_ref, wo_ref, wrT_ref,
                 ws1_ref, ws3_ref, ws2_ref, base_ref, h2_ref, lg_ref):
    x1 = x_ref[0] + g1_ref[0] * jnp.dot(mx_ref[0], wo_ref[...], preferred_element_type=F32)
    ms = jnp.mean(x1 * x1, axis=-1, keepdims=True)
    h2 = x1 * lax.rsqrt(ms + EPS) * nw_ref[...] * (1.0 + sc_ref[0]) + sh_ref[0]
    h2_ref[...] = h2
    lg_ref[...] = lax.dot_general(wrT_ref[...], h2, (((1,), (1,)), ((), ())),
                                  precision=lax.Precision.HIGHEST, preferred_element_type=F32)
    hb = h2.astype(BF16)
    a = jnp.dot(hb, ws1_ref[...], preferred_element_type=F32)
    u = jnp.dot(hb, ws3_ref[...], preferred_element_type=F32)
    shared = jnp.dot((a * _sigmoid(a) * u).astype(BF16), ws2_ref[...], preferred_element_type=F32)
    base_ref[0] = x1 + g2_ref[0] * shared


def _post(x, mixed, g1, norm_w, sc, sh, g2, w_o_b, w_router_t, ws1_b, ws3_b, ws2_b):
    b, s, d = x.shape
    tm = POST_TM
    nt = s // tm
    vec = pl.BlockSpec((1, 1, d), lambda bi, i: (bi, 0, 0))
    const = lambda shape: pl.BlockSpec(shape, lambda bi, i: (0,) * len(shape))
    return pl.pallas_call(
        _post_kernel,
        grid=(b, nt),
        in_specs=[pl.BlockSpec((1, tm, d), lambda bi, i: (bi, i, 0)),
                  pl.BlockSpec((1, tm, d), lambda bi, i: (bi, i, 0)),
                  vec, const((1, d)), vec, vec, vec,
                  const((d, d)), const((N_EXPERTS, d)),
                  const((d, D_EXPERT)), const((d, D_EXPERT)), const((D_EXPERT, d))],
        out_specs=[pl.BlockSpec((1, tm, d), lambda bi, i: (bi, i, 0)),
                   pl.BlockSpec((tm, d), lambda bi, i: (bi * nt + i, 0)),
                   pl.BlockSpec((N_EXPERTS, tm), lambda bi, i: (0, bi * nt + i))],
        out_shape=[jax.ShapeDtypeStruct((b, s, d), F32),
                   jax.ShapeDtypeStruct((b * s, d), F32),
                   jax.ShapeDtypeStruct((N_EXPERTS, b * s), F32)],
        compiler_params=_params("arbitrary", "arbitrary"),
        name="post",
    )(x, mixed, g1, norm_w.reshape(1, d), sc, sh, g2, w_o_b, w_router_t, ws1_b, ws3_b, ws2_b)


def _first_max(cur, ids, sentinel):
    m = jnp.max(cur, axis=0, keepdims=True)
    first = jnp.min(jnp.where(cur == m, ids, sentinel), axis=0, keepdims=True)
    return m, first


def _route_kernel(lg_ref, rb_ref, idx_ref, w_ref):
    tn = lg_ref.shape[1]
    gsz = N_EXPERTS // N_GROUPS
    scores = _sigmoid(lg_ref[...])
    sel = scores + rb_ref[...]
    sub = lax.broadcasted_iota(I32, (gsz, tn), 0).astype(F32)

    gs = []
    for g in range(N_GROUPS):
        v = sel[g * gsz:(g + 1) * gsz]
        m1, first = _first_max(v, sub, float(gsz))
        m2 = jnp.max(jnp.where(sub == first, -jnp.inf, v), axis=0, keepdims=True)
        gs.append(m1 + m2)
    cur = jnp.concatenate(gs, axis=0)
    gid = lax.broadcasted_iota(I32, (N_GROUPS, tn), 0).astype(F32)
    keep = jnp.zeros((N_GROUPS, tn), F32)
    for _ in range(TOPK_GROUPS):
        _, first = _first_max(cur, gid, float(N_GROUPS))
        hit = gid == first
        keep = jnp.where(hit, 1.0, keep)
        cur = jnp.where(hit, -jnp.inf, cur)

    cur = jnp.concatenate(
        [jnp.where(keep[g:g + 1] > 0.0, sel[g * gsz:(g + 1) * gsz], NEG) for g in range(N_GROUPS)],
        axis=0)
    eid = lax.broadcasted_iota(I32, (N_EXPERTS, tn), 0).astype(F32)
    ids, ws = [], []
    for _ in range(TOP_K):
        _, first = _first_max(cur, eid, float(N_EXPERTS))
        hit = eid == first
        ids.append(first)
        ws.append(jnp.sum(jnp.where(hit, scores, 0.0), axis=0, keepdims=True))
        cur = jnp.where(hit, -jnp.inf, cur)
    w = jnp.concatenate(ws, axis=0)
    idx_ref[...] = jnp.concatenate(ids, axis=0).astype(I32)
    w_ref[...] = w / jnp.sum(w, axis=0, keepdims=True) * ROUTED_SCALE


def _route(logits_t, router_bias):
    e, n = logits_t.shape
    tn = ROUTE_TN
    return pl.pallas_call(
        _route_kernel,
        grid=(n // tn,),
        in_specs=[pl.BlockSpec((e, tn), lambda j: (0, j)),
                  pl.BlockSpec((e, 1), lambda j: (0, 0))],
        out_specs=[pl.BlockSpec((TOP_K, tn), lambda j: (0, j)),
                   pl.BlockSpec((TOP_K, tn), lambda j: (0, j))],
        out_shape=[jax.ShapeDtypeStruct((TOP_K, n), I32),
                   jax.ShapeDtypeStruct((TOP_K, n), F32)],
        compiler_params=_params("arbitrary"),
        name="route",
    )(logits_t, router_bias.reshape(e, 1))


def _experts_kernel(te_ref, nu_ref, rt_cur_ref, rt_nxt_ref, x_hbm, w1_ref, w3_ref, w2_ref, y_ref,
                    xbuf, sem, w1b, w3b, w2b):
    j = pl.program_id(0)
    n_used = nu_ref[0]
    slot = lax.rem(j, 2)

    def row_copy(tok, dst_slot, r):
        return pltpu.make_async_copy(x_hbm.at[pl.ds(tok, 1), :],
                                     xbuf.at[dst_slot, pl.ds(r, 1), :], sem.at[dst_slot])

    def issue(rt_ref, dst_slot):
        def body(r, carry):
            row_copy(rt_ref[0, 0, r], dst_slot, r).start()
            return carry
        lax.fori_loop(0, MOE_TM, body, 0)

    @pl.when(j == 0)
    def _():
        issue(rt_cur_ref, 0)

    @pl.when(j + 1 < n_used)
    def _():
        issue(rt_nxt_ref, 1 - slot)

    @pl.when(j < n_used)
    def _():
        pltpu.make_async_copy(xbuf.at[slot], xbuf.at[slot], sem.at[slot]).wait()

        @pl.when((j == 0) | (te_ref[j] != te_ref[jnp.maximum(j - 1, 0)]))
        def _():
            w1b[...] = w1_ref[0].astype(BF16)
            w3b[...] = w3_ref[0].astype(BF16)
            w2b[...] = w2_ref[0].astype(BF16)

        x = xbuf[slot].astype(BF16)
        a = jnp.dot(x, w1b[...], preferred_element_type=F32)
        u = jnp.dot(x, w3b[...], preferred_element_type=F32)
        y_ref[...] = jnp.dot((a * _sigmoid(a) * u).astype(BF16), w2b[...], preferred_element_type=F32)

    @pl.when(j >= n_used)
    def _():
        y_ref[...] = jnp.zeros(y_ref.shape, F32)


def _experts(h2, tile_expert, n_used, row_token, w1, w3, w2):
    n, d = h2.shape
    nt = row_token.shape[0]
    f = w1.shape[2]
    grid_spec = pltpu.PrefetchScalarGridSpec(
        num_scalar_prefetch=2,
        grid=(nt,),
        in_specs=[pl.BlockSpec((1, 1, MOE_TM), lambda j, te, nu: (j, 0, 0), memory_space=pltpu.SMEM),
                  pl.BlockSpec((1, 1, MOE_TM), lambda j, te, nu: (jnp.minimum(j + 1, nt - 1), 0, 0),
                               memory_space=pltpu.SMEM),
                  pl.BlockSpec(memory_space=pl.ANY),
                  pl.BlockSpec((1, d, f), lambda j, te, nu: (te[j], 0, 0)),
                  pl.BlockSpec((1, d, f), lambda j, te, nu: (te[j], 0, 0)),
                  pl.BlockSpec((1, f, d), lambda j, te, nu: (te[j], 0, 0))],
        out_specs=pl.BlockSpec((MOE_TM, d), lambda j, te, nu: (j, 0)),
        scratch_shapes=[pltpu.VMEM((2, MOE_TM, d), F32),
                        pltpu.SemaphoreType.DMA((2,)),
                        pltpu.VMEM((d, f), BF16),
                        pltpu.VMEM((d, f), BF16),
                        pltpu.VMEM((f, d), BF16)],
    )
    return pl.pallas_call(
        _experts_kernel,
        grid_spec=grid_spec,
        out_shape=jax.ShapeDtypeStruct((nt * MOE_TM, d), F32),
        compiler_params=_params("arbitrary"),
        name="experts",
    )(tile_expert, n_used, row_token, row_token, h2, w1, w3, w2)


def _combine_kernel(pos_cur_ref, pos_nxt_ref, ys_hbm, base_ref, g2_ref, w_ref, o_ref, buf, sem):
    bi, i = pl.program_id(0), pl.program_id(1)
    step = bi * pl.num_programs(1) + i
    n_steps = pl.num_programs(0) * pl.num_programs(1)
    slot = lax.rem(step, 2)

    def issue(pos_ref, dst_slot):
        def body(r, carry):
            for k in range(TOP_K):
                pltpu.make_async_copy(ys_hbm.at[pl.ds(pos_ref[0, 0, r * TOP_K + k], 1), :],
                                      buf.at[dst_slot, k, pl.ds(r, 1), :], sem.at[dst_slot]).start()
            return carry
        lax.fori_loop(0, COMB_TM, body, 0)

    @pl.when(step == 0)
    def _():
        issue(pos_cur_ref, 0)

    @pl.when(step + 1 < n_steps)
    def _():
        issue(pos_nxt_ref, 1 - slot)

    pltpu.make_async_copy(buf.at[slot], buf.at[slot], sem.at[slot]).wait()
    w = w_ref[...]
    acc = jnp.zeros(o_ref.shape[1:], F32)
    for k in range(TOP_K):
        acc = acc + w[:, k:k + 1] * buf[slot, k]
    o_ref[0] = base_ref[0] + g2_ref[0] * acc


def _combine(ys, pos, w_sel, base, g2):
    b, s, d = base.shape
    tm = COMB_TM
    nt = s // tm
    n_tiles = b * nt
    pos_t = pos.reshape(n_tiles, 1, tm * TOP_K)
    return pl.pallas_call(
        _combine_kernel,
        grid=(b, nt),
        in_specs=[pl.BlockSpec((1, 1, tm * TOP_K), lambda bi, i: (bi * nt + i, 0, 0),
                               memory_space=pltpu.SMEM),
                  pl.BlockSpec((1, 1, tm * TOP_K),
                               lambda bi, i: (jnp.minimum(bi * nt + i + 1, n_tiles - 1), 0, 0),
                               memory_space=pltpu.SMEM),
                  pl.BlockSpec(memory_space=pl.ANY),
                  pl.BlockSpec((1, tm, d), lambda bi, i: (bi, i, 0)),
                  pl.BlockSpec((1, 1, d), lambda bi, i: (bi, 0, 0)),
                  pl.BlockSpec((tm, TOP_K), lambda bi, i: (bi * nt + i, 0))],
        out_specs=pl.BlockSpec((1, tm, d), lambda bi, i: (bi, i, 0)),
        out_shape=jax.ShapeDtypeStruct((b, s, d), F32),
        scratch_shapes=[pltpu.VMEM((2, TOP_K, tm, d), F32),
                        pltpu.SemaphoreType.DMA((2,))],
        compiler_params=_params("arbitrary", "arbitrary"),
        name="combine",
    )(pos_t, pos_t, ys, base, g2, w_sel)


def _dispatch_plan(top_e):
    n = top_e.shape[0]
    n_tiles = n * TOP_K // MOE_TM + N_EXPERTS
    hot = jnp.sum(jax.nn.one_hot(top_e, N_EXPERTS, dtype=I32), axis=1)
    before = jnp.cumsum(hot, axis=0) - hot
    counts = before[-1] + hot[-1]
    tiles = (counts + MOE_TM - 1) // MOE_TM
    tile_end = jnp.cumsum(tiles)
    row_start = (tile_end - tiles) * MOE_TM
    pos = row_start[top_e] + jnp.take_along_axis(before, top_e, axis=1)
    tok = jnp.broadcast_to(jnp.arange(n, dtype=I32)[:, None], pos.shape)
    row_token = jnp.zeros((n_tiles * MOE_TM,), I32).at[pos.reshape(-1)].set(
        tok.reshape(-1), unique_indices=True)
    tile_expert = jnp.minimum(
        jnp.searchsorted(tile_end, jnp.arange(n_tiles, dtype=I32), side="right"), N_EXPERTS - 1)
    n_used = tile_end[-1:].astype(I32)
    return pos.astype(I32), row_token.reshape(n_tiles, 1, MOE_TM), tile_expert.astype(I32), n_used


def _layer(x, c, rel_bias, norm1_w, norm2_w, w_ada, b_ada, w_in, conv_w, w_conv_out, q_norm_w,
           k_norm_w, idx_k_norm_w, idx_k_norm_b, w_attn_out, w_o, w_router, router_bias,
           w1, w3, w2, ws1, ws3, ws2):
    b, s, d = x.shape
    mod = _mod(c, w_ada, b_ada).reshape(b, 6, 1, d)
    sh1, sc1, g1, sh2, sc2, g2 = [mod[:, m] for m in range(6)]

    cols = [w_in[:, _SEG[name][0]:_SEG[name][1]] for name in _ORDER]
    cols.append(jnp.zeros((d, PROJ_W - sum(c.shape[1] for c in cols)), w_in.dtype))
    w_in_p = jnp.concatenate(cols, axis=1).astype(BF16)

    proj = _proj(x, norm1_w, sc1, sh1, w_in_p)
    qT, qiT, kh, vT, kin, wT = _prep(proj, q_norm_w, k_norm_w, idx_k_norm_w, idx_k_norm_b)
    attn = _attention(qT, qiT, wT, kh, vT, kin, _bias_strips(rel_bias), rel_bias[REL_BUCKETS - 1])
    mixed = _mix(proj, attn, conv_w, w_conv_out.astype(BF16), w_attn_out.astype(BF16))
    base, h2, logits_t = _post(x, mixed, g1, norm2_w, sc2, sh2, g2, w_o.astype(BF16), w_router.T,
                               ws1.astype(BF16), ws3.astype(BF16), ws2.astype(BF16))
    top_e_t, w_sel_t = _route(logits_t, router_bias)
    pos, row_token, tile_expert, n_used = _dispatch_plan(top_e_t.T)
    ys = _experts(h2, tile_expert, n_used, row_token, w1, w3, w2)
    return _combine(ys, pos, w_sel_t.T, base, g2)


def kernel(x, c, rel_bias, norm1_w, norm2_w, w_ada, b_ada, w_in, conv_w, w_conv_out, q_norm_w,
           k_norm_w, idx_k_norm_w, idx_k_norm_b, w_attn_out, w_o, w_router, router_bias,
           w1, w3, w2, ws1, ws3, ws2):
    assert x.shape[1] % PROJ_TM == 0 and x.shape[2] == D_MODEL and w_ada.shape[0] == 1
    return _layer(x, c, rel_bias, norm1_w[0], norm2_w[0], w_ada[0], b_ada[0], w_in[0], conv_w[0],
                  w_conv_out[0], q_norm_w[0], k_norm_w[0], idx_k_norm_w[0], idx_k_norm_b[0],
                  w_attn_out[0], w_o[0], w_router[0], router_bias[0], w1[0], w3[0], w2[0],
                  ws1[0], ws3[0], ws2[0])
```

```python
import functools
import math

import numpy as np
import jax
import jax.numpy as jnp
from jax import lax
from jax.experimental import pallas as pl
from jax.experimental.pallas import tpu as pltpu

F32 = jnp.float32
BF16 = jnp.bfloat16
I32 = jnp.int32

D_MODEL = 2048
CONV_WIDTH = D_MODEL // 2
CONV_K = 3
N_HEADS = 16
N_KV_HEADS = 4
HEAD_DIM = 64
ATTN_WIDTH = N_HEADS * HEAD_DIM
KV_WIDTH = N_KV_HEADS * HEAD_DIM
IDX_HEADS = 16
IDX_DIM = 64
IDX_TOPK_MAX = 256
REL_BUCKETS = 32
REL_MAX_DIST = 128
N_EXPERTS = 64
N_GROUPS = 8
TOPK_GROUPS = 4
TOP_K = 8
D_EXPERT = 512
ROUTED_SCALE = 2.5
EPS = 1e-6
NEG = -1e30

REP = N_HEADS // N_KV_HEADS

LANES = 128
VMEM_LIMIT = 56 * 1024 * 1024

TQ = 128
KB = 128
KC = 4 * KB
PROJ_TM = 1024
PROJ_TN = 896
PREP_TM = 512
MIX_TM = 512
POST_TM = 512
ROUTE_TN = 512
MOE_TM = 512
COMB_TM = 128
PLAN_TN = 512
ISSUE_UNROLL = 8

QK_DIM = 128
LOG2E = math.log2(math.e)

_SEG = dict(cb=(0, 1024), cc=(1024, 2048), cu=(2048, 3072), q=(3072, 4096), k=(4096, 4352),
            v=(4352, 4608), qi=(4608, 5632), ki=(5632, 5696), wi=(5696, 5712),
            ga=(5712, 7760), gb=(7760, 9808))
_ORDER = ["ga", "gb", "cb", "cc", "cu", "q", "qi", "k", "v", "ki", "wi"]
PROJ_W = 9856
COL_GA, COL_GB = 0, 1
COL_CB, COL_CC, COL_CU, COL_Q, COL_QI = 4, 5, 6, 7, 8
COL_K, COL_V = 36, 37
COL_KW = 76

INT_MIN = -(2 ** 31)
INT_MAX = 2 ** 31 - 1


def _sortable_key_of(x):
    bits = int(np.float32(x).view(np.int32))
    return bits ^ 0x7FFFFFFF if bits < 0 else bits


NEG_KEY = _sortable_key_of(NEG)


def _sigmoid(x):
    return 1.0 / (1.0 + jnp.exp(-x))


def _params(*sem):
    return pltpu.CompilerParams(dimension_semantics=sem, vmem_limit_bytes=VMEM_LIMIT)


def _mod_kernel(c_ref, w_ref, b_ref, o_ref):
    c = c_ref[...]
    s = (c * _sigmoid(c)).astype(BF16)
    o_ref[...] = jnp.dot(s, w_ref[...].astype(BF16), preferred_element_type=F32) + b_ref[...]


def _mod(c, w_ada, b_ada):
    b = c.shape[0]
    rows = 8
    cp = jnp.pad(c, ((0, rows - b), (0, 0)))
    n = w_ada.shape[1]
    tn = 1024
    out = pl.pallas_call(
        _mod_kernel,
        grid=(n // tn,),
        in_specs=[pl.BlockSpec((rows, D_MODEL), lambda j: (0, 0)),
                  pl.BlockSpec((D_MODEL, tn), lambda j: (0, j)),
                  pl.BlockSpec((1, tn), lambda j: (0, j))],
        out_specs=pl.BlockSpec((rows, tn), lambda j: (0, j)),
        out_shape=jax.ShapeDtypeStruct((rows, n), F32),
        compiler_params=_params("arbitrary"),
        name="mod",
    )(cp, w_ada, b_ada.reshape(1, n))
    return out[:b]


def _proj_kernel(x_ref, nw_ref, sc_ref, sh_ref, w_ref, o_ref, h_ref):
    @pl.when(pl.program_id(2) == 0)
    def _():
        x = x_ref[0]
        ms = jnp.mean(x * x, axis=-1, keepdims=True)
        y = x * lax.rsqrt(ms + EPS) * nw_ref[...]
        h_ref[...] = (y * (1.0 + sc_ref[0]) + sh_ref[0]).astype(BF16)

    o_ref[0] = jnp.dot(h_ref[...], w_ref[...], preferred_element_type=F32).astype(BF16)


def _proj(x, norm_w, sc, sh, w_in_p):
    b, s, d = x.shape
    tm, tn = PROJ_TM, PROJ_TN
    return pl.pallas_call(
        _proj_kernel,
        grid=(b, s // tm, PROJ_W // tn),
        in_specs=[pl.BlockSpec((1, tm, d), lambda bi, i, j: (bi, i, 0)),
                  pl.BlockSpec((1, d), lambda bi, i, j: (0, 0)),
                  pl.BlockSpec((1, 1, d), lambda bi, i, j: (bi, 0, 0)),
                  pl.BlockSpec((1, 1, d), lambda bi, i, j: (bi, 0, 0)),
                  pl.BlockSpec((d, tn), lambda bi, i, j: (0, j))],
        out_specs=pl.BlockSpec((1, tm, tn), lambda bi, i, j: (bi, i, j)),
        out_shape=jax.ShapeDtypeStruct((b, s, PROJ_W), BF16),
        scratch_shapes=[pltpu.VMEM((tm, d), BF16)],
        compiler_params=_params("arbitrary", "arbitrary", "arbitrary"),
        name="proj",
    )(x, norm_w.reshape(1, d), sc, sh, w_in_p)


def _prep_kernel(q_ref, qi_ref, k_ref, v_ref, kw_ref, qnw_ref, knw_ref, inw_ref, inb_ref, qtail_ref,
                 qT_ref, qiT_ref, kh_ref, vT_ref, kin_ref, wT_ref):
    tm = q_ref.shape[1]
    nqb = tm // TQ

    q3 = q_ref[0].astype(F32).T.reshape(N_HEADS, HEAD_DIM, tm)
    ms = jnp.mean(q3 * q3, axis=1, keepdims=True)
    qn = q3 * lax.rsqrt(ms + EPS) * (qnw_ref[...] * (HEAD_DIM ** -0.5 * LOG2E))
    qi3 = qi_ref[0].astype(F32).T.reshape(IDX_HEADS, IDX_DIM, tm)
    for jb in range(nqb):
        for h in range(N_HEADS):
            g, r = divmod(h, REP)
            qT_ref[0, jb, g, :HEAD_DIM, r * TQ:(r + 1) * TQ] = qn[h, :, jb * TQ:(jb + 1) * TQ].astype(BF16)
        for g in range(N_KV_HEADS):
            qT_ref[0, jb, g, HEAD_DIM:, :] = qtail_ref[g]
        for h in range(IDX_HEADS):
            qiT_ref[0, jb, :, h * TQ:(h + 1) * TQ] = qi3[h, :, jb * TQ:(jb + 1) * TQ].astype(BF16)

    k = k_ref[0].astype(F32)
    ones_cols = jnp.where(lax.broadcasted_iota(I32, (tm, QK_DIM - HEAD_DIM), 1) < 2, 1.0, 0.0)
    for g in range(N_KV_HEADS):
        kg = k[:, g * HEAD_DIM:(g + 1) * HEAD_DIM]
        msk = jnp.mean(kg * kg, axis=-1, keepdims=True)
        kn = kg * lax.rsqrt(msk + EPS) * knw_ref[...]
        kh_ref[0, g] = jnp.concatenate([kn, ones_cols], axis=1).astype(BF16)

    v3 = v_ref[0].astype(F32).T.reshape(N_KV_HEADS, HEAD_DIM, tm)
    for g in range(N_KV_HEADS):
        for jb in range(tm // KB):
            vT_ref[0, g, jb] = v3[g, :, jb * KB:(jb + 1) * KB].astype(BF16)

    kw = kw_ref[0].astype(F32)
    ki = kw[:, :IDX_DIM]
    mu = jnp.mean(ki, axis=-1, keepdims=True)
    var = jnp.mean(jnp.square(ki - mu), axis=-1, keepdims=True)
    kin_ref[0] = ((ki - mu) * lax.rsqrt(var + EPS) * inw_ref[...] + inb_ref[...]).astype(BF16)
    wiT = kw.T[IDX_DIM:IDX_DIM + IDX_HEADS] * (IDX_HEADS ** -0.5 * IDX_DIM ** -0.5)
    for jb in range(nqb):
        wT_ref[0, jb] = wiT[:, jb * TQ:(jb + 1) * TQ]


def _prep(proj, q_norm_w, k_norm_w, idx_k_norm_w, idx_k_norm_b, far_bias):
    b, s, _ = proj.shape
    tm = PREP_TM
    nqb = tm // TQ
    nq = s // TQ
    fb2 = (far_bias * LOG2E).reshape(N_KV_HEADS, REP)
    hi = fb2.astype(BF16)
    lo = (fb2 - hi.astype(F32)).astype(BF16)
    tail = jnp.stack([hi, lo], axis=1)
    tail = jnp.broadcast_to(tail[..., None], (N_KV_HEADS, 2, REP, TQ)).reshape(N_KV_HEADS, 2, REP * TQ)
    qtail = jnp.pad(tail, ((0, 0), (0, QK_DIM - HEAD_DIM - 2), (0, 0)))
    return pl.pallas_call(
        _prep_kernel,
        grid=(b, s // tm),
        in_specs=[pl.BlockSpec((1, tm, ATTN_WIDTH), lambda bi, i: (bi, i, COL_Q)),
                  pl.BlockSpec((1, tm, IDX_HEADS * IDX_DIM), lambda bi, i: (bi, i, COL_QI)),
                  pl.BlockSpec((1, tm, KV_WIDTH), lambda bi, i: (bi, i, COL_K)),
                  pl.BlockSpec((1, tm, KV_WIDTH), lambda bi, i: (bi, i, COL_V)),
                  pl.BlockSpec((1, tm, LANES), lambda bi, i: (bi, i, COL_KW)),
                  pl.BlockSpec((1, HEAD_DIM, 1), lambda bi, i: (0, 0, 0)),
                  pl.BlockSpec((1, HEAD_DIM), lambda bi, i: (0, 0)),
                  pl.BlockSpec((1, IDX_DIM), lambda bi, i: (0, 0)),
                  pl.BlockSpec((1, IDX_DIM), lambda bi, i: (0, 0)),
                  pl.BlockSpec((N_KV_HEADS, QK_DIM - HEAD_DIM, REP * TQ), lambda bi, i: (0, 0, 0))],
        out_specs=[pl.BlockSpec((1, nqb, N_KV_HEADS, QK_DIM, REP * TQ), lambda bi, i: (bi, i, 0, 0, 0)),
                   pl.BlockSpec((1, nqb, IDX_DIM, IDX_HEADS * TQ), lambda bi, i: (bi, i, 0, 0)),
                   pl.BlockSpec((1, N_KV_HEADS, tm, QK_DIM), lambda bi, i: (bi, 0, i, 0)),
                   pl.BlockSpec((1, N_KV_HEADS, tm // KB, HEAD_DIM, KB), lambda bi, i: (bi, 0, i, 0, 0)),
                   pl.BlockSpec((1, tm, IDX_DIM), lambda bi, i: (bi, i, 0)),
                   pl.BlockSpec((1, nqb, IDX_HEADS, TQ), lambda bi, i: (bi, i, 0, 0))],
        out_shape=[jax.ShapeDtypeStruct((b, nq, N_KV_HEADS, QK_DIM, REP * TQ), BF16),
                   jax.ShapeDtypeStruct((b, nq, IDX_DIM, IDX_HEADS * TQ), BF16),
                   jax.ShapeDtypeStruct((b, N_KV_HEADS, s, QK_DIM), BF16),
                   jax.ShapeDtypeStruct((b, N_KV_HEADS, s // KB, HEAD_DIM, KB), BF16),
                   jax.ShapeDtypeStruct((b, s, IDX_DIM), BF16),
                   jax.ShapeDtypeStruct((b, nq, IDX_HEADS, TQ), F32)],
        compiler_params=_params("arbitrary", "arbitrary"),
        name="prep",
    )(proj, proj, proj, proj, proj,
      q_norm_w.reshape(1, HEAD_DIM, 1), k_norm_w.reshape(1, HEAD_DIM),
      idx_k_norm_w.reshape(1, IDX_DIM), idx_k_norm_b.reshape(1, IDX_DIM), qtail)


def _t5_bucket_np(n):
    n = np.maximum(n, 0)
    max_exact = REL_BUCKETS // 2
    nf = np.maximum(n, 1).astype(np.float64)
    large = max_exact + np.floor(np.log(nf / max_exact) / math.log(REL_MAX_DIST / max_exact)
                                 * (REL_BUCKETS - max_exact)).astype(np.int64)
    large = np.minimum(large, REL_BUCKETS - 1)
    return np.where(n < max_exact, n, large).astype(np.int32)


def _bias_kernel(rb_ref, bucket_ref, o_ref):
    h = pl.program_id(0)
    bucket = bucket_ref[...]
    acc = jnp.zeros(bucket.shape, F32)
    for bkt in range(REL_BUCKETS):
        acc = jnp.where(bucket == bkt, rb_ref[bkt, h], acc)
    o_ref[0] = (acc - rb_ref[REL_BUCKETS - 1, h]) * LOG2E


def _bias_strips(rel_bias):
    kk = np.arange(3 * TQ)[:, None]
    qq = np.arange(TQ)[None, :]
    bucket = jnp.asarray(_t5_bucket_np(qq + TQ - kk))
    return pl.pallas_call(
        _bias_kernel,
        grid=(N_HEADS,),
        in_specs=[pl.BlockSpec(memory_space=pltpu.SMEM),
                  pl.BlockSpec((3 * TQ, TQ), lambda h: (0, 0))],
        out_specs=pl.BlockSpec((1, 3 * TQ, TQ), lambda h: (h, 0, 0)),
        out_shape=jax.ShapeDtypeStruct((N_HEADS, 3 * TQ, TQ), F32),
        compiler_params=_params("arbitrary"),
        name="bias",
    )(rel_bias, bucket)


def _attn_kernel(qT_ref, qiT_ref, wT_ref, kh_ref, vT_ref, kin_ref, biasT_ref, o_ref,
                 keys_ref, am_ref, amf_ref, p_ref, m_ref, l_ref, acc_ref, *, n_sel):
    i = pl.program_id(1)
    seq = kin_ref.shape[1]
    t0 = i * TQ
    n_chunks = lax.shift_right_logical(i + 4, 2)
    q_pos = t0 + lax.broadcasted_iota(I32, (KB, TQ), 1)
    k_off = lax.broadcasted_iota(I32, (KB, TQ), 0)

    qiT = qiT_ref[0, 0]
    wT = wT_ref[0, 0]

    def score_chunk(c, carry):
        k0 = pl.multiple_of(c * KC, KC)
        d = jnp.dot(kin_ref[0, pl.ds(k0, KC), :], qiT, preferred_element_type=F32)
        acc = jnp.zeros((KC, TQ), F32)
        for h in range(IDX_HEADS):
            acc = acc + wT[h:h + 1, :] * jnp.maximum(d[:, h * TQ:(h + 1) * TQ], 0.0)
        for j in range(KC // KB):
            blk = c * (KC // KB) + j
            sc = jnp.where(blk * KB + k_off <= q_pos, acc[j * KB:(j + 1) * KB], NEG)
            bits = lax.bitcast_convert_type(sc, I32)
            keys_ref[blk] = jnp.where(bits < 0, bits ^ 0x7FFFFFFF, bits)
        return carry

    lax.fori_loop(0, n_chunks, score_chunk, 0)

    n_virtual = (seq - n_chunks * KC).astype(F32)

    def count(pred):
        def body(c, acc):
            for j in range(KC // KB):
                blk = c * (KC // KB) + j
                hit = jnp.where(pred(keys_ref[blk], blk), 1.0, 0.0)
                acc = acc + jnp.sum(hit.reshape(KB // 8, 8, TQ), axis=0)
            return acc
        acc = lax.fori_loop(0, n_chunks, body, jnp.zeros((8, TQ), F32))
        return jnp.sum(acc, axis=0, keepdims=True)

    def bit_body(it, thr):
        cand = thr + lax.shift_left(jnp.int32(1), 31 - it)
        cnt = count(lambda kb, blk: kb >= cand) + jnp.where(NEG_KEY >= cand, n_virtual, 0.0)
        return jnp.where(cnt >= n_sel, cand, thr)

    thr = lax.fori_loop(0, 32, bit_body, jnp.full((1, TQ), INT_MIN, I32))

    cnt_gt = count(lambda kb, blk: kb > thr) + jnp.where(NEG_KEY > thr, n_virtual, 0.0)
    cnt_eq = count(lambda kb, blk: kb == thr) + jnp.where(NEG_KEY == thr, n_virtual, 0.0)
    need = n_sel - cnt_gt
    p_ref[...] = jnp.full((8, TQ), INT_MAX, I32)
    has_tie = jnp.max(jnp.where(cnt_eq > need, 1.0, 0.0)) > 0.0

    @pl.when(has_tie)
    def _():
        idx_bits = int(seq).bit_length()

        def p_body(it, p):
            cand = p | lax.shift_left(jnp.int32(1), idx_bits - 1 - it)
            below = count(lambda kb, blk: (kb == thr) & (blk * KB + k_off < cand))
            return jnp.where(below < need, cand, p)

        p = lax.fori_loop(0, idx_bits, p_body, jnp.zeros((1, TQ), I32))
        p_ref[...] = jnp.broadcast_to(p, (8, TQ))

    p_last = p_ref[0:1, :]

    bw = jnp.maximum(i - 1, 0)
    ws = pl.multiple_of(bw * KB, KB)

    def mask_chunk(c, carry):
        for j in range(KC // KB):
            blk = c * (KC // KB) + j
            kb = keys_ref[blk]
            k_pos = blk * KB + k_off
            sel = (kb > thr) | ((kb == thr) & (k_pos <= p_last))
            v = jnp.where(sel & (k_pos <= q_pos), 0.0, NEG)
            am_ref[blk] = v
            amf_ref[blk] = jnp.where(k_pos < ws, v, NEG)
        return carry

    lax.fori_loop(0, n_chunks, mask_chunk, 0)

    off = pl.multiple_of(TQ - (t0 - ws), TQ)
    n_far = lax.shift_right_logical(bw + 3, 2)
    am_near = jnp.concatenate([am_ref[bw], am_ref[bw + 1]], axis=0)
    for g in range(N_KV_HEADS):
        s = jnp.dot(kh_ref[0, g, pl.ds(ws, 2 * KB), :], qT_ref[0, 0, g], preferred_element_type=F32)
        s = jnp.concatenate(
            [s[:, r * TQ:(r + 1) * TQ] + (biasT_ref[REP * g + r, pl.ds(off, 2 * KB), :] + am_near)
             for r in range(REP)], axis=1)
        m = jnp.max(s, axis=0, keepdims=True)
        p = jnp.exp2(s - m)
        pb = p.astype(BF16)
        m_ref[g] = m
        l_ref[g] = jnp.sum(p, axis=0, keepdims=True)
        acc_ref[g] = (jnp.dot(vT_ref[0, g, bw], pb[:KB], preferred_element_type=F32)
                      + jnp.dot(vT_ref[0, g, bw + 1], pb[KB:], preferred_element_type=F32))

    def far_body(f, carry):
        k0 = pl.multiple_of(f * KC, KC)
        amf = jnp.concatenate([amf_ref[f * (KC // KB) + j] for j in range(KC // KB)], axis=0)
        for g in range(N_KV_HEADS):
            s = jnp.dot(kh_ref[0, g, pl.ds(k0, KC), :], qT_ref[0, 0, g], preferred_element_type=F32)
            s = jnp.concatenate([s[:, r * TQ:(r + 1) * TQ] + amf for r in range(REP)], axis=1)
            m_old = m_ref[g]
            m_new = jnp.maximum(m_old, jnp.max(s, axis=0, keepdims=True))
            alpha = jnp.exp2(m_old - m_new)
            p = jnp.exp2(s - m_new)
            vc = jnp.concatenate([vT_ref[0, g, f * (KC // KB) + j] for j in range(KC // KB)], axis=1)
            m_ref[g] = m_new
            l_ref[g] = alpha * l_ref[g] + jnp.sum(p, axis=0, keepdims=True)
            acc_ref[g] = alpha * acc_ref[g] + jnp.dot(vc, p.astype(BF16), preferred_element_type=F32)
        return carry

    lax.fori_loop(0, n_far, far_body, 0)

    outs = []
    for g in range(N_KV_HEADS):
        og = acc_ref[g] / l_ref[g]
        outs.extend(og[:, r * TQ:(r + 1) * TQ] for r in range(REP))
    o_ref[0] = jnp.concatenate(outs, axis=0).T.astype(BF16)


def _attention(qT, qiT, wT, kh, vT, kin, bias_strips):
    b, nq = qT.shape[0], qT.shape[1]
    s = kin.shape[1]
    n_sel = min(IDX_TOPK_MAX, s // 4)
    nb = s // KB
    return pl.pallas_call(
        functools.partial(_attn_kernel, n_sel=n_sel),
        grid=(b, nq),
        in_specs=[pl.BlockSpec((1, 1, N_KV_HEADS, QK_DIM, REP * TQ), lambda bi, i: (bi, i, 0, 0, 0)),
                  pl.BlockSpec((1, 1, IDX_DIM, IDX_HEADS * TQ), lambda bi, i: (bi, i, 0, 0)),
                  pl.BlockSpec((1, 1, IDX_HEADS, TQ), lambda bi, i: (bi, i, 0, 0)),
                  pl.BlockSpec((1, N_KV_HEADS, s, QK_DIM), lambda bi, i: (bi, 0, 0, 0)),
                  pl.BlockSpec((1, N_KV_HEADS, nb, HEAD_DIM, KB), lambda bi, i: (bi, 0, 0, 0, 0)),
                  pl.BlockSpec((1, s, IDX_DIM), lambda bi, i: (bi, 0, 0)),
                  pl.BlockSpec((N_HEADS, 3 * TQ, TQ), lambda bi, i: (0, 0, 0))],
        out_specs=pl.BlockSpec((1, TQ, ATTN_WIDTH), lambda bi, i: (bi, i, 0)),
        out_shape=jax.ShapeDtypeStruct((b, s, ATTN_WIDTH), BF16),
        scratch_shapes=[pltpu.VMEM((nb, KB, TQ), I32),
                        pltpu.VMEM((nb, KB, TQ), F32),
                        pltpu.VMEM((nb, KB, TQ), F32),
                        pltpu.VMEM((8, TQ), I32),
                        pltpu.VMEM((N_KV_HEADS, 1, REP * TQ), F32),
                        pltpu.VMEM((N_KV_HEADS, 1, REP * TQ), F32),
                        pltpu.VMEM((N_KV_HEADS, HEAD_DIM, REP * TQ), F32)],
        compiler_params=_params("arbitrary", "arbitrary"),
        name="attn",
    )(qT, qiT, wT, kh, vT, kin, bias_strips)


HALO = 16


def _mix_kernel(cb_ref, cc_ref, cu_ref, ccp_ref, cup_ref, at_ref, ga_ref, gb_ref,
                cw_ref, wco_ref, wao_ref, o_ref):
    tm = cb_ref.shape[1]
    v = cc_ref[0].astype(F32) * cu_ref[0].astype(F32)
    first = pl.program_id(1) == 0
    hv = ccp_ref[0].astype(F32) * cup_ref[0].astype(F32)
    hv = jnp.where(first, 0.0, hv)
    row = lax.broadcasted_iota(I32, v.shape, 0)
    v1 = jnp.where(row == 0, hv[HALO - 1:HALO], pltpu.roll(v, 1, 0))
    v2 = pltpu.roll(v, 2, 0)
    v2 = jnp.where(row == 0, hv[HALO - 2:HALO - 1], jnp.where(row == 1, hv[HALO - 1:HALO], v2))
    y = cw_ref[0:1] * v2 + cw_ref[1:2] * v1 + cw_ref[2:3] * v
    yc = (cb_ref[0].astype(F32) * y).astype(BF16)
    y_conv = jnp.dot(yc, wco_ref[...], preferred_element_type=F32)
    y_attn = jnp.dot(at_ref[0], wao_ref[...], preferred_element_type=F32)
    mixed = _sigmoid(ga_ref[0].astype(F32)) * y_conv + _sigmoid(gb_ref[0].astype(F32)) * y_attn
    o_ref[0] = mixed.astype(BF16)


def _mix(proj, attn, conv_w, w_conv_out_b, w_attn_out_b):
    b, s, _ = proj.shape
    tm = MIX_TM
    hb = tm // HALO
    prev = lambda col: (lambda bi, i: (bi, jnp.maximum(i * hb - 1, 0), col))
    return pl.pallas_call(
        _mix_kernel,
        grid=(b, s // tm),
        in_specs=[pl.BlockSpec((1, tm, CONV_WIDTH), lambda bi, i: (bi, i, COL_CB)),
                  pl.BlockSpec((1, tm, CONV_WIDTH), lambda bi, i: (bi, i, COL_CC)),
                  pl.BlockSpec((1, tm, CONV_WIDTH), lambda bi, i: (bi, i, COL_CU)),
                  pl.BlockSpec((1, HALO, CONV_WIDTH), prev(COL_CC)),
                  pl.BlockSpec((1, HALO, CONV_WIDTH), prev(COL_CU)),
                  pl.BlockSpec((1, tm, ATTN_WIDTH), lambda bi, i: (bi, i, 0)),
                  pl.BlockSpec((1, tm, D_MODEL), lambda bi, i: (bi, i, COL_GA)),
                  pl.BlockSpec((1, tm, D_MODEL), lambda bi, i: (bi, i, COL_GB)),
                  pl.BlockSpec((8, CONV_WIDTH), lambda bi, i: (0, 0)),
                  pl.BlockSpec((CONV_WIDTH, D_MODEL), lambda bi, i: (0, 0)),
                  pl.BlockSpec((ATTN_WIDTH, D_MODEL), lambda bi, i: (0, 0))],
        out_specs=pl.BlockSpec((1, tm, D_MODEL), lambda bi, i: (bi, i, 0)),
        out_shape=jax.ShapeDtypeStruct((b, s, D_MODEL), BF16),
        compiler_params=_params("arbitrary", "arbitrary"),
        name="mix",
    )(proj, proj, proj, proj, proj, attn, proj, proj,
      jnp.pad(conv_w, ((0, 8 - CONV_K), (0, 0))), w_conv_out_b, w_attn_out_b)


def _post_kernel(x_ref, mx_ref, g1_ref, nw_ref, sc_ref, sh_ref, g2_ref, wo_ref, wrT_ref,
                 ws1_ref, ws3_ref, ws2_ref, base_ref, h2_ref, lg_ref):
    x1 = x_ref[0] + g1_ref[0] * jnp.dot(mx_ref[0], wo_ref[...], preferred_element_type=F32)
    ms = jnp.mean(x1 * x1, axis=-1, keepdims=True)
    h2 = x1 * lax.rsqrt(ms + EPS) * nw_ref[...] * (1.0 + sc_ref[0]) + sh_ref[0]
    h2_ref[...] = h2
    lg_ref[...] = lax.dot_general(wrT_ref[...], h2, (((1,), (1,)), ((), ())),
                                  precision=lax.Precision.HIGHEST, preferred_element_type=F32)
    hb = h2.astype(BF16)
    a = jnp.dot(hb, ws1_ref[...], preferred_element_type=F32)
    u = jnp.dot(hb, ws3_ref[...], preferred_element_type=F32)
    shared = jnp.dot((a * _sigmoid(a) * u).astype(BF16), ws2_ref[...], preferred_element_type=F32)
    base_ref[0] = x1 + g2_ref[0] * shared


def _post(x, mixed, g1, norm_w, sc, sh, g2, w_o_b, w_router_t, ws1_b, ws3_b, ws2_b):
    b, s, d = x.shape
    tm = POST_TM
    nt = s // tm
    vec = pl.BlockSpec((1, 1, d), lambda bi, i: (bi, 0, 0))
    const = lambda shape: pl.BlockSpec(shape, lambda bi, i: (0,) * len(shape))
    return pl.pallas_call(
        _post_kernel,
        grid=(b, nt),
        in_specs=[pl.BlockSpec((1, tm, d), lambda bi, i: (bi, i, 0)),
                  pl.BlockSpec((1, tm, d), lambda bi, i: (bi, i, 0)),
                  vec, const((1, d)), vec, vec, vec,
                  const((d, d)), const((N_EXPERTS, d)),
                  const((d, D_EXPERT)), const((d, D_EXPERT)), const((D_EXPERT, d))],
        out_specs=[pl.BlockSpec((1, tm, d), lambda bi, i: (bi, i, 0)),
                   pl.BlockSpec((tm, d), lambda bi, i: (bi * nt + i, 0)),
                   pl.BlockSpec((N_EXPERTS, tm), lambda bi, i: (0, bi * nt + i))],
        out_shape=[jax.ShapeDtypeStruct((b, s, d), F32),
                   jax.ShapeDtypeStruct((b * s, d), F32),
                   jax.ShapeDtypeStruct((N_EXPERTS, b * s), F32)],
        compiler_params=_params("arbitrary", "arbitrary"),
        name="post",
    )(x, mixed, g1, norm_w.reshape(1, d), sc, sh, g2, w_o_b, w_router_t, ws1_b, ws3_b, ws2_b)


def _first_max(cur, ids, sentinel):
    m = jnp.max(cur, axis=0, keepdims=True)
    first = jnp.min(jnp.where(cur == m, ids, sentinel), axis=0, keepdims=True)
    return m, first


def _route_kernel(lg_ref, rb_ref, idx_ref, w_ref):
    tn = lg_ref.shape[1]
    gsz = N_EXPERTS // N_GROUPS
    scores = _sigmoid(lg_ref[...])
    sel = scores + rb_ref[...]
    sub = lax.broadcasted_iota(I32, (gsz, tn), 0).astype(F32)

    gs = []
    for g in range(N_GROUPS):
        v = sel[g * gsz:(g + 1) * gsz]
        m1, first = _first_max(v, sub, float(gsz))
        m2 = jnp.max(jnp.where(sub == first, -jnp.inf, v), axis=0, keepdims=True)
        gs.append(m1 + m2)
    cur = jnp.concatenate(gs, axis=0)
    gid = lax.broadcasted_iota(I32, (N_GROUPS, tn), 0).astype(F32)
    keep = jnp.zeros((N_GROUPS, tn), F32)
    for _ in range(TOPK_GROUPS):
        _, first = _first_max(cur, gid, float(N_GROUPS))
        hit = gid == first
        keep = jnp.where(hit, 1.0, keep)
        cur = jnp.where(hit, -jnp.inf, cur)

    cur = jnp.concatenate(
        [jnp.where(keep[g:g + 1] > 0.0, sel[g * gsz:(g + 1) * gsz], NEG) for g in range(N_GROUPS)],
        axis=0)
    eid = lax.broadcasted_iota(I32, (N_EXPERTS, tn), 0).astype(F32)
    ids, ws = [], []
    for _ in range(TOP_K):
        _, first = _first_max(cur, eid, float(N_EXPERTS))
        hit = eid == first
        ids.append(first)
        ws.append(jnp.sum(jnp.where(hit, scores, 0.0), axis=0, keepdims=True))
        cur = jnp.where(hit, -jnp.inf, cur)
    w = jnp.concatenate(ws, axis=0)
    idx_ref[...] = jnp.concatenate(ids, axis=0).astype(I32)
    w_ref[...] = w / jnp.sum(w, axis=0, keepdims=True) * ROUTED_SCALE


def _route(logits_t, router_bias):
    e, n = logits_t.shape
    tn = ROUTE_TN
    return pl.pallas_call(
        _route_kernel,
        grid=(n // tn,),
        in_specs=[pl.BlockSpec((e, tn), lambda j: (0, j)),
                  pl.BlockSpec((e, 1), lambda j: (0, 0))],
        out_specs=[pl.BlockSpec((TOP_K, tn), lambda j: (0, j)),
                   pl.BlockSpec((TOP_K, tn), lambda j: (0, j))],
        out_shape=[jax.ShapeDtypeStruct((TOP_K, n), I32),
                   jax.ShapeDtypeStruct((TOP_K, n), F32)],
        compiler_params=_params("arbitrary"),
        name="route",
    )(logits_t, router_bias.reshape(e, 1))


def _experts_kernel(te_ref, nu_ref, rt_cur_ref, rt_nxt_ref, x_hbm, w1_ref, w3_ref, w2_ref, y_ref,
                    xbuf, sem, w1b, w3b, w2b):
    j = pl.program_id(0)
    n_used = nu_ref[0]
    slot = lax.rem(j, 2)

    def issue(rt_ref, dst_slot):
        def body(i, carry):
            for u in range(ISSUE_UNROLL):
                r = i * ISSUE_UNROLL + u
                pltpu.make_async_copy(x_hbm.at[pl.ds(rt_ref[0, 0, r], 1), :],
                                      xbuf.at[dst_slot, pl.ds(r, 1), :], sem.at[dst_slot]).start()
            return carry
        lax.fori_loop(0, MOE_TM // ISSUE_UNROLL, body, 0)

    @pl.when(j == 0)
    def _():
        issue(rt_cur_ref, 0)

    for parity in (0, 1):
        @pl.when((j + 1 < n_used) & (slot == parity))
        def _(parity=parity):
            issue(rt_nxt_ref, 1 - parity)

    @pl.when(j < n_used)
    def _():
        pltpu.make_async_copy(xbuf.at[slot], xbuf.at[slot], sem.at[slot]).wait()

        @pl.when((j == 0) | (te_ref[j] != te_ref[jnp.maximum(j - 1, 0)]))
        def _():
            w1b[...] = w1_ref[0].astype(BF16)
            w3b[...] = w3_ref[0].astype(BF16)
            w2b[...] = w2_ref[0].astype(BF16)

        x = xbuf[slot].astype(BF16)
        a = jnp.dot(x, w1b[...], preferred_element_type=F32)
        u = jnp.dot(x, w3b[...], preferred_element_type=F32)
        y_ref[...] = jnp.dot((a * _sigmoid(a) * u).astype(BF16), w2b[...], preferred_element_type=F32)

    @pl.when(j >= n_used)
    def _():
        y_ref[...] = jnp.zeros(y_ref.shape, F32)


def _experts(h2, tile_expert, n_used, row_token, w1, w3, w2):
    n, d = h2.shape
    nt = row_token.shape[0]
    f = w1.shape[2]
    grid_spec = pltpu.PrefetchScalarGridSpec(
        num_scalar_prefetch=2,
        grid=(nt,),
        in_specs=[pl.BlockSpec((1, 1, MOE_TM), lambda j, te, nu: (j, 0, 0), memory_space=pltpu.SMEM),
                  pl.BlockSpec((1, 1, MOE_TM), lambda j, te, nu: (jnp.minimum(j + 1, nt - 1), 0, 0),
                               memory_space=pltpu.SMEM),
                  pl.BlockSpec(memory_space=pl.ANY),
                  pl.BlockSpec((1, d, f), lambda j, te, nu: (te[j], 0, 0)),
                  pl.BlockSpec((1, d, f), lambda j, te, nu: (te[j], 0, 0)),
                  pl.BlockSpec((1, f, d), lambda j, te, nu: (te[j], 0, 0))],
        out_specs=pl.BlockSpec((MOE_TM, d), lambda j, te, nu: (j, 0)),
        scratch_shapes=[pltpu.VMEM((2, MOE_TM, d), F32),
                        pltpu.SemaphoreType.DMA((2,)),
                        pltpu.VMEM((d, f), BF16),
                        pltpu.VMEM((d, f), BF16),
                        pltpu.VMEM((f, d), BF16)],
    )
    return pl.pallas_call(
        _experts_kernel,
        grid_spec=grid_spec,
        out_shape=jax.ShapeDtypeStruct((nt * MOE_TM, d), F32),
        compiler_params=_params("arbitrary"),
        name="experts",
    )(tile_expert, n_used, row_token, row_token, h2, w1, w3, w2)


def _combine_kernel(pos_cur_ref, pos_nxt_ref, ys_hbm, base_ref, g2_ref, w_ref, o_ref, buf, sem):
    bi, i = pl.program_id(0), pl.program_id(1)
    step = bi * pl.num_programs(1) + i
    n_steps = pl.num_programs(0) * pl.num_programs(1)
    slot = lax.rem(step, 2)

    def issue(pos_ref, dst_slot):
        def body(i, carry):
            for k in range(TOP_K):
                for u in range(2):
                    r = i * 2 + u
                    pltpu.make_async_copy(ys_hbm.at[pl.ds(pos_ref[0, 0, k * COMB_TM + r], 1), :],
                                          buf.at[dst_slot, k, pl.ds(r, 1), :], sem.at[dst_slot]).start()
            return carry
        lax.fori_loop(0, COMB_TM // 2, body, 0)

    @pl.when(step == 0)
    def _():
        issue(pos_cur_ref, 0)

    for parity in (0, 1):
        @pl.when((step + 1 < n_steps) & (slot == parity))
        def _(parity=parity):
            issue(pos_nxt_ref, 1 - parity)

    pltpu.make_async_copy(buf.at[slot], buf.at[slot], sem.at[slot]).wait()
    w = w_ref[...]
    acc = jnp.zeros(o_ref.shape[1:], F32)
    for k in range(TOP_K):
        acc = acc + w[:, k:k + 1] * buf[slot, k]
    o_ref[0] = base_ref[0] + g2_ref[0] * acc


def _combine(ys, pos_t, w_sel, base, g2):
    b, s, d = base.shape
    tm = COMB_TM
    nt = s // tm
    n_tiles = b * nt
    pos_t = pos_t.reshape(TOP_K, n_tiles, tm).transpose(1, 0, 2).reshape(n_tiles, 1, TOP_K * tm)
    return pl.pallas_call(
        _combine_kernel,
        grid=(b, nt),
        in_specs=[pl.BlockSpec((1, 1, tm * TOP_K), lambda bi, i: (bi * nt + i, 0, 0),
                               memory_space=pltpu.SMEM),
                  pl.BlockSpec((1, 1, tm * TOP_K),
                               lambda bi, i: (jnp.minimum(bi * nt + i + 1, n_tiles - 1), 0, 0),
                               memory_space=pltpu.SMEM),
                  pl.BlockSpec(memory_space=pl.ANY),
                  pl.BlockSpec((1, tm, d), lambda bi, i: (bi, i, 0)),
                  pl.BlockSpec((1, 1, d), lambda bi, i: (bi, 0, 0)),
                  pl.BlockSpec((tm, TOP_K), lambda bi, i: (bi * nt + i, 0))],
        out_specs=pl.BlockSpec((1, tm, d), lambda bi, i: (bi, i, 0)),
        out_shape=jax.ShapeDtypeStruct((b, s, d), F32),
        scratch_shapes=[pltpu.VMEM((2, TOP_K, tm, d), F32),
                        pltpu.SemaphoreType.DMA((2,))],
        compiler_params=_params("arbitrary", "arbitrary"),
        name="combine",
    )(pos_t, pos_t, ys, base, g2, w_sel)


def _plan_kernel(te_ref, tri_ref, low_ref, pos_ref, cnt_ref, run_ref, start_ref):
    phase, j = pl.program_id(0), pl.program_id(1)
    tn = te_ref.shape[1]
    te = te_ref[...]
    eid = lax.broadcasted_iota(I32, (N_EXPERTS, tn), 0)
    hot = jnp.zeros((N_EXPERTS, tn), F32)
    for k in range(TOP_K):
        hot = hot + jnp.where(te[k:k + 1, :] == eid, 1.0, 0.0)
    tile_count = jnp.sum(hot, axis=1, keepdims=True)

    @pl.when((phase == 0) & (j == 0))
    def _():
        run_ref[...] = jnp.zeros(run_ref.shape, F32)

    @pl.when((phase == 1) & (j == 0))
    def _():
        counts = run_ref[...]
        cnt_ref[...] = counts
        tiles = jnp.floor((counts + (MOE_TM - 1)) * (1.0 / MOE_TM))
        start_ref[...] = jnp.dot(low_ref[...], tiles.astype(BF16), preferred_element_type=F32) * MOE_TM
        run_ref[...] = jnp.zeros(run_ref.shape, F32)

    @pl.when(phase == 1)
    def _():
        before = jnp.dot(hot.astype(BF16), tri_ref[...], preferred_element_type=F32)
        val = before + (run_ref[:, 0:1] + start_ref[:, 0:1])
        rows = [jnp.sum(jnp.where(te[k:k + 1, :] == eid, val, 0.0), axis=0, keepdims=True)
                for k in range(TOP_K)]
        pos_ref[...] = jnp.concatenate(rows, axis=0).astype(I32)

    run_ref[...] = run_ref[...] + tile_count


def _dispatch_plan(top_e_t):
    n = top_e_t.shape[1]
    tn = PLAN_TN
    n_tiles = n * TOP_K // MOE_TM + N_EXPERTS
    tri = jnp.asarray(np.triu(np.ones((tn, tn), np.float32), 1), BF16)
    low = jnp.asarray(np.tril(np.ones((N_EXPERTS, N_EXPERTS), np.float32), -1), BF16)
    pos_t, cnt = pl.pallas_call(
        _plan_kernel,
        grid=(2, n // tn),
        in_specs=[pl.BlockSpec((TOP_K, tn), lambda ph, j: (0, j)),
                  pl.BlockSpec((tn, tn), lambda ph, j: (0, 0)),
                  pl.BlockSpec((N_EXPERTS, N_EXPERTS), lambda ph, j: (0, 0))],
        out_specs=[pl.BlockSpec((TOP_K, tn), lambda ph, j: (0, j * ph)),
                   pl.BlockSpec((N_EXPERTS, LANES), lambda ph, j: (0, 0))],
        out_shape=[jax.ShapeDtypeStruct((TOP_K, n), I32),
                   jax.ShapeDtypeStruct((N_EXPERTS, LANES), F32)],
        scratch_shapes=[pltpu.VMEM((N_EXPERTS, LANES), F32),
                        pltpu.VMEM((N_EXPERTS, LANES), F32)],
        compiler_params=_params("arbitrary", "arbitrary"),
        name="plan",
    )(top_e_t, tri, low)
    counts = cnt[:, 0].astype(I32)
    tile_end = jnp.cumsum((counts + MOE_TM - 1) // MOE_TM)
    tok = jnp.broadcast_to(jnp.arange(n, dtype=I32)[None, :], pos_t.shape)
    row_token = jnp.zeros((n_tiles * MOE_TM,), I32).at[pos_t.reshape(-1)].set(
        tok.reshape(-1), unique_indices=True)
    tile_expert = jnp.minimum(
        jnp.searchsorted(tile_end, jnp.arange(n_tiles, dtype=I32), side="right"), N_EXPERTS - 1)
    n_used = tile_end[-1:].astype(I32)
    return pos_t, row_token.reshape(n_tiles, 1, MOE_TM), tile_expert.astype(I32), n_used


def _layer(x, c, rel_bias, norm1_w, norm2_w, w_ada, b_ada, w_in, conv_w, w_conv_out, q_norm_w,
           k_norm_w, idx_k_norm_w, idx_k_norm_b, w_attn_out, w_o, w_router, router_bias,
           w1, w3, w2, ws1, ws3, ws2):
    b, s, d = x.shape
    mod = _mod(c, w_ada, b_ada).reshape(b, 6, 1, d)
    sh1, sc1, g1, sh2, sc2, g2 = [mod[:, m] for m in range(6)]

    cols = [w_in[:, _SEG[name][0]:_SEG[name][1]] for name in _ORDER]
    cols.append(jnp.zeros((d, PROJ_W - sum(col.shape[1] for col in cols)), w_in.dtype))
    w_in_p = jnp.concatenate(cols, axis=1).astype(BF16)

    proj = _proj(x, norm1_w, sc1, sh1, w_in_p)
    qT, qiT, kh, vT, kin, wT = _prep(proj, q_norm_w, k_norm_w, idx_k_norm_w, idx_k_norm_b,
                                     rel_bias[REL_BUCKETS - 1])
    attn = _attention(qT, qiT, wT, kh, vT, kin, _bias_strips(rel_bias))
    mixed = _mix(proj, attn, conv_w, w_conv_out.astype(BF16), w_attn_out.astype(BF16))
    base, h2, logits_t = _post(x, mixed, g1, norm2_w, sc2, sh2, g2, w_o.astype(BF16), w_router.T,
                               ws1.astype(BF16), ws3.astype(BF16), ws2.astype(BF16))
    top_e_t, w_sel_t = _route(logits_t, router_bias)
    pos_t, row_token, tile_expert, n_used = _dispatch_plan(top_e_t)
    ys = _experts(h2, tile_expert, n_used, row_token, w1, w3, w2)
    return _combine(ys, pos_t, w_sel_t.T, base, g2)


def kernel(x, c, rel_bias, norm1_w, norm2_w, w_ada, b_ada, w_in, conv_w, w_conv_out, q_norm_w,
           k_norm_w, idx_k_norm_w, idx_k_norm_b, w_attn_out, w_o, w_router, router_bias,
           w1, w3, w2, ws1, ws3, ws2):
    assert x.shape[1] % PROJ_TM == 0 and x.shape[2] == D_MODEL and w_ada.shape[0] == 1
    return _layer(x, c, rel_bias, norm1_w[0], norm2_w[0], w_ada[0], b_ada[0], w_in[0], conv_w[0],
                  w_conv_out[0], q_norm_w[0], k_norm_w[0], idx_k_norm_w[0], idx_k_norm_b[0],
                  w_attn_out[0], w_o[0], w_router[0], router_bias[0], w1[0], w3[0], w2[0],
                  ws1[0], ws3[0], ws2[0])
```

```python
import functools
import math

import numpy as np
import jax
import jax.numpy as jnp
from jax import lax
from jax.experimental import pallas as pl
from jax.experimental.pallas import tpu as pltpu

F32 = jnp.float32
BF16 = jnp.bfloat16
I32 = jnp.int32

D_MODEL = 2048
CONV_WIDTH = D_MODEL // 2
CONV_K = 3
N_HEADS = 16
N_KV_HEADS = 4
HEAD_DIM = 64
ATTN_WIDTH = N_HEADS * HEAD_DIM
KV_WIDTH = N_KV_HEADS * HEAD_DIM
IDX_HEADS = 16
IDX_DIM = 64
IDX_TOPK_MAX = 256
REL_BUCKETS = 32
REL_MAX_DIST = 128
N_EXPERTS = 64
N_GROUPS = 8
TOPK_GROUPS = 4
TOP_K = 8
D_EXPERT = 512
ROUTED_SCALE = 2.5
EPS = 1e-6
NEG = -1e30

REP = N_HEADS // N_KV_HEADS

LANES = 128
VMEM_LIMIT = 56 * 1024 * 1024

TQ = 128
KB = 128
KC = 4 * KB
PROJ_TM = 1024
PROJ_TN = 896
PREP_TM = 512
MIX_TM = 512
POST_TM = 512
ROUTE_TN = 512
MOE_TM = 512
COMB_TM = 128
DISP_TM = 128
PLAN_TN = 512

QK_DIM = 128
LOG2E = math.log2(math.e)

_SEG = dict(cb=(0, 1024), cc=(1024, 2048), cu=(2048, 3072), q=(3072, 4096), k=(4096, 4352),
            v=(4352, 4608), qi=(4608, 5632), ki=(5632, 5696), wi=(5696, 5712),
            ga=(5712, 7760), gb=(7760, 9808))
_ORDER = ["ga", "gb", "cb", "cc", "cu", "q", "qi", "k", "v", "ki", "wi"]
PROJ_W = 9856
COL_GA, COL_GB = 0, 1
COL_CB, COL_CC, COL_CU, COL_Q, COL_QI = 4, 5, 6, 7, 8
COL_K, COL_V = 36, 37
COL_KW = 76

INT_MIN = -(2 ** 31)
INT_MAX = 2 ** 31 - 1


def _sortable_key_of(x):
    bits = int(np.float32(x).view(np.int32))
    return bits ^ 0x7FFFFFFF if bits < 0 else bits


NEG_KEY = _sortable_key_of(NEG)


def _sigmoid(x):
    return 1.0 / (1.0 + jnp.exp(-x))


def _params(*sem):
    return pltpu.CompilerParams(dimension_semantics=sem, vmem_limit_bytes=VMEM_LIMIT)


def _mod_kernel(c_ref, w_ref, b_ref, o_ref):
    c = c_ref[...]
    s = (c * _sigmoid(c)).astype(BF16)
    o_ref[...] = jnp.dot(s, w_ref[...].astype(BF16), preferred_element_type=F32) + b_ref[...]


def _mod(c, w_ada, b_ada):
    b = c.shape[0]
    rows = 8
    cp = jnp.pad(c, ((0, rows - b), (0, 0)))
    n = w_ada.shape[1]
    tn = 1024
    out = pl.pallas_call(
        _mod_kernel,
        grid=(n // tn,),
        in_specs=[pl.BlockSpec((rows, D_MODEL), lambda j: (0, 0)),
                  pl.BlockSpec((D_MODEL, tn), lambda j: (0, j)),
                  pl.BlockSpec((1, tn), lambda j: (0, j))],
        out_specs=pl.BlockSpec((rows, tn), lambda j: (0, j)),
        out_shape=jax.ShapeDtypeStruct((rows, n), F32),
        compiler_params=_params("arbitrary"),
        name="mod",
    )(cp, w_ada, b_ada.reshape(1, n))
    return out[:b]


def _proj_kernel(x_ref, nw_ref, sc_ref, sh_ref, w_ref, o_ref, h_ref):
    @pl.when(pl.program_id(2) == 0)
    def _():
        x = x_ref[0]
        ms = jnp.mean(x * x, axis=-1, keepdims=True)
        y = x * lax.rsqrt(ms + EPS) * nw_ref[...]
        h_ref[...] = (y * (1.0 + sc_ref[0]) + sh_ref[0]).astype(BF16)

    o_ref[0] = jnp.dot(h_ref[...], w_ref[...], preferred_element_type=F32).astype(BF16)


def _proj(x, norm_w, sc, sh, w_in_p):
    b, s, d = x.shape
    tm, tn = PROJ_TM, PROJ_TN
    return pl.pallas_call(
        _proj_kernel,
        grid=(b, s // tm, PROJ_W // tn),
        in_specs=[pl.BlockSpec((1, tm, d), lambda bi, i, j: (bi, i, 0)),
                  pl.BlockSpec((1, d), lambda bi, i, j: (0, 0)),
                  pl.BlockSpec((1, 1, d), lambda bi, i, j: (bi, 0, 0)),
                  pl.BlockSpec((1, 1, d), lambda bi, i, j: (bi, 0, 0)),
                  pl.BlockSpec((d, tn), lambda bi, i, j: (0, j))],
        out_specs=pl.BlockSpec((1, tm, tn), lambda bi, i, j: (bi, i, j)),
        out_shape=jax.ShapeDtypeStruct((b, s, PROJ_W), BF16),
        scratch_shapes=[pltpu.VMEM((tm, d), BF16)],
        compiler_params=_params("arbitrary", "arbitrary", "arbitrary"),
        name="proj",
    )(x, norm_w.reshape(1, d), sc, sh, w_in_p)


def _prep_kernel(q_ref, qi_ref, k_ref, v_ref, kw_ref, qnw_ref, knw_ref, inw_ref, inb_ref, qtail_ref,
                 qT_ref, qiT_ref, kh_ref, vT_ref, kin_ref, wT_ref):
    tm = q_ref.shape[1]
    nqb = tm // TQ

    q3 = q_ref[0].astype(F32).T.reshape(N_HEADS, HEAD_DIM, tm)
    ms = jnp.mean(q3 * q3, axis=1, keepdims=True)
    qn = q3 * lax.rsqrt(ms + EPS) * (qnw_ref[...] * (HEAD_DIM ** -0.5 * LOG2E))
    qi3 = qi_ref[0].astype(F32).T.reshape(IDX_HEADS, IDX_DIM, tm)
    for jb in range(nqb):
        for h in range(N_HEADS):
            g, r = divmod(h, REP)
            qT_ref[0, jb, g, :HEAD_DIM, r * TQ:(r + 1) * TQ] = qn[h, :, jb * TQ:(jb + 1) * TQ].astype(BF16)
        for g in range(N_KV_HEADS):
            qT_ref[0, jb, g, HEAD_DIM:, :] = qtail_ref[g]
        for h in range(IDX_HEADS):
            qiT_ref[0, jb, :, h * TQ:(h + 1) * TQ] = qi3[h, :, jb * TQ:(jb + 1) * TQ].astype(BF16)

    k = k_ref[0].astype(F32)
    ones_cols = jnp.where(lax.broadcasted_iota(I32, (tm, QK_DIM - HEAD_DIM), 1) < 2, 1.0, 0.0)
    for g in range(N_KV_HEADS):
        kg = k[:, g * HEAD_DIM:(g + 1) * HEAD_DIM]
        msk = jnp.mean(kg * kg, axis=-1, keepdims=True)
        kn = kg * lax.rsqrt(msk + EPS) * knw_ref[...]
        kh_ref[0, g] = jnp.concatenate([kn, ones_cols], axis=1).astype(BF16)

    v3 = v_ref[0].astype(F32).T.reshape(N_KV_HEADS, HEAD_DIM, tm)
    for g in range(N_KV_HEADS):
        for jb in range(tm // KB):
            vT_ref[0, g, jb] = v3[g, :, jb * KB:(jb + 1) * KB].astype(BF16)

    kw = kw_ref[0].astype(F32)
    ki = kw[:, :IDX_DIM]
    mu = jnp.mean(ki, axis=-1, keepdims=True)
    var = jnp.mean(jnp.square(ki - mu), axis=-1, keepdims=True)
    kin_ref[0] = ((ki - mu) * lax.rsqrt(var + EPS) * inw_ref[...] + inb_ref[...]).astype(BF16)
    wiT = kw.T[IDX_DIM:IDX_DIM + IDX_HEADS] * (IDX_HEADS ** -0.5 * IDX_DIM ** -0.5)
    for jb in range(nqb):
        wT_ref[0, jb] = wiT[:, jb * TQ:(jb + 1) * TQ]


def _prep(proj, q_norm_w, k_norm_w, idx_k_norm_w, idx_k_norm_b, far_bias):
    b, s, _ = proj.shape
    tm = PREP_TM
    nqb = tm // TQ
    nq = s // TQ
    fb2 = (far_bias * LOG2E).reshape(N_KV_HEADS, REP)
    hi = fb2.astype(BF16)
    lo = (fb2 - hi.astype(F32)).astype(BF16)
    tail = jnp.stack([hi, lo], axis=1)
    tail = jnp.broadcast_to(tail[..., None], (N_KV_HEADS, 2, REP, TQ)).reshape(N_KV_HEADS, 2, REP * TQ)
    qtail = jnp.pad(tail, ((0, 0), (0, QK_DIM - HEAD_DIM - 2), (0, 0)))
    return pl.pallas_call(
        _prep_kernel,
        grid=(b, s // tm),
        in_specs=[pl.BlockSpec((1, tm, ATTN_WIDTH), lambda bi, i: (bi, i, COL_Q)),
                  pl.BlockSpec((1, tm, IDX_HEADS * IDX_DIM), lambda bi, i: (bi, i, COL_QI)),
                  pl.BlockSpec((1, tm, KV_WIDTH), lambda bi, i: (bi, i, COL_K)),
                  pl.BlockSpec((1, tm, KV_WIDTH), lambda bi, i: (bi, i, COL_V)),
                  pl.BlockSpec((1, tm, LANES), lambda bi, i: (bi, i, COL_KW)),
                  pl.BlockSpec((1, HEAD_DIM, 1), lambda bi, i: (0, 0, 0)),
                  pl.BlockSpec((1, HEAD_DIM), lambda bi, i: (0, 0)),
                  pl.BlockSpec((1, IDX_DIM), lambda bi, i: (0, 0)),
                  pl.BlockSpec((1, IDX_DIM), lambda bi, i: (0, 0)),
                  pl.BlockSpec((N_KV_HEADS, QK_DIM - HEAD_DIM, REP * TQ), lambda bi, i: (0, 0, 0))],
        out_specs=[pl.BlockSpec((1, nqb, N_KV_HEADS, QK_DIM, REP * TQ), lambda bi, i: (bi, i, 0, 0, 0)),
                   pl.BlockSpec((1, nqb, IDX_DIM, IDX_HEADS * TQ), lambda bi, i: (bi, i, 0, 0)),
                   pl.BlockSpec((1, N_KV_HEADS, tm, QK_DIM), lambda bi, i: (bi, 0, i, 0)),
                   pl.BlockSpec((1, N_KV_HEADS, tm // KB, HEAD_DIM, KB), lambda bi, i: (bi, 0, i, 0, 0)),
                   pl.BlockSpec((1, tm, IDX_DIM), lambda bi, i: (bi, i, 0)),
                   pl.BlockSpec((1, nqb, IDX_HEADS, TQ), lambda bi, i: (bi, i, 0, 0))],
        out_shape=[jax.ShapeDtypeStruct((b, nq, N_KV_HEADS, QK_DIM, REP * TQ), BF16),
                   jax.ShapeDtypeStruct((b, nq, IDX_DIM, IDX_HEADS * TQ), BF16),
                   jax.ShapeDtypeStruct((b, N_KV_HEADS, s, QK_DIM), BF16),
                   jax.ShapeDtypeStruct((b, N_KV_HEADS, s // KB, HEAD_DIM, KB), BF16),
                   jax.ShapeDtypeStruct((b, s, IDX_DIM), BF16),
                   jax.ShapeDtypeStruct((b, nq, IDX_HEADS, TQ), F32)],
        compiler_params=_params("arbitrary", "arbitrary"),
        name="prep",
    )(proj, proj, proj, proj, proj,
      q_norm_w.reshape(1, HEAD_DIM, 1), k_norm_w.reshape(1, HEAD_DIM),
      idx_k_norm_w.reshape(1, IDX_DIM), idx_k_norm_b.reshape(1, IDX_DIM), qtail)


def _t5_bucket_np(n):
    n = np.maximum(n, 0)
    max_exact = REL_BUCKETS // 2
    nf = np.maximum(n, 1).astype(np.float64)
    large = max_exact + np.floor(np.log(nf / max_exact) / math.log(REL_MAX_DIST / max_exact)
                                 * (REL_BUCKETS - max_exact)).astype(np.int64)
    large = np.minimum(large, REL_BUCKETS - 1)
    return np.where(n < max_exact, n, large).astype(np.int32)


def _bias_kernel(rb_ref, bucket_ref, o_ref):
    h = pl.program_id(0)
    bucket = bucket_ref[...]
    acc = jnp.zeros(bucket.shape, F32)
    for bkt in range(REL_BUCKETS):
        acc = jnp.where(bucket == bkt, rb_ref[bkt, h], acc)
    o_ref[0] = (acc - rb_ref[REL_BUCKETS - 1, h]) * LOG2E


def _bias_strips(rel_bias):
    kk = np.arange(3 * TQ)[:, None]
    qq = np.arange(TQ)[None, :]
    bucket = jnp.asarray(_t5_bucket_np(qq + TQ - kk))
    return pl.pallas_call(
        _bias_kernel,
        grid=(N_HEADS,),
        in_specs=[pl.BlockSpec(memory_space=pltpu.SMEM),
                  pl.BlockSpec((3 * TQ, TQ), lambda h: (0, 0))],
        out_specs=pl.BlockSpec((1, 3 * TQ, TQ), lambda h: (h, 0, 0)),
        out_shape=jax.ShapeDtypeStruct((N_HEADS, 3 * TQ, TQ), F32),
        compiler_params=_params("arbitrary"),
        name="bias",
    )(rel_bias, bucket)


def _attn_kernel(qT_ref, qiT_ref, wT_ref, kh_ref, vT_ref, kin_ref, biasT_ref, o_ref,
                 keys_ref, am_ref, amf_ref, p_ref, m_ref, l_ref, acc_ref, *, n_sel):
    i = pl.program_id(1)
    seq = kin_ref.shape[1]
    t0 = i * TQ
    n_chunks = lax.shift_right_logical(i + 4, 2)
    q_pos = t0 + lax.broadcasted_iota(I32, (KB, TQ), 1)
    k_off = lax.broadcasted_iota(I32, (KB, TQ), 0)

    qiT = qiT_ref[0, 0]
    wT = wT_ref[0, 0]

    def score_chunk(c, carry):
        k0 = pl.multiple_of(c * KC, KC)
        d = jnp.dot(kin_ref[0, pl.ds(k0, KC), :], qiT, preferred_element_type=F32)
        acc = jnp.zeros((KC, TQ), F32)
        for h in range(IDX_HEADS):
            acc = acc + wT[h:h + 1, :] * jnp.maximum(d[:, h * TQ:(h + 1) * TQ], 0.0)
        for j in range(KC // KB):
            blk = c * (KC // KB) + j
            sc = jnp.where(blk * KB + k_off <= q_pos, acc[j * KB:(j + 1) * KB], NEG)
            bits = lax.bitcast_convert_type(sc, I32)
            keys_ref[blk] = jnp.where(bits < 0, bits ^ 0x7FFFFFFF, bits)
        return carry

    lax.fori_loop(0, n_chunks, score_chunk, 0)

    n_virtual = (seq - n_chunks * KC).astype(F32)

    def count(pred):
        def body(c, acc):
            for j in range(KC // KB):
                blk = c * (KC // KB) + j
                hit = jnp.where(pred(keys_ref[blk], blk), 1.0, 0.0)
                acc = acc + jnp.sum(hit.reshape(KB // 8, 8, TQ), axis=0)
            return acc
        acc = lax.fori_loop(0, n_chunks, body, jnp.zeros((8, TQ), F32))
        return jnp.sum(acc, axis=0, keepdims=True)

    def bit_body(it, thr):
        cand = thr + lax.shift_left(jnp.int32(1), 31 - it)
        cnt = count(lambda kb, blk: kb >= cand) + jnp.where(NEG_KEY >= cand, n_virtual, 0.0)
        return jnp.where(cnt >= n_sel, cand, thr)

    thr = lax.fori_loop(0, 32, bit_body, jnp.full((1, TQ), INT_MIN, I32))

    cnt_gt = count(lambda kb, blk: kb > thr) + jnp.where(NEG_KEY > thr, n_virtual, 0.0)
    cnt_eq = count(lambda kb, blk: kb == thr) + jnp.where(NEG_KEY == thr, n_virtual, 0.0)
    need = n_sel - cnt_gt
    p_ref[...] = jnp.full((8, TQ), INT_MAX, I32)
    has_tie = jnp.max(jnp.where(cnt_eq > need, 1.0, 0.0)) > 0.0

    @pl.when(has_tie)
    def _():
        idx_bits = int(seq).bit_length()

        def p_body(it, p):
            cand = p | lax.shift_left(jnp.int32(1), idx_bits - 1 - it)
            below = count(lambda kb, blk: (kb == thr) & (blk * KB + k_off < cand))
            return jnp.where(below < need, cand, p)

        p = lax.fori_loop(0, idx_bits, p_body, jnp.zeros((1, TQ), I32))
        p_ref[...] = jnp.broadcast_to(p, (8, TQ))

    p_last = p_ref[0:1, :]

    bw = jnp.maximum(i - 1, 0)
    ws = pl.multiple_of(bw * KB, KB)

    def mask_chunk(c, carry):
        for j in range(KC // KB):
            blk = c * (KC // KB) + j
            kb = keys_ref[blk]
            k_pos = blk * KB + k_off
            sel = (kb > thr) | ((kb == thr) & (k_pos <= p_last))
            v = jnp.where(sel & (k_pos <= q_pos), 0.0, NEG)
            am_ref[blk] = v
            amf_ref[blk] = jnp.where(k_pos < ws, v, NEG)
        return carry

    lax.fori_loop(0, n_chunks, mask_chunk, 0)

    off = pl.multiple_of(TQ - (t0 - ws), TQ)
    n_far = lax.shift_right_logical(bw + 3, 2)
    am_near = jnp.concatenate([am_ref[bw], am_ref[bw + 1]], axis=0)
    for g in range(N_KV_HEADS):
        s = jnp.dot(kh_ref[0, g, pl.ds(ws, 2 * KB), :], qT_ref[0, 0, g], preferred_element_type=F32)
        s = jnp.concatenate(
            [s[:, r * TQ:(r + 1) * TQ] + (biasT_ref[REP * g + r, pl.ds(off, 2 * KB), :] + am_near)
             for r in range(REP)], axis=1)
        m = jnp.max(s, axis=0, keepdims=True)
        p = jnp.exp2(s - m)
        pb = p.astype(BF16)
        m_ref[g] = m
        l_ref[g] = jnp.sum(p, axis=0, keepdims=True)
        acc_ref[g] = (jnp.dot(vT_ref[0, g, bw], pb[:KB], preferred_element_type=F32)
                      + jnp.dot(vT_ref[0, g, bw + 1], pb[KB:], preferred_element_type=F32))

    def far_body(f, carry):
        k0 = pl.multiple_of(f * KC, KC)
        amf = jnp.concatenate([amf_ref[f * (KC // KB) + j] for j in range(KC // KB)], axis=0)
        for g in range(N_KV_HEADS):
            s = jnp.dot(kh_ref[0, g, pl.ds(k0, KC), :], qT_ref[0, 0, g], preferred_element_type=F32)
            s = jnp.concatenate([s[:, r * TQ:(r + 1) * TQ] + amf for r in range(REP)], axis=1)
            m_old = m_ref[g]
            m_new = jnp.maximum(m_old, jnp.max(s, axis=0, keepdims=True))
            alpha = jnp.exp2(m_old - m_new)
            p = jnp.exp2(s - m_new)
            vc = jnp.concatenate([vT_ref[0, g, f * (KC // KB) + j] for j in range(KC // KB)], axis=1)
            m_ref[g] = m_new
            l_ref[g] = alpha * l_ref[g] + jnp.sum(p, axis=0, keepdims=True)
            acc_ref[g] = alpha * acc_ref[g] + jnp.dot(vc, p.astype(BF16), preferred_element_type=F32)
        return carry

    lax.fori_loop(0, n_far, far_body, 0)

    outs = []
    for g in range(N_KV_HEADS):
        og = acc_ref[g] / l_ref[g]
        outs.extend(og[:, r * TQ:(r + 1) * TQ] for r in range(REP))
    o_ref[0] = jnp.concatenate(outs, axis=0).T.astype(BF16)


def _attention(qT, qiT, wT, kh, vT, kin, bias_strips):
    b, nq = qT.shape[0], qT.shape[1]
    s = kin.shape[1]
    n_sel = min(IDX_TOPK_MAX, s // 4)
    nb = s // KB
    return pl.pallas_call(
        functools.partial(_attn_kernel, n_sel=n_sel),
        grid=(b, nq),
        in_specs=[pl.BlockSpec((1, 1, N_KV_HEADS, QK_DIM, REP * TQ), lambda bi, i: (bi, i, 0, 0, 0)),
                  pl.BlockSpec((1, 1, IDX_DIM, IDX_HEADS * TQ), lambda bi, i: (bi, i, 0, 0)),
                  pl.BlockSpec((1, 1, IDX_HEADS, TQ), lambda bi, i: (bi, i, 0, 0)),
                  pl.BlockSpec((1, N_KV_HEADS, s, QK_DIM), lambda bi, i: (bi, 0, 0, 0)),
                  pl.BlockSpec((1, N_KV_HEADS, nb, HEAD_DIM, KB), lambda bi, i: (bi, 0, 0, 0, 0)),
                  pl.BlockSpec((1, s, IDX_DIM), lambda bi, i: (bi, 0, 0)),
                  pl.BlockSpec((N_HEADS, 3 * TQ, TQ), lambda bi, i: (0, 0, 0))],
        out_specs=pl.BlockSpec((1, TQ, ATTN_WIDTH), lambda bi, i: (bi, i, 0)),
        out_shape=jax.ShapeDtypeStruct((b, s, ATTN_WIDTH), BF16),
        scratch_shapes=[pltpu.VMEM((nb, KB, TQ), I32),
                        pltpu.VMEM((nb, KB, TQ), F32),
                        pltpu.VMEM((nb, KB, TQ), F32),
                        pltpu.VMEM((8, TQ), I32),
                        pltpu.VMEM((N_KV_HEADS, 1, REP * TQ), F32),
                        pltpu.VMEM((N_KV_HEADS, 1, REP * TQ), F32),
                        pltpu.VMEM((N_KV_HEADS, HEAD_DIM, REP * TQ), F32)],
        compiler_params=_params("arbitrary", "arbitrary"),
        name="attn",
    )(qT, qiT, wT, kh, vT, kin, bias_strips)


HALO = 16


def _mix_kernel(cb_ref, cc_ref, cu_ref, ccp_ref, cup_ref, at_ref, ga_ref, gb_ref,
                cw_ref, wco_ref, wao_ref, o_ref):
    tm = cb_ref.shape[1]
    v = cc_ref[0].astype(F32) * cu_ref[0].astype(F32)
    first = pl.program_id(1) == 0
    hv = ccp_ref[0].astype(F32) * cup_ref[0].astype(F32)
    hv = jnp.where(first, 0.0, hv)
    row = lax.broadcasted_iota(I32, v.shape, 0)
    v1 = jnp.where(row == 0, hv[HALO - 1:HALO], pltpu.roll(v, 1, 0))
    v2 = pltpu.roll(v, 2, 0)
    v2 = jnp.where(row == 0, hv[HALO - 2:HALO - 1], jnp.where(row == 1, hv[HALO - 1:HALO], v2))
    y = cw_ref[0:1] * v2 + cw_ref[1:2] * v1 + cw_ref[2:3] * v
    yc = (cb_ref[0].astype(F32) * y).astype(BF16)
    y_conv = jnp.dot(yc, wco_ref[...], preferred_element_type=F32)
    y_attn = jnp.dot(at_ref[0], wao_ref[...], preferred_element_type=F32)
    mixed = _sigmoid(ga_ref[0].astype(F32)) * y_conv + _sigmoid(gb_ref[0].astype(F32)) * y_attn
    o_ref[0] = mixed.astype(BF16)


def _mix(proj, attn, conv_w, w_conv_out_b, w_attn_out_b):
    b, s, _ = proj.shape
    tm = MIX_TM
    hb = tm // HALO
    prev = lambda col: (lambda bi, i: (bi, jnp.maximum(i * hb - 1, 0), col))
    return pl.pallas_call(
        _mix_kernel,
        grid=(b, s // tm),
        in_specs=[pl.BlockSpec((1, tm, CONV_WIDTH), lambda bi, i: (bi, i, COL_CB)),
                  pl.BlockSpec((1, tm, CONV_WIDTH), lambda bi, i: (bi, i, COL_CC)),
                  pl.BlockSpec((1, tm, CONV_WIDTH), lambda bi, i: (bi, i, COL_CU)),
                  pl.BlockSpec((1, HALO, CONV_WIDTH), prev(COL_CC)),
                  pl.BlockSpec((1, HALO, CONV_WIDTH), prev(COL_CU)),
                  pl.BlockSpec((1, tm, ATTN_WIDTH), lambda bi, i: (bi, i, 0)),
                  pl.BlockSpec((1, tm, D_MODEL), lambda bi, i: (bi, i, COL_GA)),
                  pl.BlockSpec((1, tm, D_MODEL), lambda bi, i: (bi, i, COL_GB)),
                  pl.BlockSpec((8, CONV_WIDTH), lambda bi, i: (0, 0)),
                  pl.BlockSpec((CONV_WIDTH, D_MODEL), lambda bi, i: (0, 0)),
                  pl.BlockSpec((ATTN_WIDTH, D_MODEL), lambda bi, i: (0, 0))],
        out_specs=pl.BlockSpec((1, tm, D_MODEL), lambda bi, i: (bi, i, 0)),
        out_shape=jax.ShapeDtypeStruct((b, s, D_MODEL), BF16),
        compiler_params=_params("arbitrary", "arbitrary"),
        name="mix",
    )(proj, proj, proj, proj, proj, attn, proj, proj,
      jnp.pad(conv_w, ((0, 8 - CONV_K), (0, 0))), w_conv_out_b, w_attn_out_b)


def _post_kernel(x_ref, mx_ref, g1_ref, nw_ref, sc_ref, sh_ref, g2_ref, wo_ref, wrT_ref,
                 ws1_ref, ws3_ref, ws2_ref, base_ref, h2_ref, lg_ref):
    x1 = x_ref[0] + g1_ref[0] * jnp.dot(mx_ref[0], wo_ref[...], preferred_element_type=F32)
    ms = jnp.mean(x1 * x1, axis=-1, keepdims=True)
    h2 = x1 * lax.rsqrt(ms + EPS) * nw_ref[...] * (1.0 + sc_ref[0]) + sh_ref[0]
    h2_ref[...] = h2
    lg_ref[...] = lax.dot_general(wrT_ref[...], h2, (((1,), (1,)), ((), ())),
                                  precision=lax.Precision.HIGHEST, preferred_element_type=F32)
    hb = h2.astype(BF16)
    a = jnp.dot(hb, ws1_ref[...], preferred_element_type=F32)
    u = jnp.dot(hb, ws3_ref[...], preferred_element_type=F32)
    shared = jnp.dot((a * _sigmoid(a) * u).astype(BF16), ws2_ref[...], preferred_element_type=F32)
    base_ref[0] = x1 + g2_ref[0] * shared


def _post(x, mixed, g1, norm_w, sc, sh, g2, w_o_b, w_router_t, ws1_b, ws3_b, ws2_b):
    b, s, d = x.shape
    tm = POST_TM
    nt = s // tm
    vec = pl.BlockSpec((1, 1, d), lambda bi, i: (bi, 0, 0))
    const = lambda shape: pl.BlockSpec(shape, lambda bi, i: (0,) * len(shape))
    return pl.pallas_call(
        _post_kernel,
        grid=(b, nt),
        in_specs=[pl.BlockSpec((1, tm, d), lambda bi, i: (bi, i, 0)),
                  pl.BlockSpec((1, tm, d), lambda bi, i: (bi, i, 0)),
                  vec, const((1, d)), vec, vec, vec,
                  const((d, d)), const((N_EXPERTS, d)),
                  const((d, D_EXPERT)), const((d, D_EXPERT)), const((D_EXPERT, d))],
        out_specs=[pl.BlockSpec((1, tm, d), lambda bi, i: (bi, i, 0)),
                   pl.BlockSpec((tm, d), lambda bi, i: (bi * nt + i, 0)),
                   pl.BlockSpec((N_EXPERTS, tm), lambda bi, i: (0, bi * nt + i))],
        out_shape=[jax.ShapeDtypeStruct((b, s, d), F32),
                   jax.ShapeDtypeStruct((b * s, d), F32),
                   jax.ShapeDtypeStruct((N_EXPERTS, b * s), F32)],
        compiler_params=_params("arbitrary", "arbitrary"),
        name="post",
    )(x, mixed, g1, norm_w.reshape(1, d), sc, sh, g2, w_o_b, w_router_t, ws1_b, ws3_b, ws2_b)


def _first_max(cur, ids, sentinel):
    m = jnp.max(cur, axis=0, keepdims=True)
    first = jnp.min(jnp.where(cur == m, ids, sentinel), axis=0, keepdims=True)
    return m, first


def _route_kernel(lg_ref, rb_ref, idx_ref, w_ref):
    tn = lg_ref.shape[1]
    gsz = N_EXPERTS // N_GROUPS
    scores = _sigmoid(lg_ref[...])
    sel = scores + rb_ref[...]
    sub = lax.broadcasted_iota(I32, (gsz, tn), 0).astype(F32)

    gs = []
    for g in range(N_GROUPS):
        v = sel[g * gsz:(g + 1) * gsz]
        m1, first = _first_max(v, sub, float(gsz))
        m2 = jnp.max(jnp.where(sub == first, -jnp.inf, v), axis=0, keepdims=True)
        gs.append(m1 + m2)
    cur = jnp.concatenate(gs, axis=0)
    gid = lax.broadcasted_iota(I32, (N_GROUPS, tn), 0).astype(F32)
    keep = jnp.zeros((N_GROUPS, tn), F32)
    for _ in range(TOPK_GROUPS):
        _, first = _first_max(cur, gid, float(N_GROUPS))
        hit = gid == first
        keep = jnp.where(hit, 1.0, keep)
        cur = jnp.where(hit, -jnp.inf, cur)

    cur = jnp.concatenate(
        [jnp.where(keep[g:g + 1] > 0.0, sel[g * gsz:(g + 1) * gsz], NEG) for g in range(N_GROUPS)],
        axis=0)
    eid = lax.broadcasted_iota(I32, (N_EXPERTS, tn), 0).astype(F32)
    ids, ws = [], []
    for _ in range(TOP_K):
        _, first = _first_max(cur, eid, float(N_EXPERTS))
        hit = eid == first
        ids.append(first)
        ws.append(jnp.sum(jnp.where(hit, scores, 0.0), axis=0, keepdims=True))
        cur = jnp.where(hit, -jnp.inf, cur)
    w = jnp.concatenate(ws, axis=0)
    idx_ref[...] = jnp.concatenate(ids, axis=0).astype(I32)
    w_ref[...] = w / jnp.sum(w, axis=0, keepdims=True) * ROUTED_SCALE


def _route(logits_t, router_bias):
    e, n = logits_t.shape
    tn = ROUTE_TN
    return pl.pallas_call(
        _route_kernel,
        grid=(n // tn,),
        in_specs=[pl.BlockSpec((e, tn), lambda j: (0, j)),
                  pl.BlockSpec((e, 1), lambda j: (0, 0))],
        out_specs=[pl.BlockSpec((TOP_K, tn), lambda j: (0, j)),
                   pl.BlockSpec((TOP_K, tn), lambda j: (0, j))],
        out_shape=[jax.ShapeDtypeStruct((TOP_K, n), I32),
                   jax.ShapeDtypeStruct((TOP_K, n), F32)],
        compiler_params=_params("arbitrary"),
        name="route",
    )(logits_t, router_bias.reshape(e, 1))


def _tile_major(a_t, n_tiles, tm):
    return a_t.reshape(TOP_K, n_tiles, tm).transpose(1, 0, 2).reshape(n_tiles, 1, TOP_K * tm)


def _dispatch_kernel(zs_ref, pos_ref, x_ref, xs_hbm, zbuf, sem):
    step = pl.program_id(0)

    @pl.when(step == 0)
    def _():
        zbuf[...] = jnp.zeros(zbuf.shape, F32)

        def zero_copy(t):
            start = pl.multiple_of(t * MOE_TM, MOE_TM)
            return pltpu.make_async_copy(zbuf, xs_hbm.at[pl.ds(start, MOE_TM), :], sem.at[1])

        def start_body(t, carry):
            @pl.when(zs_ref[t] != 0)
            def _():
                zero_copy(t).start()
            return carry

        def wait_body(t, carry):
            @pl.when(zs_ref[t] != 0)
            def _():
                zero_copy(t).wait()
            return carry

        lax.fori_loop(0, zs_ref.shape[0], start_body, 0)
        lax.fori_loop(0, zs_ref.shape[0], wait_body, 0)

    def row_copy(k, r):
        return pltpu.make_async_copy(x_ref.at[pl.ds(r, 1), :],
                                     xs_hbm.at[pl.ds(pos_ref[0, 0, k * DISP_TM + r], 1), :], sem.at[0])

    def body(i, carry):
        for k in range(TOP_K):
            for u in range(2):
                row_copy(k, i * 2 + u).start()
        return carry

    lax.fori_loop(0, DISP_TM // 2, body, 0)
    for _ in range(TOP_K):
        pltpu.make_async_copy(x_ref, x_ref, sem.at[0]).wait()


def _dispatch(h2, pos_t, zero_start, n_rows):
    n, d = h2.shape
    tm = DISP_TM
    n_tiles = n // tm
    grid_spec = pltpu.PrefetchScalarGridSpec(
        num_scalar_prefetch=1,
        grid=(n_tiles,),
        in_specs=[pl.BlockSpec((1, 1, tm * TOP_K), lambda t, zs: (t, 0, 0), memory_space=pltpu.SMEM),
                  pl.BlockSpec((tm, d), lambda t, zs: (t, 0))],
        out_specs=pl.BlockSpec(memory_space=pl.ANY),
        scratch_shapes=[pltpu.VMEM((MOE_TM, d), F32),
                        pltpu.SemaphoreType.DMA((2,))],
    )
    return pl.pallas_call(
        _dispatch_kernel,
        grid_spec=grid_spec,
        out_shape=jax.ShapeDtypeStruct((n_rows, d), F32),
        compiler_params=_params("arbitrary"),
        name="dispatch",
    )(zero_start, _tile_major(pos_t, n_tiles, tm), h2)


def _experts_kernel(te_ref, nu_ref, x_ref, w1_ref, w3_ref, w2_ref, y_ref, w1b, w3b, w2b):
    j = pl.program_id(0)
    n_used = nu_ref[0]

    @pl.when(j < n_used)
    def _():
        @pl.when((j == 0) | (te_ref[j] != te_ref[jnp.maximum(j - 1, 0)]))
        def _():
            w1b[...] = w1_ref[0].astype(BF16)
            w3b[...] = w3_ref[0].astype(BF16)
            w2b[...] = w2_ref[0].astype(BF16)

        x = x_ref[...].astype(BF16)
        a = jnp.dot(x, w1b[...], preferred_element_type=F32)
        u = jnp.dot(x, w3b[...], preferred_element_type=F32)
        y_ref[...] = jnp.dot((a * _sigmoid(a) * u).astype(BF16), w2b[...], preferred_element_type=F32)

    @pl.when(j >= n_used)
    def _():
        y_ref[...] = jnp.zeros(y_ref.shape, F32)


def _experts(xs, tile_expert, n_used, w1, w3, w2):
    nt = xs.shape[0] // MOE_TM
    d, f = w1.shape[1], w1.shape[2]
    grid_spec = pltpu.PrefetchScalarGridSpec(
        num_scalar_prefetch=2,
        grid=(nt,),
        in_specs=[pl.BlockSpec((MOE_TM, d), lambda j, te, nu: (jnp.minimum(j, nu[0] - 1), 0)),
                  pl.BlockSpec((1, d, f), lambda j, te, nu: (te[j], 0, 0)),
                  pl.BlockSpec((1, d, f), lambda j, te, nu: (te[j], 0, 0)),
                  pl.BlockSpec((1, f, d), lambda j, te, nu: (te[j], 0, 0))],
        out_specs=pl.BlockSpec((MOE_TM, d), lambda j, te, nu: (j, 0)),
        scratch_shapes=[pltpu.VMEM((d, f), BF16),
                        pltpu.VMEM((d, f), BF16),
                        pltpu.VMEM((f, d), BF16)],
    )
    return pl.pallas_call(
        _experts_kernel,
        grid_spec=grid_spec,
        out_shape=jax.ShapeDtypeStruct((nt * MOE_TM, d), F32),
        compiler_params=_params("arbitrary"),
        name="experts",
    )(tile_expert, n_used, xs, w1, w3, w2)


def _combine_kernel(pos_cur_ref, pos_nxt_ref, ys_hbm, base_ref, g2_ref, w_ref, o_ref, buf, sem):
    bi, i = pl.program_id(0), pl.program_id(1)
    step = bi * pl.num_programs(1) + i
    n_steps = pl.num_programs(0) * pl.num_programs(1)
    slot = lax.rem(step, 2)

    def issue(pos_ref, dst_slot):
        def body(i, carry):
            for k in range(TOP_K):
                for u in range(2):
                    r = i * 2 + u
                    pltpu.make_async_copy(ys_hbm.at[pl.ds(pos_ref[0, 0, k * COMB_TM + r], 1), :],
                                          buf.at[dst_slot, k, pl.ds(r, 1), :], sem.at[dst_slot]).start()
            return carry
        lax.fori_loop(0, COMB_TM // 2, body, 0)

    @pl.when(step == 0)
    def _():
        issue(pos_cur_ref, 0)

    for parity in (0, 1):
        @pl.when((step + 1 < n_steps) & (slot == parity))
        def _(parity=parity):
            issue(pos_nxt_ref, 1 - parity)

    pltpu.make_async_copy(buf.at[slot], buf.at[slot], sem.at[slot]).wait()

    w = w_ref[...]
    acc = jnp.zeros(o_ref.shape[1:], F32)
    for k in range(TOP_K):
        acc = acc + w[:, k:k + 1] * buf[slot, k]
    o_ref[0] = base_ref[0] + g2_ref[0] * acc


def _combine(ys, pos_t, w_sel, base, g2):
    b, s, d = base.shape
    tm = COMB_TM
    nt = s // tm
    n_tiles = b * nt
    pos_t = _tile_major(pos_t, n_tiles, tm)
    return pl.pallas_call(
        _combine_kernel,
        grid=(b, nt),
        in_specs=[pl.BlockSpec((1, 1, tm * TOP_K), lambda bi, i: (bi * nt + i, 0, 0),
                               memory_space=pltpu.SMEM),
                  pl.BlockSpec((1, 1, tm * TOP_K),
                               lambda bi, i: (jnp.minimum(bi * nt + i + 1, n_tiles - 1), 0, 0),
                               memory_space=pltpu.SMEM),
                  pl.BlockSpec(memory_space=pl.ANY),
                  pl.BlockSpec((1, tm, d), lambda bi, i: (bi, i, 0)),
                  pl.BlockSpec((1, 1, d), lambda bi, i: (bi, 0, 0)),
                  pl.BlockSpec((tm, TOP_K), lambda bi, i: (bi * nt + i, 0))],
        out_specs=pl.BlockSpec((1, tm, d), lambda bi, i: (bi, i, 0)),
        out_shape=jax.ShapeDtypeStruct((b, s, d), F32),
        scratch_shapes=[pltpu.VMEM((2, TOP_K, tm, d), F32),
                        pltpu.SemaphoreType.DMA((2,))],
        compiler_params=_params("arbitrary", "arbitrary"),
        name="combine",
    )(pos_t, pos_t, ys, base, g2, w_sel)


def _plan_kernel(te_ref, tri_ref, low_ref, pos_ref, cnt_ref, run_ref, start_ref):
    phase, j = pl.program_id(0), pl.program_id(1)
    tn = te_ref.shape[1]
    te = te_ref[...]
    eid = lax.broadcasted_iota(I32, (N_EXPERTS, tn), 0)
    hot = jnp.zeros((N_EXPERTS, tn), F32)
    for k in range(TOP_K):
        hot = hot + jnp.where(te[k:k + 1, :] == eid, 1.0, 0.0)
    tile_count = jnp.sum(hot, axis=1, keepdims=True)

    @pl.when((phase == 0) & (j == 0))
    def _():
        run_ref[...] = jnp.zeros(run_ref.shape, F32)

    @pl.when((phase == 1) & (j == 0))
    def _():
        counts = run_ref[...]
        cnt_ref[...] = counts
        tiles = jnp.floor((counts + (MOE_TM - 1)) * (1.0 / MOE_TM))
        start_ref[...] = jnp.dot(low_ref[...], tiles.astype(BF16), preferred_element_type=F32) * MOE_TM
        run_ref[...] = jnp.zeros(run_ref.shape, F32)

    @pl.when(phase == 1)
    def _():
        before = jnp.dot(hot.astype(BF16), tri_ref[...], preferred_element_type=F32)
        val = before + (run_ref[:, 0:1] + start_ref[:, 0:1])
        rows = [jnp.sum(jnp.where(te[k:k + 1, :] == eid, val, 0.0), axis=0, keepdims=True)
                for k in range(TOP_K)]
        pos_ref[...] = jnp.concatenate(rows, axis=0).astype(I32)

    run_ref[...] = run_ref[...] + tile_count


def _dispatch_plan(top_e_t):
    n = top_e_t.shape[1]
    tn = PLAN_TN
    n_tiles = n * TOP_K // MOE_TM + N_EXPERTS
    tri = jnp.asarray(np.triu(np.ones((tn, tn), np.float32), 1), BF16)
    low = jnp.asarray(np.tril(np.ones((N_EXPERTS, N_EXPERTS), np.float32), -1), BF16)
    pos_t, cnt = pl.pallas_call(
        _plan_kernel,
        grid=(2, n // tn),
        in_specs=[pl.BlockSpec((TOP_K, tn), lambda ph, j: (0, j)),
                  pl.BlockSpec((tn, tn), lambda ph, j: (0, 0)),
                  pl.BlockSpec((N_EXPERTS, N_EXPERTS), lambda ph, j: (0, 0))],
        out_specs=[pl.BlockSpec((TOP_K, tn), lambda ph, j: (0, j * ph)),
                   pl.BlockSpec((N_EXPERTS, LANES), lambda ph, j: (0, 0))],
        out_shape=[jax.ShapeDtypeStruct((TOP_K, n), I32),
                   jax.ShapeDtypeStruct((N_EXPERTS, LANES), F32)],
        scratch_shapes=[pltpu.VMEM((N_EXPERTS, LANES), F32),
                        pltpu.VMEM((N_EXPERTS, LANES), F32)],
        compiler_params=_params("arbitrary", "arbitrary"),
        name="plan",
    )(top_e_t, tri, low)
    counts = cnt[:, 0].astype(I32)
    tile_end = jnp.cumsum((counts + MOE_TM - 1) // MOE_TM)
    tile_expert = jnp.minimum(
        jnp.sum((tile_end[None, :] <= jnp.arange(n_tiles, dtype=I32)[:, None]).astype(I32), axis=1),
        N_EXPERTS - 1)
    n_used = tile_end[-1:].astype(I32)
    t_ids = jnp.arange(n_tiles, dtype=I32)
    is_last = jnp.any((tile_end[None, :] - 1 == t_ids[:, None]) & (counts[None, :] > 0), axis=1)
    zero_tile = (is_last | (t_ids >= n_used[0])).astype(I32)
    return pos_t, zero_tile, tile_expert.astype(I32), n_used, n_tiles * MOE_TM


def _layer(x, c, rel_bias, norm1_w, norm2_w, w_ada, b_ada, w_in, conv_w, w_conv_out, q_norm_w,
           k_norm_w, idx_k_norm_w, idx_k_norm_b, w_attn_out, w_o, w_router, router_bias,
           w1, w3, w2, ws1, ws3, ws2):
    b, s, d = x.shape
    mod = _mod(c, w_ada, b_ada).reshape(b, 6, 1, d)
    sh1, sc1, g1, sh2, sc2, g2 = [mod[:, m] for m in range(6)]

    cols = [w_in[:, _SEG[name][0]:_SEG[name][1]] for name in _ORDER]
    cols.append(jnp.zeros((d, PROJ_W - sum(col.shape[1] for col in cols)), w_in.dtype))
    w_in_p = jnp.concatenate(cols, axis=1).astype(BF16)

    proj = _proj(x, norm1_w, sc1, sh1, w_in_p)
    qT, qiT, kh, vT, kin, wT = _prep(proj, q_norm_w, k_norm_w, idx_k_norm_w, idx_k_norm_b,
                                     rel_bias[REL_BUCKETS - 1])
    attn = _attention(qT, qiT, wT, kh, vT, kin, _bias_strips(rel_bias))
    mixed = _mix(proj, attn, conv_w, w_conv_out.astype(BF16), w_attn_out.astype(BF16))
    base, h2, logits_t = _post(x, mixed, g1, norm2_w, sc2, sh2, g2, w_o.astype(BF16), w_router.T,
                               ws1.astype(BF16), ws3.astype(BF16), ws2.astype(BF16))
    top_e_t, w_sel_t = _route(logits_t, router_bias)
    pos_t, zero_start, tile_expert, n_used, n_rows = _dispatch_plan(top_e_t)
    xs = _dispatch(h2, pos_t, zero_start, n_rows)
    ys = _experts(xs, tile_expert, n_used, w1, w3, w2)
    return _combine(ys, pos_t, w_sel_t.T, base, g2)


def kernel(x, c, rel_bias, norm1_w, norm2_w, w_ada, b_ada, w_in, conv_w, w_conv_out, q_norm_w,
           k_norm_w, idx_k_norm_w, idx_k_norm_b, w_attn_out, w_o, w_router, router_bias,
           w1, w3, w2, ws1, ws3, ws2):
    assert x.shape[1] % PROJ_TM == 0 and x.shape[2] == D_MODEL and w_ada.shape[0] == 1
    return _layer(x, c, rel_bias, norm1_w[0], norm2_w[0], w_ada[0], b_ada[0], w_in[0], conv_w[0],
                  w_conv_out[0], q_norm_w[0], k_norm_w[0], idx_k_norm_w[0], idx_k_norm_b[0],
                  w_attn_out[0], w_o[0], w_router[0], router_bias[0], w1[0], w3[0], w2[0],
                  ws1[0], ws3[0], ws2[0])
```

```python
import functools
import math

import numpy as np
import jax
import jax.numpy as jnp
from jax import lax
from jax.experimental import pallas as pl
from jax.experimental.pallas import tpu as pltpu

F32 = jnp.float32
BF16 = jnp.bfloat16
I32 = jnp.int32
U32 = jnp.uint32

D_MODEL = 2048
CONV_WIDTH = D_MODEL // 2
CONV_K = 3
N_HEADS = 16
N_KV_HEADS = 4
HEAD_DIM = 64
ATTN_WIDTH = N_HEADS * HEAD_DIM
KV_WIDTH = N_KV_HEADS * HEAD_DIM
IDX_HEADS = 16
IDX_DIM = 64
IDX_TOPK_MAX = 256
REL_BUCKETS = 32
REL_MAX_DIST = 128
N_EXPERTS = 64
N_GROUPS = 8
TOPK_GROUPS = 4
TOP_K = 8
D_EXPERT = 512
ROUTED_SCALE = 2.5
EPS = 1e-6
NEG = -1e30

REP = N_HEADS // N_KV_HEADS

LANES = 128
VMEM_LIMIT = 56 * 1024 * 1024

TQ = 128
KB = 128
KC = 4 * KB
FAR_KC = 4 * KB
PROJ_TM = 1024
PROJ_TN = 768
PREP_TM = 512
MIX_TM = 512
POST_TM = 512
ROUTE_TN = 512
MOE_TM = 512
COMB_TM = 128
DISP_TM = 128
PLAN_TN = 512

QK_DIM = 128
V_ROWS = HEAD_DIM + 16
LOG2E = math.log2(math.e)

_SEG = dict(cb=(0, 1024), cc=(1024, 2048), cu=(2048, 3072), q=(3072, 4096), k=(4096, 4352),
            v=(4352, 4608), qi=(4608, 5632), ki=(5632, 5696), wi=(5696, 5712),
            ga=(5712, 7760), gb=(7760, 9808))
_ORDER = ["ga", "gb", "cb", "cc", "cu", "q", "qi", "k", "v", "ki", "wi"]
PROJ_W = 9984
COL_GA, COL_GB = 0, 1
COL_CB, COL_CC, COL_CU, COL_Q, COL_QI = 4, 5, 6, 7, 8
COL_K, COL_V = 36, 37
COL_KW = 76

INT_MIN = -(2 ** 31)
INT_MAX = 2 ** 31 - 1


def _sortable_key_of(x):
    bits = int(np.float32(x).view(np.int32))
    return bits ^ 0x7FFFFFFF if bits < 0 else bits


NEG_KEY = _sortable_key_of(NEG)


def _sigmoid(x):
    return 1.0 / (1.0 + jnp.exp(-x))


def _pack_bf16_pairs(x):
    half = x.shape[1] // 2
    lo = lax.bitcast_convert_type(x[:, :half].astype(BF16).astype(F32), U32)
    hi = lax.bitcast_convert_type(x[:, half:].astype(BF16).astype(F32), U32)
    return (lo >> 16) | (hi & jnp.uint32(0xFFFF0000))


def _unpack_bf16_pairs(w):
    lo = lax.bitcast_convert_type(w << 16, F32)
    hi = lax.bitcast_convert_type(w & jnp.uint32(0xFFFF0000), F32)
    return lo, hi


def _params(*sem):
    return pltpu.CompilerParams(dimension_semantics=sem, vmem_limit_bytes=VMEM_LIMIT)


def _mod_kernel(c_ref, w_ref, b_ref, o_ref):
    c = c_ref[...]
    s = (c * _sigmoid(c)).astype(BF16)
    o_ref[...] = jnp.dot(s, w_ref[...].astype(BF16), preferred_element_type=F32) + b_ref[...]


def _mod(c, w_ada, b_ada):
    b = c.shape[0]
    rows = 8
    cp = jnp.pad(c, ((0, rows - b), (0, 0)))
    n = w_ada.shape[1]
    tn = 1024
    out = pl.pallas_call(
        _mod_kernel,
        grid=(n // tn,),
        in_specs=[pl.BlockSpec((rows, D_MODEL), lambda j: (0, 0)),
                  pl.BlockSpec((D_MODEL, tn), lambda j: (0, j)),
                  pl.BlockSpec((1, tn), lambda j: (0, j))],
        out_specs=pl.BlockSpec((rows, tn), lambda j: (0, j)),
        out_shape=jax.ShapeDtypeStruct((rows, n), F32),
        compiler_params=_params("arbitrary"),
        name="mod",
    )(cp, w_ada, b_ada.reshape(1, n))
    return out[:b]


def _proj_kernel(x_ref, nw_ref, sc_ref, sh_ref, w_ref, o_ref, h_ref):
    @pl.when(pl.program_id(2) == 0)
    def _():
        x = x_ref[0]
        ms = jnp.mean(x * x, axis=-1, keepdims=True)
        y = x * lax.rsqrt(ms + EPS) * nw_ref[...]
        h_ref[...] = (y * (1.0 + sc_ref[0]) + sh_ref[0]).astype(BF16)

    o_ref[0] = jnp.dot(h_ref[...], w_ref[...], preferred_element_type=F32).astype(BF16)


def _proj(x, norm_w, sc, sh, w_in_p):
    b, s, d = x.shape
    tm, tn = PROJ_TM, PROJ_TN
    return pl.pallas_call(
        _proj_kernel,
        grid=(b, s // tm, PROJ_W // tn),
        in_specs=[pl.BlockSpec((1, tm, d), lambda bi, i, j: (bi, i, 0)),
                  pl.BlockSpec((1, d), lambda bi, i, j: (0, 0)),
                  pl.BlockSpec((1, 1, d), lambda bi, i, j: (bi, 0, 0)),
                  pl.BlockSpec((1, 1, d), lambda bi, i, j: (bi, 0, 0)),
                  pl.BlockSpec((d, tn), lambda bi, i, j: (0, j))],
        out_specs=pl.BlockSpec((1, tm, tn), lambda bi, i, j: (bi, i, j)),
        out_shape=jax.ShapeDtypeStruct((b, s, PROJ_W), BF16),
        scratch_shapes=[pltpu.VMEM((tm, d), BF16)],
        compiler_params=_params("arbitrary", "arbitrary", "arbitrary"),
        name="proj",
    )(x, norm_w.reshape(1, d), sc, sh, w_in_p)


def _prep_kernel(q_ref, qi_ref, k_ref, v_ref, kw_ref, qnw_ref, knw_ref, inw_ref, inb_ref, qtail_ref,
                 qT_ref, qiT_ref, kh_ref, vT_ref, kin_ref, wT_ref):
    tm = q_ref.shape[1]
    nqb = tm // TQ

    q3 = q_ref[0].astype(F32).T.reshape(N_HEADS, HEAD_DIM, tm)
    ms = jnp.mean(q3 * q3, axis=1, keepdims=True)
    qn = q3 * lax.rsqrt(ms + EPS) * (qnw_ref[...] * (HEAD_DIM ** -0.5 * LOG2E))
    qi3 = qi_ref[0].astype(F32).T.reshape(IDX_HEADS, IDX_DIM, tm)
    for jb in range(nqb):
        for h in range(N_HEADS):
            g, r = divmod(h, REP)
            qT_ref[0, jb, g, :HEAD_DIM, r * TQ:(r + 1) * TQ] = qn[h, :, jb * TQ:(jb + 1) * TQ].astype(BF16)
        for g in range(N_KV_HEADS):
            qT_ref[0, jb, g, HEAD_DIM:, :] = qtail_ref[g]
        for h in range(IDX_HEADS):
            qiT_ref[0, jb, :, h * TQ:(h + 1) * TQ] = qi3[h, :, jb * TQ:(jb + 1) * TQ].astype(BF16)

    k = k_ref[0].astype(F32)
    ones_cols = jnp.where(lax.broadcasted_iota(I32, (tm, QK_DIM - HEAD_DIM), 1) < 2, 1.0, 0.0)
    for g in range(N_KV_HEADS):
        kg = k[:, g * HEAD_DIM:(g + 1) * HEAD_DIM]
        msk = jnp.mean(kg * kg, axis=-1, keepdims=True)
        kn = kg * lax.rsqrt(msk + EPS) * knw_ref[...]
        kh_ref[0, g] = jnp.concatenate([kn, ones_cols], axis=1).astype(BF16)

    v3 = v_ref[0].astype(F32).T.reshape(N_KV_HEADS, HEAD_DIM, tm)
    ones_rows = jnp.where(lax.broadcasted_iota(I32, (V_ROWS - HEAD_DIM, KB), 0) == 0, 1.0, 0.0)
    for g in range(N_KV_HEADS):
        for jb in range(tm // KB):
            vT_ref[0, g, jb] = jnp.concatenate(
                [v3[g, :, jb * KB:(jb + 1) * KB], ones_rows], axis=0).astype(BF16)

    kw = kw_ref[0].astype(F32)
    ki = kw[:, :IDX_DIM]
    mu = jnp.mean(ki, axis=-1, keepdims=True)
    var = jnp.mean(jnp.square(ki - mu), axis=-1, keepdims=True)
    kin_ref[0] = ((ki - mu) * lax.rsqrt(var + EPS) * inw_ref[...] + inb_ref[...]).astype(BF16)
    wiT = kw.T[IDX_DIM:IDX_DIM + IDX_HEADS] * (IDX_HEADS ** -0.5 * IDX_DIM ** -0.5)
    for jb in range(nqb):
        wT_ref[0, jb] = wiT[:, jb * TQ:(jb + 1) * TQ]


def _prep(proj, q_norm_w, k_norm_w, idx_k_norm_w, idx_k_norm_b, far_bias):
    b, s, _ = proj.shape
    tm = PREP_TM
    nqb = tm // TQ
    nq = s // TQ
    fb2 = (far_bias * LOG2E).reshape(N_KV_HEADS, REP)
    hi = fb2.astype(BF16)
    lo = (fb2 - hi.astype(F32)).astype(BF16)
    tail = jnp.stack([hi, lo], axis=1)
    tail = jnp.broadcast_to(tail[..., None], (N_KV_HEADS, 2, REP, TQ)).reshape(N_KV_HEADS, 2, REP * TQ)
    qtail = jnp.pad(tail, ((0, 0), (0, QK_DIM - HEAD_DIM - 2), (0, 0)))
    return pl.pallas_call(
        _prep_kernel,
        grid=(b, s // tm),
        in_specs=[pl.BlockSpec((1, tm, ATTN_WIDTH), lambda bi, i: (bi, i, COL_Q)),
                  pl.BlockSpec((1, tm, IDX_HEADS * IDX_DIM), lambda bi, i: (bi, i, COL_QI)),
                  pl.BlockSpec((1, tm, KV_WIDTH), lambda bi, i: (bi, i, COL_K)),
                  pl.BlockSpec((1, tm, KV_WIDTH), lambda bi, i: (bi, i, COL_V)),
                  pl.BlockSpec((1, tm, LANES), lambda bi, i: (bi, i, COL_KW)),
                  pl.BlockSpec((1, HEAD_DIM, 1), lambda bi, i: (0, 0, 0)),
                  pl.BlockSpec((1, HEAD_DIM), lambda bi, i: (0, 0)),
                  pl.BlockSpec((1, IDX_DIM), lambda bi, i: (0, 0)),
                  pl.BlockSpec((1, IDX_DIM), lambda bi, i: (0, 0)),
                  pl.BlockSpec((N_KV_HEADS, QK_DIM - HEAD_DIM, REP * TQ), lambda bi, i: (0, 0, 0))],
        out_specs=[pl.BlockSpec((1, nqb, N_KV_HEADS, QK_DIM, REP * TQ), lambda bi, i: (bi, i, 0, 0, 0)),
                   pl.BlockSpec((1, nqb, IDX_DIM, IDX_HEADS * TQ), lambda bi, i: (bi, i, 0, 0)),
                   pl.BlockSpec((1, N_KV_HEADS, tm, QK_DIM), lambda bi, i: (bi, 0, i, 0)),
                   pl.BlockSpec((1, N_KV_HEADS, tm // KB, V_ROWS, KB), lambda bi, i: (bi, 0, i, 0, 0)),
                   pl.BlockSpec((1, tm, IDX_DIM), lambda bi, i: (bi, i, 0)),
                   pl.BlockSpec((1, nqb, IDX_HEADS, TQ), lambda bi, i: (bi, i, 0, 0))],
        out_shape=[jax.ShapeDtypeStruct((b, nq, N_KV_HEADS, QK_DIM, REP * TQ), BF16),
                   jax.ShapeDtypeStruct((b, nq, IDX_DIM, IDX_HEADS * TQ), BF16),
                   jax.ShapeDtypeStruct((b, N_KV_HEADS, s, QK_DIM), BF16),
                   jax.ShapeDtypeStruct((b, N_KV_HEADS, s // KB, V_ROWS, KB), BF16),
                   jax.ShapeDtypeStruct((b, s, IDX_DIM), BF16),
                   jax.ShapeDtypeStruct((b, nq, IDX_HEADS, TQ), F32)],
        compiler_params=_params("arbitrary", "arbitrary"),
        name="prep",
    )(proj, proj, proj, proj, proj,
      q_norm_w.reshape(1, HEAD_DIM, 1), k_norm_w.reshape(1, HEAD_DIM),
      idx_k_norm_w.reshape(1, IDX_DIM), idx_k_norm_b.reshape(1, IDX_DIM), qtail)


def _t5_bucket_np(n):
    n = np.maximum(n, 0)
    max_exact = REL_BUCKETS // 2
    nf = np.maximum(n, 1).astype(np.float64)
    large = max_exact + np.floor(np.log(nf / max_exact) / math.log(REL_MAX_DIST / max_exact)
                                 * (REL_BUCKETS - max_exact)).astype(np.int64)
    large = np.minimum(large, REL_BUCKETS - 1)
    return np.where(n < max_exact, n, large).astype(np.int32)


def _bias_kernel(rb_ref, bucket_ref, o_ref):
    h = pl.program_id(0)
    bucket = bucket_ref[...]
    acc = jnp.zeros(bucket.shape, F32)
    for bkt in range(REL_BUCKETS):
        acc = jnp.where(bucket == bkt, rb_ref[bkt, h], acc)
    o_ref[0] = (acc - rb_ref[REL_BUCKETS - 1, h]) * LOG2E


def _bias_strips(rel_bias):
    kk = np.arange(3 * TQ)[:, None]
    qq = np.arange(TQ)[None, :]
    bucket = jnp.asarray(_t5_bucket_np(qq + TQ - kk))
    return pl.pallas_call(
        _bias_kernel,
        grid=(N_HEADS,),
        in_specs=[pl.BlockSpec(memory_space=pltpu.SMEM),
                  pl.BlockSpec((3 * TQ, TQ), lambda h: (0, 0))],
        out_specs=pl.BlockSpec((1, 3 * TQ, TQ), lambda h: (h, 0, 0)),
        out_shape=jax.ShapeDtypeStruct((N_HEADS, 3 * TQ, TQ), F32),
        compiler_params=_params("arbitrary"),
        name="bias",
    )(rel_bias, bucket)


def _attn_kernel(qT_ref, qiT_ref, wT_ref, kh_ref, vT_ref, kin_ref, biasT_ref, o_ref,
                 keys_ref, am_ref, amf_ref, p_ref, m_ref, acc_ref, *, n_sel):
    i = pl.program_id(1)
    seq = kin_ref.shape[1]
    t0 = i * TQ
    n_chunks = lax.shift_right_logical(i + 4, 2)
    q_pos = t0 + lax.broadcasted_iota(I32, (KB, TQ), 1)
    k_off = lax.broadcasted_iota(I32, (KB, TQ), 0)

    qiT = qiT_ref[0, 0]
    wT = wT_ref[0, 0]

    def score_chunk(c, carry):
        k0 = pl.multiple_of(c * KC, KC)
        d = jnp.dot(kin_ref[0, pl.ds(k0, KC), :], qiT, preferred_element_type=F32)
        acc = jnp.zeros((KC, TQ), F32)
        for h in range(IDX_HEADS):
            acc = acc + wT[h:h + 1, :] * jnp.maximum(d[:, h * TQ:(h + 1) * TQ], 0.0)
        for j in range(KC // KB):
            blk = c * (KC // KB) + j
            sc = jnp.where(blk * KB + k_off <= q_pos, acc[j * KB:(j + 1) * KB], NEG)
            bits = lax.bitcast_convert_type(sc, I32)
            keys_ref[blk] = jnp.where(bits < 0, bits ^ 0x7FFFFFFF, bits)
        return carry

    lax.fori_loop(0, n_chunks, score_chunk, 0)

    n_virtual = (seq - n_chunks * KC).astype(F32)

    def count(pred):
        def body(c, acc):
            for j in range(KC // KB):
                blk = c * (KC // KB) + j
                hit = jnp.where(pred(keys_ref[blk], blk), 1.0, 0.0)
                acc = acc + jnp.sum(hit.reshape(KB // 8, 8, TQ), axis=0)
            return acc
        acc = lax.fori_loop(0, n_chunks, body, jnp.zeros((8, TQ), F32))
        return jnp.sum(acc, axis=0, keepdims=True)

    def bit_body(it, thr):
        cand = thr + lax.shift_left(jnp.int32(1), 31 - it)
        cnt = count(lambda kb, blk: kb >= cand) + jnp.where(NEG_KEY >= cand, n_virtual, 0.0)
        return jnp.where(cnt >= n_sel, cand, thr)

    thr = lax.fori_loop(0, 32, bit_body, jnp.full((1, TQ), INT_MIN, I32))

    cnt_gt = count(lambda kb, blk: kb > thr) + jnp.where(NEG_KEY > thr, n_virtual, 0.0)
    cnt_eq = count(lambda kb, blk: kb == thr) + jnp.where(NEG_KEY == thr, n_virtual, 0.0)
    need = n_sel - cnt_gt
    p_ref[...] = jnp.full((8, TQ), INT_MAX, I32)
    has_tie = jnp.max(jnp.where(cnt_eq > need, 1.0, 0.0)) > 0.0

    @pl.when(has_tie)
    def _():
        idx_bits = int(seq).bit_length()

        def p_body(it, p):
            cand = p | lax.shift_left(jnp.int32(1), idx_bits - 1 - it)
            below = count(lambda kb, blk: (kb == thr) & (blk * KB + k_off < cand))
            return jnp.where(below < need, cand, p)

        p = lax.fori_loop(0, idx_bits, p_body, jnp.zeros((1, TQ), I32))
        p_ref[...] = jnp.broadcast_to(p, (8, TQ))

    p_last = p_ref[0:1, :]

    bw = jnp.maximum(i - 1, 0)
    ws = pl.multiple_of(bw * KB, KB)

    def mask_chunk(c, carry):
        for j in range(KC // KB):
            blk = c * (KC // KB) + j
            kb = keys_ref[blk]
            k_pos = blk * KB + k_off
            sel = (kb > thr) | ((kb == thr) & (k_pos <= p_last))
            v = jnp.where(sel & (k_pos <= q_pos), 0.0, NEG)
            am_ref[blk] = v
            amf_ref[blk] = jnp.where(k_pos < ws, v, NEG)
        return carry

    lax.fori_loop(0, n_chunks, mask_chunk, 0)

    off = pl.multiple_of(TQ - (t0 - ws), TQ)
    am_near = jnp.concatenate([am_ref[bw], am_ref[bw + 1]], axis=0)
    for g in range(N_KV_HEADS):
        s = jnp.dot(kh_ref[0, g, pl.ds(ws, 2 * KB), :], qT_ref[0, 0, g], preferred_element_type=F32)
        s = jnp.concatenate(
            [s[:, r * TQ:(r + 1) * TQ] + (biasT_ref[REP * g + r, pl.ds(off, 2 * KB), :] + am_near)
             for r in range(REP)], axis=1)
        m = jnp.max(s, axis=0, keepdims=True)
        pb = jnp.exp2(s - m).astype(BF16)
        m_ref[g] = m
        acc_ref[g] = (jnp.dot(vT_ref[0, g, bw], pb[:KB], preferred_element_type=F32)
                      + jnp.dot(vT_ref[0, g, bw + 1], pb[KB:], preferred_element_type=F32))

    fb = FAR_KC // KB

    def far_body(f, carry):
        k0 = pl.multiple_of(f * FAR_KC, FAR_KC)
        amf = jnp.concatenate([amf_ref[f * fb + j] for j in range(fb)], axis=0)
        for g in range(N_KV_HEADS):
            s = jnp.dot(kh_ref[0, g, pl.ds(k0, FAR_KC), :], qT_ref[0, 0, g], preferred_element_type=F32)
            s = jnp.concatenate([s[:, r * TQ:(r + 1) * TQ] + amf for r in range(REP)], axis=1)
            m_old = m_ref[g]
            m_new = jnp.maximum(m_old, jnp.max(s, axis=0, keepdims=True))
            pb = jnp.exp2(s - m_new).astype(BF16)
            vc = jnp.concatenate([vT_ref[0, g, f * fb + j] for j in range(fb)], axis=1)
            m_ref[g] = m_new
            acc_ref[g] = (jnp.exp2(m_old - m_new) * acc_ref[g]
                          + jnp.dot(vc, pb, preferred_element_type=F32))
        return carry

    lax.fori_loop(0, (bw + fb - 1) // fb, far_body, 0)

    outs = []
    for g in range(N_KV_HEADS):
        og = acc_ref[g, :HEAD_DIM] / acc_ref[g, HEAD_DIM:HEAD_DIM + 1]
        outs.extend(og[:, r * TQ:(r + 1) * TQ] for r in range(REP))
    o_ref[0] = jnp.concatenate(outs, axis=0).T.astype(BF16)


def _attention(qT, qiT, wT, kh, vT, kin, bias_strips):
    b, nq = qT.shape[0], qT.shape[1]
    s = kin.shape[1]
    n_sel = min(IDX_TOPK_MAX, s // 4)
    nb = s // KB
    return pl.pallas_call(
        functools.partial(_attn_kernel, n_sel=n_sel),
        grid=(b, nq),
        in_specs=[pl.BlockSpec((1, 1, N_KV_HEADS, QK_DIM, REP * TQ), lambda bi, i: (bi, i, 0, 0, 0)),
                  pl.BlockSpec((1, 1, IDX_DIM, IDX_HEADS * TQ), lambda bi, i: (bi, i, 0, 0)),
                  pl.BlockSpec((1, 1, IDX_HEADS, TQ), lambda bi, i: (bi, i, 0, 0)),
                  pl.BlockSpec((1, N_KV_HEADS, s, QK_DIM), lambda bi, i: (bi, 0, 0, 0)),
                  pl.BlockSpec((1, N_KV_HEADS, nb, V_ROWS, KB), lambda bi, i: (bi, 0, 0, 0, 0)),
                  pl.BlockSpec((1, s, IDX_DIM), lambda bi, i: (bi, 0, 0)),
                  pl.BlockSpec((N_HEADS, 3 * TQ, TQ), lambda bi, i: (0, 0, 0))],
        out_specs=pl.BlockSpec((1, TQ, ATTN_WIDTH), lambda bi, i: (bi, i, 0)),
        out_shape=jax.ShapeDtypeStruct((b, s, ATTN_WIDTH), BF16),
        scratch_shapes=[pltpu.VMEM((nb, KB, TQ), I32),
                        pltpu.VMEM((nb, KB, TQ), F32),
                        pltpu.VMEM((nb, KB, TQ), F32),
                        pltpu.VMEM((8, TQ), I32),
                        pltpu.VMEM((N_KV_HEADS, 1, REP * TQ), F32),
                        pltpu.VMEM((N_KV_HEADS, V_ROWS, REP * TQ), F32)],
        compiler_params=_params("arbitrary", "arbitrary"),
        name="attn",
    )(qT, qiT, wT, kh, vT, kin, bias_strips)


HALO = 16


def _mix_kernel(cb_ref, cc_ref, cu_ref, ccp_ref, cup_ref, at_ref, ga_ref, gb_ref,
                cw_ref, wco_ref, wao_ref, o_ref):
    tm = cb_ref.shape[1]
    v = cc_ref[0].astype(F32) * cu_ref[0].astype(F32)
    first = pl.program_id(1) == 0
    hv = ccp_ref[0].astype(F32) * cup_ref[0].astype(F32)
    hv = jnp.where(first, 0.0, hv)
    row = lax.broadcasted_iota(I32, v.shape, 0)
    v1 = jnp.where(row == 0, hv[HALO - 1:HALO], pltpu.roll(v, 1, 0))
    v2 = pltpu.roll(v, 2, 0)
    v2 = jnp.where(row == 0, hv[HALO - 2:HALO - 1], jnp.where(row == 1, hv[HALO - 1:HALO], v2))
    y = cw_ref[0:1] * v2 + cw_ref[1:2] * v1 + cw_ref[2:3] * v
    yc = (cb_ref[0].astype(F32) * y).astype(BF16)
    y_conv = jnp.dot(yc, wco_ref[...], preferred_element_type=F32)
    y_attn = jnp.dot(at_ref[0], wao_ref[...], preferred_element_type=F32)
    mixed = _sigmoid(ga_ref[0].astype(F32)) * y_conv + _sigmoid(gb_ref[0].astype(F32)) * y_attn
    o_ref[0] = mixed.astype(BF16)


def _mix(proj, attn, conv_w, w_conv_out_b, w_attn_out_b):
    b, s, _ = proj.shape
    tm = MIX_TM
    hb = tm // HALO
    prev = lambda col: (lambda bi, i: (bi, jnp.maximum(i * hb - 1, 0), col))
    return pl.pallas_call(
        _mix_kernel,
        grid=(b, s // tm),
        in_specs=[pl.BlockSpec((1, tm, CONV_WIDTH), lambda bi, i: (bi, i, COL_CB)),
                  pl.BlockSpec((1, tm, CONV_WIDTH), lambda bi, i: (bi, i, COL_CC)),
                  pl.BlockSpec((1, tm, CONV_WIDTH), lambda bi, i: (bi, i, COL_CU)),
                  pl.BlockSpec((1, HALO, CONV_WIDTH), prev(COL_CC)),
                  pl.BlockSpec((1, HALO, CONV_WIDTH), prev(COL_CU)),
                  pl.BlockSpec((1, tm, ATTN_WIDTH), lambda bi, i: (bi, i, 0)),
                  pl.BlockSpec((1, tm, D_MODEL), lambda bi, i: (bi, i, COL_GA)),
                  pl.BlockSpec((1, tm, D_MODEL), lambda bi, i: (bi, i, COL_GB)),
                  pl.BlockSpec((8, CONV_WIDTH), lambda bi, i: (0, 0)),
                  pl.BlockSpec((CONV_WIDTH, D_MODEL), lambda bi, i: (0, 0)),
                  pl.BlockSpec((ATTN_WIDTH, D_MODEL), lambda bi, i: (0, 0))],
        out_specs=pl.BlockSpec((1, tm, D_MODEL), lambda bi, i: (bi, i, 0)),
        out_shape=jax.ShapeDtypeStruct((b, s, D_MODEL), BF16),
        compiler_params=_params("arbitrary", "arbitrary"),
        name="mix",
    )(proj, proj, proj, proj, proj, attn, proj, proj,
      jnp.pad(conv_w, ((0, 8 - CONV_K), (0, 0))), w_conv_out_b, w_attn_out_b)


def _post_kernel(x_ref, mx_ref, g1_ref, nw_ref, sc_ref, sh_ref, g2_ref, wo_ref, wrT_ref,
                 ws1_ref, ws3_ref, ws2_ref, base_ref, h2_ref, lg_ref):
    x1 = x_ref[0] + g1_ref[0] * jnp.dot(mx_ref[0], wo_ref[...], preferred_element_type=F32)
    ms = jnp.mean(x1 * x1, axis=-1, keepdims=True)
    h2 = x1 * lax.rsqrt(ms + EPS) * nw_ref[...] * (1.0 + sc_ref[0]) + sh_ref[0]
    h2_ref[...] = _pack_bf16_pairs(h2)
    lg_ref[...] = lax.dot_general(wrT_ref[...], h2, (((1,), (1,)), ((), ())),
                                  precision=lax.Precision.HIGHEST, preferred_element_type=F32)
    hb = h2.astype(BF16)
    a = jnp.dot(hb, ws1_ref[...], preferred_element_type=F32)
    u = jnp.dot(hb, ws3_ref[...], preferred_element_type=F32)
    shared = jnp.dot((a * _sigmoid(a) * u).astype(BF16), ws2_ref[...], preferred_element_type=F32)
    base_ref[0] = x1 + g2_ref[0] * shared


def _post(x, mixed, g1, norm_w, sc, sh, g2, w_o_b, w_router_t, ws1_b, ws3_b, ws2_b):
    b, s, d = x.shape
    tm = POST_TM
    nt = s // tm
    vec = pl.BlockSpec((1, 1, d), lambda bi, i: (bi, 0, 0))
    const = lambda shape: pl.BlockSpec(shape, lambda bi, i: (0,) * len(shape))
    return pl.pallas_call(
        _post_kernel,
        grid=(b, nt),
        in_specs=[pl.BlockSpec((1, tm, d), lambda bi, i: (bi, i, 0)),
                  pl.BlockSpec((1, tm, d), lambda bi, i: (bi, i, 0)),
                  vec, const((1, d)), vec, vec, vec,
                  const((d, d)), const((N_EXPERTS, d)),
                  const((d, D_EXPERT)), const((d, D_EXPERT)), const((D_EXPERT, d))],
        out_specs=[pl.BlockSpec((1, tm, d), lambda bi, i: (bi, i, 0)),
                   pl.BlockSpec((tm, d // 2), lambda bi, i: (bi * nt + i, 0)),
                   pl.BlockSpec((N_EXPERTS, tm), lambda bi, i: (0, bi * nt + i))],
        out_shape=[jax.ShapeDtypeStruct((b, s, d), F32),
                   jax.ShapeDtypeStruct((b * s, d // 2), U32),
                   jax.ShapeDtypeStruct((N_EXPERTS, b * s), F32)],
        compiler_params=_params("arbitrary", "arbitrary"),
        name="post",
    )(x, mixed, g1, norm_w.reshape(1, d), sc, sh, g2, w_o_b, w_router_t, ws1_b, ws3_b, ws2_b)


def _first_max(cur, ids, sentinel):
    m = jnp.max(cur, axis=0, keepdims=True)
    first = jnp.min(jnp.where(cur == m, ids, sentinel), axis=0, keepdims=True)
    return m, first


def _route_kernel(lg_ref, rb_ref, idx_ref, w_ref):
    tn = lg_ref.shape[1]
    gsz = N_EXPERTS // N_GROUPS
    scores = _sigmoid(lg_ref[...])
    sel = scores + rb_ref[...]
    sub = lax.broadcasted_iota(I32, (gsz, tn), 0).astype(F32)

    gs = []
    for g in range(N_GROUPS):
        v = sel[g * gsz:(g + 1) * gsz]
        m1, first = _first_max(v, sub, float(gsz))
        m2 = jnp.max(jnp.where(sub == first, -jnp.inf, v), axis=0, keepdims=True)
        gs.append(m1 + m2)
    cur = jnp.concatenate(gs, axis=0)
    gid = lax.broadcasted_iota(I32, (N_GROUPS, tn), 0).astype(F32)
    keep = jnp.zeros((N_GROUPS, tn), F32)
    for _ in range(TOPK_GROUPS):
        _, first = _first_max(cur, gid, float(N_GROUPS))
        hit = gid == first
        keep = jnp.where(hit, 1.0, keep)
        cur = jnp.where(hit, -jnp.inf, cur)

    cur = jnp.concatenate(
        [jnp.where(keep[g:g + 1] > 0.0, sel[g * gsz:(g + 1) * gsz], NEG) for g in range(N_GROUPS)],
        axis=0)
    eid = lax.broadcasted_iota(I32, (N_EXPERTS, tn), 0).astype(F32)
    ids, ws = [], []
    for _ in range(TOP_K):
        _, first = _first_max(cur, eid, float(N_EXPERTS))
        hit = eid == first
        ids.append(first)
        ws.append(jnp.sum(jnp.where(hit, scores, 0.0), axis=0, keepdims=True))
        cur = jnp.where(hit, -jnp.inf, cur)
    w = jnp.concatenate(ws, axis=0)
    idx_ref[...] = jnp.concatenate(ids, axis=0).astype(I32)
    w_ref[...] = w / jnp.sum(w, axis=0, keepdims=True) * ROUTED_SCALE


def _route(logits_t, router_bias):
    e, n = logits_t.shape
    tn = ROUTE_TN
    return pl.pallas_call(
        _route_kernel,
        grid=(n // tn,),
        in_specs=[pl.BlockSpec((e, tn), lambda j: (0, j)),
                  pl.BlockSpec((e, 1), lambda j: (0, 0))],
        out_specs=[pl.BlockSpec((TOP_K, tn), lambda j: (0, j)),
                   pl.BlockSpec((TOP_K, tn), lambda j: (0, j))],
        out_shape=[jax.ShapeDtypeStruct((TOP_K, n), I32),
                   jax.ShapeDtypeStruct((TOP_K, n), F32)],
        compiler_params=_params("arbitrary"),
        name="route",
    )(logits_t, router_bias.reshape(e, 1))


def _tile_major(a_t, n_tiles, tm):
    return a_t.reshape(TOP_K, n_tiles, tm).transpose(1, 0, 2).reshape(n_tiles, 1, TOP_K * tm)


def _dispatch_kernel(zs_ref, pos_ref, x_ref, xs_hbm, zbuf, sem):
    step = pl.program_id(0)

    @pl.when(step == 0)
    def _():
        zbuf[...] = jnp.zeros(zbuf.shape, zbuf.dtype)

        def zero_copy(t):
            start = pl.multiple_of(t * MOE_TM, MOE_TM)
            return pltpu.make_async_copy(zbuf, xs_hbm.at[pl.ds(start, MOE_TM), :], sem.at[1])

        def start_body(t, carry):
            @pl.when(zs_ref[t] != 0)
            def _():
                zero_copy(t).start()
            return carry

        def wait_body(t, carry):
            @pl.when(zs_ref[t] != 0)
            def _():
                zero_copy(t).wait()
            return carry

        lax.fori_loop(0, zs_ref.shape[0], start_body, 0)
        lax.fori_loop(0, zs_ref.shape[0], wait_body, 0)

    def row_copy(k, r):
        return pltpu.make_async_copy(x_ref.at[pl.ds(r, 1), :],
                                     xs_hbm.at[pl.ds(pos_ref[0, 0, k * DISP_TM + r], 1), :], sem.at[0])

    def body(i, carry):
        for k in range(TOP_K):
            for u in range(2):
                row_copy(k, i * 2 + u).start()
        return carry

    lax.fori_loop(0, DISP_TM // 2, body, 0)
    for _ in range(TOP_K):
        pltpu.make_async_copy(x_ref, x_ref, sem.at[0]).wait()


def _dispatch(h2, pos_t, zero_start, n_rows):
    n, d = h2.shape
    tm = DISP_TM
    n_tiles = n // tm
    grid_spec = pltpu.PrefetchScalarGridSpec(
        num_scalar_prefetch=1,
        grid=(n_tiles,),
        in_specs=[pl.BlockSpec((1, 1, tm * TOP_K), lambda t, zs: (t, 0, 0), memory_space=pltpu.SMEM),
                  pl.BlockSpec((tm, d), lambda t, zs: (t, 0))],
        out_specs=pl.BlockSpec(memory_space=pl.ANY),
        scratch_shapes=[pltpu.VMEM((MOE_TM, d), h2.dtype),
                        pltpu.SemaphoreType.DMA((2,))],
    )
    return pl.pallas_call(
        _dispatch_kernel,
        grid_spec=grid_spec,
        out_shape=jax.ShapeDtypeStruct((n_rows, d), h2.dtype),
        compiler_params=_params("arbitrary"),
        name="dispatch",
    )(zero_start, _tile_major(pos_t, n_tiles, tm), h2)


def _experts_kernel(te_ref, nu_ref, x_ref, w1_ref, w3_ref, w2_ref, y_ref, w1b, w3b, w2b):
    j = pl.program_id(0)
    n_used = nu_ref[0]

    @pl.when(j < n_used)
    def _():
        @pl.when((j == 0) | (te_ref[j] != te_ref[jnp.maximum(j - 1, 0)]))
        def _():
            w1b[...] = w1_ref[0].astype(BF16)
            w3b[...] = w3_ref[0].astype(BF16)
            w2b[...] = w2_ref[0].astype(BF16)

        lo, hi = _unpack_bf16_pairs(x_ref[...])
        x = jnp.concatenate([lo.astype(BF16), hi.astype(BF16)], axis=1)
        a = jnp.dot(x, w1b[...], preferred_element_type=F32)
        u = jnp.dot(x, w3b[...], preferred_element_type=F32)
        y = jnp.dot((a * _sigmoid(a) * u).astype(BF16), w2b[...], preferred_element_type=F32)
        y_ref[...] = _pack_bf16_pairs(y)

    @pl.when(j >= n_used)
    def _():
        y_ref[...] = jnp.zeros(y_ref.shape, y_ref.dtype)


def _experts(xs, tile_expert, n_used, w1, w3, w2):
    nt = xs.shape[0] // MOE_TM
    d, f = w1.shape[1], w1.shape[2]
    grid_spec = pltpu.PrefetchScalarGridSpec(
        num_scalar_prefetch=2,
        grid=(nt,),
        in_specs=[pl.BlockSpec((MOE_TM, d // 2), lambda j, te, nu: (jnp.minimum(j, nu[0] - 1), 0)),
                  pl.BlockSpec((1, d, f), lambda j, te, nu: (te[j], 0, 0)),
                  pl.BlockSpec((1, d, f), lambda j, te, nu: (te[j], 0, 0)),
                  pl.BlockSpec((1, f, d), lambda j, te, nu: (te[j], 0, 0))],
        out_specs=pl.BlockSpec((MOE_TM, d // 2), lambda j, te, nu: (j, 0)),
        scratch_shapes=[pltpu.VMEM((d, f), BF16),
                        pltpu.VMEM((d, f), BF16),
                        pltpu.VMEM((f, d), BF16)],
    )
    return pl.pallas_call(
        _experts_kernel,
        grid_spec=grid_spec,
        out_shape=jax.ShapeDtypeStruct((nt * MOE_TM, d // 2), U32),
        compiler_params=_params("arbitrary"),
        name="experts",
    )(tile_expert, n_used, xs, w1, w3, w2)


def _combine_kernel(pos_cur_ref, pos_nxt_ref, ys_hbm, base_ref, g2_ref, w_ref, o_ref, buf, sem):
    bi, i = pl.program_id(0), pl.program_id(1)
    step = bi * pl.num_programs(1) + i
    n_steps = pl.num_programs(0) * pl.num_programs(1)
    slot = lax.rem(step, 2)

    def issue(pos_ref, dst_slot):
        def body(i, carry):
            for k in range(TOP_K):
                for u in range(2):
                    r = i * 2 + u
                    pltpu.make_async_copy(ys_hbm.at[pl.ds(pos_ref[0, 0, k * COMB_TM + r], 1), :],
                                          buf.at[dst_slot, k, pl.ds(r, 1), :], sem.at[dst_slot]).start()
            return carry
        lax.fori_loop(0, COMB_TM // 2, body, 0)

    @pl.when(step == 0)
    def _():
        issue(pos_cur_ref, 0)

    for parity in (0, 1):
        @pl.when((step + 1 < n_steps) & (slot == parity))
        def _(parity=parity):
            issue(pos_nxt_ref, 1 - parity)

    pltpu.make_async_copy(buf.at[slot], buf.at[slot], sem.at[slot]).wait()

    w = w_ref[...]
    half = o_ref.shape[2] // 2
    acc_lo = jnp.zeros((COMB_TM, half), F32)
    acc_hi = jnp.zeros((COMB_TM, half), F32)
    for k in range(TOP_K):
        lo, hi = _unpack_bf16_pairs(buf[slot, k])
        acc_lo = acc_lo + w[:, k:k + 1] * lo
        acc_hi = acc_hi + w[:, k:k + 1] * hi
    o_ref[0] = base_ref[0] + g2_ref[0] * jnp.concatenate([acc_lo, acc_hi], axis=1)


def _combine(ys, pos_t, w_sel, base, g2):
    b, s, d = base.shape
    tm = COMB_TM
    nt = s // tm
    n_tiles = b * nt
    pos_t = _tile_major(pos_t, n_tiles, tm)
    return pl.pallas_call(
        _combine_kernel,
        grid=(b, nt),
        in_specs=[pl.BlockSpec((1, 1, tm * TOP_K), lambda bi, i: (bi * nt + i, 0, 0),
                               memory_space=pltpu.SMEM),
                  pl.BlockSpec((1, 1, tm * TOP_K),
                               lambda bi, i: (jnp.minimum(bi * nt + i + 1, n_tiles - 1), 0, 0),
                               memory_space=pltpu.SMEM),
                  pl.BlockSpec(memory_space=pl.ANY),
                  pl.BlockSpec((1, tm, d), lambda bi, i: (bi, i, 0)),
                  pl.BlockSpec((1, 1, d), lambda bi, i: (bi, 0, 0)),
                  pl.BlockSpec((tm, TOP_K), lambda bi, i: (bi * nt + i, 0))],
        out_specs=pl.BlockSpec((1, tm, d), lambda bi, i: (bi, i, 0)),
        out_shape=jax.ShapeDtypeStruct((b, s, d), F32),
        scratch_shapes=[pltpu.VMEM((2, TOP_K, tm, d // 2), ys.dtype),
                        pltpu.SemaphoreType.DMA((2,))],
        compiler_params=_params("arbitrary", "arbitrary"),
        name="combine",
    )(pos_t, pos_t, ys, base, g2, w_sel)


def _plan_kernel(te_ref, tri_ref, low_ref, pos_ref, cnt_ref, run_ref, start_ref):
    phase, j = pl.program_id(0), pl.program_id(1)
    tn = te_ref.shape[1]
    te = te_ref[...]
    eid = lax.broadcasted_iota(I32, (N_EXPERTS, tn), 0)
    hot = jnp.zeros((N_EXPERTS, tn), F32)
    for k in range(TOP_K):
        hot = hot + jnp.where(te[k:k + 1, :] == eid, 1.0, 0.0)
    tile_count = jnp.sum(hot, axis=1, keepdims=True)

    @pl.when((phase == 0) & (j == 0))
    def _():
        run_ref[...] = jnp.zeros(run_ref.shape, F32)

    @pl.when((phase == 1) & (j == 0))
    def _():
        counts = run_ref[...]
        cnt_ref[...] = counts
        tiles = jnp.floor((counts + (MOE_TM - 1)) * (1.0 / MOE_TM))
        start_ref[...] = jnp.dot(low_ref[...], tiles.astype(BF16), preferred_element_type=F32) * MOE_TM
        run_ref[...] = jnp.zeros(run_ref.shape, F32)

    @pl.when(phase == 1)
    def _():
        before = jnp.dot(hot.astype(BF16), tri_ref[...], preferred_element_type=F32)
        val = before + (run_ref[:, 0:1] + start_ref[:, 0:1])
        rows = [jnp.sum(jnp.where(te[k:k + 1, :] == eid, val, 0.0), axis=0, keepdims=True)
                for k in range(TOP_K)]
        pos_ref[...] = jnp.concatenate(rows, axis=0).astype(I32)

    run_ref[...] = run_ref[...] + tile_count


def _dispatch_plan(top_e_t):
    n = top_e_t.shape[1]
    tn = PLAN_TN
    n_tiles = n * TOP_K // MOE_TM + N_EXPERTS
    tri = jnp.asarray(np.triu(np.ones((tn, tn), np.float32), 1), BF16)
    low = jnp.asarray(np.tril(np.ones((N_EXPERTS, N_EXPERTS), np.float32), -1), BF16)
    pos_t, cnt = pl.pallas_call(
        _plan_kernel,
        grid=(2, n // tn),
        in_specs=[pl.BlockSpec((TOP_K, tn), lambda ph, j: (0, j)),
                  pl.BlockSpec((tn, tn), lambda ph, j: (0, 0)),
                  pl.BlockSpec((N_EXPERTS, N_EXPERTS), lambda ph, j: (0, 0))],
        out_specs=[pl.BlockSpec((TOP_K, tn), lambda ph, j: (0, j * ph)),
                   pl.BlockSpec((N_EXPERTS, LANES), lambda ph, j: (0, 0))],
        out_shape=[jax.ShapeDtypeStruct((TOP_K, n), I32),
                   jax.ShapeDtypeStruct((N_EXPERTS, LANES), F32)],
        scratch_shapes=[pltpu.VMEM((N_EXPERTS, LANES), F32),
                        pltpu.VMEM((N_EXPERTS, LANES), F32)],
        compiler_params=_params("arbitrary", "arbitrary"),
        name="plan",
    )(top_e_t, tri, low)
    counts = cnt[:, 0].astype(I32)
    tile_end = jnp.cumsum((counts + MOE_TM - 1) // MOE_TM)
    tile_expert = jnp.minimum(
        jnp.sum((tile_end[None, :] <= jnp.arange(n_tiles, dtype=I32)[:, None]).astype(I32), axis=1),
        N_EXPERTS - 1)
    n_used = tile_end[-1:].astype(I32)
    t_ids = jnp.arange(n_tiles, dtype=I32)
    is_last = jnp.any((tile_end[None, :] - 1 == t_ids[:, None]) & (counts[None, :] > 0), axis=1)
    zero_tile = (is_last | (t_ids >= n_used[0])).astype(I32)
    return pos_t, zero_tile, tile_expert.astype(I32), n_used, n_tiles * MOE_TM


def _layer(x, c, rel_bias, norm1_w, norm2_w, w_ada, b_ada, w_in, conv_w, w_conv_out, q_norm_w,
           k_norm_w, idx_k_norm_w, idx_k_norm_b, w_attn_out, w_o, w_router, router_bias,
           w1, w3, w2, ws1, ws3, ws2):
    b, s, d = x.shape
    mod = _mod(c, w_ada, b_ada).reshape(b, 6, 1, d)
    sh1, sc1, g1, sh2, sc2, g2 = [mod[:, m] for m in range(6)]

    cols = [w_in[:, _SEG[name][0]:_SEG[name][1]] for name in _ORDER]
    cols.append(jnp.zeros((d, PROJ_W - sum(col.shape[1] for col in cols)), w_in.dtype))
    w_in_p = jnp.concatenate(cols, axis=1).astype(BF16)

    proj = _proj(x, norm1_w, sc1, sh1, w_in_p)
    qT, qiT, kh, vT, kin, wT = _prep(proj, q_norm_w, k_norm_w, idx_k_norm_w, idx_k_norm_b,
                                     rel_bias[REL_BUCKETS - 1])
    attn = _attention(qT, qiT, wT, kh, vT, kin, _bias_strips(rel_bias))
    mixed = _mix(proj, attn, conv_w, w_conv_out.astype(BF16), w_attn_out.astype(BF16))
    base, h2, logits_t = _post(x, mixed, g1, norm2_w, sc2, sh2, g2, w_o.astype(BF16), w_router.T,
                               ws1.astype(BF16), ws3.astype(BF16), ws2.astype(BF16))
    top_e_t, w_sel_t = _route(logits_t, router_bias)
    pos_t, zero_start, tile_expert, n_used, n_rows = _dispatch_plan(top_e_t)
    xs = _dispatch(h2, pos_t, zero_start, n_rows)
    ys = _experts(xs, tile_expert, n_used, w1, w3, w2)
    return _combine(ys, pos_t, w_sel_t.T, base, g2)


def kernel(x, c, rel_bias, norm1_w, norm2_w, w_ada, b_ada, w_in, conv_w, w_conv_out, q_norm_w,
           k_norm_w, idx_k_norm_w, idx_k_norm_b, w_attn_out, w_o, w_router, router_bias,
           w1, w3, w2, ws1, ws3, ws2):
    assert x.shape[1] % PROJ_TM == 0 and x.shape[2] == D_MODEL and w_ada.shape[0] == 1
    return _layer(x, c, rel_bias, norm1_w[0], norm2_w[0], w_ada[0], b_ada[0], w_in[0], conv_w[0],
                  w_conv_out[0], q_norm_w[0], k_norm_w[0], idx_k_norm_w[0], idx_k_norm_b[0],
                  w_attn_out[0], w_o[0], w_router[0], router_bias[0], w1[0], w3[0], w2[0],
                  ws1[0], ws3[0], ws2[0])
```

```python
import functools
import math

import numpy as np
import jax
import jax.numpy as jnp
from jax import lax
from jax.experimental import pallas as pl
from jax.experimental.pallas import tpu as pltpu

F32 = jnp.float32
BF16 = jnp.bfloat16
I32 = jnp.int32
PACKED = jnp.int32

D_MODEL = 2048
CONV_WIDTH = D_MODEL // 2
CONV_K = 3
N_HEADS = 16
N_KV_HEADS = 4
HEAD_DIM = 64
ATTN_WIDTH = N_HEADS * HEAD_DIM
KV_WIDTH = N_KV_HEADS * HEAD_DIM
IDX_HEADS = 16
IDX_DIM = 64
IDX_TOPK_MAX = 256
REL_BUCKETS = 32
REL_MAX_DIST = 128
N_EXPERTS = 64
N_GROUPS = 8
TOPK_GROUPS = 4
TOP_K = 8
D_EXPERT = 512
ROUTED_SCALE = 2.5
EPS = 1e-6
NEG = -1e30

REP = N_HEADS // N_KV_HEADS

LANES = 128
VMEM_LIMIT = 56 * 1024 * 1024

TQ = 128
KB = 128
KC = 4 * KB
FAR_KC = 4 * KB
PROJ_TM = 1024
PROJ_TN = 768
PREP_TM = 512
MIX_TM = 512
POST_TM = 512
ROUTE_TN = 512
MOE_TM = 512
COMB_TM = 128
DISP_TM = 128
PLAN_TN = 512

ROW_SUB = D_MODEL // 2 // LANES
QK_DIM = 128
V_ROWS = HEAD_DIM + 16
LOG2E = math.log2(math.e)

_SEG = dict(cb=(0, 1024), cc=(1024, 2048), cu=(2048, 3072), q=(3072, 4096), k=(4096, 4352),
            v=(4352, 4608), qi=(4608, 5632), ki=(5632, 5696), wi=(5696, 5712),
            ga=(5712, 7760), gb=(7760, 9808))
_ORDER = ["ga", "gb", "cb", "cc", "cu", "q", "qi", "k", "v", "ki", "wi"]
PROJ_W = 9984
COL_GA, COL_GB = 0, 1
COL_CB, COL_CC, COL_CU, COL_Q, COL_QI = 4, 5, 6, 7, 8
COL_K, COL_V = 36, 37
COL_KW = 76

INT_MIN = -(2 ** 31)
INT_MAX = 2 ** 31 - 1


def _sortable_key_of(x):
    bits = int(np.float32(x).view(np.int32))
    return bits ^ 0x7FFFFFFF if bits < 0 else bits


NEG_KEY = _sortable_key_of(NEG)


def _sigmoid(x):
    return 1.0 / (1.0 + jnp.exp(-x))


def _pack_bf16_pairs(x):
    half = x.shape[1] // 2
    lo = lax.bitcast_convert_type(x[:, :half].astype(BF16).astype(F32), PACKED)
    hi = lax.bitcast_convert_type(x[:, half:].astype(BF16).astype(F32), PACKED)
    return lax.shift_right_logical(lo, jnp.full_like(lo, 16)) | (hi & jnp.int32(-65536))


def _unpack_bf16_pairs(w):
    lo = lax.bitcast_convert_type(w << 16, F32)
    hi = lax.bitcast_convert_type(w & jnp.int32(-65536), F32)
    return lo, hi


def _store_row_tiles(ref, words):
    m = words.shape[0]
    for sl in range(ROW_SUB):
        ref[pl.ds(sl, m, stride=ROW_SUB), :] = words[:, sl * LANES:(sl + 1) * LANES]


def _load_row_tiles(ref, m):
    return jnp.concatenate([ref[pl.ds(sl, m, stride=ROW_SUB), :] for sl in range(ROW_SUB)], axis=1)


def _params(*sem):
    return pltpu.CompilerParams(dimension_semantics=sem, vmem_limit_bytes=VMEM_LIMIT)


def _mod_kernel(c_ref, w_ref, b_ref, o_ref):
    c = c_ref[...]
    s = (c * _sigmoid(c)).astype(BF16)
    o_ref[...] = jnp.dot(s, w_ref[...].astype(BF16), preferred_element_type=F32) + b_ref[...]


def _mod(c, w_ada, b_ada):
    b = c.shape[0]
    rows = 8
    cp = jnp.pad(c, ((0, rows - b), (0, 0)))
    n = w_ada.shape[1]
    tn = 1024
    out = pl.pallas_call(
        _mod_kernel,
        grid=(n // tn,),
        in_specs=[pl.BlockSpec((rows, D_MODEL), lambda j: (0, 0)),
                  pl.BlockSpec((D_MODEL, tn), lambda j: (0, j)),
                  pl.BlockSpec((1, tn), lambda j: (0, j))],
        out_specs=pl.BlockSpec((rows, tn), lambda j: (0, j)),
        out_shape=jax.ShapeDtypeStruct((rows, n), F32),
        compiler_params=_params("arbitrary"),
        name="mod",
    )(cp, w_ada, b_ada.reshape(1, n))
    return out[:b]


def _proj_kernel(x_ref, nw_ref, sc_ref, sh_ref, w_ref, o_ref, h_ref):
    @pl.when(pl.program_id(2) == 0)
    def _():
        x = x_ref[0]
        ms = jnp.mean(x * x, axis=-1, keepdims=True)
        y = x * lax.rsqrt(ms + EPS) * nw_ref[...]
        h_ref[...] = (y * (1.0 + sc_ref[0]) + sh_ref[0]).astype(BF16)

    o_ref[0] = jnp.dot(h_ref[...], w_ref[...], preferred_element_type=F32).astype(BF16)


def _proj(x, norm_w, sc, sh, w_in_p):
    b, s, d = x.shape
    tm, tn = PROJ_TM, PROJ_TN
    return pl.pallas_call(
        _proj_kernel,
        grid=(b, s // tm, PROJ_W // tn),
        in_specs=[pl.BlockSpec((1, tm, d), lambda bi, i, j: (bi, i, 0)),
                  pl.BlockSpec((1, d), lambda bi, i, j: (0, 0)),
                  pl.BlockSpec((1, 1, d), lambda bi, i, j: (bi, 0, 0)),
                  pl.BlockSpec((1, 1, d), lambda bi, i, j: (bi, 0, 0)),
                  pl.BlockSpec((d, tn), lambda bi, i, j: (0, j))],
        out_specs=pl.BlockSpec((1, tm, tn), lambda bi, i, j: (bi, i, j)),
        out_shape=jax.ShapeDtypeStruct((b, s, PROJ_W), BF16),
        scratch_shapes=[pltpu.VMEM((tm, d), BF16)],
        compiler_params=_params("arbitrary", "arbitrary", "arbitrary"),
        name="proj",
    )(x, norm_w.reshape(1, d), sc, sh, w_in_p)


def _prep_kernel(q_ref, qi_ref, k_ref, v_ref, kw_ref, qnw_ref, knw_ref, inw_ref, inb_ref, qtail_ref,
                 qT_ref, qiT_ref, kh_ref, vT_ref, kin_ref, wT_ref):
    tm = q_ref.shape[1]
    nqb = tm // TQ

    q3 = q_ref[0].astype(F32).T.reshape(N_HEADS, HEAD_DIM, tm)
    ms = jnp.mean(q3 * q3, axis=1, keepdims=True)
    qn = q3 * lax.rsqrt(ms + EPS) * (qnw_ref[...] * (HEAD_DIM ** -0.5 * LOG2E))
    qi3 = qi_ref[0].astype(F32).T.reshape(IDX_HEADS, IDX_DIM, tm)
    for jb in range(nqb):
        for h in range(N_HEADS):
            g, r = divmod(h, REP)
            qT_ref[0, jb, g, :HEAD_DIM, r * TQ:(r + 1) * TQ] = qn[h, :, jb * TQ:(jb + 1) * TQ].astype(BF16)
        for g in range(N_KV_HEADS):
            qT_ref[0, jb, g, HEAD_DIM:, :] = qtail_ref[g]
        for h in range(IDX_HEADS):
            qiT_ref[0, jb, :, h * TQ:(h + 1) * TQ] = qi3[h, :, jb * TQ:(jb + 1) * TQ].astype(BF16)

    k = k_ref[0].astype(F32)
    ones_cols = jnp.where(lax.broadcasted_iota(I32, (tm, QK_DIM - HEAD_DIM), 1) < 2, 1.0, 0.0)
    for g in range(N_KV_HEADS):
        kg = k[:, g * HEAD_DIM:(g + 1) * HEAD_DIM]
        msk = jnp.mean(kg * kg, axis=-1, keepdims=True)
        kn = kg * lax.rsqrt(msk + EPS) * knw_ref[...]
        kh_ref[0, g] = jnp.concatenate([kn, ones_cols], axis=1).astype(BF16)

    v3 = v_ref[0].astype(F32).T.reshape(N_KV_HEADS, HEAD_DIM, tm)
    ones_rows = jnp.where(lax.broadcasted_iota(I32, (V_ROWS - HEAD_DIM, KB), 0) == 0, 1.0, 0.0)
    for g in range(N_KV_HEADS):
        for jb in range(tm // KB):
            vT_ref[0, g, jb] = jnp.concatenate(
                [v3[g, :, jb * KB:(jb + 1) * KB], ones_rows], axis=0).astype(BF16)

    kw = kw_ref[0].astype(F32)
    ki = kw[:, :IDX_DIM]
    mu = jnp.mean(ki, axis=-1, keepdims=True)
    var = jnp.mean(jnp.square(ki - mu), axis=-1, keepdims=True)
    kin_ref[0] = ((ki - mu) * lax.rsqrt(var + EPS) * inw_ref[...] + inb_ref[...]).astype(BF16)
    wiT = kw.T[IDX_DIM:IDX_DIM + IDX_HEADS] * (IDX_HEADS ** -0.5 * IDX_DIM ** -0.5)
    for jb in range(nqb):
        wT_ref[0, jb] = wiT[:, jb * TQ:(jb + 1) * TQ]


def _prep(proj, q_norm_w, k_norm_w, idx_k_norm_w, idx_k_norm_b, far_bias):
    b, s, _ = proj.shape
    tm = PREP_TM
    nqb = tm // TQ
    nq = s // TQ
    fb2 = (far_bias * LOG2E).reshape(N_KV_HEADS, REP)
    hi = fb2.astype(BF16)
    lo = (fb2 - hi.astype(F32)).astype(BF16)
    tail = jnp.stack([hi, lo], axis=1)
    tail = jnp.broadcast_to(tail[..., None], (N_KV_HEADS, 2, REP, TQ)).reshape(N_KV_HEADS, 2, REP * TQ)
    qtail = jnp.pad(tail, ((0, 0), (0, QK_DIM - HEAD_DIM - 2), (0, 0)))
    return pl.pallas_call(
        _prep_kernel,
        grid=(b, s // tm),
        in_specs=[pl.BlockSpec((1, tm, ATTN_WIDTH), lambda bi, i: (bi, i, COL_Q)),
                  pl.BlockSpec((1, tm, IDX_HEADS * IDX_DIM), lambda bi, i: (bi, i, COL_QI)),
                  pl.BlockSpec((1, tm, KV_WIDTH), lambda bi, i: (bi, i, COL_K)),
                  pl.BlockSpec((1, tm, KV_WIDTH), lambda bi, i: (bi, i, COL_V)),
                  pl.BlockSpec((1, tm, LANES), lambda bi, i: (bi, i, COL_KW)),
                  pl.BlockSpec((1, HEAD_DIM, 1), lambda bi, i: (0, 0, 0)),
                  pl.BlockSpec((1, HEAD_DIM), lambda bi, i: (0, 0)),
                  pl.BlockSpec((1, IDX_DIM), lambda bi, i: (0, 0)),
                  pl.BlockSpec((1, IDX_DIM), lambda bi, i: (0, 0)),
                  pl.BlockSpec((N_KV_HEADS, QK_DIM - HEAD_DIM, REP * TQ), lambda bi, i: (0, 0, 0))],
        out_specs=[pl.BlockSpec((1, nqb, N_KV_HEADS, QK_DIM, REP * TQ), lambda bi, i: (bi, i, 0, 0, 0)),
                   pl.BlockSpec((1, nqb, IDX_DIM, IDX_HEADS * TQ), lambda bi, i: (bi, i, 0, 0)),
                   pl.BlockSpec((1, N_KV_HEADS, tm, QK_DIM), lambda bi, i: (bi, 0, i, 0)),
                   pl.BlockSpec((1, N_KV_HEADS, tm // KB, V_ROWS, KB), lambda bi, i: (bi, 0, i, 0, 0)),
                   pl.BlockSpec((1, tm, IDX_DIM), lambda bi, i: (bi, i, 0)),
                   pl.BlockSpec((1, nqb, IDX_HEADS, TQ), lambda bi, i: (bi, i, 0, 0))],
        out_shape=[jax.ShapeDtypeStruct((b, nq, N_KV_HEADS, QK_DIM, REP * TQ), BF16),
                   jax.ShapeDtypeStruct((b, nq, IDX_DIM, IDX_HEADS * TQ), BF16),
                   jax.ShapeDtypeStruct((b, N_KV_HEADS, s, QK_DIM), BF16),
                   jax.ShapeDtypeStruct((b, N_KV_HEADS, s // KB, V_ROWS, KB), BF16),
                   jax.ShapeDtypeStruct((b, s, IDX_DIM), BF16),
                   jax.ShapeDtypeStruct((b, nq, IDX_HEADS, TQ), F32)],
        compiler_params=_params("arbitrary", "arbitrary"),
        name="prep",
    )(proj, proj, proj, proj, proj,
      q_norm_w.reshape(1, HEAD_DIM, 1), k_norm_w.reshape(1, HEAD_DIM),
      idx_k_norm_w.reshape(1, IDX_DIM), idx_k_norm_b.reshape(1, IDX_DIM), qtail)


def _t5_bucket_np(n):
    n = np.maximum(n, 0)
    max_exact = REL_BUCKETS // 2
    nf = np.maximum(n, 1).astype(np.float64)
    large = max_exact + np.floor(np.log(nf / max_exact) / math.log(REL_MAX_DIST / max_exact)
                                 * (REL_BUCKETS - max_exact)).astype(np.int64)
    large = np.minimum(large, REL_BUCKETS - 1)
    return np.where(n < max_exact, n, large).astype(np.int32)


def _bias_kernel(rb_ref, bucket_ref, o_ref):
    h = pl.program_id(0)
    bucket = bucket_ref[...]
    acc = jnp.zeros(bucket.shape, F32)
    for bkt in range(REL_BUCKETS):
        acc = jnp.where(bucket == bkt, rb_ref[bkt, h], acc)
    o_ref[0] = (acc - rb_ref[REL_BUCKETS - 1, h]) * LOG2E


def _bias_strips(rel_bias):
    kk = np.arange(3 * TQ)[:, None]
    qq = np.arange(TQ)[None, :]
    bucket = jnp.asarray(_t5_bucket_np(qq + TQ - kk))
    return pl.pallas_call(
        _bias_kernel,
        grid=(N_HEADS,),
        in_specs=[pl.BlockSpec(memory_space=pltpu.SMEM),
                  pl.BlockSpec((3 * TQ, TQ), lambda h: (0, 0))],
        out_specs=pl.BlockSpec((1, 3 * TQ, TQ), lambda h: (h, 0, 0)),
        out_shape=jax.ShapeDtypeStruct((N_HEADS, 3 * TQ, TQ), F32),
        compiler_params=_params("arbitrary"),
        name="bias",
    )(rel_bias, bucket)


def _attn_kernel(qT_ref, qiT_ref, wT_ref, kh_ref, vT_ref, kin_ref, biasT_ref, o_ref,
                 keys_ref, am_ref, amf_ref, p_ref, m_ref, acc_ref, *, n_sel):
    i = pl.program_id(1)
    seq = kin_ref.shape[1]
    t0 = i * TQ
    n_chunks = lax.shift_right_logical(i + 4, 2)
    q_pos = t0 + lax.broadcasted_iota(I32, (KB, TQ), 1)
    k_off = lax.broadcasted_iota(I32, (KB, TQ), 0)

    qiT = qiT_ref[0, 0]
    wT = wT_ref[0, 0]

    def score_chunk(c, carry):
        k0 = pl.multiple_of(c * KC, KC)
        d = jnp.dot(kin_ref[0, pl.ds(k0, KC), :], qiT, preferred_element_type=F32)
        acc = jnp.zeros((KC, TQ), F32)
        for h in range(IDX_HEADS):
            acc = acc + wT[h:h + 1, :] * jnp.maximum(d[:, h * TQ:(h + 1) * TQ], 0.0)
        for j in range(KC // KB):
            blk = c * (KC // KB) + j
            sc = jnp.where(blk * KB + k_off <= q_pos, acc[j * KB:(j + 1) * KB], NEG)
            bits = lax.bitcast_convert_type(sc, I32)
            keys_ref[blk] = jnp.where(bits < 0, bits ^ 0x7FFFFFFF, bits)
        return carry

    lax.fori_loop(0, n_chunks, score_chunk, 0)

    n_virtual = (seq - n_chunks * KC).astype(F32)

    def count(pred):
        def body(c, acc):
            for j in range(KC // KB):
                blk = c * (KC // KB) + j
                hit = jnp.where(pred(keys_ref[blk], blk), 1.0, 0.0)
                acc = acc + jnp.sum(hit.reshape(KB // 8, 8, TQ), axis=0)
            return acc
        acc = lax.fori_loop(0, n_chunks, body, jnp.zeros((8, TQ), F32))
        return jnp.sum(acc, axis=0, keepdims=True)

    def bit_body(it, thr):
        cand = thr + lax.shift_left(jnp.int32(1), 31 - it)
        cnt = count(lambda kb, blk: kb >= cand) + jnp.where(NEG_KEY >= cand, n_virtual, 0.0)
        return jnp.where(cnt >= n_sel, cand, thr)

    thr = lax.fori_loop(0, 32, bit_body, jnp.full((1, TQ), INT_MIN, I32))

    cnt_gt = count(lambda kb, blk: kb > thr) + jnp.where(NEG_KEY > thr, n_virtual, 0.0)
    cnt_eq = count(lambda kb, blk: kb == thr) + jnp.where(NEG_KEY == thr, n_virtual, 0.0)
    need = n_sel - cnt_gt
    p_ref[...] = jnp.full((8, TQ), INT_MAX, I32)
    has_tie = jnp.max(jnp.where(cnt_eq > need, 1.0, 0.0)) > 0.0

    @pl.when(has_tie)
    def _():
        idx_bits = int(seq).bit_length()

        def p_body(it, p):
            cand = p | lax.shift_left(jnp.int32(1), idx_bits - 1 - it)
            below = count(lambda kb, blk: (kb == thr) & (blk * KB + k_off < cand))
            return jnp.where(below < need, cand, p)

        p = lax.fori_loop(0, idx_bits, p_body, jnp.zeros((1, TQ), I32))
        p_ref[...] = jnp.broadcast_to(p, (8, TQ))

    p_last = p_ref[0:1, :]

    bw = jnp.maximum(i - 1, 0)
    ws = pl.multiple_of(bw * KB, KB)

    def mask_chunk(c, carry):
        for j in range(KC // KB):
            blk = c * (KC // KB) + j
            kb = keys_ref[blk]
            k_pos = blk * KB + k_off
            sel = (kb > thr) | ((kb == thr) & (k_pos <= p_last))
            v = jnp.where(sel & (k_pos <= q_pos), 0.0, NEG)
            am_ref[blk] = v
            amf_ref[blk] = jnp.where(k_pos < ws, v, NEG)
        return carry

    lax.fori_loop(0, n_chunks, mask_chunk, 0)

    off = pl.multiple_of(TQ - (t0 - ws), TQ)
    am_near = jnp.concatenate([am_ref[bw], am_ref[bw + 1]], axis=0)
    for g in range(N_KV_HEADS):
        s = jnp.dot(kh_ref[0, g, pl.ds(ws, 2 * KB), :], qT_ref[0, 0, g], preferred_element_type=F32)
        s = jnp.concatenate(
            [s[:, r * TQ:(r + 1) * TQ] + (biasT_ref[REP * g + r, pl.ds(off, 2 * KB), :] + am_near)
             for r in range(REP)], axis=1)
        m = jnp.max(s, axis=0, keepdims=True)
        pb = jnp.exp2(s - m).astype(BF16)
        m_ref[g] = m
        acc_ref[g] = (jnp.dot(vT_ref[0, g, bw], pb[:KB], preferred_element_type=F32)
                      + jnp.dot(vT_ref[0, g, bw + 1], pb[KB:], preferred_element_type=F32))

    fb = FAR_KC // KB

    def far_body(f, carry):
        k0 = pl.multiple_of(f * FAR_KC, FAR_KC)
        amf = jnp.concatenate([amf_ref[f * fb + j] for j in range(fb)], axis=0)
        for g in range(N_KV_HEADS):
            s = jnp.dot(kh_ref[0, g, pl.ds(k0, FAR_KC), :], qT_ref[0, 0, g], preferred_element_type=F32)
            s = jnp.concatenate([s[:, r * TQ:(r + 1) * TQ] + amf for r in range(REP)], axis=1)
            m_old = m_ref[g]
            m_new = jnp.maximum(m_old, jnp.max(s, axis=0, keepdims=True))
            pb = jnp.exp2(s - m_new).astype(BF16)
            vc = jnp.concatenate([vT_ref[0, g, f * fb + j] for j in range(fb)], axis=1)
            m_ref[g] = m_new
            acc_ref[g] = (jnp.exp2(m_old - m_new) * acc_ref[g]
                          + jnp.dot(vc, pb, preferred_element_type=F32))
        return carry

    lax.fori_loop(0, (bw + fb - 1) // fb, far_body, 0)

    outs = []
    for g in range(N_KV_HEADS):
        og = acc_ref[g, :HEAD_DIM] / acc_ref[g, HEAD_DIM:HEAD_DIM + 1]
        outs.extend(og[:, r * TQ:(r + 1) * TQ] for r in range(REP))
    o_ref[0] = jnp.concatenate(outs, axis=0).T.astype(BF16)


def _attention(qT, qiT, wT, kh, vT, kin, bias_strips):
    b, nq = qT.shape[0], qT.shape[1]
    s = kin.shape[1]
    n_sel = min(IDX_TOPK_MAX, s // 4)
    nb = s // KB
    return pl.pallas_call(
        functools.partial(_attn_kernel, n_sel=n_sel),
        grid=(b, nq),
        in_specs=[pl.BlockSpec((1, 1, N_KV_HEADS, QK_DIM, REP * TQ), lambda bi, i: (bi, i, 0, 0, 0)),
                  pl.BlockSpec((1, 1, IDX_DIM, IDX_HEADS * TQ), lambda bi, i: (bi, i, 0, 0)),
                  pl.BlockSpec((1, 1, IDX_HEADS, TQ), lambda bi, i: (bi, i, 0, 0)),
                  pl.BlockSpec((1, N_KV_HEADS, s, QK_DIM), lambda bi, i: (bi, 0, 0, 0)),
                  pl.BlockSpec((1, N_KV_HEADS, nb, V_ROWS, KB), lambda bi, i: (bi, 0, 0, 0, 0)),
                  pl.BlockSpec((1, s, IDX_DIM), lambda bi, i: (bi, 0, 0)),
                  pl.BlockSpec((N_HEADS, 3 * TQ, TQ), lambda bi, i: (0, 0, 0))],
        out_specs=pl.BlockSpec((1, TQ, ATTN_WIDTH), lambda bi, i: (bi, i, 0)),
        out_shape=jax.ShapeDtypeStruct((b, s, ATTN_WIDTH), BF16),
        scratch_shapes=[pltpu.VMEM((nb, KB, TQ), I32),
                        pltpu.VMEM((nb, KB, TQ), F32),
                        pltpu.VMEM((nb, KB, TQ), F32),
                        pltpu.VMEM((8, TQ), I32),
                        pltpu.VMEM((N_KV_HEADS, 1, REP * TQ), F32),
                        pltpu.VMEM((N_KV_HEADS, V_ROWS, REP * TQ), F32)],
        compiler_params=_params("arbitrary", "arbitrary"),
        name="attn",
    )(qT, qiT, wT, kh, vT, kin, bias_strips)


HALO = 16


def _mix_kernel(cb_ref, cc_ref, cu_ref, ccp_ref, cup_ref, at_ref, ga_ref, gb_ref,
                cw_ref, wco_ref, wao_ref, o_ref):
    tm = cb_ref.shape[1]
    v = cc_ref[0].astype(F32) * cu_ref[0].astype(F32)
    first = pl.program_id(1) == 0
    hv = ccp_ref[0].astype(F32) * cup_ref[0].astype(F32)
    hv = jnp.where(first, 0.0, hv)
    row = lax.broadcasted_iota(I32, v.shape, 0)
    v1 = jnp.where(row == 0, hv[HALO - 1:HALO], pltpu.roll(v, 1, 0))
    v2 = pltpu.roll(v, 2, 0)
    v2 = jnp.where(row == 0, hv[HALO - 2:HALO - 1], jnp.where(row == 1, hv[HALO - 1:HALO], v2))
    y = cw_ref[0:1] * v2 + cw_ref[1:2] * v1 + cw_ref[2:3] * v
    yc = (cb_ref[0].astype(F32) * y).astype(BF16)
    y_conv = jnp.dot(yc, wco_ref[...], preferred_element_type=F32)
    y_attn = jnp.dot(at_ref[0], wao_ref[...], preferred_element_type=F32)
    mixed = _sigmoid(ga_ref[0].astype(F32)) * y_conv + _sigmoid(gb_ref[0].astype(F32)) * y_attn
    o_ref[0] = mixed.astype(BF16)


def _mix(proj, attn, conv_w, w_conv_out_b, w_attn_out_b):
    b, s, _ = proj.shape
    tm = MIX_TM
    hb = tm // HALO
    prev = lambda col: (lambda bi, i: (bi, jnp.maximum(i * hb - 1, 0), col))
    return pl.pallas_call(
        _mix_kernel,
        grid=(b, s // tm),
        in_specs=[pl.BlockSpec((1, tm, CONV_WIDTH), lambda bi, i: (bi, i, COL_CB)),
                  pl.BlockSpec((1, tm, CONV_WIDTH), lambda bi, i: (bi, i, COL_CC)),
                  pl.BlockSpec((1, tm, CONV_WIDTH), lambda bi, i: (bi, i, COL_CU)),
                  pl.BlockSpec((1, HALO, CONV_WIDTH), prev(COL_CC)),
                  pl.BlockSpec((1, HALO, CONV_WIDTH), prev(COL_CU)),
                  pl.BlockSpec((1, tm, ATTN_WIDTH), lambda bi, i: (bi, i, 0)),
                  pl.BlockSpec((1, tm, D_MODEL), lambda bi, i: (bi, i, COL_GA)),
                  pl.BlockSpec((1, tm, D_MODEL), lambda bi, i: (bi, i, COL_GB)),
                  pl.BlockSpec((8, CONV_WIDTH), lambda bi, i: (0, 0)),
                  pl.BlockSpec((CONV_WIDTH, D_MODEL), lambda bi, i: (0, 0)),
                  pl.BlockSpec((ATTN_WIDTH, D_MODEL), lambda bi, i: (0, 0))],
        out_specs=pl.BlockSpec((1, tm, D_MODEL), lambda bi, i: (bi, i, 0)),
        out_shape=jax.ShapeDtypeStruct((b, s, D_MODEL), BF16),
        compiler_params=_params("arbitrary", "arbitrary"),
        name="mix",
    )(proj, proj, proj, proj, proj, attn, proj, proj,
      jnp.pad(conv_w, ((0, 8 - CONV_K), (0, 0))), w_conv_out_b, w_attn_out_b)


def _post_kernel(x_ref, mx_ref, g1_ref, nw_ref, sc_ref, sh_ref, g2_ref, wo_ref, wrT_ref,
                 ws1_ref, ws3_ref, ws2_ref, base_ref, h2_ref, lg_ref):
    x1 = x_ref[0] + g1_ref[0] * jnp.dot(mx_ref[0], wo_ref[...], preferred_element_type=F32)
    ms = jnp.mean(x1 * x1, axis=-1, keepdims=True)
    h2 = x1 * lax.rsqrt(ms + EPS) * nw_ref[...] * (1.0 + sc_ref[0]) + sh_ref[0]
    _store_row_tiles(h2_ref, _pack_bf16_pairs(h2))
    lg_ref[...] = lax.dot_general(wrT_ref[...], h2, (((1,), (1,)), ((), ())),
                                  precision=lax.Precision.HIGHEST, preferred_element_type=F32)
    hb = h2.astype(BF16)
    a = jnp.dot(hb, ws1_ref[...], preferred_element_type=F32)
    u = jnp.dot(hb, ws3_ref[...], preferred_element_type=F32)
    shared = jnp.dot((a * _sigmoid(a) * u).astype(BF16), ws2_ref[...], preferred_element_type=F32)
    base_ref[0] = x1 + g2_ref[0] * shared


def _post(x, mixed, g1, norm_w, sc, sh, g2, w_o_b, w_router_t, ws1_b, ws3_b, ws2_b):
    b, s, d = x.shape
    tm = POST_TM
    nt = s // tm
    vec = pl.BlockSpec((1, 1, d), lambda bi, i: (bi, 0, 0))
    const = lambda shape: pl.BlockSpec(shape, lambda bi, i: (0,) * len(shape))
    return pl.pallas_call(
        _post_kernel,
        grid=(b, nt),
        in_specs=[pl.BlockSpec((1, tm, d), lambda bi, i: (bi, i, 0)),
                  pl.BlockSpec((1, tm, d), lambda bi, i: (bi, i, 0)),
                  vec, const((1, d)), vec, vec, vec,
                  const((d, d)), const((N_EXPERTS, d)),
                  const((d, D_EXPERT)), const((d, D_EXPERT)), const((D_EXPERT, d))],
        out_specs=[pl.BlockSpec((1, tm, d), lambda bi, i: (bi, i, 0)),
                   pl.BlockSpec((tm * ROW_SUB, LANES), lambda bi, i: (bi * nt + i, 0)),
                   pl.BlockSpec((N_EXPERTS, tm), lambda bi, i: (0, bi * nt + i))],
        out_shape=[jax.ShapeDtypeStruct((b, s, d), F32),
                   jax.ShapeDtypeStruct((b * s * ROW_SUB, LANES), PACKED),
                   jax.ShapeDtypeStruct((N_EXPERTS, b * s), F32)],
        compiler_params=_params("arbitrary", "arbitrary"),
        name="post",
    )(x, mixed, g1, norm_w.reshape(1, d), sc, sh, g2, w_o_b, w_router_t, ws1_b, ws3_b, ws2_b)


def _first_max(cur, ids, sentinel):
    m = jnp.max(cur, axis=0, keepdims=True)
    first = jnp.min(jnp.where(cur == m, ids, sentinel), axis=0, keepdims=True)
    return m, first


def _route_kernel(lg_ref, rb_ref, idx_ref, w_ref):
    tn = lg_ref.shape[1]
    gsz = N_EXPERTS // N_GROUPS
    scores = _sigmoid(lg_ref[...])
    sel = scores + rb_ref[...]
    sub = lax.broadcasted_iota(I32, (gsz, tn), 0).astype(F32)

    gs = []
    for g in range(N_GROUPS):
        v = sel[g * gsz:(g + 1) * gsz]
        m1, first = _first_max(v, sub, float(gsz))
        m2 = jnp.max(jnp.where(sub == first, -jnp.inf, v), axis=0, keepdims=True)
        gs.append(m1 + m2)
    cur = jnp.concatenate(gs, axis=0)
    gid = lax.broadcasted_iota(I32, (N_GROUPS, tn), 0).astype(F32)
    keep = jnp.zeros((N_GROUPS, tn), F32)
    for _ in range(TOPK_GROUPS):
        _, first = _first_max(cur, gid, float(N_GROUPS))
        hit = gid == first
        keep = jnp.where(hit, 1.0, keep)
        cur = jnp.where(hit, -jnp.inf, cur)

    cur = jnp.concatenate(
        [jnp.where(keep[g:g + 1] > 0.0, sel[g * gsz:(g + 1) * gsz], NEG) for g in range(N_GROUPS)],
        axis=0)
    eid = lax.broadcasted_iota(I32, (N_EXPERTS, tn), 0).astype(F32)
    ids, ws = [], []
    for _ in range(TOP_K):
        _, first = _first_max(cur, eid, float(N_EXPERTS))
        hit = eid == first
        ids.append(first)
        ws.append(jnp.sum(jnp.where(hit, scores, 0.0), axis=0, keepdims=True))
        cur = jnp.where(hit, -jnp.inf, cur)
    w = jnp.concatenate(ws, axis=0)
    idx_ref[...] = jnp.concatenate(ids, axis=0).astype(I32)
    w_ref[...] = w / jnp.sum(w, axis=0, keepdims=True) * ROUTED_SCALE


def _route(logits_t, router_bias):
    e, n = logits_t.shape
    tn = ROUTE_TN
    return pl.pallas_call(
        _route_kernel,
        grid=(n // tn,),
        in_specs=[pl.BlockSpec((e, tn), lambda j: (0, j)),
                  pl.BlockSpec((e, 1), lambda j: (0, 0))],
        out_specs=[pl.BlockSpec((TOP_K, tn), lambda j: (0, j)),
                   pl.BlockSpec((TOP_K, tn), lambda j: (0, j))],
        out_shape=[jax.ShapeDtypeStruct((TOP_K, n), I32),
                   jax.ShapeDtypeStruct((TOP_K, n), F32)],
        compiler_params=_params("arbitrary"),
        name="route",
    )(logits_t, router_bias.reshape(e, 1))


def _tile_major(a_t, n_tiles, tm):
    return a_t.reshape(TOP_K, n_tiles, tm).transpose(1, 0, 2).reshape(n_tiles, 1, TOP_K * tm)


def _dispatch_kernel(zs_ref, pos_ref, x_ref, xs_hbm, zbuf, sem):
    step = pl.program_id(0)

    @pl.when(step == 0)
    def _():
        zbuf[...] = jnp.zeros(zbuf.shape, zbuf.dtype)

        def zero_copy(t):
            start = pl.multiple_of(t * MOE_TM, MOE_TM)
            return pltpu.make_async_copy(zbuf, xs_hbm.at[pl.ds(start, MOE_TM)], sem.at[1])

        def start_body(t, carry):
            @pl.when(zs_ref[t] != 0)
            def _():
                zero_copy(t).start()
            return carry

        def wait_body(t, carry):
            @pl.when(zs_ref[t] != 0)
            def _():
                zero_copy(t).wait()
            return carry

        lax.fori_loop(0, zs_ref.shape[0], start_body, 0)
        lax.fori_loop(0, zs_ref.shape[0], wait_body, 0)

    def row_copy(k, r):
        return pltpu.make_async_copy(x_ref.at[r], xs_hbm.at[pos_ref[0, 0, k * DISP_TM + r]], sem.at[0])

    def body(i, carry):
        for k in range(TOP_K):
            for u in range(2):
                row_copy(k, i * 2 + u).start()
        return carry

    lax.fori_loop(0, DISP_TM // 2, body, 0)
    for _ in range(TOP_K):
        pltpu.make_async_copy(x_ref, x_ref, sem.at[0]).wait()


def _dispatch(h2, pos_t, zero_start, n_rows):
    n = h2.shape[0]
    tm = DISP_TM
    n_tiles = n // tm
    grid_spec = pltpu.PrefetchScalarGridSpec(
        num_scalar_prefetch=1,
        grid=(n_tiles,),
        in_specs=[pl.BlockSpec((1, 1, tm * TOP_K), lambda t, zs: (t, 0, 0), memory_space=pltpu.SMEM),
                  pl.BlockSpec((tm, ROW_SUB, LANES), lambda t, zs: (t, 0, 0))],
        out_specs=pl.BlockSpec(memory_space=pl.ANY),
        scratch_shapes=[pltpu.VMEM((MOE_TM, ROW_SUB, LANES), h2.dtype),
                        pltpu.SemaphoreType.DMA((2,))],
    )
    return pl.pallas_call(
        _dispatch_kernel,
        grid_spec=grid_spec,
        out_shape=jax.ShapeDtypeStruct((n_rows, ROW_SUB, LANES), h2.dtype),
        compiler_params=_params("arbitrary"),
        name="dispatch",
    )(zero_start, _tile_major(pos_t, n_tiles, tm), h2)


def _experts_kernel(te_ref, nu_ref, x_ref, w1_ref, w3_ref, w2_ref, y_ref, w1b, w3b, w2b):
    j = pl.program_id(0)
    n_used = nu_ref[0]

    @pl.when(j < n_used)
    def _():
        @pl.when((j == 0) | (te_ref[j] != te_ref[jnp.maximum(j - 1, 0)]))
        def _():
            w1b[...] = w1_ref[0].astype(BF16)
            w3b[...] = w3_ref[0].astype(BF16)
            w2b[...] = w2_ref[0].astype(BF16)

        lo, hi = _unpack_bf16_pairs(_load_row_tiles(x_ref, MOE_TM))
        x = jnp.concatenate([lo.astype(BF16), hi.astype(BF16)], axis=1)
        a = jnp.dot(x, w1b[...], preferred_element_type=F32)
        u = jnp.dot(x, w3b[...], preferred_element_type=F32)
        y = jnp.dot((a * _sigmoid(a) * u).astype(BF16), w2b[...], preferred_element_type=F32)
        _store_row_tiles(y_ref, _pack_bf16_pairs(y))

    @pl.when(j >= n_used)
    def _():
        y_ref[...] = jnp.zeros(y_ref.shape, y_ref.dtype)


def _experts(xs, tile_expert, n_used, w1, w3, w2):
    n_rows = xs.shape[0]
    nt = n_rows // MOE_TM
    d, f = w1.shape[1], w1.shape[2]
    blk = (MOE_TM * ROW_SUB, LANES)
    grid_spec = pltpu.PrefetchScalarGridSpec(
        num_scalar_prefetch=2,
        grid=(nt,),
        in_specs=[pl.BlockSpec(blk, lambda j, te, nu: (jnp.minimum(j, nu[0] - 1), 0)),
                  pl.BlockSpec((1, d, f), lambda j, te, nu: (te[j], 0, 0)),
                  pl.BlockSpec((1, d, f), lambda j, te, nu: (te[j], 0, 0)),
                  pl.BlockSpec((1, f, d), lambda j, te, nu: (te[j], 0, 0))],
        out_specs=pl.BlockSpec(blk, lambda j, te, nu: (j, 0)),
        scratch_shapes=[pltpu.VMEM((d, f), BF16),
                        pltpu.VMEM((d, f), BF16),
                        pltpu.VMEM((f, d), BF16)],
    )
    ys = pl.pallas_call(
        _experts_kernel,
        grid_spec=grid_spec,
        out_shape=jax.ShapeDtypeStruct((n_rows * ROW_SUB, LANES), PACKED),
        compiler_params=_params("arbitrary"),
        name="experts",
    )(tile_expert, n_used, xs.reshape(n_rows * ROW_SUB, LANES), w1, w3, w2)
    return ys.reshape(n_rows, ROW_SUB, LANES)


def _combine_kernel(pos_cur_ref, pos_nxt_ref, ys_hbm, base_ref, g2_ref, w_ref, o_ref, buf, sem):
    bi, i = pl.program_id(0), pl.program_id(1)
    step = bi * pl.num_programs(1) + i
    n_steps = pl.num_programs(0) * pl.num_programs(1)
    slot = lax.rem(step, 2)

    def issue(pos_ref, dst_slot):
        def body(i, carry):
            for k in range(TOP_K):
                for u in range(2):
                    r = i * 2 + u
                    pltpu.make_async_copy(ys_hbm.at[pos_ref[0, 0, k * COMB_TM + r]],
                                          buf.at[dst_slot, k, pl.ds(r * ROW_SUB, ROW_SUB), :],
                                          sem.at[dst_slot]).start()
            return carry
        lax.fori_loop(0, COMB_TM // 2, body, 0)

    @pl.when(step == 0)
    def _():
        issue(pos_cur_ref, 0)

    for parity in (0, 1):
        @pl.when((step + 1 < n_steps) & (slot == parity))
        def _(parity=parity):
            issue(pos_nxt_ref, 1 - parity)

    pltpu.make_async_copy(buf.at[slot], buf.at[slot], sem.at[slot]).wait()

    w = w_ref[...]
    half = o_ref.shape[2] // 2
    for parity in (0, 1):
        @pl.when(slot == parity)
        def _(parity=parity):
            acc_lo = jnp.zeros((COMB_TM, half), F32)
            acc_hi = jnp.zeros((COMB_TM, half), F32)
            for k in range(TOP_K):
                lo, hi = _unpack_bf16_pairs(_load_row_tiles(buf.at[parity, k], COMB_TM))
                acc_lo = acc_lo + w[:, k:k + 1] * lo
                acc_hi = acc_hi + w[:, k:k + 1] * hi
            o_ref[0] = base_ref[0] + g2_ref[0] * jnp.concatenate([acc_lo, acc_hi], axis=1)


def _combine(ys, pos_t, w_sel, base, g2):
    b, s, d = base.shape
    tm = COMB_TM
    nt = s // tm
    n_tiles = b * nt
    pos_t = _tile_major(pos_t, n_tiles, tm)
    return pl.pallas_call(
        _combine_kernel,
        grid=(b, nt),
        in_specs=[pl.BlockSpec((1, 1, tm * TOP_K), lambda bi, i: (bi * nt + i, 0, 0),
                               memory_space=pltpu.SMEM),
                  pl.BlockSpec((1, 1, tm * TOP_K),
                               lambda bi, i: (jnp.minimum(bi * nt + i + 1, n_tiles - 1), 0, 0),
                               memory_space=pltpu.SMEM),
                  pl.BlockSpec(memory_space=pl.ANY),
                  pl.BlockSpec((1, tm, d), lambda bi, i: (bi, i, 0)),
                  pl.BlockSpec((1, 1, d), lambda bi, i: (bi, 0, 0)),
                  pl.BlockSpec((tm, TOP_K), lambda bi, i: (bi * nt + i, 0))],
        out_specs=pl.BlockSpec((1, tm, d), lambda bi, i: (bi, i, 0)),
        out_shape=jax.ShapeDtypeStruct((b, s, d), F32),
        scratch_shapes=[pltpu.VMEM((2, TOP_K, tm * ROW_SUB, LANES), ys.dtype),
                        pltpu.SemaphoreType.DMA((2,))],
        compiler_params=_params("arbitrary", "arbitrary"),
        name="combine",
    )(pos_t, pos_t, ys, base, g2, w_sel)


def _plan_kernel(te_ref, tri_ref, low_ref, pos_ref, cnt_ref, run_ref, start_ref):
    phase, j = pl.program_id(0), pl.program_id(1)
    tn = te_ref.shape[1]
    te = te_ref[...]
    eid = lax.broadcasted_iota(I32, (N_EXPERTS, tn), 0)
    hot = jnp.zeros((N_EXPERTS, tn), F32)
    for k in range(TOP_K):
        hot = hot + jnp.where(te[k:k + 1, :] == eid, 1.0, 0.0)
    tile_count = jnp.sum(hot, axis=1, keepdims=True)

    @pl.when((phase == 0) & (j == 0))
    def _():
        run_ref[...] = jnp.zeros(run_ref.shape, F32)

    @pl.when((phase == 1) & (j == 0))
    def _():
        counts = run_ref[...]
        cnt_ref[...] = counts
        tiles = jnp.floor((counts + (MOE_TM - 1)) * (1.0 / MOE_TM))
        start_ref[...] = jnp.dot(low_ref[...], tiles.astype(BF16), preferred_element_type=F32) * MOE_TM
        run_ref[...] = jnp.zeros(run_ref.shape, F32)

    @pl.when(phase == 1)
    def _():
        before = jnp.dot(hot.astype(BF16), tri_ref[...], preferred_element_type=F32)
        val = before + (run_ref[:, 0:1] + start_ref[:, 0:1])
        rows = [jnp.sum(jnp.where(te[k:k + 1, :] == eid, val, 0.0), axis=0, keepdims=True)
                for k in range(TOP_K)]
        pos_ref[...] = jnp.concatenate(rows, axis=0).astype(I32)

    run_ref[...] = run_ref[...] + tile_count


def _dispatch_plan(top_e_t):
    n = top_e_t.shape[1]
    tn = PLAN_TN
    n_tiles = n * TOP_K // MOE_TM + N_EXPERTS
    tri = jnp.asarray(np.triu(np.ones((tn, tn), np.float32), 1), BF16)
    low = jnp.asarray(np.tril(np.ones((N_EXPERTS, N_EXPERTS), np.float32), -1), BF16)
    pos_t, cnt = pl.pallas_call(
        _plan_kernel,
        grid=(2, n // tn),
        in_specs=[pl.BlockSpec((TOP_K, tn), lambda ph, j: (0, j)),
                  pl.BlockSpec((tn, tn), lambda ph, j: (0, 0)),
                  pl.BlockSpec((N_EXPERTS, N_EXPERTS), lambda ph, j: (0, 0))],
        out_specs=[pl.BlockSpec((TOP_K, tn), lambda ph, j: (0, j * ph)),
                   pl.BlockSpec((N_EXPERTS, LANES), lambda ph, j: (0, 0))],
        out_shape=[jax.ShapeDtypeStruct((TOP_K, n), I32),
                   jax.ShapeDtypeStruct((N_EXPERTS, LANES), F32)],
        scratch_shapes=[pltpu.VMEM((N_EXPERTS, LANES), F32),
                        pltpu.VMEM((N_EXPERTS, LANES), F32)],
        compiler_params=_params("arbitrary", "arbitrary"),
        name="plan",
    )(top_e_t, tri, low)
    counts = cnt[:, 0].astype(I32)
    tile_end = jnp.cumsum((counts + MOE_TM - 1) // MOE_TM)
    tile_expert = jnp.minimum(
        jnp.sum((tile_end[None, :] <= jnp.arange(n_tiles, dtype=I32)[:, None]).astype(I32), axis=1),
        N_EXPERTS - 1)
    n_used = tile_end[-1:].astype(I32)
    t_ids = jnp.arange(n_tiles, dtype=I32)
    is_last = jnp.any((tile_end[None, :] - 1 == t_ids[:, None]) & (counts[None, :] > 0), axis=1)
    zero_tile = (is_last | (t_ids >= n_used[0])).astype(I32)
    return pos_t, zero_tile, tile_expert.astype(I32), n_used, n_tiles * MOE_TM


def _layer(x, c, rel_bias, norm1_w, norm2_w, w_ada, b_ada, w_in, conv_w, w_conv_out, q_norm_w,
           k_norm_w, idx_k_norm_w, idx_k_norm_b, w_attn_out, w_o, w_router, router_bias,
           w1, w3, w2, ws1, ws3, ws2):
    b, s, d = x.shape
    mod = _mod(c, w_ada, b_ada).reshape(b, 6, 1, d)
    sh1, sc1, g1, sh2, sc2, g2 = [mod[:, m] for m in range(6)]

    cols = [w_in[:, _SEG[name][0]:_SEG[name][1]] for name in _ORDER]
    cols.append(jnp.zeros((d, PROJ_W - sum(col.shape[1] for col in cols)), w_in.dtype))
    w_in_p = jnp.concatenate(cols, axis=1).astype(BF16)

    proj = _proj(x, norm1_w, sc1, sh1, w_in_p)
    qT, qiT, kh, vT, kin, wT = _prep(proj, q_norm_w, k_norm_w, idx_k_norm_w, idx_k_norm_b,
                                     rel_bias[REL_BUCKETS - 1])
    attn = _attention(qT, qiT, wT, kh, vT, kin, _bias_strips(rel_bias))
    mixed = _mix(proj, attn, conv_w, w_conv_out.astype(BF16), w_attn_out.astype(BF16))
    base, h2, logits_t = _post(x, mixed, g1, norm2_w, sc2, sh2, g2, w_o.astype(BF16), w_router.T,
                               ws1.astype(BF16), ws3.astype(BF16), ws2.astype(BF16))
    top_e_t, w_sel_t = _route(logits_t, router_bias)
    pos_t, zero_start, tile_expert, n_used, n_rows = _dispatch_plan(top_e_t)
    xs = _dispatch(h2.reshape(b * s, ROW_SUB, LANES), pos_t, zero_start, n_rows)
    ys = _experts(xs, tile_expert, n_used, w1, w3, w2)
    return _combine(ys, pos_t, w_sel_t.T, base, g2)


def kernel(x, c, rel_bias, norm1_w, norm2_w, w_ada, b_ada, w_in, conv_w, w_conv_out, q_norm_w,
           k_norm_w, idx_k_norm_w, idx_k_norm_b, w_attn_out, w_o, w_router, router_bias,
           w1, w3, w2, ws1, ws3, ws2):
    assert x.shape[1] % PROJ_TM == 0 and x.shape[2] == D_MODEL and w_ada.shape[0] == 1
    return _layer(x, c, rel_bias, norm1_w[0], norm2_w[0], w_ada[0], b_ada[0], w_in[0], conv_w[0],
                  w_conv_out[0], q_norm_w[0], k_norm_w[0], idx_k_norm_w[0], idx_k_norm_b[0],
                  w_attn_out[0], w_o[0], w_router[0], router_bias[0], w1[0], w3[0], w2[0],
                  ws1[0], ws3[0], ws2[0])
```

```python
import functools
import math

import numpy as np
import jax
import jax.numpy as jnp
from jax import lax
from jax.experimental import pallas as pl
from jax.experimental.pallas import tpu as pltpu

F32 = jnp.float32
BF16 = jnp.bfloat16
I32 = jnp.int32
PACKED = jnp.int32

D_MODEL = 2048
CONV_WIDTH = D_MODEL // 2
CONV_K = 3
N_HEADS = 16
N_KV_HEADS = 4
HEAD_DIM = 64
ATTN_WIDTH = N_HEADS * HEAD_DIM
KV_WIDTH = N_KV_HEADS * HEAD_DIM
IDX_HEADS = 16
IDX_DIM = 64
IDX_TOPK_MAX = 256
REL_BUCKETS = 32
REL_MAX_DIST = 128
N_EXPERTS = 64
N_GROUPS = 8
TOPK_GROUPS = 4
TOP_K = 8
D_EXPERT = 512
ROUTED_SCALE = 2.5
EPS = 1e-6
NEG = -1e30

REP = N_HEADS // N_KV_HEADS

LANES = 128
VMEM_LIMIT = 56 * 1024 * 1024

TQ = 128
KB = 128
KC = 4 * KB
FAR_KC = 4 * KB
PROJ_TM = 1024
PROJ_TN = 768
PREP_TM = 512
MIX_TM = 512
POST_TM = 512
ROUTE_TN = 512
MOE_TM = 512
COMB_TM = 128
DISP_TM = 256
PLAN_TN = 512

ROW_SUB = D_MODEL // 2 // LANES
QK_DIM = 128
V_ROWS = HEAD_DIM + 16
LOG2E = math.log2(math.e)

_SEG = dict(cb=(0, 1024), cc=(1024, 2048), cu=(2048, 3072), q=(3072, 4096), k=(4096, 4352),
            v=(4352, 4608), qi=(4608, 5632), ki=(5632, 5696), wi=(5696, 5712),
            ga=(5712, 7760), gb=(7760, 9808))
_ORDER = ["ga", "gb", "cb", "cc", "cu", "q", "qi", "k", "v", "ki", "wi"]
PROJ_W = 9984
COL_GA, COL_GB = 0, 1
COL_CB, COL_CC, COL_CU, COL_Q, COL_QI = 4, 5, 6, 7, 8
COL_K, COL_V = 36, 37
COL_KW = 76

INT_MIN = -(2 ** 31)
INT_MAX = 2 ** 31 - 1


def _sortable_key_of(x):
    bits = int(np.float32(x).view(np.int32))
    return bits ^ 0x7FFFFFFF if bits < 0 else bits


NEG_KEY = _sortable_key_of(NEG)


def _sigmoid(x):
    return 1.0 / (1.0 + jnp.exp(-x))


def _pack_bf16_pairs(x):
    half = x.shape[1] // 2
    lo = lax.bitcast_convert_type(x[:, :half].astype(BF16).astype(F32), PACKED)
    hi = lax.bitcast_convert_type(x[:, half:].astype(BF16).astype(F32), PACKED)
    return lax.shift_right_logical(lo, jnp.full_like(lo, 16)) | (hi & jnp.int32(-65536))


def _unpack_bf16_pairs(w):
    lo = lax.bitcast_convert_type(w << 16, F32)
    hi = lax.bitcast_convert_type(w & jnp.int32(-65536), F32)
    return lo, hi


def _store_row_tiles(ref, words):
    m = words.shape[0]
    for sl in range(ROW_SUB):
        ref[pl.ds(sl, m, stride=ROW_SUB), :] = words[:, sl * LANES:(sl + 1) * LANES]


def _load_row_tiles(ref, m):
    return jnp.concatenate([ref[pl.ds(sl, m, stride=ROW_SUB), :] for sl in range(ROW_SUB)], axis=1)


def _params(*sem):
    return pltpu.CompilerParams(dimension_semantics=sem, vmem_limit_bytes=VMEM_LIMIT)


def _mod_kernel(c_ref, w_ref, b_ref, o_ref):
    c = c_ref[...]
    s = (c * _sigmoid(c)).astype(BF16)
    o_ref[...] = jnp.dot(s, w_ref[...].astype(BF16), preferred_element_type=F32) + b_ref[...]


def _mod(c, w_ada, b_ada):
    b = c.shape[0]
    rows = 8
    cp = jnp.pad(c, ((0, rows - b), (0, 0)))
    n = w_ada.shape[1]
    tn = 1024
    out = pl.pallas_call(
        _mod_kernel,
        grid=(n // tn,),
        in_specs=[pl.BlockSpec((rows, D_MODEL), lambda j: (0, 0)),
                  pl.BlockSpec((D_MODEL, tn), lambda j: (0, j)),
                  pl.BlockSpec((1, tn), lambda j: (0, j))],
        out_specs=pl.BlockSpec((rows, tn), lambda j: (0, j)),
        out_shape=jax.ShapeDtypeStruct((rows, n), F32),
        compiler_params=_params("arbitrary"),
        name="mod",
    )(cp, w_ada, b_ada.reshape(1, n))
    return out[:b]


def _proj_kernel(x_ref, nw_ref, sc_ref, sh_ref, w_ref, o_ref, h_ref):
    @pl.when(pl.program_id(2) == 0)
    def _():
        x = x_ref[0]
        ms = jnp.mean(x * x, axis=-1, keepdims=True)
        y = x * lax.rsqrt(ms + EPS) * nw_ref[...]
        h_ref[...] = (y * (1.0 + sc_ref[0]) + sh_ref[0]).astype(BF16)

    o_ref[0] = jnp.dot(h_ref[...], w_ref[...], preferred_element_type=F32).astype(BF16)


def _proj(x, norm_w, sc, sh, w_in_p):
    b, s, d = x.shape
    tm, tn = PROJ_TM, PROJ_TN
    return pl.pallas_call(
        _proj_kernel,
        grid=(b, s // tm, PROJ_W // tn),
        in_specs=[pl.BlockSpec((1, tm, d), lambda bi, i, j: (bi, i, 0)),
                  pl.BlockSpec((1, d), lambda bi, i, j: (0, 0)),
                  pl.BlockSpec((1, 1, d), lambda bi, i, j: (bi, 0, 0)),
                  pl.BlockSpec((1, 1, d), lambda bi, i, j: (bi, 0, 0)),
                  pl.BlockSpec((d, tn), lambda bi, i, j: (0, j))],
        out_specs=pl.BlockSpec((1, tm, tn), lambda bi, i, j: (bi, i, j)),
        out_shape=jax.ShapeDtypeStruct((b, s, PROJ_W), BF16),
        scratch_shapes=[pltpu.VMEM((tm, d), BF16)],
        compiler_params=_params("arbitrary", "arbitrary", "arbitrary"),
        name="proj",
    )(x, norm_w.reshape(1, d), sc, sh, w_in_p)


def _prep_kernel(q_ref, qi_ref, k_ref, v_ref, kw_ref, qnw_ref, knw_ref, inw_ref, inb_ref, qtail_ref,
                 qT_ref, qiT_ref, kh_ref, vT_ref, kin_ref, wT_ref):
    tm = q_ref.shape[1]
    nqb = tm // TQ

    q3 = q_ref[0].astype(F32).T.reshape(N_HEADS, HEAD_DIM, tm)
    ms = jnp.mean(q3 * q3, axis=1, keepdims=True)
    qn = q3 * lax.rsqrt(ms + EPS) * (qnw_ref[...] * (HEAD_DIM ** -0.5 * LOG2E))
    qi3 = qi_ref[0].astype(F32).T.reshape(IDX_HEADS, IDX_DIM, tm)
    for jb in range(nqb):
        for h in range(N_HEADS):
            g, r = divmod(h, REP)
            qT_ref[0, jb, g, :HEAD_DIM, r * TQ:(r + 1) * TQ] = qn[h, :, jb * TQ:(jb + 1) * TQ].astype(BF16)
        for g in range(N_KV_HEADS):
            qT_ref[0, jb, g, HEAD_DIM:, :] = qtail_ref[g]
        for h in range(IDX_HEADS):
            qiT_ref[0, jb, :, h * TQ:(h + 1) * TQ] = qi3[h, :, jb * TQ:(jb + 1) * TQ].astype(BF16)

    k = k_ref[0].astype(F32)
    ones_cols = jnp.where(lax.broadcasted_iota(I32, (tm, QK_DIM - HEAD_DIM), 1) < 2, 1.0, 0.0)
    for g in range(N_KV_HEADS):
        kg = k[:, g * HEAD_DIM:(g + 1) * HEAD_DIM]
        msk = jnp.mean(kg * kg, axis=-1, keepdims=True)
        kn = kg * lax.rsqrt(msk + EPS) * knw_ref[...]
        kh_ref[0, g] = jnp.concatenate([kn, ones_cols], axis=1).astype(BF16)

    v3 = v_ref[0].astype(F32).T.reshape(N_KV_HEADS, HEAD_DIM, tm)
    ones_rows = jnp.where(lax.broadcasted_iota(I32, (V_ROWS - HEAD_DIM, KB), 0) == 0, 1.0, 0.0)
    for g in range(N_KV_HEADS):
        for jb in range(tm // KB):
            vT_ref[0, g, jb] = jnp.concatenate(
                [v3[g, :, jb * KB:(jb + 1) * KB], ones_rows], axis=0).astype(BF16)

    kw = kw_ref[0].astype(F32)
    ki = kw[:, :IDX_DIM]
    mu = jnp.mean(ki, axis=-1, keepdims=True)
    var = jnp.mean(jnp.square(ki - mu), axis=-1, keepdims=True)
    kin_ref[0] = ((ki - mu) * lax.rsqrt(var + EPS) * inw_ref[...] + inb_ref[...]).astype(BF16)
    wiT = kw.T[IDX_DIM:IDX_DIM + IDX_HEADS] * (IDX_HEADS ** -0.5 * IDX_DIM ** -0.5)
    for jb in range(nqb):
        wT_ref[0, jb] = wiT[:, jb * TQ:(jb + 1) * TQ]


def _prep(proj, q_norm_w, k_norm_w, idx_k_norm_w, idx_k_norm_b, far_bias):
    b, s, _ = proj.shape
    tm = PREP_TM
    nqb = tm // TQ
    nq = s // TQ
    fb2 = (far_bias * LOG2E).reshape(N_KV_HEADS, REP)
    hi = fb2.astype(BF16)
    lo = (fb2 - hi.astype(F32)).astype(BF16)
    tail = jnp.stack([hi, lo], axis=1)
    tail = jnp.broadcast_to(tail[..., None], (N_KV_HEADS, 2, REP, TQ)).reshape(N_KV_HEADS, 2, REP * TQ)
    qtail = jnp.pad(tail, ((0, 0), (0, QK_DIM - HEAD_DIM - 2), (0, 0)))
    return pl.pallas_call(
        _prep_kernel,
        grid=(b, s // tm),
        in_specs=[pl.BlockSpec((1, tm, ATTN_WIDTH), lambda bi, i: (bi, i, COL_Q)),
                  pl.BlockSpec((1, tm, IDX_HEADS * IDX_DIM), lambda bi, i: (bi, i, COL_QI)),
                  pl.BlockSpec((1, tm, KV_WIDTH), lambda bi, i: (bi, i, COL_K)),
                  pl.BlockSpec((1, tm, KV_WIDTH), lambda bi, i: (bi, i, COL_V)),
                  pl.BlockSpec((1, tm, LANES), lambda bi, i: (bi, i, COL_KW)),
                  pl.BlockSpec((1, HEAD_DIM, 1), lambda bi, i: (0, 0, 0)),
                  pl.BlockSpec((1, HEAD_DIM), lambda bi, i: (0, 0)),
                  pl.BlockSpec((1, IDX_DIM), lambda bi, i: (0, 0)),
                  pl.BlockSpec((1, IDX_DIM), lambda bi, i: (0, 0)),
                  pl.BlockSpec((N_KV_HEADS, QK_DIM - HEAD_DIM, REP * TQ), lambda bi, i: (0, 0, 0))],
        out_specs=[pl.BlockSpec((1, nqb, N_KV_HEADS, QK_DIM, REP * TQ), lambda bi, i: (bi, i, 0, 0, 0)),
                   pl.BlockSpec((1, nqb, IDX_DIM, IDX_HEADS * TQ), lambda bi, i: (bi, i, 0, 0)),
                   pl.BlockSpec((1, N_KV_HEADS, tm, QK_DIM), lambda bi, i: (bi, 0, i, 0)),
                   pl.BlockSpec((1, N_KV_HEADS, tm // KB, V_ROWS, KB), lambda bi, i: (bi, 0, i, 0, 0)),
                   pl.BlockSpec((1, tm, IDX_DIM), lambda bi, i: (bi, i, 0)),
                   pl.BlockSpec((1, nqb, IDX_HEADS, TQ), lambda bi, i: (bi, i, 0, 0))],
        out_shape=[jax.ShapeDtypeStruct((b, nq, N_KV_HEADS, QK_DIM, REP * TQ), BF16),
                   jax.ShapeDtypeStruct((b, nq, IDX_DIM, IDX_HEADS * TQ), BF16),
                   jax.ShapeDtypeStruct((b, N_KV_HEADS, s, QK_DIM), BF16),
                   jax.ShapeDtypeStruct((b, N_KV_HEADS, s // KB, V_ROWS, KB), BF16),
                   jax.ShapeDtypeStruct((b, s, IDX_DIM), BF16),
                   jax.ShapeDtypeStruct((b, nq, IDX_HEADS, TQ), F32)],
        compiler_params=_params("arbitrary", "arbitrary"),
        name="prep",
    )(proj, proj, proj, proj, proj,
      q_norm_w.reshape(1, HEAD_DIM, 1), k_norm_w.reshape(1, HEAD_DIM),
      idx_k_norm_w.reshape(1, IDX_DIM), idx_k_norm_b.reshape(1, IDX_DIM), qtail)


def _t5_bucket_np(n):
    n = np.maximum(n, 0)
    max_exact = REL_BUCKETS // 2
    nf = np.maximum(n, 1).astype(np.float64)
    large = max_exact + np.floor(np.log(nf / max_exact) / math.log(REL_MAX_DIST / max_exact)
                                 * (REL_BUCKETS - max_exact)).astype(np.int64)
    large = np.minimum(large, REL_BUCKETS - 1)
    return np.where(n < max_exact, n, large).astype(np.int32)


def _bias_kernel(rb_ref, bucket_ref, o_ref):
    h = pl.program_id(0)
    bucket = bucket_ref[...]
    acc = jnp.zeros(bucket.shape, F32)
    for bkt in range(REL_BUCKETS):
        acc = jnp.where(bucket == bkt, rb_ref[bkt, h], acc)
    o_ref[0] = (acc - rb_ref[REL_BUCKETS - 1, h]) * LOG2E


def _bias_strips(rel_bias):
    kk = np.arange(3 * TQ)[:, None]
    qq = np.arange(TQ)[None, :]
    bucket = jnp.asarray(_t5_bucket_np(qq + TQ - kk))
    return pl.pallas_call(
        _bias_kernel,
        grid=(N_HEADS,),
        in_specs=[pl.BlockSpec(memory_space=pltpu.SMEM),
                  pl.BlockSpec((3 * TQ, TQ), lambda h: (0, 0))],
        out_specs=pl.BlockSpec((1, 3 * TQ, TQ), lambda h: (h, 0, 0)),
        out_shape=jax.ShapeDtypeStruct((N_HEADS, 3 * TQ, TQ), F32),
        compiler_params=_params("arbitrary"),
        name="bias",
    )(rel_bias, bucket)


def _attn_kernel(qT_ref, qiT_ref, wT_ref, kh_ref, vT_ref, kin_ref, biasT_ref, o_ref,
                 keys_ref, am_ref, amf_ref, p_ref, m_ref, acc_ref, *, n_sel):
    i = pl.program_id(1)
    seq = kin_ref.shape[1]
    t0 = i * TQ
    n_chunks = lax.shift_right_logical(i + 4, 2)
    q_pos = t0 + lax.broadcasted_iota(I32, (KB, TQ), 1)
    k_off = lax.broadcasted_iota(I32, (KB, TQ), 0)

    qiT = qiT_ref[0, 0]
    wT = wT_ref[0, 0]

    def score_chunk(c, carry):
        k0 = pl.multiple_of(c * KC, KC)
        d = jnp.dot(kin_ref[0, pl.ds(k0, KC), :], qiT, preferred_element_type=F32)
        acc = jnp.zeros((KC, TQ), F32)
        for h in range(IDX_HEADS):
            acc = acc + wT[h:h + 1, :] * jnp.maximum(d[:, h * TQ:(h + 1) * TQ], 0.0)
        for j in range(KC // KB):
            blk = c * (KC // KB) + j
            sc = jnp.where(blk * KB + k_off <= q_pos, acc[j * KB:(j + 1) * KB], NEG)
            bits = lax.bitcast_convert_type(sc, I32)
            keys_ref[blk] = jnp.where(bits < 0, bits ^ 0x7FFFFFFF, bits)
        return carry

    lax.fori_loop(0, n_chunks, score_chunk, 0)

    n_virtual = (seq - n_chunks * KC).astype(F32)

    def count(pred):
        def body(c, acc):
            for j in range(KC // KB):
                blk = c * (KC // KB) + j
                hit = jnp.where(pred(keys_ref[blk], blk), 1.0, 0.0)
                acc = acc + jnp.sum(hit.reshape(KB // 8, 8, TQ), axis=0)
            return acc
        acc = lax.fori_loop(0, n_chunks, body, jnp.zeros((8, TQ), F32))
        return jnp.sum(acc, axis=0, keepdims=True)

    def bit_body(it, thr):
        cand = thr + lax.shift_left(jnp.int32(1), 31 - it)
        cnt = count(lambda kb, blk: kb >= cand) + jnp.where(NEG_KEY >= cand, n_virtual, 0.0)
        return jnp.where(cnt >= n_sel, cand, thr)

    thr = lax.fori_loop(0, 32, bit_body, jnp.full((1, TQ), INT_MIN, I32))

    cnt_gt = count(lambda kb, blk: kb > thr) + jnp.where(NEG_KEY > thr, n_virtual, 0.0)
    cnt_eq = count(lambda kb, blk: kb == thr) + jnp.where(NEG_KEY == thr, n_virtual, 0.0)
    need = n_sel - cnt_gt
    p_ref[...] = jnp.full((8, TQ), INT_MAX, I32)
    has_tie = jnp.max(jnp.where(cnt_eq > need, 1.0, 0.0)) > 0.0

    @pl.when(has_tie)
    def _():
        idx_bits = int(seq).bit_length()

        def p_body(it, p):
            cand = p | lax.shift_left(jnp.int32(1), idx_bits - 1 - it)
            below = count(lambda kb, blk: (kb == thr) & (blk * KB + k_off < cand))
            return jnp.where(below < need, cand, p)

        p = lax.fori_loop(0, idx_bits, p_body, jnp.zeros((1, TQ), I32))
        p_ref[...] = jnp.broadcast_to(p, (8, TQ))

    p_last = p_ref[0:1, :]

    bw = jnp.maximum(i - 1, 0)
    ws = pl.multiple_of(bw * KB, KB)

    def mask_chunk(c, carry):
        for j in range(KC // KB):
            blk = c * (KC // KB) + j
            kb = keys_ref[blk]
            k_pos = blk * KB + k_off
            sel = (kb > thr) | ((kb == thr) & (k_pos <= p_last))
            v = jnp.where(sel & (k_pos <= q_pos), 0.0, NEG)
            am_ref[blk] = v
            amf_ref[blk] = jnp.where(k_pos < ws, v, NEG)
        return carry

    lax.fori_loop(0, n_chunks, mask_chunk, 0)

    off = pl.multiple_of(TQ - (t0 - ws), TQ)
    am_near = jnp.concatenate([am_ref[bw], am_ref[bw + 1]], axis=0)
    for g in range(N_KV_HEADS):
        s = jnp.dot(kh_ref[0, g, pl.ds(ws, 2 * KB), :], qT_ref[0, 0, g], preferred_element_type=F32)
        s = jnp.concatenate(
            [s[:, r * TQ:(r + 1) * TQ] + (biasT_ref[REP * g + r, pl.ds(off, 2 * KB), :] + am_near)
             for r in range(REP)], axis=1)
        m = jnp.max(s, axis=0, keepdims=True)
        pb = jnp.exp2(s - m).astype(BF16)
        m_ref[g] = m
        acc_ref[g] = (jnp.dot(vT_ref[0, g, bw], pb[:KB], preferred_element_type=F32)
                      + jnp.dot(vT_ref[0, g, bw + 1], pb[KB:], preferred_element_type=F32))

    fb = FAR_KC // KB

    def far_body(f, carry):
        k0 = pl.multiple_of(f * FAR_KC, FAR_KC)
        amf = jnp.concatenate([amf_ref[f * fb + j] for j in range(fb)], axis=0)
        for g in range(N_KV_HEADS):
            s = jnp.dot(kh_ref[0, g, pl.ds(k0, FAR_KC), :], qT_ref[0, 0, g], preferred_element_type=F32)
            s = jnp.concatenate([s[:, r * TQ:(r + 1) * TQ] + amf for r in range(REP)], axis=1)
            m_old = m_ref[g]
            m_new = jnp.maximum(m_old, jnp.max(s, axis=0, keepdims=True))
            pb = jnp.exp2(s - m_new).astype(BF16)
            vc = jnp.concatenate([vT_ref[0, g, f * fb + j] for j in range(fb)], axis=1)
            m_ref[g] = m_new
            acc_ref[g] = (jnp.exp2(m_old - m_new) * acc_ref[g]
                          + jnp.dot(vc, pb, preferred_element_type=F32))
        return carry

    lax.fori_loop(0, (bw + fb - 1) // fb, far_body, 0)

    outs = []
    for g in range(N_KV_HEADS):
        og = acc_ref[g, :HEAD_DIM] / acc_ref[g, HEAD_DIM:HEAD_DIM + 1]
        outs.extend(og[:, r * TQ:(r + 1) * TQ] for r in range(REP))
    o_ref[0] = jnp.concatenate(outs, axis=0).T.astype(BF16)


def _attention(qT, qiT, wT, kh, vT, kin, bias_strips):
    b, nq = qT.shape[0], qT.shape[1]
    s = kin.shape[1]
    n_sel = min(IDX_TOPK_MAX, s // 4)
    nb = s // KB
    return pl.pallas_call(
        functools.partial(_attn_kernel, n_sel=n_sel),
        grid=(b, nq),
        in_specs=[pl.BlockSpec((1, 1, N_KV_HEADS, QK_DIM, REP * TQ), lambda bi, i: (bi, i, 0, 0, 0)),
                  pl.BlockSpec((1, 1, IDX_DIM, IDX_HEADS * TQ), lambda bi, i: (bi, i, 0, 0)),
                  pl.BlockSpec((1, 1, IDX_HEADS, TQ), lambda bi, i: (bi, i, 0, 0)),
                  pl.BlockSpec((1, N_KV_HEADS, s, QK_DIM), lambda bi, i: (bi, 0, 0, 0)),
                  pl.BlockSpec((1, N_KV_HEADS, nb, V_ROWS, KB), lambda bi, i: (bi, 0, 0, 0, 0)),
                  pl.BlockSpec((1, s, IDX_DIM), lambda bi, i: (bi, 0, 0)),
                  pl.BlockSpec((N_HEADS, 3 * TQ, TQ), lambda bi, i: (0, 0, 0))],
        out_specs=pl.BlockSpec((1, TQ, ATTN_WIDTH), lambda bi, i: (bi, i, 0)),
        out_shape=jax.ShapeDtypeStruct((b, s, ATTN_WIDTH), BF16),
        scratch_shapes=[pltpu.VMEM((nb, KB, TQ), I32),
                        pltpu.VMEM((nb, KB, TQ), F32),
                        pltpu.VMEM((nb, KB, TQ), F32),
                        pltpu.VMEM((8, TQ), I32),
                        pltpu.VMEM((N_KV_HEADS, 1, REP * TQ), F32),
                        pltpu.VMEM((N_KV_HEADS, V_ROWS, REP * TQ), F32)],
        compiler_params=_params("arbitrary", "arbitrary"),
        name="attn",
    )(qT, qiT, wT, kh, vT, kin, bias_strips)


HALO = 16


def _mix_kernel(cb_ref, cc_ref, cu_ref, ccp_ref, cup_ref, at_ref, ga_ref, gb_ref,
                cw_ref, wco_ref, wao_ref, o_ref):
    tm = cb_ref.shape[1]
    v = cc_ref[0].astype(F32) * cu_ref[0].astype(F32)
    first = pl.program_id(1) == 0
    hv = ccp_ref[0].astype(F32) * cup_ref[0].astype(F32)
    hv = jnp.where(first, 0.0, hv)
    row = lax.broadcasted_iota(I32, v.shape, 0)
    v1 = jnp.where(row == 0, hv[HALO - 1:HALO], pltpu.roll(v, 1, 0))
    v2 = pltpu.roll(v, 2, 0)
    v2 = jnp.where(row == 0, hv[HALO - 2:HALO - 1], jnp.where(row == 1, hv[HALO - 1:HALO], v2))
    y = cw_ref[0:1] * v2 + cw_ref[1:2] * v1 + cw_ref[2:3] * v
    yc = (cb_ref[0].astype(F32) * y).astype(BF16)
    y_conv = jnp.dot(yc, wco_ref[...], preferred_element_type=F32)
    y_attn = jnp.dot(at_ref[0], wao_ref[...], preferred_element_type=F32)
    mixed = _sigmoid(ga_ref[0].astype(F32)) * y_conv + _sigmoid(gb_ref[0].astype(F32)) * y_attn
    o_ref[0] = mixed.astype(BF16)


def _mix(proj, attn, conv_w, w_conv_out_b, w_attn_out_b):
    b, s, _ = proj.shape
    tm = MIX_TM
    hb = tm // HALO
    prev = lambda col: (lambda bi, i: (bi, jnp.maximum(i * hb - 1, 0), col))
    return pl.pallas_call(
        _mix_kernel,
        grid=(b, s // tm),
        in_specs=[pl.BlockSpec((1, tm, CONV_WIDTH), lambda bi, i: (bi, i, COL_CB)),
                  pl.BlockSpec((1, tm, CONV_WIDTH), lambda bi, i: (bi, i, COL_CC)),
                  pl.BlockSpec((1, tm, CONV_WIDTH), lambda bi, i: (bi, i, COL_CU)),
                  pl.BlockSpec((1, HALO, CONV_WIDTH), prev(COL_CC)),
                  pl.BlockSpec((1, HALO, CONV_WIDTH), prev(COL_CU)),
                  pl.BlockSpec((1, tm, ATTN_WIDTH), lambda bi, i: (bi, i, 0)),
                  pl.BlockSpec((1, tm, D_MODEL), lambda bi, i: (bi, i, COL_GA)),
                  pl.BlockSpec((1, tm, D_MODEL), lambda bi, i: (bi, i, COL_GB)),
                  pl.BlockSpec((8, CONV_WIDTH), lambda bi, i: (0, 0)),
                  pl.BlockSpec((CONV_WIDTH, D_MODEL), lambda bi, i: (0, 0)),
                  pl.BlockSpec((ATTN_WIDTH, D_MODEL), lambda bi, i: (0, 0))],
        out_specs=pl.BlockSpec((1, tm, D_MODEL), lambda bi, i: (bi, i, 0)),
        out_shape=jax.ShapeDtypeStruct((b, s, D_MODEL), BF16),
        compiler_params=_params("arbitrary", "arbitrary"),
        name="mix",
    )(proj, proj, proj, proj, proj, attn, proj, proj,
      jnp.pad(conv_w, ((0, 8 - CONV_K), (0, 0))), w_conv_out_b, w_attn_out_b)


def _post_kernel(x_ref, mx_ref, g1_ref, nw_ref, sc_ref, sh_ref, g2_ref, wo_ref, wrT_ref,
                 ws1_ref, ws3_ref, ws2_ref, base_ref, h2_ref, lg_ref):
    x1 = x_ref[0] + g1_ref[0] * jnp.dot(mx_ref[0], wo_ref[...], preferred_element_type=F32)
    ms = jnp.mean(x1 * x1, axis=-1, keepdims=True)
    h2 = x1 * lax.rsqrt(ms + EPS) * nw_ref[...] * (1.0 + sc_ref[0]) + sh_ref[0]
    _store_row_tiles(h2_ref, _pack_bf16_pairs(h2))
    lg_ref[...] = lax.dot_general(wrT_ref[...], h2, (((1,), (1,)), ((), ())),
                                  precision=lax.Precision.HIGHEST, preferred_element_type=F32)
    hb = h2.astype(BF16)
    a = jnp.dot(hb, ws1_ref[...], preferred_element_type=F32)
    u = jnp.dot(hb, ws3_ref[...], preferred_element_type=F32)
    shared = jnp.dot((a * _sigmoid(a) * u).astype(BF16), ws2_ref[...], preferred_element_type=F32)
    base_ref[0] = x1 + g2_ref[0] * shared


def _post(x, mixed, g1, norm_w, sc, sh, g2, w_o_b, w_router_t, ws1_b, ws3_b, ws2_b):
    b, s, d = x.shape
    tm = POST_TM
    nt = s // tm
    vec = pl.BlockSpec((1, 1, d), lambda bi, i: (bi, 0, 0))
    const = lambda shape: pl.BlockSpec(shape, lambda bi, i: (0,) * len(shape))
    return pl.pallas_call(
        _post_kernel,
        grid=(b, nt),
        in_specs=[pl.BlockSpec((1, tm, d), lambda bi, i: (bi, i, 0)),
                  pl.BlockSpec((1, tm, d), lambda bi, i: (bi, i, 0)),
                  vec, const((1, d)), vec, vec, vec,
                  const((d, d)), const((N_EXPERTS, d)),
                  const((d, D_EXPERT)), const((d, D_EXPERT)), const((D_EXPERT, d))],
        out_specs=[pl.BlockSpec((1, tm, d), lambda bi, i: (bi, i, 0)),
                   pl.BlockSpec((tm * ROW_SUB, LANES), lambda bi, i: (bi * nt + i, 0)),
                   pl.BlockSpec((N_EXPERTS, tm), lambda bi, i: (0, bi * nt + i))],
        out_shape=[jax.ShapeDtypeStruct((b, s, d), F32),
                   jax.ShapeDtypeStruct((b * s * ROW_SUB, LANES), PACKED),
                   jax.ShapeDtypeStruct((N_EXPERTS, b * s), F32)],
        compiler_params=_params("arbitrary", "arbitrary"),
        name="post",
    )(x, mixed, g1, norm_w.reshape(1, d), sc, sh, g2, w_o_b, w_router_t, ws1_b, ws3_b, ws2_b)


def _first_max(cur, ids, sentinel):
    m = jnp.max(cur, axis=0, keepdims=True)
    first = jnp.min(jnp.where(cur == m, ids, sentinel), axis=0, keepdims=True)
    return m, first


def _route_kernel(lg_ref, rb_ref, idx_ref, w_ref):
    tn = lg_ref.shape[1]
    gsz = N_EXPERTS // N_GROUPS
    scores = _sigmoid(lg_ref[...])
    sel = scores + rb_ref[...]
    sub = lax.broadcasted_iota(I32, (gsz, tn), 0).astype(F32)

    gs = []
    for g in range(N_GROUPS):
        v = sel[g * gsz:(g + 1) * gsz]
        m1, first = _first_max(v, sub, float(gsz))
        m2 = jnp.max(jnp.where(sub == first, -jnp.inf, v), axis=0, keepdims=True)
        gs.append(m1 + m2)
    cur = jnp.concatenate(gs, axis=0)
    gid = lax.broadcasted_iota(I32, (N_GROUPS, tn), 0).astype(F32)
    keep = jnp.zeros((N_GROUPS, tn), F32)
    for _ in range(TOPK_GROUPS):
        _, first = _first_max(cur, gid, float(N_GROUPS))
        hit = gid == first
        keep = jnp.where(hit, 1.0, keep)
        cur = jnp.where(hit, -jnp.inf, cur)

    cur = jnp.concatenate(
        [jnp.where(keep[g:g + 1] > 0.0, sel[g * gsz:(g + 1) * gsz], NEG) for g in range(N_GROUPS)],
        axis=0)
    eid = lax.broadcasted_iota(I32, (N_EXPERTS, tn), 0).astype(F32)
    ids, ws = [], []
    for _ in range(TOP_K):
        _, first = _first_max(cur, eid, float(N_EXPERTS))
        hit = eid == first
        ids.append(first)
        ws.append(jnp.sum(jnp.where(hit, scores, 0.0), axis=0, keepdims=True))
        cur = jnp.where(hit, -jnp.inf, cur)
    w = jnp.concatenate(ws, axis=0)
    idx_ref[...] = jnp.concatenate(ids, axis=0).astype(I32)
    w_ref[...] = w / jnp.sum(w, axis=0, keepdims=True) * ROUTED_SCALE


def _route(logits_t, router_bias):
    e, n = logits_t.shape
    tn = ROUTE_TN
    return pl.pallas_call(
        _route_kernel,
        grid=(n // tn,),
        in_specs=[pl.BlockSpec((e, tn), lambda j: (0, j)),
                  pl.BlockSpec((e, 1), lambda j: (0, 0))],
        out_specs=[pl.BlockSpec((TOP_K, tn), lambda j: (0, j)),
                   pl.BlockSpec((TOP_K, tn), lambda j: (0, j))],
        out_shape=[jax.ShapeDtypeStruct((TOP_K, n), I32),
                   jax.ShapeDtypeStruct((TOP_K, n), F32)],
        compiler_params=_params("arbitrary"),
        name="route",
    )(logits_t, router_bias.reshape(e, 1))


def _tile_major(a_t, n_tiles, tm):
    return a_t.reshape(TOP_K, n_tiles, tm).transpose(1, 0, 2).reshape(n_tiles, 1, TOP_K * tm)


def _dispatch_kernel(zs_ref, pos_ref, x_ref, xs_hbm, zbuf, sem):
    step = pl.program_id(0)

    @pl.when(step == 0)
    def _():
        zbuf[...] = jnp.zeros(zbuf.shape, zbuf.dtype)

        def zero_copy(t):
            start = pl.multiple_of(t * MOE_TM, MOE_TM)
            return pltpu.make_async_copy(zbuf, xs_hbm.at[pl.ds(start, MOE_TM)], sem.at[1])

        def start_body(t, carry):
            @pl.when(zs_ref[t] != 0)
            def _():
                zero_copy(t).start()
            return carry

        def wait_body(t, carry):
            @pl.when(zs_ref[t] != 0)
            def _():
                zero_copy(t).wait()
            return carry

        lax.fori_loop(0, zs_ref.shape[0], start_body, 0)
        lax.fori_loop(0, zs_ref.shape[0], wait_body, 0)

    def row_copy(k, r):
        return pltpu.make_async_copy(x_ref.at[r], xs_hbm.at[pos_ref[0, 0, k * DISP_TM + r]], sem.at[0])

    def body(i, carry):
        for k in range(TOP_K):
            for u in range(2):
                row_copy(k, i * 2 + u).start(priority=u)
        return carry

    lax.fori_loop(0, DISP_TM // 2, body, 0)
    for _ in range(TOP_K):
        pltpu.make_async_copy(x_ref, x_ref, sem.at[0]).wait()


def _dispatch(h2, pos_t, zero_start, n_rows):
    n = h2.shape[0]
    tm = DISP_TM
    n_tiles = n // tm
    grid_spec = pltpu.PrefetchScalarGridSpec(
        num_scalar_prefetch=1,
        grid=(n_tiles,),
        in_specs=[pl.BlockSpec((1, 1, tm * TOP_K), lambda t, zs: (t, 0, 0), memory_space=pltpu.SMEM),
                  pl.BlockSpec((tm, ROW_SUB, LANES), lambda t, zs: (t, 0, 0))],
        out_specs=pl.BlockSpec(memory_space=pl.ANY),
        scratch_shapes=[pltpu.VMEM((MOE_TM, ROW_SUB, LANES), h2.dtype),
                        pltpu.SemaphoreType.DMA((2,))],
    )
    return pl.pallas_call(
        _dispatch_kernel,
        grid_spec=grid_spec,
        out_shape=jax.ShapeDtypeStruct((n_rows, ROW_SUB, LANES), h2.dtype),
        compiler_params=_params("arbitrary"),
        name="dispatch",
    )(zero_start, _tile_major(pos_t, n_tiles, tm), h2)


def _experts_kernel(te_ref, nu_ref, x_ref, w1_ref, w3_ref, w2_ref, y_ref, w1b, w3b, w2b):
    j = pl.program_id(0)
    n_used = nu_ref[0]

    @pl.when(j < n_used)
    def _():
        @pl.when((j == 0) | (te_ref[j] != te_ref[jnp.maximum(j - 1, 0)]))
        def _():
            w1b[...] = w1_ref[0].astype(BF16)
            w3b[...] = w3_ref[0].astype(BF16)
            w2b[...] = w2_ref[0].astype(BF16)

        lo, hi = _unpack_bf16_pairs(_load_row_tiles(x_ref, MOE_TM))
        x = jnp.concatenate([lo.astype(BF16), hi.astype(BF16)], axis=1)
        a = jnp.dot(x, w1b[...], preferred_element_type=F32)
        u = jnp.dot(x, w3b[...], preferred_element_type=F32)
        y = jnp.dot((a * _sigmoid(a) * u).astype(BF16), w2b[...], preferred_element_type=F32)
        _store_row_tiles(y_ref, _pack_bf16_pairs(y))

    @pl.when(j >= n_used)
    def _():
        y_ref[...] = jnp.zeros(y_ref.shape, y_ref.dtype)


def _experts(xs, tile_expert, n_used, w1, w3, w2):
    n_rows = xs.shape[0]
    nt = n_rows // MOE_TM
    d, f = w1.shape[1], w1.shape[2]
    blk = (MOE_TM * ROW_SUB, LANES)
    grid_spec = pltpu.PrefetchScalarGridSpec(
        num_scalar_prefetch=2,
        grid=(nt,),
        in_specs=[pl.BlockSpec(blk, lambda j, te, nu: (jnp.minimum(j, nu[0] - 1), 0)),
                  pl.BlockSpec((1, d, f), lambda j, te, nu: (te[j], 0, 0)),
                  pl.BlockSpec((1, d, f), lambda j, te, nu: (te[j], 0, 0)),
                  pl.BlockSpec((1, f, d), lambda j, te, nu: (te[j], 0, 0))],
        out_specs=pl.BlockSpec(blk, lambda j, te, nu: (j, 0)),
        scratch_shapes=[pltpu.VMEM((d, f), BF16),
                        pltpu.VMEM((d, f), BF16),
                        pltpu.VMEM((f, d), BF16)],
    )
    ys = pl.pallas_call(
        _experts_kernel,
        grid_spec=grid_spec,
        out_shape=jax.ShapeDtypeStruct((n_rows * ROW_SUB, LANES), PACKED),
        compiler_params=_params("arbitrary"),
        name="experts",
    )(tile_expert, n_used, xs.reshape(n_rows * ROW_SUB, LANES), w1, w3, w2)
    return ys.reshape(n_rows, ROW_SUB, LANES)


def _combine_kernel(pos_cur_ref, pos_nxt_ref, ys_hbm, base_ref, g2_ref, w_ref, o_ref, buf, sem):
    bi, i = pl.program_id(0), pl.program_id(1)
    step = bi * pl.num_programs(1) + i
    n_steps = pl.num_programs(0) * pl.num_programs(1)
    slot = lax.rem(step, 2)

    def issue(pos_ref, dst_slot):
        def body(i, carry):
            for k in range(TOP_K):
                for u in range(2):
                    r = i * 2 + u
                    pltpu.make_async_copy(ys_hbm.at[pos_ref[0, 0, k * COMB_TM + r]],
                                          buf.at[dst_slot, k, pl.ds(r * ROW_SUB, ROW_SUB), :],
                                          sem.at[dst_slot]).start(priority=u)
            return carry
        lax.fori_loop(0, COMB_TM // 2, body, 0)

    @pl.when(step == 0)
    def _():
        issue(pos_cur_ref, 0)

    for parity in (0, 1):
        @pl.when((step + 1 < n_steps) & (slot == parity))
        def _(parity=parity):
            issue(pos_nxt_ref, 1 - parity)

    pltpu.make_async_copy(buf.at[slot], buf.at[slot], sem.at[slot]).wait()

    w = w_ref[...]
    half = o_ref.shape[2] // 2
    for parity in (0, 1):
        @pl.when(slot == parity)
        def _(parity=parity):
            acc_lo = jnp.zeros((COMB_TM, half), F32)
            acc_hi = jnp.zeros((COMB_TM, half), F32)
            for k in range(TOP_K):
                lo, hi = _unpack_bf16_pairs(_load_row_tiles(buf.at[parity, k], COMB_TM))
                acc_lo = acc_lo + w[:, k:k + 1] * lo
                acc_hi = acc_hi + w[:, k:k + 1] * hi
            o_ref[0] = base_ref[0] + g2_ref[0] * jnp.concatenate([acc_lo, acc_hi], axis=1)


def _combine(ys, pos_t, w_sel, base, g2):
    b, s, d = base.shape
    tm = COMB_TM
    nt = s // tm
    n_tiles = b * nt
    pos_t = _tile_major(pos_t, n_tiles, tm)
    return pl.pallas_call(
        _combine_kernel,
        grid=(b, nt),
        in_specs=[pl.BlockSpec((1, 1, tm * TOP_K), lambda bi, i: (bi * nt + i, 0, 0),
                               memory_space=pltpu.SMEM),
                  pl.BlockSpec((1, 1, tm * TOP_K),
                               lambda bi, i: (jnp.minimum(bi * nt + i + 1, n_tiles - 1), 0, 0),
                               memory_space=pltpu.SMEM),
                  pl.BlockSpec(memory_space=pl.ANY),
                  pl.BlockSpec((1, tm, d), lambda bi, i: (bi, i, 0)),
                  pl.BlockSpec((1, 1, d), lambda bi, i: (bi, 0, 0)),
                  pl.BlockSpec((tm, TOP_K), lambda bi, i: (bi * nt + i, 0))],
        out_specs=pl.BlockSpec((1, tm, d), lambda bi, i: (bi, i, 0)),
        out_shape=jax.ShapeDtypeStruct((b, s, d), F32),
        scratch_shapes=[pltpu.VMEM((2, TOP_K, tm * ROW_SUB, LANES), ys.dtype),
                        pltpu.SemaphoreType.DMA((2,))],
        compiler_params=_params("arbitrary", "arbitrary"),
        name="combine",
    )(pos_t, pos_t, ys, base, g2, w_sel)


def _plan_kernel(te_ref, tri_ref, low_ref, pos_ref, cnt_ref, run_ref, start_ref):
    phase, j = pl.program_id(0), pl.program_id(1)
    tn = te_ref.shape[1]
    te = te_ref[...]
    eid = lax.broadcasted_iota(I32, (N_EXPERTS, tn), 0)
    hot = jnp.zeros((N_EXPERTS, tn), F32)
    for k in range(TOP_K):
        hot = hot + jnp.where(te[k:k + 1, :] == eid, 1.0, 0.0)
    tile_count = jnp.sum(hot, axis=1, keepdims=True)

    @pl.when((phase == 0) & (j == 0))
    def _():
        run_ref[...] = jnp.zeros(run_ref.shape, F32)

    @pl.when((phase == 1) & (j == 0))
    def _():
        counts = run_ref[...]
        cnt_ref[...] = counts
        tiles = jnp.floor((counts + (MOE_TM - 1)) * (1.0 / MOE_TM))
        start_ref[...] = jnp.dot(low_ref[...], tiles.astype(BF16), preferred_element_type=F32) * MOE_TM
        run_ref[...] = jnp.zeros(run_ref.shape, F32)

    @pl.when(phase == 1)
    def _():
        before = jnp.dot(hot.astype(BF16), tri_ref[...], preferred_element_type=F32)
        val = before + (run_ref[:, 0:1] + start_ref[:, 0:1])
        rows = [jnp.sum(jnp.where(te[k:k + 1, :] == eid, val, 0.0), axis=0, keepdims=True)
                for k in range(TOP_K)]
        pos_ref[...] = jnp.concatenate(rows, axis=0).astype(I32)

    run_ref[...] = run_ref[...] + tile_count


def _dispatch_plan(top_e_t):
    n = top_e_t.shape[1]
    tn = PLAN_TN
    n_tiles = n * TOP_K // MOE_TM + N_EXPERTS
    tri = jnp.asarray(np.triu(np.ones((tn, tn), np.float32), 1), BF16)
    low = jnp.asarray(np.tril(np.ones((N_EXPERTS, N_EXPERTS), np.float32), -1), BF16)
    pos_t, cnt = pl.pallas_call(
        _plan_kernel,
        grid=(2, n // tn),
        in_specs=[pl.BlockSpec((TOP_K, tn), lambda ph, j: (0, j)),
                  pl.BlockSpec((tn, tn), lambda ph, j: (0, 0)),
                  pl.BlockSpec((N_EXPERTS, N_EXPERTS), lambda ph, j: (0, 0))],
        out_specs=[pl.BlockSpec((TOP_K, tn), lambda ph, j: (0, j * ph)),
                   pl.BlockSpec((N_EXPERTS, LANES), lambda ph, j: (0, 0))],
        out_shape=[jax.ShapeDtypeStruct((TOP_K, n), I32),
                   jax.ShapeDtypeStruct((N_EXPERTS, LANES), F32)],
        scratch_shapes=[pltpu.VMEM((N_EXPERTS, LANES), F32),
                        pltpu.VMEM((N_EXPERTS, LANES), F32)],
        compiler_params=_params("arbitrary", "arbitrary"),
        name="plan",
    )(top_e_t, tri, low)
    counts = cnt[:, 0].astype(I32)
    tile_end = jnp.cumsum((counts + MOE_TM - 1) // MOE_TM)
    tile_expert = jnp.minimum(
        jnp.sum((tile_end[None, :] <= jnp.arange(n_tiles, dtype=I32)[:, None]).astype(I32), axis=1),
        N_EXPERTS - 1)
    n_used = tile_end[-1:].astype(I32)
    t_ids = jnp.arange(n_tiles, dtype=I32)
    is_last = jnp.any((tile_end[None, :] - 1 == t_ids[:, None]) & (counts[None, :] > 0), axis=1)
    zero_tile = (is_last | (t_ids >= n_used[0])).astype(I32)
    return pos_t, zero_tile, tile_expert.astype(I32), n_used, n_tiles * MOE_TM


def _layer(x, c, rel_bias, norm1_w, norm2_w, w_ada, b_ada, w_in, conv_w, w_conv_out, q_norm_w,
           k_norm_w, idx_k_norm_w, idx_k_norm_b, w_attn_out, w_o, w_router, router_bias,
           w1, w3, w2, ws1, ws3, ws2):
    b, s, d = x.shape
    mod = _mod(c, w_ada, b_ada).reshape(b, 6, 1, d)
    sh1, sc1, g1, sh2, sc2, g2 = [mod[:, m] for m in range(6)]

    cols = [w_in[:, _SEG[name][0]:_SEG[name][1]] for name in _ORDER]
    cols.append(jnp.zeros((d, PROJ_W - sum(col.shape[1] for col in cols)), w_in.dtype))
    w_in_p = jnp.concatenate(cols, axis=1).astype(BF16)

    proj = _proj(x, norm1_w, sc1, sh1, w_in_p)
    qT, qiT, kh, vT, kin, wT = _prep(proj, q_norm_w, k_norm_w, idx_k_norm_w, idx_k_norm_b,
                                     rel_bias[REL_BUCKETS - 1])
    attn = _attention(qT, qiT, wT, kh, vT, kin, _bias_strips(rel_bias))
    mixed = _mix(proj, attn, conv_w, w_conv_out.astype(BF16), w_attn_out.astype(BF16))
    base, h2, logits_t = _post(x, mixed, g1, norm2_w, sc2, sh2, g2, w_o.astype(BF16), w_router.T,
                               ws1.astype(BF16), ws3.astype(BF16), ws2.astype(BF16))
    top_e_t, w_sel_t = _route(logits_t, router_bias)
    pos_t, zero_start, tile_expert, n_used, n_rows = _dispatch_plan(top_e_t)
    xs = _dispatch(h2.reshape(b * s, ROW_SUB, LANES), pos_t, zero_start, n_rows)
    ys = _experts(xs, tile_expert, n_used, w1, w3, w2)
    return _combine(ys, pos_t, w_sel_t.T, base, g2)


def kernel(x, c, rel_bias, norm1_w, norm2_w, w_ada, b_ada, w_in, conv_w, w_conv_out, q_norm_w,
           k_norm_w, idx_k_norm_w, idx_k_norm_b, w_attn_out, w_o, w_router, router_bias,
           w1, w3, w2, ws1, ws3, ws2):
    assert x.shape[1] % PROJ_TM == 0 and x.shape[2] == D_MODEL and w_ada.shape[0] == 1
    return _layer(x, c, rel_bias, norm1_w[0], norm2_w[0], w_ada[0], b_ada[0], w_in[0], conv_w[0],
                  w_conv_out[0], q_norm_w[0], k_norm_w[0], idx_k_norm_w[0], idx_k_norm_b[0],
                  w_attn_out[0], w_o[0], w_router[0], router_bias[0], w1[0], w3[0], w2[0],
                  ws1[0], ws3[0], ws2[0])
```

```python
import functools
import math

import numpy as np
import jax
import jax.numpy as jnp
from jax import lax
from jax.experimental import pallas as pl
from jax.experimental.pallas import tpu as pltpu

F32 = jnp.float32
BF16 = jnp.bfloat16
I32 = jnp.int32
PACKED = jnp.int32

D_MODEL = 2048
CONV_WIDTH = D_MODEL // 2
CONV_K = 3
N_HEADS = 16
N_KV_HEADS = 4
HEAD_DIM = 64
ATTN_WIDTH = N_HEADS * HEAD_DIM
KV_WIDTH = N_KV_HEADS * HEAD_DIM
IDX_HEADS = 16
IDX_DIM = 64
IDX_TOPK_MAX = 256
REL_BUCKETS = 32
REL_MAX_DIST = 128
N_EXPERTS = 64
N_GROUPS = 8
TOPK_GROUPS = 4
TOP_K = 8
D_EXPERT = 512
ROUTED_SCALE = 2.5
EPS = 1e-6
NEG = -1e30

REP = N_HEADS // N_KV_HEADS

LANES = 128
VMEM_LIMIT = 56 * 1024 * 1024

TQ = 128
KB = 128
KC = 4 * KB
FAR_KC = 4 * KB
PROJ_TM = 1024
PROJ_TN = 768
PREP_TM = 512
MIX_TM = 512
POST_TM = 512
ROUTE_TN = 512
MOE_TM = 512
COMB_TM = 128
DISP_TM = 256
PLAN_TN = 512

ROW_SUB = D_MODEL // 2 // LANES
QK_DIM = 128
V_ROWS = HEAD_DIM + 16
LOG2E = math.log2(math.e)

_SEG = dict(cb=(0, 1024), cc=(1024, 2048), cu=(2048, 3072), q=(3072, 4096), k=(4096, 4352),
            v=(4352, 4608), qi=(4608, 5632), ki=(5632, 5696), wi=(5696, 5712),
            ga=(5712, 7760), gb=(7760, 9808))
_ORDER = ["ga", "gb", "cb", "cc", "cu", "q", "qi", "k", "v", "ki", "wi"]
PROJ_W = 9984
COL_GA, COL_GB = 0, 1
COL_CB, COL_CC, COL_CU, COL_Q, COL_QI = 4, 5, 6, 7, 8
COL_K, COL_V = 36, 37
COL_KW = 76

INT_MIN = -(2 ** 31)
INT_MAX = 2 ** 31 - 1


def _sortable_key_of(x):
    bits = int(np.float32(x).view(np.int32))
    return bits ^ 0x7FFFFFFF if bits < 0 else bits


NEG_KEY = _sortable_key_of(NEG)


def _sigmoid(x):
    return 1.0 / (1.0 + jnp.exp(-x))


def _pack_bf16_pairs(x):
    half = x.shape[1] // 2
    lo = lax.bitcast_convert_type(x[:, :half].astype(BF16).astype(F32), PACKED)
    hi = lax.bitcast_convert_type(x[:, half:].astype(BF16).astype(F32), PACKED)
    return lax.shift_right_logical(lo, jnp.full_like(lo, 16)) | (hi & jnp.int32(-65536))


def _unpack_bf16_pairs(w):
    lo = lax.bitcast_convert_type(w << 16, F32)
    hi = lax.bitcast_convert_type(w & jnp.int32(-65536), F32)
    return lo, hi


def _store_row_tiles(ref, words):
    m = words.shape[0]
    for sl in range(ROW_SUB):
        ref[pl.ds(sl, m, stride=ROW_SUB), :] = words[:, sl * LANES:(sl + 1) * LANES]


def _load_row_tiles(ref, m):
    return jnp.concatenate([ref[pl.ds(sl, m, stride=ROW_SUB), :] for sl in range(ROW_SUB)], axis=1)


def _params(*sem):
    return pltpu.CompilerParams(dimension_semantics=sem, vmem_limit_bytes=VMEM_LIMIT)


def _mod_kernel(c_ref, w_ref, b_ref, o_ref):
    c = c_ref[...]
    s = (c * _sigmoid(c)).astype(BF16)
    o_ref[...] = jnp.dot(s, w_ref[...].astype(BF16), preferred_element_type=F32) + b_ref[...]


def _mod(c, w_ada, b_ada):
    b = c.shape[0]
    rows = 8
    cp = jnp.pad(c, ((0, rows - b), (0, 0)))
    n = w_ada.shape[1]
    tn = 1024
    out = pl.pallas_call(
        _mod_kernel,
        grid=(n // tn,),
        in_specs=[pl.BlockSpec((rows, D_MODEL), lambda j: (0, 0)),
                  pl.BlockSpec((D_MODEL, tn), lambda j: (0, j)),
                  pl.BlockSpec((1, tn), lambda j: (0, j))],
        out_specs=pl.BlockSpec((rows, tn), lambda j: (0, j)),
        out_shape=jax.ShapeDtypeStruct((rows, n), F32),
        compiler_params=_params("arbitrary"),
        name="mod",
    )(cp, w_ada, b_ada.reshape(1, n))
    return out[:b]


def _proj_kernel(x_ref, nw_ref, sc_ref, sh_ref, w_ref, o_ref, h_ref):
    @pl.when(pl.program_id(2) == 0)
    def _():
        x = x_ref[0]
        ms = jnp.mean(x * x, axis=-1, keepdims=True)
        y = x * lax.rsqrt(ms + EPS) * nw_ref[...]
        h_ref[...] = (y * (1.0 + sc_ref[0]) + sh_ref[0]).astype(BF16)

    o_ref[0] = jnp.dot(h_ref[...], w_ref[...], preferred_element_type=F32).astype(BF16)


def _proj(x, norm_w, sc, sh, w_in_p):
    b, s, d = x.shape
    tm, tn = PROJ_TM, PROJ_TN
    return pl.pallas_call(
        _proj_kernel,
        grid=(b, s // tm, PROJ_W // tn),
        in_specs=[pl.BlockSpec((1, tm, d), lambda bi, i, j: (bi, i, 0)),
                  pl.BlockSpec((1, d), lambda bi, i, j: (0, 0)),
                  pl.BlockSpec((1, 1, d), lambda bi, i, j: (bi, 0, 0)),
                  pl.BlockSpec((1, 1, d), lambda bi, i, j: (bi, 0, 0)),
                  pl.BlockSpec((d, tn), lambda bi, i, j: (0, j))],
        out_specs=pl.BlockSpec((1, tm, tn), lambda bi, i, j: (bi, i, j)),
        out_shape=jax.ShapeDtypeStruct((b, s, PROJ_W), BF16),
        scratch_shapes=[pltpu.VMEM((tm, d), BF16)],
        compiler_params=_params("arbitrary", "arbitrary", "arbitrary"),
        name="proj",
    )(x, norm_w.reshape(1, d), sc, sh, w_in_p)


def _prep_kernel(q_ref, qi_ref, k_ref, v_ref, kw_ref, qnw_ref, knw_ref, inw_ref, inb_ref, qtail_ref,
                 qT_ref, qiT_ref, kh_ref, vT_ref, kin_ref, wT_ref):
    tm = q_ref.shape[1]
    nqb = tm // TQ

    q3 = q_ref[0].astype(F32).T.reshape(N_HEADS, HEAD_DIM, tm)
    ms = jnp.mean(q3 * q3, axis=1, keepdims=True)
    qn = q3 * lax.rsqrt(ms + EPS) * (qnw_ref[...] * (HEAD_DIM ** -0.5 * LOG2E))
    qi3 = qi_ref[0].astype(F32).T.reshape(IDX_HEADS, IDX_DIM, tm)
    for jb in range(nqb):
        for h in range(N_HEADS):
            g, r = divmod(h, REP)
            qT_ref[0, jb, g, :HEAD_DIM, r * TQ:(r + 1) * TQ] = qn[h, :, jb * TQ:(jb + 1) * TQ].astype(BF16)
        for g in range(N_KV_HEADS):
            qT_ref[0, jb, g, HEAD_DIM:, :] = qtail_ref[g]
        for h in range(IDX_HEADS):
            qiT_ref[0, jb, :, h * TQ:(h + 1) * TQ] = qi3[h, :, jb * TQ:(jb + 1) * TQ].astype(BF16)

    k = k_ref[0].astype(F32)
    ones_cols = jnp.where(lax.broadcasted_iota(I32, (tm, QK_DIM - HEAD_DIM), 1) < 2, 1.0, 0.0)
    for g in range(N_KV_HEADS):
        kg = k[:, g * HEAD_DIM:(g + 1) * HEAD_DIM]
        msk = jnp.mean(kg * kg, axis=-1, keepdims=True)
        kn = kg * lax.rsqrt(msk + EPS) * knw_ref[...]
        kh_ref[0, g] = jnp.concatenate([kn, ones_cols], axis=1).astype(BF16)

    v3 = v_ref[0].astype(F32).T.reshape(N_KV_HEADS, HEAD_DIM, tm)
    ones_rows = jnp.where(lax.broadcasted_iota(I32, (V_ROWS - HEAD_DIM, KB), 0) == 0, 1.0, 0.0)
    for g in range(N_KV_HEADS):
        for jb in range(tm // KB):
            vT_ref[0, g, jb] = jnp.concatenate(
                [v3[g, :, jb * KB:(jb + 1) * KB], ones_rows], axis=0).astype(BF16)

    kw = kw_ref[0].astype(F32)
    ki = kw[:, :IDX_DIM]
    mu = jnp.mean(ki, axis=-1, keepdims=True)
    var = jnp.mean(jnp.square(ki - mu), axis=-1, keepdims=True)
    kin_ref[0] = ((ki - mu) * lax.rsqrt(var + EPS) * inw_ref[...] + inb_ref[...]).astype(BF16)
    wiT = kw.T[IDX_DIM:IDX_DIM + IDX_HEADS] * (IDX_HEADS ** -0.5 * IDX_DIM ** -0.5)
    for jb in range(nqb):
        wT_ref[0, jb] = wiT[:, jb * TQ:(jb + 1) * TQ]


def _prep(proj, q_norm_w, k_norm_w, idx_k_norm_w, idx_k_norm_b, far_bias):
    b, s, _ = proj.shape
    tm = PREP_TM
    nqb = tm // TQ
    nq = s // TQ
    fb2 = (far_bias * LOG2E).reshape(N_KV_HEADS, REP)
    hi = fb2.astype(BF16)
    lo = (fb2 - hi.astype(F32)).astype(BF16)
    tail = jnp.stack([hi, lo], axis=1)
    tail = jnp.broadcast_to(tail[..., None], (N_KV_HEADS, 2, REP, TQ)).reshape(N_KV_HEADS, 2, REP * TQ)
    qtail = jnp.pad(tail, ((0, 0), (0, QK_DIM - HEAD_DIM - 2), (0, 0)))
    return pl.pallas_call(
        _prep_kernel,
        grid=(b, s // tm),
        in_specs=[pl.BlockSpec((1, tm, ATTN_WIDTH), lambda bi, i: (bi, i, COL_Q)),
                  pl.BlockSpec((1, tm, IDX_HEADS * IDX_DIM), lambda bi, i: (bi, i, COL_QI)),
                  pl.BlockSpec((1, tm, KV_WIDTH), lambda bi, i: (bi, i, COL_K)),
                  pl.BlockSpec((1, tm, KV_WIDTH), lambda bi, i: (bi, i, COL_V)),
                  pl.BlockSpec((1, tm, LANES), lambda bi, i: (bi, i, COL_KW)),
                  pl.BlockSpec((1, HEAD_DIM, 1), lambda bi, i: (0, 0, 0)),
                  pl.BlockSpec((1, HEAD_DIM), lambda bi, i: (0, 0)),
                  pl.BlockSpec((1, IDX_DIM), lambda bi, i: (0, 0)),
                  pl.BlockSpec((1, IDX_DIM), lambda bi, i: (0, 0)),
                  pl.BlockSpec((N_KV_HEADS, QK_DIM - HEAD_DIM, REP * TQ), lambda bi, i: (0, 0, 0))],
        out_specs=[pl.BlockSpec((1, nqb, N_KV_HEADS, QK_DIM, REP * TQ), lambda bi, i: (bi, i, 0, 0, 0)),
                   pl.BlockSpec((1, nqb, IDX_DIM, IDX_HEADS * TQ), lambda bi, i: (bi, i, 0, 0)),
                   pl.BlockSpec((1, N_KV_HEADS, tm, QK_DIM), lambda bi, i: (bi, 0, i, 0)),
                   pl.BlockSpec((1, N_KV_HEADS, tm // KB, V_ROWS, KB), lambda bi, i: (bi, 0, i, 0, 0)),
                   pl.BlockSpec((1, tm, IDX_DIM), lambda bi, i: (bi, i, 0)),
                   pl.BlockSpec((1, nqb, IDX_HEADS, TQ), lambda bi, i: (bi, i, 0, 0))],
        out_shape=[jax.ShapeDtypeStruct((b, nq, N_KV_HEADS, QK_DIM, REP * TQ), BF16),
                   jax.ShapeDtypeStruct((b, nq, IDX_DIM, IDX_HEADS * TQ), BF16),
                   jax.ShapeDtypeStruct((b, N_KV_HEADS, s, QK_DIM), BF16),
                   jax.ShapeDtypeStruct((b, N_KV_HEADS, s // KB, V_ROWS, KB), BF16),
                   jax.ShapeDtypeStruct((b, s, IDX_DIM), BF16),
                   jax.ShapeDtypeStruct((b, nq, IDX_HEADS, TQ), F32)],
        compiler_params=_params("arbitrary", "arbitrary"),
        name="prep",
    )(proj, proj, proj, proj, proj,
      q_norm_w.reshape(1, HEAD_DIM, 1), k_norm_w.reshape(1, HEAD_DIM),
      idx_k_norm_w.reshape(1, IDX_DIM), idx_k_norm_b.reshape(1, IDX_DIM), qtail)


def _t5_bucket_np(n):
    n = np.maximum(n, 0)
    max_exact = REL_BUCKETS // 2
    nf = np.maximum(n, 1).astype(np.float64)
    large = max_exact + np.floor(np.log(nf / max_exact) / math.log(REL_MAX_DIST / max_exact)
                                 * (REL_BUCKETS - max_exact)).astype(np.int64)
    large = np.minimum(large, REL_BUCKETS - 1)
    return np.where(n < max_exact, n, large).astype(np.int32)


def _bias_kernel(rb_ref, bucket_ref, o_ref):
    h = pl.program_id(0)
    bucket = bucket_ref[...]
    acc = jnp.zeros(bucket.shape, F32)
    for bkt in range(REL_BUCKETS):
        acc = jnp.where(bucket == bkt, rb_ref[bkt, h], acc)
    o_ref[0] = (acc - rb_ref[REL_BUCKETS - 1, h]) * LOG2E


def _bias_strips(rel_bias):
    kk = np.arange(3 * TQ)[:, None]
    qq = np.arange(TQ)[None, :]
    bucket = jnp.asarray(_t5_bucket_np(qq + TQ - kk))
    return pl.pallas_call(
        _bias_kernel,
        grid=(N_HEADS,),
        in_specs=[pl.BlockSpec(memory_space=pltpu.SMEM),
                  pl.BlockSpec((3 * TQ, TQ), lambda h: (0, 0))],
        out_specs=pl.BlockSpec((1, 3 * TQ, TQ), lambda h: (h, 0, 0)),
        out_shape=jax.ShapeDtypeStruct((N_HEADS, 3 * TQ, TQ), F32),
        compiler_params=_params("arbitrary"),
        name="bias",
    )(rel_bias, bucket)


def _attn_kernel(qT_ref, qiT_ref, wT_ref, kh_ref, vT_ref, kin_ref, biasT_ref, o_ref,
                 keys_ref, am_ref, amf_ref, p_ref, m_ref, acc_ref, sa_ref, sb_ref, *, n_sel):
    i = pl.program_id(1)
    seq = kin_ref.shape[1]
    t0 = i * TQ
    n_chunks = lax.shift_right_logical(i + 4, 2)
    q_pos = t0 + lax.broadcasted_iota(I32, (KB, TQ), 1)
    k_off = lax.broadcasted_iota(I32, (KB, TQ), 0)

    qiT = qiT_ref[0, 0]
    wT = wT_ref[0, 0]

    def score_chunk(c, carry):
        k0 = pl.multiple_of(c * KC, KC)
        d = jnp.dot(kin_ref[0, pl.ds(k0, KC), :], qiT, preferred_element_type=F32)
        acc = jnp.zeros((KC, TQ), F32)
        for h in range(IDX_HEADS):
            acc = acc + wT[h:h + 1, :] * jnp.maximum(d[:, h * TQ:(h + 1) * TQ], 0.0)
        for j in range(KC // KB):
            blk = c * (KC // KB) + j
            sc = jnp.where(blk * KB + k_off <= q_pos, acc[j * KB:(j + 1) * KB], NEG)
            bits = lax.bitcast_convert_type(sc, I32)
            keys_ref[blk] = jnp.where(bits < 0, bits ^ 0x7FFFFFFF, bits)
        return carry

    lax.fori_loop(0, n_chunks, score_chunk, 0)

    n_virtual = (seq - n_chunks * KC).astype(F32)

    def count(pred):
        def body(c, acc):
            for j in range(KC // KB):
                blk = c * (KC // KB) + j
                hit = jnp.where(pred(keys_ref[blk], blk), 1.0, 0.0)
                acc = acc + jnp.sum(hit.reshape(KB // 8, 8, TQ), axis=0)
            return acc
        acc = lax.fori_loop(0, n_chunks, body, jnp.zeros((8, TQ), F32))
        return jnp.sum(acc, axis=0, keepdims=True)

    def bit_body(it, thr):
        cand = thr + lax.shift_left(jnp.int32(1), 31 - it)
        cnt = count(lambda kb, blk: kb >= cand) + jnp.where(NEG_KEY >= cand, n_virtual, 0.0)
        return jnp.where(cnt >= n_sel, cand, thr)

    thr = lax.fori_loop(0, 32, bit_body, jnp.full((1, TQ), INT_MIN, I32))

    cnt_gt = count(lambda kb, blk: kb > thr) + jnp.where(NEG_KEY > thr, n_virtual, 0.0)
    cnt_eq = count(lambda kb, blk: kb == thr) + jnp.where(NEG_KEY == thr, n_virtual, 0.0)
    need = n_sel - cnt_gt
    p_ref[...] = jnp.full((8, TQ), INT_MAX, I32)
    has_tie = jnp.max(jnp.where(cnt_eq > need, 1.0, 0.0)) > 0.0

    @pl.when(has_tie)
    def _():
        idx_bits = int(seq).bit_length()

        def p_body(it, p):
            cand = p | lax.shift_left(jnp.int32(1), idx_bits - 1 - it)
            below = count(lambda kb, blk: (kb == thr) & (blk * KB + k_off < cand))
            return jnp.where(below < need, cand, p)

        p = lax.fori_loop(0, idx_bits, p_body, jnp.zeros((1, TQ), I32))
        p_ref[...] = jnp.broadcast_to(p, (8, TQ))

    p_last = p_ref[0:1, :]

    bw = jnp.maximum(i - 1, 0)
    ws = pl.multiple_of(bw * KB, KB)

    def mask_chunk(c, carry):
        for j in range(KC // KB):
            blk = c * (KC // KB) + j
            kb = keys_ref[blk]
            k_pos = blk * KB + k_off
            sel = (kb > thr) | ((kb == thr) & (k_pos <= p_last))
            v = jnp.where(sel & (k_pos <= q_pos), 0.0, NEG)
            am_ref[blk] = v
            amf_ref[blk] = jnp.where(k_pos < ws, v, NEG)
        return carry

    lax.fori_loop(0, n_chunks, mask_chunk, 0)

    off = pl.multiple_of(TQ - (t0 - ws), TQ)
    am_near = jnp.concatenate([am_ref[bw], am_ref[bw + 1]], axis=0)
    for g in range(N_KV_HEADS):
        s = jnp.dot(kh_ref[0, g, pl.ds(ws, 2 * KB), :], qT_ref[0, 0, g], preferred_element_type=F32)
        s = jnp.concatenate(
            [s[:, r * TQ:(r + 1) * TQ] + (biasT_ref[REP * g + r, pl.ds(off, 2 * KB), :] + am_near)
             for r in range(REP)], axis=1)
        m = jnp.max(s, axis=0, keepdims=True)
        pb = jnp.exp2(s - m).astype(BF16)
        m_ref[g] = m
        acc_ref[g] = (jnp.dot(vT_ref[0, g, bw], pb[:KB], preferred_element_type=F32)
                      + jnp.dot(vT_ref[0, g, bw + 1], pb[KB:], preferred_element_type=F32))

    fb = FAR_KC // KB
    n_far = (bw + fb - 1) // fb

    def qk(f, dst):
        k0 = pl.multiple_of(f * FAR_KC, FAR_KC)
        for g in range(N_KV_HEADS):
            dst[g] = jnp.dot(kh_ref[0, g, pl.ds(k0, FAR_KC), :], qT_ref[0, 0, g],
                             preferred_element_type=F32)

    def softmax_pv(f, src):
        amf = jnp.concatenate([amf_ref[f * fb + j] for j in range(fb)], axis=0)
        for g in range(N_KV_HEADS):
            s = src[g]
            s = jnp.concatenate([s[:, r * TQ:(r + 1) * TQ] + amf for r in range(REP)], axis=1)
            m_old = m_ref[g]
            m_new = jnp.maximum(m_old, jnp.max(s, axis=0, keepdims=True))
            pb = jnp.exp2(s - m_new).astype(BF16)
            vc = jnp.concatenate([vT_ref[0, g, f * fb + j] for j in range(fb)], axis=1)
            m_ref[g] = m_new
            acc_ref[g] = (jnp.exp2(m_old - m_new) * acc_ref[g]
                          + jnp.dot(vc, pb, preferred_element_type=F32))

    @pl.when(n_far > 0)
    def _():
        qk(0, sa_ref)

    def pair_body(pf, carry):
        f0 = 2 * pf
        qk(f0 + 1, sb_ref)
        softmax_pv(f0, sa_ref)
        qk(jnp.minimum(f0 + 2, n_far - 1), sa_ref)
        softmax_pv(f0 + 1, sb_ref)
        return carry

    lax.fori_loop(0, n_far // 2, pair_body, 0)

    @pl.when(n_far % 2 == 1)
    def _():
        softmax_pv(n_far - 1, sa_ref)

    outs = []
    for g in range(N_KV_HEADS):
        og = acc_ref[g, :HEAD_DIM] / acc_ref[g, HEAD_DIM:HEAD_DIM + 1]
        outs.extend(og[:, r * TQ:(r + 1) * TQ] for r in range(REP))
    o_ref[0] = jnp.concatenate(outs, axis=0).T.astype(BF16)


def _attention(qT, qiT, wT, kh, vT, kin, bias_strips):
    b, nq = qT.shape[0], qT.shape[1]
    s = kin.shape[1]
    n_sel = min(IDX_TOPK_MAX, s // 4)
    nb = s // KB
    return pl.pallas_call(
        functools.partial(_attn_kernel, n_sel=n_sel),
        grid=(b, nq),
        in_specs=[pl.BlockSpec((1, 1, N_KV_HEADS, QK_DIM, REP * TQ), lambda bi, i: (bi, i, 0, 0, 0)),
                  pl.BlockSpec((1, 1, IDX_DIM, IDX_HEADS * TQ), lambda bi, i: (bi, i, 0, 0)),
                  pl.BlockSpec((1, 1, IDX_HEADS, TQ), lambda bi, i: (bi, i, 0, 0)),
                  pl.BlockSpec((1, N_KV_HEADS, s, QK_DIM), lambda bi, i: (bi, 0, 0, 0)),
                  pl.BlockSpec((1, N_KV_HEADS, nb, V_ROWS, KB), lambda bi, i: (bi, 0, 0, 0, 0)),
                  pl.BlockSpec((1, s, IDX_DIM), lambda bi, i: (bi, 0, 0)),
                  pl.BlockSpec((N_HEADS, 3 * TQ, TQ), lambda bi, i: (0, 0, 0))],
        out_specs=pl.BlockSpec((1, TQ, ATTN_WIDTH), lambda bi, i: (bi, i, 0)),
        out_shape=jax.ShapeDtypeStruct((b, s, ATTN_WIDTH), BF16),
        scratch_shapes=[pltpu.VMEM((nb, KB, TQ), I32),
                        pltpu.VMEM((nb, KB, TQ), F32),
                        pltpu.VMEM((nb, KB, TQ), F32),
                        pltpu.VMEM((8, TQ), I32),
                        pltpu.VMEM((N_KV_HEADS, 1, REP * TQ), F32),
                        pltpu.VMEM((N_KV_HEADS, V_ROWS, REP * TQ), F32),
                        pltpu.VMEM((N_KV_HEADS, FAR_KC, REP * TQ), F32),
                        pltpu.VMEM((N_KV_HEADS, FAR_KC, REP * TQ), F32)],
        compiler_params=_params("arbitrary", "arbitrary"),
        name="attn",
    )(qT, qiT, wT, kh, vT, kin, bias_strips)


HALO = 16


def _mix_kernel(cb_ref, cc_ref, cu_ref, ccp_ref, cup_ref, at_ref, ga_ref, gb_ref,
                cw_ref, wco_ref, wao_ref, o_ref):
    tm = cb_ref.shape[1]
    v = cc_ref[0].astype(F32) * cu_ref[0].astype(F32)
    first = pl.program_id(1) == 0
    hv = ccp_ref[0].astype(F32) * cup_ref[0].astype(F32)
    hv = jnp.where(first, 0.0, hv)
    row = lax.broadcasted_iota(I32, v.shape, 0)
    v1 = jnp.where(row == 0, hv[HALO - 1:HALO], pltpu.roll(v, 1, 0))
    v2 = pltpu.roll(v, 2, 0)
    v2 = jnp.where(row == 0, hv[HALO - 2:HALO - 1], jnp.where(row == 1, hv[HALO - 1:HALO], v2))
    y = cw_ref[0:1] * v2 + cw_ref[1:2] * v1 + cw_ref[2:3] * v
    yc = (cb_ref[0].astype(F32) * y).astype(BF16)
    y_conv = jnp.dot(yc, wco_ref[...], preferred_element_type=F32)
    y_attn = jnp.dot(at_ref[0], wao_ref[...], preferred_element_type=F32)
    mixed = _sigmoid(ga_ref[0].astype(F32)) * y_conv + _sigmoid(gb_ref[0].astype(F32)) * y_attn
    o_ref[0] = mixed.astype(BF16)


def _mix(proj, attn, conv_w, w_conv_out_b, w_attn_out_b):
    b, s, _ = proj.shape
    tm = MIX_TM
    hb = tm // HALO
    prev = lambda col: (lambda bi, i: (bi, jnp.maximum(i * hb - 1, 0), col))
    return pl.pallas_call(
        _mix_kernel,
        grid=(b, s // tm),
        in_specs=[pl.BlockSpec((1, tm, CONV_WIDTH), lambda bi, i: (bi, i, COL_CB)),
                  pl.BlockSpec((1, tm, CONV_WIDTH), lambda bi, i: (bi, i, COL_CC)),
                  pl.BlockSpec((1, tm, CONV_WIDTH), lambda bi, i: (bi, i, COL_CU)),
                  pl.BlockSpec((1, HALO, CONV_WIDTH), prev(COL_CC)),
                  pl.BlockSpec((1, HALO, CONV_WIDTH), prev(COL_CU)),
                  pl.BlockSpec((1, tm, ATTN_WIDTH), lambda bi, i: (bi, i, 0)),
                  pl.BlockSpec((1, tm, D_MODEL), lambda bi, i: (bi, i, COL_GA)),
                  pl.BlockSpec((1, tm, D_MODEL), lambda bi, i: (bi, i, COL_GB)),
                  pl.BlockSpec((8, CONV_WIDTH), lambda bi, i: (0, 0)),
                  pl.BlockSpec((CONV_WIDTH, D_MODEL), lambda bi, i: (0, 0)),
                  pl.BlockSpec((ATTN_WIDTH, D_MODEL), lambda bi, i: (0, 0))],
        out_specs=pl.BlockSpec((1, tm, D_MODEL), lambda bi, i: (bi, i, 0)),
        out_shape=jax.ShapeDtypeStruct((b, s, D_MODEL), BF16),
        compiler_params=_params("arbitrary", "arbitrary"),
        name="mix",
    )(proj, proj, proj, proj, proj, attn, proj, proj,
      jnp.pad(conv_w, ((0, 8 - CONV_K), (0, 0))), w_conv_out_b, w_attn_out_b)


def _post_kernel(x_ref, mx_ref, g1_ref, nw_ref, sc_ref, sh_ref, g2_ref, wo_ref, wrT_ref,
                 ws1_ref, ws3_ref, ws2_ref, base_ref, h2_ref, lg_ref):
    x1 = x_ref[0] + g1_ref[0] * jnp.dot(mx_ref[0], wo_ref[...], preferred_element_type=F32)
    ms = jnp.mean(x1 * x1, axis=-1, keepdims=True)
    h2 = x1 * lax.rsqrt(ms + EPS) * nw_ref[...] * (1.0 + sc_ref[0]) + sh_ref[0]
    _store_row_tiles(h2_ref, _pack_bf16_pairs(h2))
    lg_ref[...] = lax.dot_general(wrT_ref[...], h2, (((1,), (1,)), ((), ())),
                                  precision=lax.Precision.HIGHEST, preferred_element_type=F32)
    hb = h2.astype(BF16)
    a = jnp.dot(hb, ws1_ref[...], preferred_element_type=F32)
    u = jnp.dot(hb, ws3_ref[...], preferred_element_type=F32)
    shared = jnp.dot((a * _sigmoid(a) * u).astype(BF16), ws2_ref[...], preferred_element_type=F32)
    base_ref[0] = x1 + g2_ref[0] * shared


def _post(x, mixed, g1, norm_w, sc, sh, g2, w_o_b, w_router_t, ws1_b, ws3_b, ws2_b):
    b, s, d = x.shape
    tm = POST_TM
    nt = s // tm
    vec = pl.BlockSpec((1, 1, d), lambda bi, i: (bi, 0, 0))
    const = lambda shape: pl.BlockSpec(shape, lambda bi, i: (0,) * len(shape))
    return pl.pallas_call(
        _post_kernel,
        grid=(b, nt),
        in_specs=[pl.BlockSpec((1, tm, d), lambda bi, i: (bi, i, 0)),
                  pl.BlockSpec((1, tm, d), lambda bi, i: (bi, i, 0)),
                  vec, const((1, d)), vec, vec, vec,
                  const((d, d)), const((N_EXPERTS, d)),
                  const((d, D_EXPERT)), const((d, D_EXPERT)), const((D_EXPERT, d))],
        out_specs=[pl.BlockSpec((1, tm, d), lambda bi, i: (bi, i, 0)),
                   pl.BlockSpec((tm * ROW_SUB, LANES), lambda bi, i: (bi * nt + i, 0)),
                   pl.BlockSpec((N_EXPERTS, tm), lambda bi, i: (0, bi * nt + i))],
        out_shape=[jax.ShapeDtypeStruct((b, s, d), F32),
                   jax.ShapeDtypeStruct((b * s * ROW_SUB, LANES), PACKED),
                   jax.ShapeDtypeStruct((N_EXPERTS, b * s), F32)],
        compiler_params=_params("arbitrary", "arbitrary"),
        name="post",
    )(x, mixed, g1, norm_w.reshape(1, d), sc, sh, g2, w_o_b, w_router_t, ws1_b, ws3_b, ws2_b)


def _first_max(cur, ids, sentinel):
    m = jnp.max(cur, axis=0, keepdims=True)
    first = jnp.min(jnp.where(cur == m, ids, sentinel), axis=0, keepdims=True)
    return m, first


def _route_kernel(lg_ref, rb_ref, idx_ref, w_ref):
    tn = lg_ref.shape[1]
    gsz = N_EXPERTS // N_GROUPS
    scores = _sigmoid(lg_ref[...])
    sel = scores + rb_ref[...]
    sub = lax.broadcasted_iota(I32, (gsz, tn), 0).astype(F32)

    gs = []
    for g in range(N_GROUPS):
        v = sel[g * gsz:(g + 1) * gsz]
        m1, first = _first_max(v, sub, float(gsz))
        m2 = jnp.max(jnp.where(sub == first, -jnp.inf, v), axis=0, keepdims=True)
        gs.append(m1 + m2)
    cur = jnp.concatenate(gs, axis=0)
    gid = lax.broadcasted_iota(I32, (N_GROUPS, tn), 0).astype(F32)
    keep = jnp.zeros((N_GROUPS, tn), F32)
    for _ in range(TOPK_GROUPS):
        _, first = _first_max(cur, gid, float(N_GROUPS))
        hit = gid == first
        keep = jnp.where(hit, 1.0, keep)
        cur = jnp.where(hit, -jnp.inf, cur)

    cur = jnp.concatenate(
        [jnp.where(keep[g:g + 1] > 0.0, sel[g * gsz:(g + 1) * gsz], NEG) for g in range(N_GROUPS)],
        axis=0)
    eid = lax.broadcasted_iota(I32, (N_EXPERTS, tn), 0).astype(F32)
    ids, ws = [], []
    for _ in range(TOP_K):
        _, first = _first_max(cur, eid, float(N_EXPERTS))
        hit = eid == first
        ids.append(first)
        ws.append(jnp.sum(jnp.where(hit, scores, 0.0), axis=0, keepdims=True))
        cur = jnp.where(hit, -jnp.inf, cur)
    w = jnp.concatenate(ws, axis=0)
    idx_ref[...] = jnp.concatenate(ids, axis=0).astype(I32)
    w_ref[...] = w / jnp.sum(w, axis=0, keepdims=True) * ROUTED_SCALE


def _route(logits_t, router_bias):
    e, n = logits_t.shape
    tn = ROUTE_TN
    return pl.pallas_call(
        _route_kernel,
        grid=(n // tn,),
        in_specs=[pl.BlockSpec((e, tn), lambda j: (0, j)),
                  pl.BlockSpec((e, 1), lambda j: (0, 0))],
        out_specs=[pl.BlockSpec((TOP_K, tn), lambda j: (0, j)),
                   pl.BlockSpec((TOP_K, tn), lambda j: (0, j))],
        out_shape=[jax.ShapeDtypeStruct((TOP_K, n), I32),
                   jax.ShapeDtypeStruct((TOP_K, n), F32)],
        compiler_params=_params("arbitrary"),
        name="route",
    )(logits_t, router_bias.reshape(e, 1))


def _tile_major(a_t, n_tiles, tm):
    return a_t.reshape(TOP_K, n_tiles, tm).transpose(1, 0, 2).reshape(n_tiles, 1, TOP_K * tm)


def _dispatch_kernel(zs_ref, pos_ref, x_ref, xs_hbm, zbuf, sem):
    step = pl.program_id(0)

    @pl.when(step == 0)
    def _():
        zbuf[...] = jnp.zeros(zbuf.shape, zbuf.dtype)

        def zero_copy(t):
            start = pl.multiple_of(t * MOE_TM, MOE_TM)
            return pltpu.make_async_copy(zbuf, xs_hbm.at[pl.ds(start, MOE_TM)], sem.at[1])

        def start_body(t, carry):
            @pl.when(zs_ref[t] != 0)
            def _():
                zero_copy(t).start()
            return carry

        def wait_body(t, carry):
            @pl.when(zs_ref[t] != 0)
            def _():
                zero_copy(t).wait()
            return carry

        lax.fori_loop(0, zs_ref.shape[0], start_body, 0)
        lax.fori_loop(0, zs_ref.shape[0], wait_body, 0)

    def row_copy(k, r):
        return pltpu.make_async_copy(x_ref.at[r], xs_hbm.at[pos_ref[0, 0, k * DISP_TM + r]], sem.at[0])

    def body(i, carry):
        for k in range(TOP_K):
            for u in range(2):
                row_copy(k, i * 2 + u).start(priority=u)
        return carry

    lax.fori_loop(0, DISP_TM // 2, body, 0)
    for _ in range(TOP_K):
        pltpu.make_async_copy(x_ref, x_ref, sem.at[0]).wait()


def _dispatch(h2, pos_t, zero_start, n_rows):
    n = h2.shape[0]
    tm = DISP_TM
    n_tiles = n // tm
    grid_spec = pltpu.PrefetchScalarGridSpec(
        num_scalar_prefetch=1,
        grid=(n_tiles,),
        in_specs=[pl.BlockSpec((1, 1, tm * TOP_K), lambda t, zs: (t, 0, 0), memory_space=pltpu.SMEM),
                  pl.BlockSpec((tm, ROW_SUB, LANES), lambda t, zs: (t, 0, 0))],
        out_specs=pl.BlockSpec(memory_space=pl.ANY),
        scratch_shapes=[pltpu.VMEM((MOE_TM, ROW_SUB, LANES), h2.dtype),
                        pltpu.SemaphoreType.DMA((2,))],
    )
    return pl.pallas_call(
        _dispatch_kernel,
        grid_spec=grid_spec,
        out_shape=jax.ShapeDtypeStruct((n_rows, ROW_SUB, LANES), h2.dtype),
        compiler_params=_params("arbitrary"),
        name="dispatch",
    )(zero_start, _tile_major(pos_t, n_tiles, tm), h2)


def _experts_kernel(te_ref, nu_ref, x_ref, w1_ref, w3_ref, w2_ref, y_ref, w1b, w3b, w2b):
    j = pl.program_id(0)
    n_used = nu_ref[0]

    @pl.when(j < n_used)
    def _():
        @pl.when((j == 0) | (te_ref[j] != te_ref[jnp.maximum(j - 1, 0)]))
        def _():
            w1b[...] = w1_ref[0].astype(BF16)
            w3b[...] = w3_ref[0].astype(BF16)
            w2b[...] = w2_ref[0].astype(BF16)

        lo, hi = _unpack_bf16_pairs(_load_row_tiles(x_ref, MOE_TM))
        x = jnp.concatenate([lo.astype(BF16), hi.astype(BF16)], axis=1)
        a = jnp.dot(x, w1b[...], preferred_element_type=F32)
        u = jnp.dot(x, w3b[...], preferred_element_type=F32)
        y = jnp.dot((a * _sigmoid(a) * u).astype(BF16), w2b[...], preferred_element_type=F32)
        _store_row_tiles(y_ref, _pack_bf16_pairs(y))

    @pl.when(j >= n_used)
    def _():
        y_ref[...] = jnp.zeros(y_ref.shape, y_ref.dtype)


def _experts(xs, tile_expert, n_used, w1, w3, w2):
    n_rows = xs.shape[0]
    nt = n_rows // MOE_TM
    d, f = w1.shape[1], w1.shape[2]
    blk = (MOE_TM * ROW_SUB, LANES)
    grid_spec = pltpu.PrefetchScalarGridSpec(
        num_scalar_prefetch=2,
        grid=(nt,),
        in_specs=[pl.BlockSpec(blk, lambda j, te, nu: (jnp.minimum(j, nu[0] - 1), 0)),
                  pl.BlockSpec((1, d, f), lambda j, te, nu: (te[j], 0, 0)),
                  pl.BlockSpec((1, d, f), lambda j, te, nu: (te[j], 0, 0)),
                  pl.BlockSpec((1, f, d), lambda j, te, nu: (te[j], 0, 0))],
        out_specs=pl.BlockSpec(blk, lambda j, te, nu: (j, 0)),
        scratch_shapes=[pltpu.VMEM((d, f), BF16),
                        pltpu.VMEM((d, f), BF16),
                        pltpu.VMEM((f, d), BF16)],
    )
    ys = pl.pallas_call(
        _experts_kernel,
        grid_spec=grid_spec,
        out_shape=jax.ShapeDtypeStruct((n_rows * ROW_SUB, LANES), PACKED),
        compiler_params=_params("arbitrary"),
        name="experts",
    )(tile_expert, n_used, xs.reshape(n_rows * ROW_SUB, LANES), w1, w3, w2)
    return ys.reshape(n_rows, ROW_SUB, LANES)


def _combine_kernel(pos_cur_ref, pos_nxt_ref, ys_hbm, base_ref, g2_ref, w_ref, o_ref, buf, sem):
    bi, i = pl.program_id(0), pl.program_id(1)
    step = bi * pl.num_programs(1) + i
    n_steps = pl.num_programs(0) * pl.num_programs(1)
    slot = lax.rem(step, 2)

    def issue(pos_ref, dst_slot):
        def body(i, carry):
            for k in range(TOP_K):
                for u in range(2):
                    r = i * 2 + u
                    pltpu.make_async_copy(ys_hbm.at[pos_ref[0, 0, k * COMB_TM + r]],
                                          buf.at[dst_slot, k, pl.ds(r * ROW_SUB, ROW_SUB), :],
                                          sem.at[dst_slot]).start(priority=u)
            return carry
        lax.fori_loop(0, COMB_TM // 2, body, 0)

    @pl.when(step == 0)
    def _():
        issue(pos_cur_ref, 0)

    for parity in (0, 1):
        @pl.when((step + 1 < n_steps) & (slot == parity))
        def _(parity=parity):
            issue(pos_nxt_ref, 1 - parity)

    pltpu.make_async_copy(buf.at[slot], buf.at[slot], sem.at[slot]).wait()

    w = w_ref[...]
    half = o_ref.shape[2] // 2
    for parity in (0, 1):
        @pl.when(slot == parity)
        def _(parity=parity):
            acc_lo = jnp.zeros((COMB_TM, half), F32)
            acc_hi = jnp.zeros((COMB_TM, half), F32)
            for k in range(TOP_K):
                lo, hi = _unpack_bf16_pairs(_load_row_tiles(buf.at[parity, k], COMB_TM))
                acc_lo = acc_lo + w[:, k:k + 1] * lo
                acc_hi = acc_hi + w[:, k:k + 1] * hi
            o_ref[0] = base_ref[0] + g2_ref[0] * jnp.concatenate([acc_lo, acc_hi], axis=1)


def _combine(ys, pos_t, w_sel, base, g2):
    b, s, d = base.shape
    tm = COMB_TM
    nt = s // tm
    n_tiles = b * nt
    pos_t = _tile_major(pos_t, n_tiles, tm)
    return pl.pallas_call(
        _combine_kernel,
        grid=(b, nt),
        in_specs=[pl.BlockSpec((1, 1, tm * TOP_K), lambda bi, i: (bi * nt + i, 0, 0),
                               memory_space=pltpu.SMEM),
                  pl.BlockSpec((1, 1, tm * TOP_K),
                               lambda bi, i: (jnp.minimum(bi * nt + i + 1, n_tiles - 1), 0, 0),
                               memory_space=pltpu.SMEM),
                  pl.BlockSpec(memory_space=pl.ANY),
                  pl.BlockSpec((1, tm, d), lambda bi, i: (bi, i, 0)),
                  pl.BlockSpec((1, 1, d), lambda bi, i: (bi, 0, 0)),
                  pl.BlockSpec((tm, TOP_K), lambda bi, i: (bi * nt + i, 0))],
        out_specs=pl.BlockSpec((1, tm, d), lambda bi, i: (bi, i, 0)),
        out_shape=jax.ShapeDtypeStruct((b, s, d), F32),
        scratch_shapes=[pltpu.VMEM((2, TOP_K, tm * ROW_SUB, LANES), ys.dtype),
                        pltpu.SemaphoreType.DMA((2,))],
        compiler_params=_params("arbitrary", "arbitrary"),
        name="combine",
    )(pos_t, pos_t, ys, base, g2, w_sel)


def _plan_kernel(te_ref, tri_ref, low_ref, pos_ref, cnt_ref, run_ref, start_ref):
    phase, j = pl.program_id(0), pl.program_id(1)
    tn = te_ref.shape[1]
    te = te_ref[...]
    eid = lax.broadcasted_iota(I32, (N_EXPERTS, tn), 0)
    hot = jnp.zeros((N_EXPERTS, tn), F32)
    for k in range(TOP_K):
        hot = hot + jnp.where(te[k:k + 1, :] == eid, 1.0, 0.0)
    tile_count = jnp.sum(hot, axis=1, keepdims=True)

    @pl.when((phase == 0) & (j == 0))
    def _():
        run_ref[...] = jnp.zeros(run_ref.shape, F32)

    @pl.when((phase == 1) & (j == 0))
    def _():
        counts = run_ref[...]
        cnt_ref[...] = counts
        tiles = jnp.floor((counts + (MOE_TM - 1)) * (1.0 / MOE_TM))
        start_ref[...] = jnp.dot(low_ref[...], tiles.astype(BF16), preferred_element_type=F32) * MOE_TM
        run_ref[...] = jnp.zeros(run_ref.shape, F32)

    @pl.when(phase == 1)
    def _():
        before = jnp.dot(hot.astype(BF16), tri_ref[...], preferred_element_type=F32)
        val = before + (run_ref[:, 0:1] + start_ref[:, 0:1])
        rows = [jnp.sum(jnp.where(te[k:k + 1, :] == eid, val, 0.0), axis=0, keepdims=True)
                for k in range(TOP_K)]
        pos_ref[...] = jnp.concatenate(rows, axis=0).astype(I32)

    run_ref[...] = run_ref[...] + tile_count


def _dispatch_plan(top_e_t):
    n = top_e_t.shape[1]
    tn = PLAN_TN
    n_tiles = n * TOP_K // MOE_TM + N_EXPERTS
    tri = jnp.asarray(np.triu(np.ones((tn, tn), np.float32), 1), BF16)
    low = jnp.asarray(np.tril(np.ones((N_EXPERTS, N_EXPERTS), np.float32), -1), BF16)
    pos_t, cnt = pl.pallas_call(
        _plan_kernel,
        grid=(2, n // tn),
        in_specs=[pl.BlockSpec((TOP_K, tn), lambda ph, j: (0, j)),
                  pl.BlockSpec((tn, tn), lambda ph, j: (0, 0)),
                  pl.BlockSpec((N_EXPERTS, N_EXPERTS), lambda ph, j: (0, 0))],
        out_specs=[pl.BlockSpec((TOP_K, tn), lambda ph, j: (0, j * ph)),
                   pl.BlockSpec((N_EXPERTS, LANES), lambda ph, j: (0, 0))],
        out_shape=[jax.ShapeDtypeStruct((TOP_K, n), I32),
                   jax.ShapeDtypeStruct((N_EXPERTS, LANES), F32)],
        scratch_shapes=[pltpu.VMEM((N_EXPERTS, LANES), F32),
                        pltpu.VMEM((N_EXPERTS, LANES), F32)],
        compiler_params=_params("arbitrary", "arbitrary"),
        name="plan",
    )(top_e_t, tri, low)
    counts = cnt[:, 0].astype(I32)
    tile_end = jnp.cumsum((counts + MOE_TM - 1) // MOE_TM)
    tile_expert = jnp.minimum(
        jnp.sum((tile_end[None, :] <= jnp.arange(n_tiles, dtype=I32)[:, None]).astype(I32), axis=1),
        N_EXPERTS - 1)
    n_used = tile_end[-1:].astype(I32)
    t_ids = jnp.arange(n_tiles, dtype=I32)
    is_last = jnp.any((tile_end[None, :] - 1 == t_ids[:, None]) & (counts[None, :] > 0), axis=1)
    zero_tile = (is_last | (t_ids >= n_used[0])).astype(I32)
    return pos_t, zero_tile, tile_expert.astype(I32), n_used, n_tiles * MOE_TM


def _layer(x, c, rel_bias, norm1_w, norm2_w, w_ada, b_ada, w_in, conv_w, w_conv_out, q_norm_w,
           k_norm_w, idx_k_norm_w, idx_k_norm_b, w_attn_out, w_o, w_router, router_bias,
           w1, w3, w2, ws1, ws3, ws2):
    b, s, d = x.shape
    mod = _mod(c, w_ada, b_ada).reshape(b, 6, 1, d)
    sh1, sc1, g1, sh2, sc2, g2 = [mod[:, m] for m in range(6)]

    cols = [w_in[:, _SEG[name][0]:_SEG[name][1]] for name in _ORDER]
    cols.append(jnp.zeros((d, PROJ_W - sum(col.shape[1] for col in cols)), w_in.dtype))
    w_in_p = jnp.concatenate(cols, axis=1).astype(BF16)

    proj = _proj(x, norm1_w, sc1, sh1, w_in_p)
    qT, qiT, kh, vT, kin, wT = _prep(proj, q_norm_w, k_norm_w, idx_k_norm_w, idx_k_norm_b,
                                     rel_bias[REL_BUCKETS - 1])
    attn = _attention(qT, qiT, wT, kh, vT, kin, _bias_strips(rel_bias))
    mixed = _mix(proj, attn, conv_w, w_conv_out.astype(BF16), w_attn_out.astype(BF16))
    base, h2, logits_t = _post(x, mixed, g1, norm2_w, sc2, sh2, g2, w_o.astype(BF16), w_router.T,
                               ws1.astype(BF16), ws3.astype(BF16), ws2.astype(BF16))
    top_e_t, w_sel_t = _route(logits_t, router_bias)
    pos_t, zero_start, tile_expert, n_used, n_rows = _dispatch_plan(top_e_t)
    xs = _dispatch(h2.reshape(b * s, ROW_SUB, LANES), pos_t, zero_start, n_rows)
    ys = _experts(xs, tile_expert, n_used, w1, w3, w2)
    return _combine(ys, pos_t, w_sel_t.T, base, g2)


def kernel(x, c, rel_bias, norm1_w, norm2_w, w_ada, b_ada, w_in, conv_w, w_conv_out, q_norm_w,
           k_norm_w, idx_k_norm_w, idx_k_norm_b, w_attn_out, w_o, w_router, router_bias,
           w1, w3, w2, ws1, ws3, ws2):
    assert x.shape[1] % PROJ_TM == 0 and x.shape[2] == D_MODEL and w_ada.shape[0] == 1
    return _layer(x, c, rel_bias, norm1_w[0], norm2_w[0], w_ada[0], b_ada[0], w_in[0], conv_w[0],
                  w_conv_out[0], q_norm_w[0], k_norm_w[0], idx_k_norm_w[0], idx_k_norm_b[0],
                  w_attn_out[0], w_o[0], w_router[0], router_bias[0], w1[0], w3[0], w2[0],
                  ws1[0], ws3[0], ws2[0])
```

```python
import functools
import math

import numpy as np
import jax
import jax.numpy as jnp
from jax import lax
from jax.experimental import pallas as pl
from jax.experimental.pallas import tpu as pltpu

F32 = jnp.float32
BF16 = jnp.bfloat16
I32 = jnp.int32
PACKED = jnp.int32

D_MODEL = 2048
CONV_WIDTH = D_MODEL // 2
CONV_K = 3
N_HEADS = 16
N_KV_HEADS = 4
HEAD_DIM = 64
ATTN_WIDTH = N_HEADS * HEAD_DIM
KV_WIDTH = N_KV_HEADS * HEAD_DIM
IDX_HEADS = 16
IDX_DIM = 64
IDX_TOPK_MAX = 256
REL_BUCKETS = 32
REL_MAX_DIST = 128
N_EXPERTS = 64
N_GROUPS = 8
TOPK_GROUPS = 4
TOP_K = 8
D_EXPERT = 512
ROUTED_SCALE = 2.5
EPS = 1e-6
NEG = -1e30

REP = N_HEADS // N_KV_HEADS

LANES = 128
VMEM_LIMIT = 56 * 1024 * 1024

TQ = 128
KB = 128
KC = 4 * KB
FAR_KC = 4 * KB
PROJ_TM = 1024
PROJ_TN = 768
PREP_TM = 512
MIX_TM = 512
POST_TM = 512
ROUTE_TN = 512
MOE_TM = 512
COMB_TM = 128
DISP_TM = 256
PLAN_TN = 512

ROW_SUB = D_MODEL // 2 // LANES
QK_DIM = 128
V_ROWS = HEAD_DIM + 16
LOG2E = math.log2(math.e)

_SEG = dict(cb=(0, 1024), cc=(1024, 2048), cu=(2048, 3072), q=(3072, 4096), k=(4096, 4352),
            v=(4352, 4608), qi=(4608, 5632), ki=(5632, 5696), wi=(5696, 5712),
            ga=(5712, 7760), gb=(7760, 9808))
_ORDER = ["ga", "gb", "cb", "cc", "cu", "q", "qi", "k", "v", "ki", "wi"]
PROJ_W = 9984
COL_GA, COL_GB = 0, 1
COL_CB, COL_CC, COL_CU, COL_Q, COL_QI = 4, 5, 6, 7, 8
COL_K, COL_V = 36, 37
COL_KW = 76

INT_MIN = -(2 ** 31)
INT_MAX = 2 ** 31 - 1


def _sortable_key_of(x):
    bits = int(np.float32(x).view(np.int32))
    return bits ^ 0x7FFFFFFF if bits < 0 else bits


NEG_KEY = _sortable_key_of(NEG)


def _sigmoid(x):
    return 1.0 / (1.0 + jnp.exp(-x))


def _pack_bf16_pairs(x):
    half = x.shape[1] // 2
    lo = lax.bitcast_convert_type(x[:, :half].astype(BF16).astype(F32), PACKED)
    hi = lax.bitcast_convert_type(x[:, half:].astype(BF16).astype(F32), PACKED)
    return lax.shift_right_logical(lo, jnp.full_like(lo, 16)) | (hi & jnp.int32(-65536))


def _unpack_bf16_pairs(w):
    lo = lax.bitcast_convert_type(w << 16, F32)
    hi = lax.bitcast_convert_type(w & jnp.int32(-65536), F32)
    return lo, hi


def _store_row_tiles(ref, words):
    m = words.shape[0]
    for sl in range(ROW_SUB):
        ref[pl.ds(sl, m, stride=ROW_SUB), :] = words[:, sl * LANES:(sl + 1) * LANES]


def _load_row_tiles(ref, m):
    return jnp.concatenate([ref[pl.ds(sl, m, stride=ROW_SUB), :] for sl in range(ROW_SUB)], axis=1)


def _params(*sem):
    return pltpu.CompilerParams(dimension_semantics=sem, vmem_limit_bytes=VMEM_LIMIT)


def _mod_kernel(c_ref, w_ref, b_ref, o_ref):
    c = c_ref[...]
    s = (c * _sigmoid(c)).astype(BF16)
    o_ref[...] = jnp.dot(s, w_ref[...].astype(BF16), preferred_element_type=F32) + b_ref[...]


def _mod(c, w_ada, b_ada):
    b = c.shape[0]
    rows = 8
    cp = jnp.pad(c, ((0, rows - b), (0, 0)))
    n = w_ada.shape[1]
    tn = 1024
    out = pl.pallas_call(
        _mod_kernel,
        grid=(n // tn,),
        in_specs=[pl.BlockSpec((rows, D_MODEL), lambda j: (0, 0)),
                  pl.BlockSpec((D_MODEL, tn), lambda j: (0, j)),
                  pl.BlockSpec((1, tn), lambda j: (0, j))],
        out_specs=pl.BlockSpec((rows, tn), lambda j: (0, j)),
        out_shape=jax.ShapeDtypeStruct((rows, n), F32),
        compiler_params=_params("arbitrary"),
        name="mod",
    )(cp, w_ada, b_ada.reshape(1, n))
    return out[:b]


def _proj_kernel(x_ref, nw_ref, sc_ref, sh_ref, w_ref, o_ref, h_ref):
    @pl.when(pl.program_id(2) == 0)
    def _():
        x = x_ref[0]
        ms = jnp.mean(x * x, axis=-1, keepdims=True)
        y = x * lax.rsqrt(ms + EPS) * nw_ref[...]
        h_ref[...] = (y * (1.0 + sc_ref[0]) + sh_ref[0]).astype(BF16)

    o_ref[0] = jnp.dot(h_ref[...], w_ref[...], preferred_element_type=F32).astype(BF16)


def _proj(x, norm_w, sc, sh, w_in_p):
    b, s, d = x.shape
    tm, tn = PROJ_TM, PROJ_TN
    return pl.pallas_call(
        _proj_kernel,
        grid=(b, s // tm, PROJ_W // tn),
        in_specs=[pl.BlockSpec((1, tm, d), lambda bi, i, j: (bi, i, 0)),
                  pl.BlockSpec((1, d), lambda bi, i, j: (0, 0)),
                  pl.BlockSpec((1, 1, d), lambda bi, i, j: (bi, 0, 0)),
                  pl.BlockSpec((1, 1, d), lambda bi, i, j: (bi, 0, 0)),
                  pl.BlockSpec((d, tn), lambda bi, i, j: (0, j))],
        out_specs=pl.BlockSpec((1, tm, tn), lambda bi, i, j: (bi, i, j)),
        out_shape=jax.ShapeDtypeStruct((b, s, PROJ_W), BF16),
        scratch_shapes=[pltpu.VMEM((tm, d), BF16)],
        compiler_params=_params("arbitrary", "arbitrary", "arbitrary"),
        name="proj",
    )(x, norm_w.reshape(1, d), sc, sh, w_in_p)


def _prep_kernel(q_ref, qi_ref, k_ref, v_ref, kw_ref, qnw_ref, knw_ref, inw_ref, inb_ref, qtail_ref,
                 qT_ref, qiT_ref, kh_ref, vT_ref, kin_ref, wT_ref):
    tm = q_ref.shape[1]
    nqb = tm // TQ

    q3 = q_ref[0].astype(F32).T.reshape(N_HEADS, HEAD_DIM, tm)
    ms = jnp.mean(q3 * q3, axis=1, keepdims=True)
    qn = q3 * lax.rsqrt(ms + EPS) * (qnw_ref[...] * (HEAD_DIM ** -0.5 * LOG2E))
    qi3 = qi_ref[0].astype(F32).T.reshape(IDX_HEADS, IDX_DIM, tm)
    for jb in range(nqb):
        for h in range(N_HEADS):
            g, r = divmod(h, REP)
            qT_ref[0, jb, g, :HEAD_DIM, r * TQ:(r + 1) * TQ] = qn[h, :, jb * TQ:(jb + 1) * TQ].astype(BF16)
        for g in range(N_KV_HEADS):
            qT_ref[0, jb, g, HEAD_DIM:, :] = qtail_ref[g]
        for h in range(IDX_HEADS):
            qiT_ref[0, jb, :, h * TQ:(h + 1) * TQ] = qi3[h, :, jb * TQ:(jb + 1) * TQ].astype(BF16)

    k = k_ref[0].astype(F32)
    ones_cols = jnp.where(lax.broadcasted_iota(I32, (tm, QK_DIM - HEAD_DIM), 1) < 2, 1.0, 0.0)
    for g in range(N_KV_HEADS):
        kg = k[:, g * HEAD_DIM:(g + 1) * HEAD_DIM]
        msk = jnp.mean(kg * kg, axis=-1, keepdims=True)
        kn = kg * lax.rsqrt(msk + EPS) * knw_ref[...]
        kh_ref[0, g] = jnp.concatenate([kn, ones_cols], axis=1).astype(BF16)

    v3 = v_ref[0].astype(F32).T.reshape(N_KV_HEADS, HEAD_DIM, tm)
    ones_rows = jnp.where(lax.broadcasted_iota(I32, (V_ROWS - HEAD_DIM, KB), 0) == 0, 1.0, 0.0)
    for g in range(N_KV_HEADS):
        for jb in range(tm // KB):
            vT_ref[0, g, jb] = jnp.concatenate(
                [v3[g, :, jb * KB:(jb + 1) * KB], ones_rows], axis=0).astype(BF16)

    kw = kw_ref[0].astype(F32)
    ki = kw[:, :IDX_DIM]
    mu = jnp.mean(ki, axis=-1, keepdims=True)
    var = jnp.mean(jnp.square(ki - mu), axis=-1, keepdims=True)
    kin_ref[0] = ((ki - mu) * lax.rsqrt(var + EPS) * inw_ref[...] + inb_ref[...]).astype(BF16)
    wiT = kw.T[IDX_DIM:IDX_DIM + IDX_HEADS] * (IDX_HEADS ** -0.5 * IDX_DIM ** -0.5)
    for jb in range(nqb):
        wT_ref[0, jb] = wiT[:, jb * TQ:(jb + 1) * TQ]


def _prep(proj, q_norm_w, k_norm_w, idx_k_norm_w, idx_k_norm_b, far_bias):
    b, s, _ = proj.shape
    tm = PREP_TM
    nqb = tm // TQ
    nq = s // TQ
    fb2 = (far_bias * LOG2E).reshape(N_KV_HEADS, REP)
    hi = fb2.astype(BF16)
    lo = (fb2 - hi.astype(F32)).astype(BF16)
    tail = jnp.stack([hi, lo], axis=1)
    tail = jnp.broadcast_to(tail[..., None], (N_KV_HEADS, 2, REP, TQ)).reshape(N_KV_HEADS, 2, REP * TQ)
    qtail = jnp.pad(tail, ((0, 0), (0, QK_DIM - HEAD_DIM - 2), (0, 0)))
    return pl.pallas_call(
        _prep_kernel,
        grid=(b, s // tm),
        in_specs=[pl.BlockSpec((1, tm, ATTN_WIDTH), lambda bi, i: (bi, i, COL_Q)),
                  pl.BlockSpec((1, tm, IDX_HEADS * IDX_DIM), lambda bi, i: (bi, i, COL_QI)),
                  pl.BlockSpec((1, tm, KV_WIDTH), lambda bi, i: (bi, i, COL_K)),
                  pl.BlockSpec((1, tm, KV_WIDTH), lambda bi, i: (bi, i, COL_V)),
                  pl.BlockSpec((1, tm, LANES), lambda bi, i: (bi, i, COL_KW)),
                  pl.BlockSpec((1, HEAD_DIM, 1), lambda bi, i: (0, 0, 0)),
                  pl.BlockSpec((1, HEAD_DIM), lambda bi, i: (0, 0)),
                  pl.BlockSpec((1, IDX_DIM), lambda bi, i: (0, 0)),
                  pl.BlockSpec((1, IDX_DIM), lambda bi, i: (0, 0)),
                  pl.BlockSpec((N_KV_HEADS, QK_DIM - HEAD_DIM, REP * TQ), lambda bi, i: (0, 0, 0))],
        out_specs=[pl.BlockSpec((1, nqb, N_KV_HEADS, QK_DIM, REP * TQ), lambda bi, i: (bi, i, 0, 0, 0)),
                   pl.BlockSpec((1, nqb, IDX_DIM, IDX_HEADS * TQ), lambda bi, i: (bi, i, 0, 0)),
                   pl.BlockSpec((1, N_KV_HEADS, tm, QK_DIM), lambda bi, i: (bi, 0, i, 0)),
                   pl.BlockSpec((1, N_KV_HEADS, tm // KB, V_ROWS, KB), lambda bi, i: (bi, 0, i, 0, 0)),
                   pl.BlockSpec((1, tm, IDX_DIM), lambda bi, i: (bi, i, 0)),
                   pl.BlockSpec((1, nqb, IDX_HEADS, TQ), lambda bi, i: (bi, i, 0, 0))],
        out_shape=[jax.ShapeDtypeStruct((b, nq, N_KV_HEADS, QK_DIM, REP * TQ), BF16),
                   jax.ShapeDtypeStruct((b, nq, IDX_DIM, IDX_HEADS * TQ), BF16),
                   jax.ShapeDtypeStruct((b, N_KV_HEADS, s, QK_DIM), BF16),
                   jax.ShapeDtypeStruct((b, N_KV_HEADS, s // KB, V_ROWS, KB), BF16),
                   jax.ShapeDtypeStruct((b, s, IDX_DIM), BF16),
                   jax.ShapeDtypeStruct((b, nq, IDX_HEADS, TQ), F32)],
        compiler_params=_params("arbitrary", "arbitrary"),
        name="prep",
    )(proj, proj, proj, proj, proj,
      q_norm_w.reshape(1, HEAD_DIM, 1), k_norm_w.reshape(1, HEAD_DIM),
      idx_k_norm_w.reshape(1, IDX_DIM), idx_k_norm_b.reshape(1, IDX_DIM), qtail)


def _t5_bucket_np(n):
    n = np.maximum(n, 0)
    max_exact = REL_BUCKETS // 2
    nf = np.maximum(n, 1).astype(np.float64)
    large = max_exact + np.floor(np.log(nf / max_exact) / math.log(REL_MAX_DIST / max_exact)
                                 * (REL_BUCKETS - max_exact)).astype(np.int64)
    large = np.minimum(large, REL_BUCKETS - 1)
    return np.where(n < max_exact, n, large).astype(np.int32)


def _bias_kernel(rb_ref, bucket_ref, o_ref):
    h = pl.program_id(0)
    bucket = bucket_ref[...]
    acc = jnp.zeros(bucket.shape, F32)
    for bkt in range(REL_BUCKETS):
        acc = jnp.where(bucket == bkt, rb_ref[bkt, h], acc)
    o_ref[0] = (acc - rb_ref[REL_BUCKETS - 1, h]) * LOG2E


def _bias_strips(rel_bias):
    kk = np.arange(3 * TQ)[:, None]
    qq = np.arange(TQ)[None, :]
    bucket = jnp.asarray(_t5_bucket_np(qq + TQ - kk))
    return pl.pallas_call(
        _bias_kernel,
        grid=(N_HEADS,),
        in_specs=[pl.BlockSpec(memory_space=pltpu.SMEM),
                  pl.BlockSpec((3 * TQ, TQ), lambda h: (0, 0))],
        out_specs=pl.BlockSpec((1, 3 * TQ, TQ), lambda h: (h, 0, 0)),
        out_shape=jax.ShapeDtypeStruct((N_HEADS, 3 * TQ, TQ), F32),
        compiler_params=_params("arbitrary"),
        name="bias",
    )(rel_bias, bucket)


def _attn_kernel(qT_ref, qiT_ref, wT_ref, kh_ref, vT_ref, kin_ref, biasT_ref, o_ref,
                 keys_ref, am_ref, amf_ref, p_ref, m_ref, acc_ref, sa_ref, sb_ref, *, n_sel):
    i = pl.program_id(1)
    seq = kin_ref.shape[1]
    t0 = i * TQ
    n_chunks = lax.shift_right_logical(i + 4, 2)
    q_pos = t0 + lax.broadcasted_iota(I32, (KB, TQ), 1)
    k_off = lax.broadcasted_iota(I32, (KB, TQ), 0)

    qiT = qiT_ref[0, 0]
    wT = wT_ref[0, 0]

    def score_chunk(c, carry):
        k0 = pl.multiple_of(c * KC, KC)
        d = jnp.dot(kin_ref[0, pl.ds(k0, KC), :], qiT, preferred_element_type=F32)
        acc = jnp.zeros((KC, TQ), F32)
        for h in range(IDX_HEADS):
            acc = acc + wT[h:h + 1, :] * jnp.maximum(d[:, h * TQ:(h + 1) * TQ], 0.0)
        for j in range(KC // KB):
            blk = c * (KC // KB) + j
            sc = jnp.where(blk * KB + k_off <= q_pos, acc[j * KB:(j + 1) * KB], NEG)
            bits = lax.bitcast_convert_type(sc, I32)
            keys_ref[blk] = jnp.where(bits < 0, bits ^ 0x7FFFFFFF, bits)
        return carry

    lax.fori_loop(0, n_chunks, score_chunk, 0)

    n_virtual = (seq - n_chunks * KC).astype(F32)

    def count(pred):
        def body(c, acc):
            for j in range(KC // KB):
                blk = c * (KC // KB) + j
                hit = jnp.where(pred(keys_ref[blk], blk), 1.0, 0.0)
                acc = acc + jnp.sum(hit.reshape(KB // 8, 8, TQ), axis=0)
            return acc
        acc = lax.fori_loop(0, n_chunks, body, jnp.zeros((8, TQ), F32))
        return jnp.sum(acc, axis=0, keepdims=True)

    def bit_body(it, thr):
        cand = thr + lax.shift_left(jnp.int32(1), 31 - it)
        cnt = count(lambda kb, blk: kb >= cand) + jnp.where(NEG_KEY >= cand, n_virtual, 0.0)
        return jnp.where(cnt >= n_sel, cand, thr)

    thr = lax.fori_loop(0, 32, bit_body, jnp.full((1, TQ), INT_MIN, I32))

    cnt_gt = count(lambda kb, blk: kb > thr) + jnp.where(NEG_KEY > thr, n_virtual, 0.0)
    cnt_eq = count(lambda kb, blk: kb == thr) + jnp.where(NEG_KEY == thr, n_virtual, 0.0)
    need = n_sel - cnt_gt
    p_ref[...] = jnp.full((8, TQ), INT_MAX, I32)
    has_tie = jnp.max(jnp.where(cnt_eq > need, 1.0, 0.0)) > 0.0

    @pl.when(has_tie)
    def _():
        idx_bits = int(seq).bit_length()

        def p_body(it, p):
            cand = p | lax.shift_left(jnp.int32(1), idx_bits - 1 - it)
            below = count(lambda kb, blk: (kb == thr) & (blk * KB + k_off < cand))
            return jnp.where(below < need, cand, p)

        p = lax.fori_loop(0, idx_bits, p_body, jnp.zeros((1, TQ), I32))
        p_ref[...] = jnp.broadcast_to(p, (8, TQ))

    p_last = p_ref[0:1, :]

    bw = jnp.maximum(i - 1, 0)
    ws = pl.multiple_of(bw * KB, KB)

    def mask_chunk(c, carry):
        for j in range(KC // KB):
            blk = c * (KC // KB) + j
            kb = keys_ref[blk]
            k_pos = blk * KB + k_off
            sel = (kb > thr) | ((kb == thr) & (k_pos <= p_last))
            v = jnp.where(sel & (k_pos <= q_pos), 0.0, NEG)
            am_ref[blk] = v
            amf_ref[blk] = jnp.where(k_pos < ws, v, NEG)
        return carry

    lax.fori_loop(0, n_chunks, mask_chunk, 0)

    off = pl.multiple_of(TQ - (t0 - ws), TQ)
    fb = FAR_KC // KB
    n_far = (bw + fb - 1) // fb

    def qk(f, dst):
        k0 = pl.multiple_of(f * FAR_KC, FAR_KC)
        for g in range(N_KV_HEADS):
            dst[g] = jnp.dot(kh_ref[0, g, pl.ds(k0, FAR_KC), :], qT_ref[0, 0, g],
                             preferred_element_type=F32)

    for g in range(N_KV_HEADS):
        sb_ref[g, :2 * KB] = jnp.dot(kh_ref[0, g, pl.ds(ws, 2 * KB), :], qT_ref[0, 0, g],
                                     preferred_element_type=F32)
    qk(0, sa_ref)
    am_near = jnp.concatenate([am_ref[bw], am_ref[bw + 1]], axis=0)
    for g in range(N_KV_HEADS):
        s = sb_ref[g, :2 * KB]
        s = jnp.concatenate(
            [s[:, r * TQ:(r + 1) * TQ] + (biasT_ref[REP * g + r, pl.ds(off, 2 * KB), :] + am_near)
             for r in range(REP)], axis=1)
        m = jnp.max(s, axis=0, keepdims=True)
        pb = jnp.exp2(s - m).astype(BF16)
        m_ref[g] = m
        acc_ref[g] = (jnp.dot(vT_ref[0, g, bw], pb[:KB], preferred_element_type=F32)
                      + jnp.dot(vT_ref[0, g, bw + 1], pb[KB:], preferred_element_type=F32))

    def softmax_pv(f, src):
        amf = jnp.concatenate([amf_ref[f * fb + j] for j in range(fb)], axis=0)
        for g in range(N_KV_HEADS):
            s = src[g]
            s = jnp.concatenate([s[:, r * TQ:(r + 1) * TQ] + amf for r in range(REP)], axis=1)
            m_old = m_ref[g]
            m_new = jnp.maximum(m_old, jnp.max(s, axis=0, keepdims=True))
            pb = jnp.exp2(s - m_new).astype(BF16)
            vc = jnp.concatenate([vT_ref[0, g, f * fb + j] for j in range(fb)], axis=1)
            m_ref[g] = m_new
            acc_ref[g] = (jnp.exp2(m_old - m_new) * acc_ref[g]
                          + jnp.dot(vc, pb, preferred_element_type=F32))

    def pair_body(pf, carry):
        f0 = 2 * pf
        qk(f0 + 1, sb_ref)
        softmax_pv(f0, sa_ref)
        qk(jnp.minimum(f0 + 2, n_far - 1), sa_ref)
        softmax_pv(f0 + 1, sb_ref)
        return carry

    lax.fori_loop(0, n_far // 2, pair_body, 0)

    @pl.when(n_far % 2 == 1)
    def _():
        softmax_pv(n_far - 1, sa_ref)

    outs = []
    for g in range(N_KV_HEADS):
        og = acc_ref[g, :HEAD_DIM] / acc_ref[g, HEAD_DIM:HEAD_DIM + 1]
        outs.extend(og[:, r * TQ:(r + 1) * TQ] for r in range(REP))
    o_ref[0] = jnp.concatenate(outs, axis=0).T.astype(BF16)


def _attention(qT, qiT, wT, kh, vT, kin, bias_strips):
    b, nq = qT.shape[0], qT.shape[1]
    s = kin.shape[1]
    n_sel = min(IDX_TOPK_MAX, s // 4)
    nb = s // KB
    return pl.pallas_call(
        functools.partial(_attn_kernel, n_sel=n_sel),
        grid=(b, nq),
        in_specs=[pl.BlockSpec((1, 1, N_KV_HEADS, QK_DIM, REP * TQ), lambda bi, i: (bi, i, 0, 0, 0)),
                  pl.BlockSpec((1, 1, IDX_DIM, IDX_HEADS * TQ), lambda bi, i: (bi, i, 0, 0)),
                  pl.BlockSpec((1, 1, IDX_HEADS, TQ), lambda bi, i: (bi, i, 0, 0)),
                  pl.BlockSpec((1, N_KV_HEADS, s, QK_DIM), lambda bi, i: (bi, 0, 0, 0)),
                  pl.BlockSpec((1, N_KV_HEADS, nb, V_ROWS, KB), lambda bi, i: (bi, 0, 0, 0, 0)),
                  pl.BlockSpec((1, s, IDX_DIM), lambda bi, i: (bi, 0, 0)),
                  pl.BlockSpec((N_HEADS, 3 * TQ, TQ), lambda bi, i: (0, 0, 0))],
        out_specs=pl.BlockSpec((1, TQ, ATTN_WIDTH), lambda bi, i: (bi, i, 0)),
        out_shape=jax.ShapeDtypeStruct((b, s, ATTN_WIDTH), BF16),
        scratch_shapes=[pltpu.VMEM((nb, KB, TQ), I32),
                        pltpu.VMEM((nb, KB, TQ), F32),
                        pltpu.VMEM((nb, KB, TQ), F32),
                        pltpu.VMEM((8, TQ), I32),
                        pltpu.VMEM((N_KV_HEADS, 1, REP * TQ), F32),
                        pltpu.VMEM((N_KV_HEADS, V_ROWS, REP * TQ), F32),
                        pltpu.VMEM((N_KV_HEADS, FAR_KC, REP * TQ), F32),
                        pltpu.VMEM((N_KV_HEADS, FAR_KC, REP * TQ), F32)],
        compiler_params=_params("arbitrary", "arbitrary"),
        name="attn",
    )(qT, qiT, wT, kh, vT, kin, bias_strips)


HALO = 16


def _mix_kernel(cb_ref, cc_ref, cu_ref, ccp_ref, cup_ref, at_ref, ga_ref, gb_ref,
                cw_ref, wco_ref, wao_ref, o_ref):
    tm = cb_ref.shape[1]
    v = cc_ref[0].astype(F32) * cu_ref[0].astype(F32)
    first = pl.program_id(1) == 0
    hv = ccp_ref[0].astype(F32) * cup_ref[0].astype(F32)
    hv = jnp.where(first, 0.0, hv)
    row = lax.broadcasted_iota(I32, v.shape, 0)
    v1 = jnp.where(row == 0, hv[HALO - 1:HALO], pltpu.roll(v, 1, 0))
    v2 = pltpu.roll(v, 2, 0)
    v2 = jnp.where(row == 0, hv[HALO - 2:HALO - 1], jnp.where(row == 1, hv[HALO - 1:HALO], v2))
    y = cw_ref[0:1] * v2 + cw_ref[1:2] * v1 + cw_ref[2:3] * v
    yc = (cb_ref[0].astype(F32) * y).astype(BF16)
    y_conv = jnp.dot(yc, wco_ref[...], preferred_element_type=F32)
    y_attn = jnp.dot(at_ref[0], wao_ref[...], preferred_element_type=F32)
    mixed = _sigmoid(ga_ref[0].astype(F32)) * y_conv + _sigmoid(gb_ref[0].astype(F32)) * y_attn
    o_ref[0] = mixed.astype(BF16)


def _mix(proj, attn, conv_w, w_conv_out_b, w_attn_out_b):
    b, s, _ = proj.shape
    tm = MIX_TM
    hb = tm // HALO
    prev = lambda col: (lambda bi, i: (bi, jnp.maximum(i * hb - 1, 0), col))
    return pl.pallas_call(
        _mix_kernel,
        grid=(b, s // tm),
        in_specs=[pl.BlockSpec((1, tm, CONV_WIDTH), lambda bi, i: (bi, i, COL_CB)),
                  pl.BlockSpec((1, tm, CONV_WIDTH), lambda bi, i: (bi, i, COL_CC)),
                  pl.BlockSpec((1, tm, CONV_WIDTH), lambda bi, i: (bi, i, COL_CU)),
                  pl.BlockSpec((1, HALO, CONV_WIDTH), prev(COL_CC)),
                  pl.BlockSpec((1, HALO, CONV_WIDTH), prev(COL_CU)),
                  pl.BlockSpec((1, tm, ATTN_WIDTH), lambda bi, i: (bi, i, 0)),
                  pl.BlockSpec((1, tm, D_MODEL), lambda bi, i: (bi, i, COL_GA)),
                  pl.BlockSpec((1, tm, D_MODEL), lambda bi, i: (bi, i, COL_GB)),
                  pl.BlockSpec((8, CONV_WIDTH), lambda bi, i: (0, 0)),
                  pl.BlockSpec((CONV_WIDTH, D_MODEL), lambda bi, i: (0, 0)),
                  pl.BlockSpec((ATTN_WIDTH, D_MODEL), lambda bi, i: (0, 0))],
        out_specs=pl.BlockSpec((1, tm, D_MODEL), lambda bi, i: (bi, i, 0)),
        out_shape=jax.ShapeDtypeStruct((b, s, D_MODEL), BF16),
        compiler_params=_params("arbitrary", "arbitrary"),
        name="mix",
    )(proj, proj, proj, proj, proj, attn, proj, proj,
      jnp.pad(conv_w, ((0, 8 - CONV_K), (0, 0))), w_conv_out_b, w_attn_out_b)


def _post_kernel(x_ref, mx_ref, g1_ref, nw_ref, sc_ref, sh_ref, g2_ref, wo_ref, wrT_ref,
                 ws1_ref, ws3_ref, ws2_ref, base_ref, h2_ref, lg_ref):
    x1 = x_ref[0] + g1_ref[0] * jnp.dot(mx_ref[0], wo_ref[...], preferred_element_type=F32)
    ms = jnp.mean(x1 * x1, axis=-1, keepdims=True)
    h2 = x1 * lax.rsqrt(ms + EPS) * nw_ref[...] * (1.0 + sc_ref[0]) + sh_ref[0]
    _store_row_tiles(h2_ref, _pack_bf16_pairs(h2))
    lg_ref[...] = lax.dot_general(wrT_ref[...], h2, (((1,), (1,)), ((), ())),
                                  precision=lax.Precision.HIGHEST, preferred_element_type=F32)
    hb = h2.astype(BF16)
    a = jnp.dot(hb, ws1_ref[...], preferred_element_type=F32)
    u = jnp.dot(hb, ws3_ref[...], preferred_element_type=F32)
    shared = jnp.dot((a * _sigmoid(a) * u).astype(BF16), ws2_ref[...], preferred_element_type=F32)
    base_ref[0] = x1 + g2_ref[0] * shared


def _post(x, mixed, g1, norm_w, sc, sh, g2, w_o_b, w_router_t, ws1_b, ws3_b, ws2_b):
    b, s, d = x.shape
    tm = POST_TM
    nt = s // tm
    vec = pl.BlockSpec((1, 1, d), lambda bi, i: (bi, 0, 0))
    const = lambda shape: pl.BlockSpec(shape, lambda bi, i: (0,) * len(shape))
    return pl.pallas_call(
        _post_kernel,
        grid=(b, nt),
        in_specs=[pl.BlockSpec((1, tm, d), lambda bi, i: (bi, i, 0)),
                  pl.BlockSpec((1, tm, d), lambda bi, i: (bi, i, 0)),
                  vec, const((1, d)), vec, vec, vec,
                  const((d, d)), const((N_EXPERTS, d)),
                  const((d, D_EXPERT)), const((d, D_EXPERT)), const((D_EXPERT, d))],
        out_specs=[pl.BlockSpec((1, tm, d), lambda bi, i: (bi, i, 0)),
                   pl.BlockSpec((tm * ROW_SUB, LANES), lambda bi, i: (bi * nt + i, 0)),
                   pl.BlockSpec((N_EXPERTS, tm), lambda bi, i: (0, bi * nt + i))],
        out_shape=[jax.ShapeDtypeStruct((b, s, d), F32),
                   jax.ShapeDtypeStruct((b * s * ROW_SUB, LANES), PACKED),
                   jax.ShapeDtypeStruct((N_EXPERTS, b * s), F32)],
        compiler_params=_params("arbitrary", "arbitrary"),
        name="post",
    )(x, mixed, g1, norm_w.reshape(1, d), sc, sh, g2, w_o_b, w_router_t, ws1_b, ws3_b, ws2_b)


def _first_max(cur, ids, sentinel):
    m = jnp.max(cur, axis=0, keepdims=True)
    first = jnp.min(jnp.where(cur == m, ids, sentinel), axis=0, keepdims=True)
    return m, first


def _route_kernel(lg_ref, rb_ref, idx_ref, w_ref):
    tn = lg_ref.shape[1]
    gsz = N_EXPERTS // N_GROUPS
    scores = _sigmoid(lg_ref[...])
    sel = scores + rb_ref[...]
    sub = lax.broadcasted_iota(I32, (gsz, tn), 0).astype(F32)

    gs = []
    for g in range(N_GROUPS):
        v = sel[g * gsz:(g + 1) * gsz]
        m1, first = _first_max(v, sub, float(gsz))
        m2 = jnp.max(jnp.where(sub == first, -jnp.inf, v), axis=0, keepdims=True)
        gs.append(m1 + m2)
    cur = jnp.concatenate(gs, axis=0)
    gid = lax.broadcasted_iota(I32, (N_GROUPS, tn), 0).astype(F32)
    keep = jnp.zeros((N_GROUPS, tn), F32)
    for _ in range(TOPK_GROUPS):
        _, first = _first_max(cur, gid, float(N_GROUPS))
        hit = gid == first
        keep = jnp.where(hit, 1.0, keep)
        cur = jnp.where(hit, -jnp.inf, cur)

    cur = jnp.concatenate(
        [jnp.where(keep[g:g + 1] > 0.0, sel[g * gsz:(g + 1) * gsz], NEG) for g in range(N_GROUPS)],
        axis=0)
    eid = lax.broadcasted_iota(I32, (N_EXPERTS, tn), 0).astype(F32)
    ids, ws = [], []
    for _ in range(TOP_K):
        _, first = _first_max(cur, eid, float(N_EXPERTS))
        hit = eid == first
        ids.append(first)
        ws.append(jnp.sum(jnp.where(hit, scores, 0.0), axis=0, keepdims=True))
        cur = jnp.where(hit, -jnp.inf, cur)
    w = jnp.concatenate(ws, axis=0)
    idx_ref[...] = jnp.concatenate(ids, axis=0).astype(I32)
    w_ref[...] = w / jnp.sum(w, axis=0, keepdims=True) * ROUTED_SCALE


def _route(logits_t, router_bias):
    e, n = logits_t.shape
    tn = ROUTE_TN
    return pl.pallas_call(
        _route_kernel,
        grid=(n // tn,),
        in_specs=[pl.BlockSpec((e, tn), lambda j: (0, j)),
                  pl.BlockSpec((e, 1), lambda j: (0, 0))],
        out_specs=[pl.BlockSpec((TOP_K, tn), lambda j: (0, j)),
                   pl.BlockSpec((TOP_K, tn), lambda j: (0, j))],
        out_shape=[jax.ShapeDtypeStruct((TOP_K, n), I32),
                   jax.ShapeDtypeStruct((TOP_K, n), F32)],
        compiler_params=_params("arbitrary"),
        name="route",
    )(logits_t, router_bias.reshape(e, 1))


def _tile_major(a_t, n_tiles, tm):
    return a_t.reshape(TOP_K, n_tiles, tm).transpose(1, 0, 2).reshape(n_tiles, 1, TOP_K * tm)


def _dispatch_kernel(zs_ref, pos_ref, x_ref, xs_hbm, zbuf, sem):
    step = pl.program_id(0)

    @pl.when(step == 0)
    def _():
        zbuf[...] = jnp.zeros(zbuf.shape, zbuf.dtype)

        def zero_copy(t):
            start = pl.multiple_of(t * MOE_TM, MOE_TM)
            return pltpu.make_async_copy(zbuf, xs_hbm.at[pl.ds(start, MOE_TM)], sem.at[1])

        def start_body(t, carry):
            @pl.when(zs_ref[t] != 0)
            def _():
                zero_copy(t).start()
            return carry

        def wait_body(t, carry):
            @pl.when(zs_ref[t] != 0)
            def _():
                zero_copy(t).wait()
            return carry

        lax.fori_loop(0, zs_ref.shape[0], start_body, 0)
        lax.fori_loop(0, zs_ref.shape[0], wait_body, 0)

    def row_copy(k, r):
        return pltpu.make_async_copy(x_ref.at[r], xs_hbm.at[pos_ref[0, 0, k * DISP_TM + r]], sem.at[0])

    def body(i, carry):
        for k in range(TOP_K):
            for u in range(2):
                row_copy(k, i * 2 + u).start(priority=u)
        return carry

    lax.fori_loop(0, DISP_TM // 2, body, 0)
    for _ in range(TOP_K):
        pltpu.make_async_copy(x_ref, x_ref, sem.at[0]).wait()


def _dispatch(h2, pos_t, zero_start, n_rows):
    n = h2.shape[0]
    tm = DISP_TM
    n_tiles = n // tm
    grid_spec = pltpu.PrefetchScalarGridSpec(
        num_scalar_prefetch=1,
        grid=(n_tiles,),
        in_specs=[pl.BlockSpec((1, 1, tm * TOP_K), lambda t, zs: (t, 0, 0), memory_space=pltpu.SMEM),
                  pl.BlockSpec((tm, ROW_SUB, LANES), lambda t, zs: (t, 0, 0))],
        out_specs=pl.BlockSpec(memory_space=pl.ANY),
        scratch_shapes=[pltpu.VMEM((MOE_TM, ROW_SUB, LANES), h2.dtype),
                        pltpu.SemaphoreType.DMA((2,))],
    )
    return pl.pallas_call(
        _dispatch_kernel,
        grid_spec=grid_spec,
        out_shape=jax.ShapeDtypeStruct((n_rows, ROW_SUB, LANES), h2.dtype),
        compiler_params=_params("arbitrary"),
        name="dispatch",
    )(zero_start, _tile_major(pos_t, n_tiles, tm), h2)


def _experts_kernel(te_ref, nu_ref, nxt_ref, par_ref, x_ref, w1_hbm, w3_hbm, w2_hbm, y_ref,
                    w1f, w3f, w2f, wsem, w1b, w3b, w2b):
    j = pl.program_id(0)
    n_used = nu_ref[0]

    def weight_copies(e, slot):
        return [pltpu.make_async_copy(src.at[e], dst.at[slot], wsem.at[slot])
                for src, dst in ((w1_hbm, w1f), (w3_hbm, w3f), (w2_hbm, w2f))]

    @pl.when(j < n_used)
    def _():
        @pl.when((j == 0) | (te_ref[j] != te_ref[jnp.maximum(j - 1, 0)]))
        def _():
            for parity in (0, 1):
                @pl.when(par_ref[j] == parity)
                def _(parity=parity):
                    if parity == 0:
                        @pl.when(j == 0)
                        def _():
                            for c in weight_copies(te_ref[0], 0):
                                c.start()
                    for c in weight_copies(te_ref[j], parity):
                        c.wait()
                    w1b[...] = w1f[parity].astype(BF16)
                    w3b[...] = w3f[parity].astype(BF16)
                    w2b[...] = w2f[parity].astype(BF16)

                    @pl.when(nxt_ref[j] >= 0)
                    def _():
                        for c in weight_copies(nxt_ref[j], 1 - parity):
                            c.start()

        lo, hi = _unpack_bf16_pairs(_load_row_tiles(x_ref, MOE_TM))
        x = jnp.concatenate([lo.astype(BF16), hi.astype(BF16)], axis=1)
        a = jnp.dot(x, w1b[...], preferred_element_type=F32)
        u = jnp.dot(x, w3b[...], preferred_element_type=F32)
        y = jnp.dot((a * _sigmoid(a) * u).astype(BF16), w2b[...], preferred_element_type=F32)
        _store_row_tiles(y_ref, _pack_bf16_pairs(y))

    @pl.when(j >= n_used)
    def _():
        y_ref[...] = jnp.zeros(y_ref.shape, y_ref.dtype)


def _experts(xs, tile_expert, n_used, next_expert, slot_parity, w1, w3, w2):
    n_rows = xs.shape[0]
    nt = n_rows // MOE_TM
    d, f = w1.shape[1], w1.shape[2]
    blk = (MOE_TM * ROW_SUB, LANES)
    grid_spec = pltpu.PrefetchScalarGridSpec(
        num_scalar_prefetch=4,
        grid=(nt,),
        in_specs=[pl.BlockSpec(blk, lambda j, te, nu, nx, pa: (jnp.minimum(j, nu[0] - 1), 0)),
                  pl.BlockSpec(memory_space=pl.ANY),
                  pl.BlockSpec(memory_space=pl.ANY),
                  pl.BlockSpec(memory_space=pl.ANY)],
        out_specs=pl.BlockSpec(blk, lambda j, te, nu, nx, pa: (j, 0)),
        scratch_shapes=[pltpu.VMEM((2, d, f), F32),
                        pltpu.VMEM((2, d, f), F32),
                        pltpu.VMEM((2, f, d), F32),
                        pltpu.SemaphoreType.DMA((2,)),
                        pltpu.VMEM((d, f), BF16),
                        pltpu.VMEM((d, f), BF16),
                        pltpu.VMEM((f, d), BF16)],
    )
    ys = pl.pallas_call(
        _experts_kernel,
        grid_spec=grid_spec,
        out_shape=jax.ShapeDtypeStruct((n_rows * ROW_SUB, LANES), PACKED),
        compiler_params=_params("arbitrary"),
        name="experts",
    )(tile_expert, n_used, next_expert, slot_parity, xs.reshape(n_rows * ROW_SUB, LANES), w1, w3, w2)
    return ys.reshape(n_rows, ROW_SUB, LANES)


def _combine_kernel(pos_cur_ref, pos_nxt_ref, ys_hbm, base_ref, g2_ref, w_ref, o_ref, buf, sem):
    bi, i = pl.program_id(0), pl.program_id(1)
    step = bi * pl.num_programs(1) + i
    n_steps = pl.num_programs(0) * pl.num_programs(1)
    slot = lax.rem(step, 2)

    def issue(pos_ref, dst_slot):
        def body(i, carry):
            for k in range(TOP_K):
                for u in range(2):
                    r = i * 2 + u
                    pltpu.make_async_copy(ys_hbm.at[pos_ref[0, 0, k * COMB_TM + r]],
                                          buf.at[dst_slot, k, pl.ds(r * ROW_SUB, ROW_SUB), :],
                                          sem.at[dst_slot]).start(priority=u)
            return carry
        lax.fori_loop(0, COMB_TM // 2, body, 0)

    @pl.when(step == 0)
    def _():
        issue(pos_cur_ref, 0)

    for parity in (0, 1):
        @pl.when((step + 1 < n_steps) & (slot == parity))
        def _(parity=parity):
            issue(pos_nxt_ref, 1 - parity)

    pltpu.make_async_copy(buf.at[slot], buf.at[slot], sem.at[slot]).wait()

    w = w_ref[...]
    half = o_ref.shape[2] // 2
    for parity in (0, 1):
        @pl.when(slot == parity)
        def _(parity=parity):
            acc_lo = jnp.zeros((COMB_TM, half), F32)
            acc_hi = jnp.zeros((COMB_TM, half), F32)
            for k in range(TOP_K):
                lo, hi = _unpack_bf16_pairs(_load_row_tiles(buf.at[parity, k], COMB_TM))
                acc_lo = acc_lo + w[:, k:k + 1] * lo
                acc_hi = acc_hi + w[:, k:k + 1] * hi
            o_ref[0] = base_ref[0] + g2_ref[0] * jnp.concatenate([acc_lo, acc_hi], axis=1)


def _combine(ys, pos_t, w_sel, base, g2):
    b, s, d = base.shape
    tm = COMB_TM
    nt = s // tm
    n_tiles = b * nt
    pos_t = _tile_major(pos_t, n_tiles, tm)
    return pl.pallas_call(
        _combine_kernel,
        grid=(b, nt),
        in_specs=[pl.BlockSpec((1, 1, tm * TOP_K), lambda bi, i: (bi * nt + i, 0, 0),
                               memory_space=pltpu.SMEM),
                  pl.BlockSpec((1, 1, tm * TOP_K),
                               lambda bi, i: (jnp.minimum(bi * nt + i + 1, n_tiles - 1), 0, 0),
                               memory_space=pltpu.SMEM),
                  pl.BlockSpec(memory_space=pl.ANY),
                  pl.BlockSpec((1, tm, d), lambda bi, i: (bi, i, 0)),
                  pl.BlockSpec((1, 1, d), lambda bi, i: (bi, 0, 0)),
                  pl.BlockSpec((tm, TOP_K), lambda bi, i: (bi * nt + i, 0))],
        out_specs=pl.BlockSpec((1, tm, d), lambda bi, i: (bi, i, 0)),
        out_shape=jax.ShapeDtypeStruct((b, s, d), F32),
        scratch_shapes=[pltpu.VMEM((2, TOP_K, tm * ROW_SUB, LANES), ys.dtype),
                        pltpu.SemaphoreType.DMA((2,))],
        compiler_params=_params("arbitrary", "arbitrary"),
        name="combine",
    )(pos_t, pos_t, ys, base, g2, w_sel)


def _plan_kernel(te_ref, tri_ref, low_ref, pos_ref, cnt_ref, run_ref, start_ref):
    phase, j = pl.program_id(0), pl.program_id(1)
    tn = te_ref.shape[1]
    te = te_ref[...]
    eid = lax.broadcasted_iota(I32, (N_EXPERTS, tn), 0)
    hot = jnp.zeros((N_EXPERTS, tn), F32)
    for k in range(TOP_K):
        hot = hot + jnp.where(te[k:k + 1, :] == eid, 1.0, 0.0)
    tile_count = jnp.sum(hot, axis=1, keepdims=True)

    @pl.when((phase == 0) & (j == 0))
    def _():
        run_ref[...] = jnp.zeros(run_ref.shape, F32)

    @pl.when((phase == 1) & (j == 0))
    def _():
        counts = run_ref[...]
        cnt_ref[...] = counts
        tiles = jnp.floor((counts + (MOE_TM - 1)) * (1.0 / MOE_TM))
        start_ref[...] = jnp.dot(low_ref[...], tiles.astype(BF16), preferred_element_type=F32) * MOE_TM
        run_ref[...] = jnp.zeros(run_ref.shape, F32)

    @pl.when(phase == 1)
    def _():
        before = jnp.dot(hot.astype(BF16), tri_ref[...], preferred_element_type=F32)
        val = before + (run_ref[:, 0:1] + start_ref[:, 0:1])
        rows = [jnp.sum(jnp.where(te[k:k + 1, :] == eid, val, 0.0), axis=0, keepdims=True)
                for k in range(TOP_K)]
        pos_ref[...] = jnp.concatenate(rows, axis=0).astype(I32)

    run_ref[...] = run_ref[...] + tile_count


def _dispatch_plan(top_e_t):
    n = top_e_t.shape[1]
    tn = PLAN_TN
    n_tiles = n * TOP_K // MOE_TM + N_EXPERTS
    tri = jnp.asarray(np.triu(np.ones((tn, tn), np.float32), 1), BF16)
    low = jnp.asarray(np.tril(np.ones((N_EXPERTS, N_EXPERTS), np.float32), -1), BF16)
    pos_t, cnt = pl.pallas_call(
        _plan_kernel,
        grid=(2, n // tn),
        in_specs=[pl.BlockSpec((TOP_K, tn), lambda ph, j: (0, j)),
                  pl.BlockSpec((tn, tn), lambda ph, j: (0, 0)),
                  pl.BlockSpec((N_EXPERTS, N_EXPERTS), lambda ph, j: (0, 0))],
        out_specs=[pl.BlockSpec((TOP_K, tn), lambda ph, j: (0, j * ph)),
                   pl.BlockSpec((N_EXPERTS, LANES), lambda ph, j: (0, 0))],
        out_shape=[jax.ShapeDtypeStruct((TOP_K, n), I32),
                   jax.ShapeDtypeStruct((N_EXPERTS, LANES), F32)],
        scratch_shapes=[pltpu.VMEM((N_EXPERTS, LANES), F32),
                        pltpu.VMEM((N_EXPERTS, LANES), F32)],
        compiler_params=_params("arbitrary", "arbitrary"),
        name="plan",
    )(top_e_t, tri, low)
    counts = cnt[:, 0].astype(I32)
    tile_end = jnp.cumsum((counts + MOE_TM - 1) // MOE_TM)
    tile_expert = jnp.minimum(
        jnp.sum((tile_end[None, :] <= jnp.arange(n_tiles, dtype=I32)[:, None]).astype(I32), axis=1),
        N_EXPERTS - 1)
    n_used = tile_end[-1:].astype(I32)
    t_ids = jnp.arange(n_tiles, dtype=I32)
    is_last = jnp.any((tile_end[None, :] - 1 == t_ids[:, None]) & (counts[None, :] > 0), axis=1)
    zero_tile = (is_last | (t_ids >= n_used[0])).astype(I32)
    e_ids = jnp.arange(N_EXPERTS, dtype=I32)
    nonempty = counts > 0
    later = lax.cummin(jnp.where(nonempty, e_ids, N_EXPERTS), axis=0, reverse=True)
    next_e = jnp.concatenate([later[1:], jnp.full((1,), N_EXPERTS, I32)])
    next_e = jnp.where(next_e < N_EXPERTS, next_e, -1)
    rank = jnp.cumsum(nonempty.astype(I32)) - nonempty.astype(I32)
    tile_expert = tile_expert.astype(I32)
    return (pos_t, zero_tile, tile_expert, n_used, next_e[tile_expert].astype(I32),
            (rank[tile_expert] % 2).astype(I32), n_tiles * MOE_TM)


def _layer(x, c, rel_bias, norm1_w, norm2_w, w_ada, b_ada, w_in, conv_w, w_conv_out, q_norm_w,
           k_norm_w, idx_k_norm_w, idx_k_norm_b, w_attn_out, w_o, w_router, router_bias,
           w1, w3, w2, ws1, ws3, ws2):
    b, s, d = x.shape
    mod = _mod(c, w_ada, b_ada).reshape(b, 6, 1, d)
    sh1, sc1, g1, sh2, sc2, g2 = [mod[:, m] for m in range(6)]

    cols = [w_in[:, _SEG[name][0]:_SEG[name][1]] for name in _ORDER]
    cols.append(jnp.zeros((d, PROJ_W - sum(col.shape[1] for col in cols)), w_in.dtype))
    w_in_p = jnp.concatenate(cols, axis=1).astype(BF16)

    proj = _proj(x, norm1_w, sc1, sh1, w_in_p)
    qT, qiT, kh, vT, kin, wT = _prep(proj, q_norm_w, k_norm_w, idx_k_norm_w, idx_k_norm_b,
                                     rel_bias[REL_BUCKETS - 1])
    attn = _attention(qT, qiT, wT, kh, vT, kin, _bias_strips(rel_bias))
    mixed = _mix(proj, attn, conv_w, w_conv_out.astype(BF16), w_attn_out.astype(BF16))
    base, h2, logits_t = _post(x, mixed, g1, norm2_w, sc2, sh2, g2, w_o.astype(BF16), w_router.T,
                               ws1.astype(BF16), ws3.astype(BF16), ws2.astype(BF16))
    top_e_t, w_sel_t = _route(logits_t, router_bias)
    pos_t, zero_start, tile_expert, n_used, next_expert, slot_parity, n_rows = _dispatch_plan(top_e_t)
    xs = _dispatch(h2.reshape(b * s, ROW_SUB, LANES), pos_t, zero_start, n_rows)
    ys = _experts(xs, tile_expert, n_used, next_expert, slot_parity, w1, w3, w2)
    return _combine(ys, pos_t, w_sel_t.T, base, g2)


def kernel(x, c, rel_bias, norm1_w, norm2_w, w_ada, b_ada, w_in, conv_w, w_conv_out, q_norm_w,
           k_norm_w, idx_k_norm_w, idx_k_norm_b, w_attn_out, w_o, w_router, router_bias,
           w1, w3, w2, ws1, ws3, ws2):
    assert x.shape[1] % PROJ_TM == 0 and x.shape[2] == D_MODEL and w_ada.shape[0] == 1
    return _layer(x, c, rel_bias, norm1_w[0], norm2_w[0], w_ada[0], b_ada[0], w_in[0], conv_w[0],
                  w_conv_out[0], q_norm_w[0], k_norm_w[0], idx_k_norm_w[0], idx_k_norm_b[0],
                  w_attn_out[0], w_o[0], w_router[0], router_bias[0], w1[0], w3[0], w2[0],
                  ws1[0], ws3[0], ws2[0])
```

```python
import functools
import math

import numpy as np
import jax
import jax.numpy as jnp
from jax import lax
from jax.experimental import pallas as pl
from jax.experimental.pallas import tpu as pltpu

F32 = jnp.float32
BF16 = jnp.bfloat16
I32 = jnp.int32
PACKED = jnp.int32

D_MODEL = 2048
CONV_WIDTH = D_MODEL // 2
CONV_K = 3
N_HEADS = 16
N_KV_HEADS = 4
HEAD_DIM = 64
ATTN_WIDTH = N_HEADS * HEAD_DIM
KV_WIDTH = N_KV_HEADS * HEAD_DIM
IDX_HEADS = 16
IDX_DIM = 64
IDX_TOPK_MAX = 256
REL_BUCKETS = 32
REL_MAX_DIST = 128
N_EXPERTS = 64
N_GROUPS = 8
TOPK_GROUPS = 4
TOP_K = 8
D_EXPERT = 512
ROUTED_SCALE = 2.5
EPS = 1e-6
NEG = -1e30

REP = N_HEADS // N_KV_HEADS

LANES = 128
VMEM_LIMIT = 56 * 1024 * 1024

TQ = 128
KB = 128
KC = 4 * KB
FAR_KC = 4 * KB
PROJ_TM = 1024
PROJ_TN = 768
PREP_TM = 512
MIX_TM = 512
POST_TM = 512
ROUTE_TN = 512
MOE_TM = 512
COMB_TM = 256
DISP_TM = 256
PLAN_TN = 512

ROW_SUB = D_MODEL // 2 // LANES
QK_DIM = 128
V_ROWS = HEAD_DIM + 16
LOG2E = math.log2(math.e)

_SEG = dict(cb=(0, 1024), cc=(1024, 2048), cu=(2048, 3072), q=(3072, 4096), k=(4096, 4352),
            v=(4352, 4608), qi=(4608, 5632), ki=(5632, 5696), wi=(5696, 5712),
            ga=(5712, 7760), gb=(7760, 9808))
_ORDER = ["ga", "gb", "cb", "cc", "cu", "q", "qi", "k", "v", "ki", "wi"]
PROJ_W = 9984
COL_GA, COL_GB = 0, 1
COL_CB, COL_CC, COL_CU, COL_Q, COL_QI = 4, 5, 6, 7, 8
COL_K, COL_V = 36, 37
COL_KW = 76

INT_MIN = -(2 ** 31)
INT_MAX = 2 ** 31 - 1


def _sortable_key_of(x):
    bits = int(np.float32(x).view(np.int32))
    return bits ^ 0x7FFFFFFF if bits < 0 else bits


NEG_KEY = _sortable_key_of(NEG)


def _sigmoid(x):
    return 1.0 / (1.0 + jnp.exp(-x))


def _pack_bf16_pairs(x):
    half = x.shape[1] // 2
    lo = lax.bitcast_convert_type(x[:, :half].astype(BF16).astype(F32), PACKED)
    hi = lax.bitcast_convert_type(x[:, half:].astype(BF16).astype(F32), PACKED)
    return lax.shift_right_logical(lo, jnp.full_like(lo, 16)) | (hi & jnp.int32(-65536))


def _unpack_bf16_pairs(w):
    lo = lax.bitcast_convert_type(w << 16, F32)
    hi = lax.bitcast_convert_type(w & jnp.int32(-65536), F32)
    return lo, hi


def _store_row_tiles(ref, words):
    m = words.shape[0]
    for sl in range(ROW_SUB):
        ref[pl.ds(sl, m, stride=ROW_SUB), :] = words[:, sl * LANES:(sl + 1) * LANES]


def _load_row_tiles(ref, m):
    return jnp.concatenate([ref[pl.ds(sl, m, stride=ROW_SUB), :] for sl in range(ROW_SUB)], axis=1)


def _params(*sem):
    return pltpu.CompilerParams(dimension_semantics=sem, vmem_limit_bytes=VMEM_LIMIT)


def _mod_kernel(c_ref, w_ref, b_ref, o_ref):
    c = c_ref[...]
    s = (c * _sigmoid(c)).astype(BF16)
    o_ref[...] = jnp.dot(s, w_ref[...].astype(BF16), preferred_element_type=F32) + b_ref[...]


def _mod(c, w_ada, b_ada):
    b = c.shape[0]
    rows = 8
    cp = jnp.pad(c, ((0, rows - b), (0, 0)))
    n = w_ada.shape[1]
    tn = 1024
    out = pl.pallas_call(
        _mod_kernel,
        grid=(n // tn,),
        in_specs=[pl.BlockSpec((rows, D_MODEL), lambda j: (0, 0)),
                  pl.BlockSpec((D_MODEL, tn), lambda j: (0, j)),
                  pl.BlockSpec((1, tn), lambda j: (0, j))],
        out_specs=pl.BlockSpec((rows, tn), lambda j: (0, j)),
        out_shape=jax.ShapeDtypeStruct((rows, n), F32),
        compiler_params=_params("arbitrary"),
        name="mod",
    )(cp, w_ada, b_ada.reshape(1, n))
    return out[:b]


def _column_spans():
    spans, dst = [], 0
    for name in _ORDER:
        lo, hi = _SEG[name]
        if spans and spans[-1][1] == lo:
            spans[-1][1] = hi
        else:
            spans.append([lo, hi, dst])
        dst += hi - lo
    return spans, dst


def _wprep_kernel(w_ref, o_ref):
    spans, used = _column_spans()
    for lo, hi, dst in spans:
        o_ref[:, dst:dst + hi - lo] = w_ref[:, lo:hi].astype(BF16)
    o_ref[:, used:] = jnp.zeros((o_ref.shape[0], o_ref.shape[1] - used), BF16)


def _relayout_w_in(w_in):
    d, n_in = w_in.shape
    tr = 256
    return pl.pallas_call(
        _wprep_kernel,
        grid=(d // tr,),
        in_specs=[pl.BlockSpec((tr, n_in), lambda i: (i, 0))],
        out_specs=pl.BlockSpec((tr, PROJ_W), lambda i: (i, 0)),
        out_shape=jax.ShapeDtypeStruct((d, PROJ_W), BF16),
        compiler_params=_params("arbitrary"),
        name="wprep",
    )(w_in)


def _proj_kernel(x_ref, nw_ref, sc_ref, sh_ref, w_ref, o_ref, h_ref):
    @pl.when(pl.program_id(2) == 0)
    def _():
        x = x_ref[0]
        ms = jnp.mean(x * x, axis=-1, keepdims=True)
        y = x * lax.rsqrt(ms + EPS) * nw_ref[...]
        h_ref[...] = (y * (1.0 + sc_ref[0]) + sh_ref[0]).astype(BF16)

    o_ref[0] = jnp.dot(h_ref[...], w_ref[...], preferred_element_type=F32).astype(BF16)


def _proj(x, norm_w, sc, sh, w_in_p):
    b, s, d = x.shape
    tm, tn = PROJ_TM, PROJ_TN
    return pl.pallas_call(
        _proj_kernel,
        grid=(b, s // tm, PROJ_W // tn),
        in_specs=[pl.BlockSpec((1, tm, d), lambda bi, i, j: (bi, i, 0)),
                  pl.BlockSpec((1, d), lambda bi, i, j: (0, 0)),
                  pl.BlockSpec((1, 1, d), lambda bi, i, j: (bi, 0, 0)),
                  pl.BlockSpec((1, 1, d), lambda bi, i, j: (bi, 0, 0)),
                  pl.BlockSpec((d, tn), lambda bi, i, j: (0, j))],
        out_specs=pl.BlockSpec((1, tm, tn), lambda bi, i, j: (bi, i, j)),
        out_shape=jax.ShapeDtypeStruct((b, s, PROJ_W), BF16),
        scratch_shapes=[pltpu.VMEM((tm, d), BF16)],
        compiler_params=_params("arbitrary", "arbitrary", "arbitrary"),
        name="proj",
    )(x, norm_w.reshape(1, d), sc, sh, w_in_p)


def _prep_kernel(q_ref, qi_ref, k_ref, v_ref, kw_ref, qnw_ref, knw_ref, inw_ref, inb_ref, qtail_ref,
                 qT_ref, qiT_ref, kh_ref, vT_ref, kin_ref, wT_ref):
    tm = q_ref.shape[1]
    nqb = tm // TQ

    q3 = q_ref[0].astype(F32).T.reshape(N_HEADS, HEAD_DIM, tm)
    ms = jnp.mean(q3 * q3, axis=1, keepdims=True)
    qn = q3 * lax.rsqrt(ms + EPS) * (qnw_ref[...] * (HEAD_DIM ** -0.5 * LOG2E))
    qi3 = qi_ref[0].astype(F32).T.reshape(IDX_HEADS, IDX_DIM, tm)
    for jb in range(nqb):
        for h in range(N_HEADS):
            g, r = divmod(h, REP)
            qT_ref[0, jb, g, :HEAD_DIM, r * TQ:(r + 1) * TQ] = qn[h, :, jb * TQ:(jb + 1) * TQ].astype(BF16)
        for g in range(N_KV_HEADS):
            qT_ref[0, jb, g, HEAD_DIM:, :] = qtail_ref[g]
        for h in range(IDX_HEADS):
            qiT_ref[0, jb, :, h * TQ:(h + 1) * TQ] = qi3[h, :, jb * TQ:(jb + 1) * TQ].astype(BF16)

    k = k_ref[0].astype(F32)
    ones_cols = jnp.where(lax.broadcasted_iota(I32, (tm, QK_DIM - HEAD_DIM), 1) < 2, 1.0, 0.0)
    for g in range(N_KV_HEADS):
        kg = k[:, g * HEAD_DIM:(g + 1) * HEAD_DIM]
        msk = jnp.mean(kg * kg, axis=-1, keepdims=True)
        kn = kg * lax.rsqrt(msk + EPS) * knw_ref[...]
        kh_ref[0, g] = jnp.concatenate([kn, ones_cols], axis=1).astype(BF16)

    v3 = v_ref[0].astype(F32).T.reshape(N_KV_HEADS, HEAD_DIM, tm)
    ones_rows = jnp.where(lax.broadcasted_iota(I32, (V_ROWS - HEAD_DIM, KB), 0) == 0, 1.0, 0.0)
    for g in range(N_KV_HEADS):
        for jb in range(tm // KB):
            vT_ref[0, g, jb] = jnp.concatenate(
                [v3[g, :, jb * KB:(jb + 1) * KB], ones_rows], axis=0).astype(BF16)

    kw = kw_ref[0].astype(F32)
    ki = kw[:, :IDX_DIM]
    mu = jnp.mean(ki, axis=-1, keepdims=True)
    var = jnp.mean(jnp.square(ki - mu), axis=-1, keepdims=True)
    kin_ref[0] = ((ki - mu) * lax.rsqrt(var + EPS) * inw_ref[...] + inb_ref[...]).astype(BF16)
    wiT = kw.T[IDX_DIM:IDX_DIM + IDX_HEADS] * (IDX_HEADS ** -0.5 * IDX_DIM ** -0.5)
    for jb in range(nqb):
        wT_ref[0, jb] = wiT[:, jb * TQ:(jb + 1) * TQ]


def _prep(proj, q_norm_w, k_norm_w, idx_k_norm_w, idx_k_norm_b, far_bias):
    b, s, _ = proj.shape
    tm = PREP_TM
    nqb = tm // TQ
    nq = s // TQ
    fb2 = (far_bias * LOG2E).reshape(N_KV_HEADS, REP)
    hi = fb2.astype(BF16)
    lo = (fb2 - hi.astype(F32)).astype(BF16)
    tail = jnp.stack([hi, lo], axis=1)
    tail = jnp.broadcast_to(tail[..., None], (N_KV_HEADS, 2, REP, TQ)).reshape(N_KV_HEADS, 2, REP * TQ)
    qtail = jnp.pad(tail, ((0, 0), (0, QK_DIM - HEAD_DIM - 2), (0, 0)))
    return pl.pallas_call(
        _prep_kernel,
        grid=(b, s // tm),
        in_specs=[pl.BlockSpec((1, tm, ATTN_WIDTH), lambda bi, i: (bi, i, COL_Q)),
                  pl.BlockSpec((1, tm, IDX_HEADS * IDX_DIM), lambda bi, i: (bi, i, COL_QI)),
                  pl.BlockSpec((1, tm, KV_WIDTH), lambda bi, i: (bi, i, COL_K)),
                  pl.BlockSpec((1, tm, KV_WIDTH), lambda bi, i: (bi, i, COL_V)),
                  pl.BlockSpec((1, tm, LANES), lambda bi, i: (bi, i, COL_KW)),
                  pl.BlockSpec((1, HEAD_DIM, 1), lambda bi, i: (0, 0, 0)),
                  pl.BlockSpec((1, HEAD_DIM), lambda bi, i: (0, 0)),
                  pl.BlockSpec((1, IDX_DIM), lambda bi, i: (0, 0)),
                  pl.BlockSpec((1, IDX_DIM), lambda bi, i: (0, 0)),
                  pl.BlockSpec((N_KV_HEADS, QK_DIM - HEAD_DIM, REP * TQ), lambda bi, i: (0, 0, 0))],
        out_specs=[pl.BlockSpec((1, nqb, N_KV_HEADS, QK_DIM, REP * TQ), lambda bi, i: (bi, i, 0, 0, 0)),
                   pl.BlockSpec((1, nqb, IDX_DIM, IDX_HEADS * TQ), lambda bi, i: (bi, i, 0, 0)),
                   pl.BlockSpec((1, N_KV_HEADS, tm, QK_DIM), lambda bi, i: (bi, 0, i, 0)),
                   pl.BlockSpec((1, N_KV_HEADS, tm // KB, V_ROWS, KB), lambda bi, i: (bi, 0, i, 0, 0)),
                   pl.BlockSpec((1, tm, IDX_DIM), lambda bi, i: (bi, i, 0)),
                   pl.BlockSpec((1, nqb, IDX_HEADS, TQ), lambda bi, i: (bi, i, 0, 0))],
        out_shape=[jax.ShapeDtypeStruct((b, nq, N_KV_HEADS, QK_DIM, REP * TQ), BF16),
                   jax.ShapeDtypeStruct((b, nq, IDX_DIM, IDX_HEADS * TQ), BF16),
                   jax.ShapeDtypeStruct((b, N_KV_HEADS, s, QK_DIM), BF16),
                   jax.ShapeDtypeStruct((b, N_KV_HEADS, s // KB, V_ROWS, KB), BF16),
                   jax.ShapeDtypeStruct((b, s, IDX_DIM), BF16),
                   jax.ShapeDtypeStruct((b, nq, IDX_HEADS, TQ), F32)],
        compiler_params=_params("arbitrary", "arbitrary"),
        name="prep",
    )(proj, proj, proj, proj, proj,
      q_norm_w.reshape(1, HEAD_DIM, 1), k_norm_w.reshape(1, HEAD_DIM),
      idx_k_norm_w.reshape(1, IDX_DIM), idx_k_norm_b.reshape(1, IDX_DIM), qtail)


def _t5_bucket_np(n):
    n = np.maximum(n, 0)
    max_exact = REL_BUCKETS // 2
    nf = np.maximum(n, 1).astype(np.float64)
    large = max_exact + np.floor(np.log(nf / max_exact) / math.log(REL_MAX_DIST / max_exact)
                                 * (REL_BUCKETS - max_exact)).astype(np.int64)
    large = np.minimum(large, REL_BUCKETS - 1)
    return np.where(n < max_exact, n, large).astype(np.int32)


def _bias_kernel(rb_ref, bucket_ref, o_ref):
    h = pl.program_id(0)
    bucket = bucket_ref[...]
    acc = jnp.zeros(bucket.shape, F32)
    for bkt in range(REL_BUCKETS):
        acc = jnp.where(bucket == bkt, rb_ref[bkt, h], acc)
    o_ref[0] = (acc - rb_ref[REL_BUCKETS - 1, h]) * LOG2E


def _bias_strips(rel_bias):
    kk = np.arange(3 * TQ)[:, None]
    qq = np.arange(TQ)[None, :]
    bucket = jnp.asarray(_t5_bucket_np(qq + TQ - kk))
    return pl.pallas_call(
        _bias_kernel,
        grid=(N_HEADS,),
        in_specs=[pl.BlockSpec(memory_space=pltpu.SMEM),
                  pl.BlockSpec((3 * TQ, TQ), lambda h: (0, 0))],
        out_specs=pl.BlockSpec((1, 3 * TQ, TQ), lambda h: (h, 0, 0)),
        out_shape=jax.ShapeDtypeStruct((N_HEADS, 3 * TQ, TQ), F32),
        compiler_params=_params("arbitrary"),
        name="bias",
    )(rel_bias, bucket)


def _attn_kernel(qT_ref, qiT_ref, wT_ref, kh_ref, vT_ref, kin_ref, biasT_ref, o_ref,
                 keys_ref, am_ref, amf_ref, p_ref, m_ref, acc_ref, sa_ref, sb_ref, *, n_sel):
    i = pl.program_id(1)
    seq = kin_ref.shape[1]
    t0 = i * TQ
    n_chunks = lax.shift_right_logical(i + 4, 2)
    q_pos = t0 + lax.broadcasted_iota(I32, (KB, TQ), 1)
    k_off = lax.broadcasted_iota(I32, (KB, TQ), 0)

    qiT = qiT_ref[0, 0]
    wT = wT_ref[0, 0]

    def score_chunk(c, carry):
        k0 = pl.multiple_of(c * KC, KC)
        d = jnp.dot(kin_ref[0, pl.ds(k0, KC), :], qiT, preferred_element_type=F32)
        acc = jnp.zeros((KC, TQ), F32)
        for h in range(IDX_HEADS):
            acc = acc + wT[h:h + 1, :] * jnp.maximum(d[:, h * TQ:(h + 1) * TQ], 0.0)
        for j in range(KC // KB):
            blk = c * (KC // KB) + j
            sc = jnp.where(blk * KB + k_off <= q_pos, acc[j * KB:(j + 1) * KB], NEG)
            bits = lax.bitcast_convert_type(sc, I32)
            keys_ref[blk] = jnp.where(bits < 0, bits ^ 0x7FFFFFFF, bits)
        return carry

    lax.fori_loop(0, n_chunks, score_chunk, 0)

    n_virtual = (seq - n_chunks * KC).astype(F32)

    def count(pred):
        def body(c, acc):
            for j in range(KC // KB):
                blk = c * (KC // KB) + j
                hit = jnp.where(pred(keys_ref[blk], blk), 1.0, 0.0)
                acc = acc + jnp.sum(hit.reshape(KB // 8, 8, TQ), axis=0)
            return acc
        acc = lax.fori_loop(0, n_chunks, body, jnp.zeros((8, TQ), F32))
        return jnp.sum(acc, axis=0, keepdims=True)

    def bit_body(it, carry):
        thr, cnt_ge = carry
        cand = thr + lax.shift_left(jnp.int32(1), 31 - it)
        cnt = count(lambda kb, blk: kb >= cand) + jnp.where(NEG_KEY >= cand, n_virtual, 0.0)
        accept = cnt >= n_sel
        return jnp.where(accept, cand, thr), jnp.where(accept, cnt, cnt_ge)

    thr, cnt_ge = lax.fori_loop(
        0, 32, bit_body, (jnp.full((1, TQ), INT_MIN, I32), jnp.full((1, TQ), float(seq), F32)))

    cnt_gt = count(lambda kb, blk: kb > thr) + jnp.where(NEG_KEY > thr, n_virtual, 0.0)
    cnt_eq = cnt_ge - cnt_gt
    need = n_sel - cnt_gt
    p_ref[...] = jnp.full((8, TQ), INT_MAX, I32)
    has_tie = jnp.max(jnp.where(cnt_eq > need, 1.0, 0.0)) > 0.0

    @pl.when(has_tie)
    def _():
        idx_bits = int(seq).bit_length()

        def p_body(it, p):
            cand = p | lax.shift_left(jnp.int32(1), idx_bits - 1 - it)
            below = count(lambda kb, blk: (kb == thr) & (blk * KB + k_off < cand))
            return jnp.where(below < need, cand, p)

        p = lax.fori_loop(0, idx_bits, p_body, jnp.zeros((1, TQ), I32))
        p_ref[...] = jnp.broadcast_to(p, (8, TQ))

    p_last = p_ref[0:1, :]

    bw = jnp.maximum(i - 1, 0)
    ws = pl.multiple_of(bw * KB, KB)

    def mask_chunk(c, carry):
        for j in range(KC // KB):
            blk = c * (KC // KB) + j
            kb = keys_ref[blk]
            k_pos = blk * KB + k_off
            sel = (kb > thr) | ((kb == thr) & (k_pos <= p_last))
            v = jnp.where(sel & (k_pos <= q_pos), 0.0, NEG)
            am_ref[blk] = v
            amf_ref[blk] = jnp.where(k_pos < ws, v, NEG)
        return carry

    lax.fori_loop(0, n_chunks, mask_chunk, 0)

    off = pl.multiple_of(TQ - (t0 - ws), TQ)
    fb = FAR_KC // KB
    n_far = (bw + fb - 1) // fb

    def qk(f, dst):
        k0 = pl.multiple_of(f * FAR_KC, FAR_KC)
        for g in range(N_KV_HEADS):
            dst[g] = jnp.dot(kh_ref[0, g, pl.ds(k0, FAR_KC), :], qT_ref[0, 0, g],
                             preferred_element_type=F32)

    for g in range(N_KV_HEADS):
        sb_ref[g, :2 * KB] = jnp.dot(kh_ref[0, g, pl.ds(ws, 2 * KB), :], qT_ref[0, 0, g],
                                     preferred_element_type=F32)
    qk(0, sa_ref)
    am_near = jnp.concatenate([am_ref[bw], am_ref[bw + 1]], axis=0)
    for g in range(N_KV_HEADS):
        s = sb_ref[g, :2 * KB]
        s = jnp.concatenate(
            [s[:, r * TQ:(r + 1) * TQ] + (biasT_ref[REP * g + r, pl.ds(off, 2 * KB), :] + am_near)
             for r in range(REP)], axis=1)
        m = jnp.max(s, axis=0, keepdims=True)
        pb = jnp.exp2(s - m).astype(BF16)
        m_ref[g] = m
        acc_ref[g] = (jnp.dot(vT_ref[0, g, bw], pb[:KB], preferred_element_type=F32)
                      + jnp.dot(vT_ref[0, g, bw + 1], pb[KB:], preferred_element_type=F32))

    def softmax_pv(f, src):
        amf = jnp.concatenate([amf_ref[f * fb + j] for j in range(fb)], axis=0)
        for g in range(N_KV_HEADS):
            s = src[g]
            s = jnp.concatenate([s[:, r * TQ:(r + 1) * TQ] + amf for r in range(REP)], axis=1)
            m_old = m_ref[g]
            m_new = jnp.maximum(m_old, jnp.max(s, axis=0, keepdims=True))
            pb = jnp.exp2(s - m_new).astype(BF16)
            vc = jnp.concatenate([vT_ref[0, g, f * fb + j] for j in range(fb)], axis=1)
            m_ref[g] = m_new
            acc_ref[g] = (jnp.exp2(m_old - m_new) * acc_ref[g]
                          + jnp.dot(vc, pb, preferred_element_type=F32))

    def pair_body(pf, carry):
        f0 = 2 * pf
        qk(f0 + 1, sb_ref)
        softmax_pv(f0, sa_ref)
        qk(jnp.minimum(f0 + 2, n_far - 1), sa_ref)
        softmax_pv(f0 + 1, sb_ref)
        return carry

    lax.fori_loop(0, n_far // 2, pair_body, 0)

    @pl.when(n_far % 2 == 1)
    def _():
        softmax_pv(n_far - 1, sa_ref)

    outs = []
    for g in range(N_KV_HEADS):
        og = acc_ref[g, :HEAD_DIM] / acc_ref[g, HEAD_DIM:HEAD_DIM + 1]
        outs.extend(og[:, r * TQ:(r + 1) * TQ] for r in range(REP))
    o_ref[0] = jnp.concatenate(outs, axis=0).T.astype(BF16)


def _attention(qT, qiT, wT, kh, vT, kin, bias_strips):
    b, nq = qT.shape[0], qT.shape[1]
    s = kin.shape[1]
    n_sel = min(IDX_TOPK_MAX, s // 4)
    nb = s // KB
    return pl.pallas_call(
        functools.partial(_attn_kernel, n_sel=n_sel),
        grid=(b, nq),
        in_specs=[pl.BlockSpec((1, 1, N_KV_HEADS, QK_DIM, REP * TQ), lambda bi, i: (bi, i, 0, 0, 0)),
                  pl.BlockSpec((1, 1, IDX_DIM, IDX_HEADS * TQ), lambda bi, i: (bi, i, 0, 0)),
                  pl.BlockSpec((1, 1, IDX_HEADS, TQ), lambda bi, i: (bi, i, 0, 0)),
                  pl.BlockSpec((1, N_KV_HEADS, s, QK_DIM), lambda bi, i: (bi, 0, 0, 0)),
                  pl.BlockSpec((1, N_KV_HEADS, nb, V_ROWS, KB), lambda bi, i: (bi, 0, 0, 0, 0)),
                  pl.BlockSpec((1, s, IDX_DIM), lambda bi, i: (bi, 0, 0)),
                  pl.BlockSpec((N_HEADS, 3 * TQ, TQ), lambda bi, i: (0, 0, 0))],
        out_specs=pl.BlockSpec((1, TQ, ATTN_WIDTH), lambda bi, i: (bi, i, 0)),
        out_shape=jax.ShapeDtypeStruct((b, s, ATTN_WIDTH), BF16),
        scratch_shapes=[pltpu.VMEM((nb, KB, TQ), I32),
                        pltpu.VMEM((nb, KB, TQ), F32),
                        pltpu.VMEM((nb, KB, TQ), F32),
                        pltpu.VMEM((8, TQ), I32),
                        pltpu.VMEM((N_KV_HEADS, 1, REP * TQ), F32),
                        pltpu.VMEM((N_KV_HEADS, V_ROWS, REP * TQ), F32),
                        pltpu.VMEM((N_KV_HEADS, FAR_KC, REP * TQ), F32),
                        pltpu.VMEM((N_KV_HEADS, FAR_KC, REP * TQ), F32)],
        compiler_params=_params("arbitrary", "arbitrary"),
        name="attn",
    )(qT, qiT, wT, kh, vT, kin, bias_strips)


HALO = 16


def _mix_kernel(cb_ref, cc_ref, cu_ref, ccp_ref, cup_ref, at_ref, ga_ref, gb_ref,
                cw_ref, wco_ref, wao_ref, o_ref):
    tm = cb_ref.shape[1]
    v = cc_ref[0].astype(F32) * cu_ref[0].astype(F32)
    first = pl.program_id(1) == 0
    hv = ccp_ref[0].astype(F32) * cup_ref[0].astype(F32)
    hv = jnp.where(first, 0.0, hv)
    row = lax.broadcasted_iota(I32, v.shape, 0)
    v1 = jnp.where(row == 0, hv[HALO - 1:HALO], pltpu.roll(v, 1, 0))
    v2 = pltpu.roll(v, 2, 0)
    v2 = jnp.where(row == 0, hv[HALO - 2:HALO - 1], jnp.where(row == 1, hv[HALO - 1:HALO], v2))
    y = cw_ref[0:1] * v2 + cw_ref[1:2] * v1 + cw_ref[2:3] * v
    yc = (cb_ref[0].astype(F32) * y).astype(BF16)
    y_conv = jnp.dot(yc, wco_ref[...], preferred_element_type=F32)
    y_attn = jnp.dot(at_ref[0], wao_ref[...], preferred_element_type=F32)
    mixed = _sigmoid(ga_ref[0].astype(F32)) * y_conv + _sigmoid(gb_ref[0].astype(F32)) * y_attn
    o_ref[0] = mixed.astype(BF16)


def _mix(proj, attn, conv_w, w_conv_out_b, w_attn_out_b):
    b, s, _ = proj.shape
    tm = MIX_TM
    hb = tm // HALO
    prev = lambda col: (lambda bi, i: (bi, jnp.maximum(i * hb - 1, 0), col))
    return pl.pallas_call(
        _mix_kernel,
        grid=(b, s // tm),
        in_specs=[pl.BlockSpec((1, tm, CONV_WIDTH), lambda bi, i: (bi, i, COL_CB)),
                  pl.BlockSpec((1, tm, CONV_WIDTH), lambda bi, i: (bi, i, COL_CC)),
                  pl.BlockSpec((1, tm, CONV_WIDTH), lambda bi, i: (bi, i, COL_CU)),
                  pl.BlockSpec((1, HALO, CONV_WIDTH), prev(COL_CC)),
                  pl.BlockSpec((1, HALO, CONV_WIDTH), prev(COL_CU)),
                  pl.BlockSpec((1, tm, ATTN_WIDTH), lambda bi, i: (bi, i, 0)),
                  pl.BlockSpec((1, tm, D_MODEL), lambda bi, i: (bi, i, COL_GA)),
                  pl.BlockSpec((1, tm, D_MODEL), lambda bi, i: (bi, i, COL_GB)),
                  pl.BlockSpec((8, CONV_WIDTH), lambda bi, i: (0, 0)),
                  pl.BlockSpec((CONV_WIDTH, D_MODEL), lambda bi, i: (0, 0)),
                  pl.BlockSpec((ATTN_WIDTH, D_MODEL), lambda bi, i: (0, 0))],
        out_specs=pl.BlockSpec((1, tm, D_MODEL), lambda bi, i: (bi, i, 0)),
        out_shape=jax.ShapeDtypeStruct((b, s, D_MODEL), BF16),
        compiler_params=_params("arbitrary", "arbitrary"),
        name="mix",
    )(proj, proj, proj, proj, proj, attn, proj, proj,
      jnp.pad(conv_w, ((0, 8 - CONV_K), (0, 0))), w_conv_out_b, w_attn_out_b)


def _post_kernel(x_ref, mx_ref, g1_ref, nw_ref, sc_ref, sh_ref, g2_ref, wo_ref, wrT_ref,
                 ws1_ref, ws3_ref, ws2_ref, base_ref, h2_ref, lg_ref):
    x1 = x_ref[0] + g1_ref[0] * jnp.dot(mx_ref[0], wo_ref[...], preferred_element_type=F32)
    ms = jnp.mean(x1 * x1, axis=-1, keepdims=True)
    h2 = x1 * lax.rsqrt(ms + EPS) * nw_ref[...] * (1.0 + sc_ref[0]) + sh_ref[0]
    _store_row_tiles(h2_ref, _pack_bf16_pairs(h2))
    lg_ref[...] = lax.dot_general(wrT_ref[...], h2, (((1,), (1,)), ((), ())),
                                  precision=lax.Precision.HIGHEST, preferred_element_type=F32)
    hb = h2.astype(BF16)
    a = jnp.dot(hb, ws1_ref[...], preferred_element_type=F32)
    u = jnp.dot(hb, ws3_ref[...], preferred_element_type=F32)
    shared = jnp.dot((a * _sigmoid(a) * u).astype(BF16), ws2_ref[...], preferred_element_type=F32)
    base_ref[0] = x1 + g2_ref[0] * shared


def _post(x, mixed, g1, norm_w, sc, sh, g2, w_o_b, w_router_t, ws1_b, ws3_b, ws2_b):
    b, s, d = x.shape
    tm = POST_TM
    nt = s // tm
    vec = pl.BlockSpec((1, 1, d), lambda bi, i: (bi, 0, 0))
    const = lambda shape: pl.BlockSpec(shape, lambda bi, i: (0,) * len(shape))
    return pl.pallas_call(
        _post_kernel,
        grid=(b, nt),
        in_specs=[pl.BlockSpec((1, tm, d), lambda bi, i: (bi, i, 0)),
                  pl.BlockSpec((1, tm, d), lambda bi, i: (bi, i, 0)),
                  vec, const((1, d)), vec, vec, vec,
                  const((d, d)), const((N_EXPERTS, d)),
                  const((d, D_EXPERT)), const((d, D_EXPERT)), const((D_EXPERT, d))],
        out_specs=[pl.BlockSpec((1, tm, d), lambda bi, i: (bi, i, 0)),
                   pl.BlockSpec((tm * ROW_SUB, LANES), lambda bi, i: (bi * nt + i, 0)),
                   pl.BlockSpec((N_EXPERTS, tm), lambda bi, i: (0, bi * nt + i))],
        out_shape=[jax.ShapeDtypeStruct((b, s, d), F32),
                   jax.ShapeDtypeStruct((b * s * ROW_SUB, LANES), PACKED),
                   jax.ShapeDtypeStruct((N_EXPERTS, b * s), F32)],
        compiler_params=_params("arbitrary", "arbitrary"),
        name="post",
    )(x, mixed, g1, norm_w.reshape(1, d), sc, sh, g2, w_o_b, w_router_t, ws1_b, ws3_b, ws2_b)


def _first_max(cur, ids, sentinel):
    m = jnp.max(cur, axis=0, keepdims=True)
    first = jnp.min(jnp.where(cur == m, ids, sentinel), axis=0, keepdims=True)
    return m, first


def _route_kernel(lg_ref, rb_ref, idx_ref, w_ref):
    tn = lg_ref.shape[1]
    gsz = N_EXPERTS // N_GROUPS
    scores = _sigmoid(lg_ref[...])
    sel = scores + rb_ref[...]
    sub = lax.broadcasted_iota(I32, (gsz, tn), 0).astype(F32)

    gs = []
    for g in range(N_GROUPS):
        v = sel[g * gsz:(g + 1) * gsz]
        m1, first = _first_max(v, sub, float(gsz))
        m2 = jnp.max(jnp.where(sub == first, -jnp.inf, v), axis=0, keepdims=True)
        gs.append(m1 + m2)
    cur = jnp.concatenate(gs, axis=0)
    gid = lax.broadcasted_iota(I32, (N_GROUPS, tn), 0).astype(F32)
    keep = jnp.zeros((N_GROUPS, tn), F32)
    for _ in range(TOPK_GROUPS):
        _, first = _first_max(cur, gid, float(N_GROUPS))
        hit = gid == first
        keep = jnp.where(hit, 1.0, keep)
        cur = jnp.where(hit, -jnp.inf, cur)

    cur = jnp.concatenate(
        [jnp.where(keep[g:g + 1] > 0.0, sel[g * gsz:(g + 1) * gsz], NEG) for g in range(N_GROUPS)],
        axis=0)
    eid = lax.broadcasted_iota(I32, (N_EXPERTS, tn), 0).astype(F32)
    ids, ws = [], []
    for _ in range(TOP_K):
        _, first = _first_max(cur, eid, float(N_EXPERTS))
        hit = eid == first
        ids.append(first)
        ws.append(jnp.sum(jnp.where(hit, scores, 0.0), axis=0, keepdims=True))
        cur = jnp.where(hit, -jnp.inf, cur)
    w = jnp.concatenate(ws, axis=0)
    idx_ref[...] = jnp.concatenate(ids, axis=0).astype(I32)
    w_ref[...] = w / jnp.sum(w, axis=0, keepdims=True) * ROUTED_SCALE


def _route(logits_t, router_bias):
    e, n = logits_t.shape
    tn = ROUTE_TN
    return pl.pallas_call(
        _route_kernel,
        grid=(n // tn,),
        in_specs=[pl.BlockSpec((e, tn), lambda j: (0, j)),
                  pl.BlockSpec((e, 1), lambda j: (0, 0))],
        out_specs=[pl.BlockSpec((TOP_K, tn), lambda j: (0, j)),
                   pl.BlockSpec((TOP_K, tn), lambda j: (0, j))],
        out_shape=[jax.ShapeDtypeStruct((TOP_K, n), I32),
                   jax.ShapeDtypeStruct((TOP_K, n), F32)],
        compiler_params=_params("arbitrary"),
        name="route",
    )(logits_t, router_bias.reshape(e, 1))


def _tile_major(a_t, n_tiles, tm):
    return a_t.reshape(TOP_K, n_tiles, tm).transpose(1, 0, 2).reshape(n_tiles, 1, TOP_K * tm)


def _dispatch_kernel(zs_ref, pos_ref, x_ref, xs_hbm, zbuf, sem):
    step = pl.program_id(0)

    @pl.when(step == 0)
    def _():
        zbuf[...] = jnp.zeros(zbuf.shape, zbuf.dtype)

        def zero_copy(t):
            start = pl.multiple_of(t * MOE_TM, MOE_TM)
            return pltpu.make_async_copy(zbuf, xs_hbm.at[pl.ds(start, MOE_TM)], sem.at[1])

        def start_body(t, carry):
            @pl.when(zs_ref[t] != 0)
            def _():
                zero_copy(t).start()
            return carry

        def wait_body(t, carry):
            @pl.when(zs_ref[t] != 0)
            def _():
                zero_copy(t).wait()
            return carry

        lax.fori_loop(0, zs_ref.shape[0], start_body, 0)
        lax.fori_loop(0, zs_ref.shape[0], wait_body, 0)

    def row_copy(k, r):
        return pltpu.make_async_copy(x_ref.at[r], xs_hbm.at[pos_ref[0, 0, k * DISP_TM + r]], sem.at[0])

    def body(i, carry):
        for k in range(TOP_K):
            for u in range(2):
                row_copy(k, i * 2 + u).start(priority=u)
        return carry

    lax.fori_loop(0, DISP_TM // 2, body, 0)
    for _ in range(TOP_K):
        pltpu.make_async_copy(x_ref, x_ref, sem.at[0]).wait()


def _dispatch(h2, pos_t, zero_start, n_rows):
    n = h2.shape[0]
    tm = DISP_TM
    n_tiles = n // tm
    grid_spec = pltpu.PrefetchScalarGridSpec(
        num_scalar_prefetch=1,
        grid=(n_tiles,),
        in_specs=[pl.BlockSpec((1, 1, tm * TOP_K), lambda t, zs: (t, 0, 0), memory_space=pltpu.SMEM),
                  pl.BlockSpec((tm, ROW_SUB, LANES), lambda t, zs: (t, 0, 0))],
        out_specs=pl.BlockSpec(memory_space=pl.ANY),
        scratch_shapes=[pltpu.VMEM((MOE_TM, ROW_SUB, LANES), h2.dtype),
                        pltpu.SemaphoreType.DMA((2,))],
    )
    return pl.pallas_call(
        _dispatch_kernel,
        grid_spec=grid_spec,
        out_shape=jax.ShapeDtypeStruct((n_rows, ROW_SUB, LANES), h2.dtype),
        compiler_params=_params("arbitrary"),
        name="dispatch",
    )(zero_start, _tile_major(pos_t, n_tiles, tm), h2)


def _experts_kernel(te_ref, nu_ref, nxt_ref, par_ref, x_ref, w1_hbm, w3_hbm, w2_hbm, y_ref,
                    w1f, w3f, w2f, wsem, w1b, w3b, w2b):
    j = pl.program_id(0)
    n_used = nu_ref[0]

    def weight_copies(e, slot):
        return [pltpu.make_async_copy(src.at[e], dst.at[slot], wsem.at[slot])
                for src, dst in ((w1_hbm, w1f), (w3_hbm, w3f), (w2_hbm, w2f))]

    @pl.when(j < n_used)
    def _():
        @pl.when((j == 0) | (te_ref[j] != te_ref[jnp.maximum(j - 1, 0)]))
        def _():
            for parity in (0, 1):
                @pl.when(par_ref[j] == parity)
                def _(parity=parity):
                    if parity == 0:
                        @pl.when(j == 0)
                        def _():
                            for c in weight_copies(te_ref[0], 0):
                                c.start()
                    for c in weight_copies(te_ref[j], parity):
                        c.wait()
                    w1b[...] = w1f[parity].astype(BF16)
                    w3b[...] = w3f[parity].astype(BF16)
                    w2b[...] = w2f[parity].astype(BF16)

                    @pl.when(nxt_ref[j] >= 0)
                    def _():
                        for c in weight_copies(nxt_ref[j], 1 - parity):
                            c.start()

        lo, hi = _unpack_bf16_pairs(_load_row_tiles(x_ref, MOE_TM))
        x = jnp.concatenate([lo.astype(BF16), hi.astype(BF16)], axis=1)
        a = jnp.dot(x, w1b[...], preferred_element_type=F32)
        u = jnp.dot(x, w3b[...], preferred_element_type=F32)
        y = jnp.dot((a * _sigmoid(a) * u).astype(BF16), w2b[...], preferred_element_type=F32)
        _store_row_tiles(y_ref, _pack_bf16_pairs(y))

    @pl.when(j >= n_used)
    def _():
        y_ref[...] = jnp.zeros(y_ref.shape, y_ref.dtype)


def _experts(xs, tile_expert, n_used, next_expert, slot_parity, w1, w3, w2):
    n_rows = xs.shape[0]
    nt = n_rows // MOE_TM
    d, f = w1.shape[1], w1.shape[2]
    blk = (MOE_TM * ROW_SUB, LANES)
    grid_spec = pltpu.PrefetchScalarGridSpec(
        num_scalar_prefetch=4,
        grid=(nt,),
        in_specs=[pl.BlockSpec(blk, lambda j, te, nu, nx, pa: (jnp.minimum(j, nu[0] - 1), 0)),
                  pl.BlockSpec(memory_space=pl.ANY),
                  pl.BlockSpec(memory_space=pl.ANY),
                  pl.BlockSpec(memory_space=pl.ANY)],
        out_specs=pl.BlockSpec(blk, lambda j, te, nu, nx, pa: (j, 0)),
        scratch_shapes=[pltpu.VMEM((2, d, f), F32),
                        pltpu.VMEM((2, d, f), F32),
                        pltpu.VMEM((2, f, d), F32),
                        pltpu.SemaphoreType.DMA((2,)),
                        pltpu.VMEM((d, f), BF16),
                        pltpu.VMEM((d, f), BF16),
                        pltpu.VMEM((f, d), BF16)],
    )
    ys = pl.pallas_call(
        _experts_kernel,
        grid_spec=grid_spec,
        out_shape=jax.ShapeDtypeStruct((n_rows * ROW_SUB, LANES), PACKED),
        compiler_params=_params("arbitrary"),
        name="experts",
    )(tile_expert, n_used, next_expert, slot_parity, xs.reshape(n_rows * ROW_SUB, LANES), w1, w3, w2)
    return ys.reshape(n_rows, ROW_SUB, LANES)


def _combine_kernel(pos_cur_ref, pos_nxt_ref, ys_hbm, base_ref, g2_ref, w_ref, o_ref, buf, sem):
    bi, i = pl.program_id(0), pl.program_id(1)
    step = bi * pl.num_programs(1) + i
    n_steps = pl.num_programs(0) * pl.num_programs(1)
    slot = lax.rem(step, 2)

    def issue(pos_ref, dst_slot):
        def body(i, carry):
            for k in range(TOP_K):
                for u in range(2):
                    r = i * 2 + u
                    pltpu.make_async_copy(ys_hbm.at[pos_ref[0, 0, k * COMB_TM + r]],
                                          buf.at[dst_slot, k, pl.ds(r * ROW_SUB, ROW_SUB), :],
                                          sem.at[dst_slot]).start(priority=u)
            return carry
        lax.fori_loop(0, COMB_TM // 2, body, 0)

    @pl.when(step == 0)
    def _():
        issue(pos_cur_ref, 0)

    for parity in (0, 1):
        @pl.when((step + 1 < n_steps) & (slot == parity))
        def _(parity=parity):
            issue(pos_nxt_ref, 1 - parity)

    pltpu.make_async_copy(buf.at[slot], buf.at[slot], sem.at[slot]).wait()

    w = w_ref[...]
    half = o_ref.shape[2] // 2
    for parity in (0, 1):
        @pl.when(slot == parity)
        def _(parity=parity):
            acc_lo = jnp.zeros((COMB_TM, half), F32)
            acc_hi = jnp.zeros((COMB_TM, half), F32)
            for k in range(TOP_K):
                lo, hi = _unpack_bf16_pairs(_load_row_tiles(buf.at[parity, k], COMB_TM))
                acc_lo = acc_lo + w[:, k:k + 1] * lo
                acc_hi = acc_hi + w[:, k:k + 1] * hi
            o_ref[0] = base_ref[0] + g2_ref[0] * jnp.concatenate([acc_lo, acc_hi], axis=1)


def _combine(ys, pos_t, w_sel, base, g2):
    b, s, d = base.shape
    tm = COMB_TM
    nt = s // tm
    n_tiles = b * nt
    pos_t = _tile_major(pos_t, n_tiles, tm)
    return pl.pallas_call(
        _combine_kernel,
        grid=(b, nt),
        in_specs=[pl.BlockSpec((1, 1, tm * TOP_K), lambda bi, i: (bi * nt + i, 0, 0),
                               memory_space=pltpu.SMEM),
                  pl.BlockSpec((1, 1, tm * TOP_K),
                               lambda bi, i: (jnp.minimum(bi * nt + i + 1, n_tiles - 1), 0, 0),
                               memory_space=pltpu.SMEM),
                  pl.BlockSpec(memory_space=pl.ANY),
                  pl.BlockSpec((1, tm, d), lambda bi, i: (bi, i, 0)),
                  pl.BlockSpec((1, 1, d), lambda bi, i: (bi, 0, 0)),
                  pl.BlockSpec((tm, TOP_K), lambda bi, i: (bi * nt + i, 0))],
        out_specs=pl.BlockSpec((1, tm, d), lambda bi, i: (bi, i, 0)),
        out_shape=jax.ShapeDtypeStruct((b, s, d), F32),
        scratch_shapes=[pltpu.VMEM((2, TOP_K, tm * ROW_SUB, LANES), ys.dtype),
                        pltpu.SemaphoreType.DMA((2,))],
        compiler_params=_params("arbitrary", "arbitrary"),
        name="combine",
    )(pos_t, pos_t, ys, base, g2, w_sel)


def _plan_kernel(te_ref, tri_ref, low_ref, pos_ref, cnt_ref, run_ref, start_ref):
    phase, j = pl.program_id(0), pl.program_id(1)
    tn = te_ref.shape[1]
    te = te_ref[...]
    eid = lax.broadcasted_iota(I32, (N_EXPERTS, tn), 0)
    hot = jnp.zeros((N_EXPERTS, tn), F32)
    for k in range(TOP_K):
        hot = hot + jnp.where(te[k:k + 1, :] == eid, 1.0, 0.0)
    tile_count = jnp.sum(hot, axis=1, keepdims=True)

    @pl.when((phase == 0) & (j == 0))
    def _():
        run_ref[...] = jnp.zeros(run_ref.shape, F32)

    @pl.when((phase == 1) & (j == 0))
    def _():
        counts = run_ref[...]
        cnt_ref[...] = counts
        tiles = jnp.floor((counts + (MOE_TM - 1)) * (1.0 / MOE_TM))
        start_ref[...] = jnp.dot(low_ref[...], tiles.astype(BF16), preferred_element_type=F32) * MOE_TM
        run_ref[...] = jnp.zeros(run_ref.shape, F32)

    @pl.when(phase == 1)
    def _():
        before = jnp.dot(hot.astype(BF16), tri_ref[...], preferred_element_type=F32)
        val = before + (run_ref[:, 0:1] + start_ref[:, 0:1])
        rows = [jnp.sum(jnp.where(te[k:k + 1, :] == eid, val, 0.0), axis=0, keepdims=True)
                for k in range(TOP_K)]
        pos_ref[...] = jnp.concatenate(rows, axis=0).astype(I32)

    run_ref[...] = run_ref[...] + tile_count


def _dispatch_plan(top_e_t):
    n = top_e_t.shape[1]
    tn = PLAN_TN
    n_tiles = n * TOP_K // MOE_TM + N_EXPERTS
    tri = jnp.asarray(np.triu(np.ones((tn, tn), np.float32), 1), BF16)
    low = jnp.asarray(np.tril(np.ones((N_EXPERTS, N_EXPERTS), np.float32), -1), BF16)
    pos_t, cnt = pl.pallas_call(
        _plan_kernel,
        grid=(2, n // tn),
        in_specs=[pl.BlockSpec((TOP_K, tn), lambda ph, j: (0, j)),
                  pl.BlockSpec((tn, tn), lambda ph, j: (0, 0)),
                  pl.BlockSpec((N_EXPERTS, N_EXPERTS), lambda ph, j: (0, 0))],
        out_specs=[pl.BlockSpec((TOP_K, tn), lambda ph, j: (0, j * ph)),
                   pl.BlockSpec((N_EXPERTS, LANES), lambda ph, j: (0, 0))],
        out_shape=[jax.ShapeDtypeStruct((TOP_K, n), I32),
                   jax.ShapeDtypeStruct((N_EXPERTS, LANES), F32)],
        scratch_shapes=[pltpu.VMEM((N_EXPERTS, LANES), F32),
                        pltpu.VMEM((N_EXPERTS, LANES), F32)],
        compiler_params=_params("arbitrary", "arbitrary"),
        name="plan",
    )(top_e_t, tri, low)
    counts = cnt[:, 0].astype(I32)
    tile_end = jnp.cumsum((counts + MOE_TM - 1) // MOE_TM)
    tile_expert = jnp.minimum(
        jnp.sum((tile_end[None, :] <= jnp.arange(n_tiles, dtype=I32)[:, None]).astype(I32), axis=1),
        N_EXPERTS - 1)
    n_used = tile_end[-1:].astype(I32)
    t_ids = jnp.arange(n_tiles, dtype=I32)
    is_last = jnp.any((tile_end[None, :] - 1 == t_ids[:, None]) & (counts[None, :] > 0), axis=1)
    zero_tile = (is_last | (t_ids >= n_used[0])).astype(I32)
    e_ids = jnp.arange(N_EXPERTS, dtype=I32)
    nonempty = counts > 0
    later = lax.cummin(jnp.where(nonempty, e_ids, N_EXPERTS), axis=0, reverse=True)
    next_e = jnp.concatenate([later[1:], jnp.full((1,), N_EXPERTS, I32)])
    next_e = jnp.where(next_e < N_EXPERTS, next_e, -1)
    rank = jnp.cumsum(nonempty.astype(I32)) - nonempty.astype(I32)
    tile_expert = tile_expert.astype(I32)
    return (pos_t, zero_tile, tile_expert, n_used, next_e[tile_expert].astype(I32),
            (rank[tile_expert] % 2).astype(I32), n_tiles * MOE_TM)


def _layer(x, c, rel_bias, norm1_w, norm2_w, w_ada, b_ada, w_in, conv_w, w_conv_out, q_norm_w,
           k_norm_w, idx_k_norm_w, idx_k_norm_b, w_attn_out, w_o, w_router, router_bias,
           w1, w3, w2, ws1, ws3, ws2):
    b, s, d = x.shape
    mod = _mod(c, w_ada, b_ada).reshape(b, 6, 1, d)
    sh1, sc1, g1, sh2, sc2, g2 = [mod[:, m] for m in range(6)]

    proj = _proj(x, norm1_w, sc1, sh1, _relayout_w_in(w_in))
    qT, qiT, kh, vT, kin, wT = _prep(proj, q_norm_w, k_norm_w, idx_k_norm_w, idx_k_norm_b,
                                     rel_bias[REL_BUCKETS - 1])
    attn = _attention(qT, qiT, wT, kh, vT, kin, _bias_strips(rel_bias))
    mixed = _mix(proj, attn, conv_w, w_conv_out.astype(BF16), w_attn_out.astype(BF16))
    base, h2, logits_t = _post(x, mixed, g1, norm2_w, sc2, sh2, g2, w_o.astype(BF16), w_router.T,
                               ws1.astype(BF16), ws3.astype(BF16), ws2.astype(BF16))
    top_e_t, w_sel_t = _route(logits_t, router_bias)
    pos_t, zero_start, tile_expert, n_used, next_expert, slot_parity, n_rows = _dispatch_plan(top_e_t)
    xs = _dispatch(h2.reshape(b * s, ROW_SUB, LANES), pos_t, zero_start, n_rows)
    ys = _experts(xs, tile_expert, n_used, next_expert, slot_parity, w1, w3, w2)
    return _combine(ys, pos_t, w_sel_t.T, base, g2)


def kernel(x, c, rel_bias, norm1_w, norm2_w, w_ada, b_ada, w_in, conv_w, w_conv_out, q_norm_w,
           k_norm_w, idx_k_norm_w, idx_k_norm_b, w_attn_out, w_o, w_router, router_bias,
           w1, w3, w2, ws1, ws3, ws2):
    assert x.shape[1] % PROJ_TM == 0 and x.shape[2] == D_MODEL and w_ada.shape[0] == 1
    return _layer(x, c, rel_bias, norm1_w[0], norm2_w[0], w_ada[0], b_ada[0], w_in[0], conv_w[0],
                  w_conv_out[0], q_norm_w[0], k_norm_w[0], idx_k_norm_w[0], idx_k_norm_b[0],
                  w_attn_out[0], w_o[0], w_router[0], router_bias[0], w1[0], w3[0], w2[0],
                  ws1[0], ws3[0], ws2[0])
```

```python
import functools
import math

import numpy as np
import jax
import jax.numpy as jnp
from jax import lax
from jax.experimental import pallas as pl
from jax.experimental.pallas import tpu as pltpu

F32 = jnp.float32
BF16 = jnp.bfloat16
I32 = jnp.int32
PACKED = jnp.int32

D_MODEL = 2048
CONV_WIDTH = D_MODEL // 2
CONV_K = 3
N_HEADS = 16
N_KV_HEADS = 4
HEAD_DIM = 64
ATTN_WIDTH = N_HEADS * HEAD_DIM
KV_WIDTH = N_KV_HEADS * HEAD_DIM
IDX_HEADS = 16
IDX_DIM = 64
IDX_TOPK_MAX = 256
REL_BUCKETS = 32
REL_MAX_DIST = 128
N_EXPERTS = 64
N_GROUPS = 8
TOPK_GROUPS = 4
TOP_K = 8
D_EXPERT = 512
ROUTED_SCALE = 2.5
EPS = 1e-6
NEG = -1e30

REP = N_HEADS // N_KV_HEADS

LANES = 128
VMEM_LIMIT = 56 * 1024 * 1024

TQ = 128
KB = 128
KC = 4 * KB
FAR_KC = 4 * KB
PROJ_TM = 1024
PROJ_TN = 768
PREP_TM = 512
MIX_TM = 512
POST_TM = 512
ROUTE_TN = 512
MOE_TM = 512
COMB_TM = 128
DISP_TM = 256
PLAN_TN = 512

ROW_SUB = D_MODEL // 2 // LANES
QK_DIM = 128
V_ROWS = HEAD_DIM + 16
LOG2E = math.log2(math.e)

_SEG = dict(cb=(0, 1024), cc=(1024, 2048), cu=(2048, 3072), q=(3072, 4096), k=(4096, 4352),
            v=(4352, 4608), qi=(4608, 5632), ki=(5632, 5696), wi=(5696, 5712),
            ga=(5712, 7760), gb=(7760, 9808))
_ORDER = ["ga", "gb", "cb", "cc", "cu", "q", "qi", "k", "v", "ki", "wi"]
PROJ_W = 9984
COL_GA, COL_GB = 0, 1
COL_CB, COL_CC, COL_CU, COL_Q, COL_QI = 4, 5, 6, 7, 8
COL_K, COL_V = 36, 37
COL_KW = 76

INT_MIN = -(2 ** 31)
INT_MAX = 2 ** 31 - 1


def _sortable_key_of(x):
    bits = int(np.float32(x).view(np.int32))
    return bits ^ 0x7FFFFFFF if bits < 0 else bits


NEG_KEY = _sortable_key_of(NEG)


def _sigmoid(x):
    return 1.0 / (1.0 + jnp.exp(-x))


def _pack_bf16_pairs(x):
    half = x.shape[1] // 2
    lo = lax.bitcast_convert_type(x[:, :half].astype(BF16).astype(F32), PACKED)
    hi = lax.bitcast_convert_type(x[:, half:].astype(BF16).astype(F32), PACKED)
    return lax.shift_right_logical(lo, jnp.full_like(lo, 16)) | (hi & jnp.int32(-65536))


def _unpack_bf16_pairs(w):
    lo = lax.bitcast_convert_type(w << 16, F32)
    hi = lax.bitcast_convert_type(w & jnp.int32(-65536), F32)
    return lo, hi


def _store_row_tiles(ref, words):
    m = words.shape[0]
    for sl in range(ROW_SUB):
        ref[pl.ds(sl, m, stride=ROW_SUB), :] = words[:, sl * LANES:(sl + 1) * LANES]


def _load_row_tiles(ref, m):
    return jnp.concatenate([ref[pl.ds(sl, m, stride=ROW_SUB), :] for sl in range(ROW_SUB)], axis=1)


def _params(*sem):
    return pltpu.CompilerParams(dimension_semantics=sem, vmem_limit_bytes=VMEM_LIMIT)


def _mod_kernel(c_ref, w_ref, b_ref, o_ref):
    c = c_ref[...]
    s = (c * _sigmoid(c)).astype(BF16)
    o_ref[...] = jnp.dot(s, w_ref[...].astype(BF16), preferred_element_type=F32) + b_ref[...]


def _mod(c, w_ada, b_ada):
    b = c.shape[0]
    rows = 8
    cp = jnp.pad(c, ((0, rows - b), (0, 0)))
    n = w_ada.shape[1]
    tn = 1024
    out = pl.pallas_call(
        _mod_kernel,
        grid=(n // tn,),
        in_specs=[pl.BlockSpec((rows, D_MODEL), lambda j: (0, 0)),
                  pl.BlockSpec((D_MODEL, tn), lambda j: (0, j)),
                  pl.BlockSpec((1, tn), lambda j: (0, j))],
        out_specs=pl.BlockSpec((rows, tn), lambda j: (0, j)),
        out_shape=jax.ShapeDtypeStruct((rows, n), F32),
        compiler_params=_params("arbitrary"),
        name="mod",
    )(cp, w_ada, b_ada.reshape(1, n))
    return out[:b]


def _column_spans():
    spans, dst = [], 0
    for name in _ORDER:
        lo, hi = _SEG[name]
        if spans and spans[-1][1] == lo:
            spans[-1][1] = hi
        else:
            spans.append([lo, hi, dst])
        dst += hi - lo
    return spans, dst


def _wprep_kernel(w_ref, o_ref):
    spans, used = _column_spans()
    for lo, hi, dst in spans:
        o_ref[:, dst:dst + hi - lo] = w_ref[0, :, lo:hi].astype(BF16)
    o_ref[:, used:] = jnp.zeros((o_ref.shape[0], o_ref.shape[1] - used), BF16)


def _relayout_w_in(w_in):
    _, d, n_in = w_in.shape
    tr = 256
    return pl.pallas_call(
        _wprep_kernel,
        grid=(d // tr,),
        in_specs=[pl.BlockSpec((1, tr, n_in), lambda i: (0, i, 0))],
        out_specs=pl.BlockSpec((tr, PROJ_W), lambda i: (i, 0)),
        out_shape=jax.ShapeDtypeStruct((d, PROJ_W), BF16),
        compiler_params=_params("arbitrary"),
        name="wprep",
    )(w_in)


def _proj_kernel(x_ref, nw_ref, sc_ref, sh_ref, w_ref, o_ref, h_ref):
    @pl.when(pl.program_id(2) == 0)
    def _():
        x = x_ref[0]
        ms = jnp.mean(x * x, axis=-1, keepdims=True)
        y = x * lax.rsqrt(ms + EPS) * nw_ref[...]
        h_ref[...] = (y * (1.0 + sc_ref[0]) + sh_ref[0]).astype(BF16)

    o_ref[0] = jnp.dot(h_ref[...], w_ref[...], preferred_element_type=F32).astype(BF16)


def _proj(x, norm_w, sc, sh, w_in_p):
    b, s, d = x.shape
    tm, tn = PROJ_TM, PROJ_TN
    return pl.pallas_call(
        _proj_kernel,
        grid=(b, s // tm, PROJ_W // tn),
        in_specs=[pl.BlockSpec((1, tm, d), lambda bi, i, j: (bi, i, 0)),
                  pl.BlockSpec((1, d), lambda bi, i, j: (0, 0)),
                  pl.BlockSpec((1, 1, d), lambda bi, i, j: (bi, 0, 0)),
                  pl.BlockSpec((1, 1, d), lambda bi, i, j: (bi, 0, 0)),
                  pl.BlockSpec((d, tn), lambda bi, i, j: (0, j))],
        out_specs=pl.BlockSpec((1, tm, tn), lambda bi, i, j: (bi, i, j)),
        out_shape=jax.ShapeDtypeStruct((b, s, PROJ_W), BF16),
        scratch_shapes=[pltpu.VMEM((tm, d), BF16)],
        compiler_params=_params("arbitrary", "arbitrary", "arbitrary"),
        name="proj",
    )(x, norm_w.reshape(1, d), sc, sh, w_in_p)


def _prep_kernel(q_ref, qi_ref, k_ref, v_ref, kw_ref, qnw_ref, knw_ref, inw_ref, inb_ref, qtail_ref,
                 qT_ref, qiT_ref, kh_ref, vT_ref, kin_ref, wT_ref):
    tm = q_ref.shape[1]
    nqb = tm // TQ

    q3 = q_ref[0].astype(F32).T.reshape(N_HEADS, HEAD_DIM, tm)
    ms = jnp.mean(q3 * q3, axis=1, keepdims=True)
    qn = q3 * lax.rsqrt(ms + EPS) * (qnw_ref[...] * (HEAD_DIM ** -0.5 * LOG2E))
    qi3 = qi_ref[0].astype(F32).T.reshape(IDX_HEADS, IDX_DIM, tm)
    for jb in range(nqb):
        for h in range(N_HEADS):
            g, r = divmod(h, REP)
            qT_ref[0, jb, g, :HEAD_DIM, r * TQ:(r + 1) * TQ] = qn[h, :, jb * TQ:(jb + 1) * TQ].astype(BF16)
        for g in range(N_KV_HEADS):
            qT_ref[0, jb, g, HEAD_DIM:, :] = qtail_ref[g]
        for h in range(IDX_HEADS):
            qiT_ref[0, jb, :, h * TQ:(h + 1) * TQ] = qi3[h, :, jb * TQ:(jb + 1) * TQ].astype(BF16)

    k = k_ref[0].astype(F32)
    ones_cols = jnp.where(lax.broadcasted_iota(I32, (tm, QK_DIM - HEAD_DIM), 1) < 2, 1.0, 0.0)
    for g in range(N_KV_HEADS):
        kg = k[:, g * HEAD_DIM:(g + 1) * HEAD_DIM]
        msk = jnp.mean(kg * kg, axis=-1, keepdims=True)
        kn = kg * lax.rsqrt(msk + EPS) * knw_ref[...]
        kh_ref[0, g] = jnp.concatenate([kn, ones_cols], axis=1).astype(BF16)

    v3 = v_ref[0].astype(F32).T.reshape(N_KV_HEADS, HEAD_DIM, tm)
    ones_rows = jnp.where(lax.broadcasted_iota(I32, (V_ROWS - HEAD_DIM, KB), 0) == 0, 1.0, 0.0)
    for g in range(N_KV_HEADS):
        for jb in range(tm // KB):
            vT_ref[0, g, jb] = jnp.concatenate(
                [v3[g, :, jb * KB:(jb + 1) * KB], ones_rows], axis=0).astype(BF16)

    kw = kw_ref[0].astype(F32)
    ki = kw[:, :IDX_DIM]
    mu = jnp.mean(ki, axis=-1, keepdims=True)
    var = jnp.mean(jnp.square(ki - mu), axis=-1, keepdims=True)
    kin_ref[0] = ((ki - mu) * lax.rsqrt(var + EPS) * inw_ref[...] + inb_ref[...]).astype(BF16)
    wiT = kw.T[IDX_DIM:IDX_DIM + IDX_HEADS] * (IDX_HEADS ** -0.5 * IDX_DIM ** -0.5)
    for jb in range(nqb):
        wT_ref[0, jb] = wiT[:, jb * TQ:(jb + 1) * TQ]


def _prep(proj, q_norm_w, k_norm_w, idx_k_norm_w, idx_k_norm_b, far_bias):
    b, s, _ = proj.shape
    tm = PREP_TM
    nqb = tm // TQ
    nq = s // TQ
    fb2 = (far_bias * LOG2E).reshape(N_KV_HEADS, REP)
    hi = fb2.astype(BF16)
    lo = (fb2 - hi.astype(F32)).astype(BF16)
    tail = jnp.stack([hi, lo], axis=1)
    tail = jnp.broadcast_to(tail[..., None], (N_KV_HEADS, 2, REP, TQ)).reshape(N_KV_HEADS, 2, REP * TQ)
    qtail = jnp.pad(tail, ((0, 0), (0, QK_DIM - HEAD_DIM - 2), (0, 0)))
    return pl.pallas_call(
        _prep_kernel,
        grid=(b, s // tm),
        in_specs=[pl.BlockSpec((1, tm, ATTN_WIDTH), lambda bi, i: (bi, i, COL_Q)),
                  pl.BlockSpec((1, tm, IDX_HEADS * IDX_DIM), lambda bi, i: (bi, i, COL_QI)),
                  pl.BlockSpec((1, tm, KV_WIDTH), lambda bi, i: (bi, i, COL_K)),
                  pl.BlockSpec((1, tm, KV_WIDTH), lambda bi, i: (bi, i, COL_V)),
                  pl.BlockSpec((1, tm, LANES), lambda bi, i: (bi, i, COL_KW)),
                  pl.BlockSpec((1, HEAD_DIM, 1), lambda bi, i: (0, 0, 0)),
                  pl.BlockSpec((1, HEAD_DIM), lambda bi, i: (0, 0)),
                  pl.BlockSpec((1, IDX_DIM), lambda bi, i: (0, 0)),
                  pl.BlockSpec((1, IDX_DIM), lambda bi, i: (0, 0)),
                  pl.BlockSpec((N_KV_HEADS, QK_DIM - HEAD_DIM, REP * TQ), lambda bi, i: (0, 0, 0))],
        out_specs=[pl.BlockSpec((1, nqb, N_KV_HEADS, QK_DIM, REP * TQ), lambda bi, i: (bi, i, 0, 0, 0)),
                   pl.BlockSpec((1, nqb, IDX_DIM, IDX_HEADS * TQ), lambda bi, i: (bi, i, 0, 0)),
                   pl.BlockSpec((1, N_KV_HEADS, tm, QK_DIM), lambda bi, i: (bi, 0, i, 0)),
                   pl.BlockSpec((1, N_KV_HEADS, tm // KB, V_ROWS, KB), lambda bi, i: (bi, 0, i, 0, 0)),
                   pl.BlockSpec((1, tm, IDX_DIM), lambda bi, i: (bi, i, 0)),
                   pl.BlockSpec((1, nqb, IDX_HEADS, TQ), lambda bi, i: (bi, i, 0, 0))],
        out_shape=[jax.ShapeDtypeStruct((b, nq, N_KV_HEADS, QK_DIM, REP * TQ), BF16),
                   jax.ShapeDtypeStruct((b, nq, IDX_DIM, IDX_HEADS * TQ), BF16),
                   jax.ShapeDtypeStruct((b, N_KV_HEADS, s, QK_DIM), BF16),
                   jax.ShapeDtypeStruct((b, N_KV_HEADS, s // KB, V_ROWS, KB), BF16),
                   jax.ShapeDtypeStruct((b, s, IDX_DIM), BF16),
                   jax.ShapeDtypeStruct((b, nq, IDX_HEADS, TQ), F32)],
        compiler_params=_params("arbitrary", "arbitrary"),
        name="prep",
    )(proj, proj, proj, proj, proj,
      q_norm_w.reshape(1, HEAD_DIM, 1), k_norm_w.reshape(1, HEAD_DIM),
      idx_k_norm_w.reshape(1, IDX_DIM), idx_k_norm_b.reshape(1, IDX_DIM), qtail)


def _t5_bucket_np(n):
    n = np.maximum(n, 0)
    max_exact = REL_BUCKETS // 2
    nf = np.maximum(n, 1).astype(np.float64)
    large = max_exact + np.floor(np.log(nf / max_exact) / math.log(REL_MAX_DIST / max_exact)
                                 * (REL_BUCKETS - max_exact)).astype(np.int64)
    large = np.minimum(large, REL_BUCKETS - 1)
    return np.where(n < max_exact, n, large).astype(np.int32)


def _bias_kernel(rb_ref, bucket_ref, o_ref):
    h = pl.program_id(0)
    bucket = bucket_ref[...]
    acc = jnp.zeros(bucket.shape, F32)
    for bkt in range(REL_BUCKETS):
        acc = jnp.where(bucket == bkt, rb_ref[bkt, h], acc)
    o_ref[0] = (acc - rb_ref[REL_BUCKETS - 1, h]) * LOG2E


def _bias_strips(rel_bias):
    kk = np.arange(3 * TQ)[:, None]
    qq = np.arange(TQ)[None, :]
    bucket = jnp.asarray(_t5_bucket_np(qq + TQ - kk))
    return pl.pallas_call(
        _bias_kernel,
        grid=(N_HEADS,),
        in_specs=[pl.BlockSpec(memory_space=pltpu.SMEM),
                  pl.BlockSpec((3 * TQ, TQ), lambda h: (0, 0))],
        out_specs=pl.BlockSpec((1, 3 * TQ, TQ), lambda h: (h, 0, 0)),
        out_shape=jax.ShapeDtypeStruct((N_HEADS, 3 * TQ, TQ), F32),
        compiler_params=_params("arbitrary"),
        name="bias",
    )(rel_bias, bucket)


def _attn_kernel(qT_ref, qiT_ref, wT_ref, kh_ref, vT_ref, kin_ref, biasT_ref, o_ref,
                 keys_ref, am_ref, amf_ref, p_ref, m_ref, acc_ref, sa_ref, sb_ref, *, n_sel):
    i = pl.program_id(1)
    seq = kin_ref.shape[1]
    t0 = i * TQ
    n_chunks = lax.shift_right_logical(i + 4, 2)
    q_pos = t0 + lax.broadcasted_iota(I32, (KB, TQ), 1)
    k_off = lax.broadcasted_iota(I32, (KB, TQ), 0)

    qiT = qiT_ref[0, 0]
    wT = wT_ref[0, 0]

    def score_chunk(c, carry):
        k0 = pl.multiple_of(c * KC, KC)
        d = jnp.dot(kin_ref[0, pl.ds(k0, KC), :], qiT, preferred_element_type=F32)
        acc = jnp.zeros((KC, TQ), F32)
        for h in range(IDX_HEADS):
            acc = acc + wT[h:h + 1, :] * jnp.maximum(d[:, h * TQ:(h + 1) * TQ], 0.0)
        for j in range(KC // KB):
            blk = c * (KC // KB) + j
            sc = jnp.where(blk * KB + k_off <= q_pos, acc[j * KB:(j + 1) * KB], NEG)
            bits = lax.bitcast_convert_type(sc, I32)
            keys_ref[blk] = jnp.where(bits < 0, bits ^ 0x7FFFFFFF, bits)
        return carry

    lax.fori_loop(0, n_chunks, score_chunk, 0)

    n_virtual = (seq - n_chunks * KC).astype(F32)

    def count(pred):
        def body(c, acc):
            for j in range(KC // KB):
                blk = c * (KC // KB) + j
                hit = jnp.where(pred(keys_ref[blk], blk), 1.0, 0.0)
                acc = acc + jnp.sum(hit.reshape(KB // 8, 8, TQ), axis=0)
            return acc
        acc = lax.fori_loop(0, n_chunks, body, jnp.zeros((8, TQ), F32))
        return jnp.sum(acc, axis=0, keepdims=True)

    def bit_body(it, carry):
        thr, cnt_ge = carry
        cand = thr + lax.shift_left(jnp.int32(1), 31 - it)
        cnt = count(lambda kb, blk: kb >= cand) + jnp.where(NEG_KEY >= cand, n_virtual, 0.0)
        accept = cnt >= n_sel
        return jnp.where(accept, cand, thr), jnp.where(accept, cnt, cnt_ge)

    thr, cnt_ge = lax.fori_loop(
        0, 32, bit_body, (jnp.full((1, TQ), INT_MIN, I32), jnp.full((1, TQ), float(seq), F32)))

    cnt_gt = count(lambda kb, blk: kb > thr) + jnp.where(NEG_KEY > thr, n_virtual, 0.0)
    cnt_eq = cnt_ge - cnt_gt
    need = n_sel - cnt_gt
    p_ref[...] = jnp.full((8, TQ), INT_MAX, I32)
    has_tie = jnp.max(jnp.where(cnt_eq > need, 1.0, 0.0)) > 0.0

    @pl.when(has_tie)
    def _():
        idx_bits = int(seq).bit_length()

        def p_body(it, p):
            cand = p | lax.shift_left(jnp.int32(1), idx_bits - 1 - it)
            below = count(lambda kb, blk: (kb == thr) & (blk * KB + k_off < cand))
            return jnp.where(below < need, cand, p)

        p = lax.fori_loop(0, idx_bits, p_body, jnp.zeros((1, TQ), I32))
        p_ref[...] = jnp.broadcast_to(p, (8, TQ))

    p_last = p_ref[0:1, :]

    bw = jnp.maximum(i - 1, 0)
    ws = pl.multiple_of(bw * KB, KB)

    def mask_chunk(c, carry):
        for j in range(KC // KB):
            blk = c * (KC // KB) + j
            kb = keys_ref[blk]
            k_pos = blk * KB + k_off
            sel = (kb > thr) | ((kb == thr) & (k_pos <= p_last))
            v = jnp.where(sel & (k_pos <= q_pos), 0.0, NEG)
            am_ref[blk] = v
            amf_ref[blk] = jnp.where(k_pos < ws, v, NEG)
        return carry

    lax.fori_loop(0, n_chunks, mask_chunk, 0)

    off = pl.multiple_of(TQ - (t0 - ws), TQ)
    fb = FAR_KC // KB
    n_far = (bw + fb - 1) // fb

    def qk(f, dst):
        k0 = pl.multiple_of(f * FAR_KC, FAR_KC)
        for g in range(N_KV_HEADS):
            dst[g] = jnp.dot(kh_ref[0, g, pl.ds(k0, FAR_KC), :], qT_ref[0, 0, g],
                             preferred_element_type=F32)

    for g in range(N_KV_HEADS):
        sb_ref[g, :2 * KB] = jnp.dot(kh_ref[0, g, pl.ds(ws, 2 * KB), :], qT_ref[0, 0, g],
                                     preferred_element_type=F32)
    qk(0, sa_ref)
    am_near = jnp.concatenate([am_ref[bw], am_ref[bw + 1]], axis=0)
    for g in range(N_KV_HEADS):
        s = sb_ref[g, :2 * KB]
        s = jnp.concatenate(
            [s[:, r * TQ:(r + 1) * TQ] + (biasT_ref[REP * g + r, pl.ds(off, 2 * KB), :] + am_near)
             for r in range(REP)], axis=1)
        m = jnp.max(s, axis=0, keepdims=True)
        pb = jnp.exp2(s - m).astype(BF16)
        m_ref[g] = m
        acc_ref[g] = (jnp.dot(vT_ref[0, g, bw], pb[:KB], preferred_element_type=F32)
                      + jnp.dot(vT_ref[0, g, bw + 1], pb[KB:], preferred_element_type=F32))

    def softmax_pv(f, src):
        amf = jnp.concatenate([amf_ref[f * fb + j] for j in range(fb)], axis=0)
        for g in range(N_KV_HEADS):
            s = src[g]
            s = jnp.concatenate([s[:, r * TQ:(r + 1) * TQ] + amf for r in range(REP)], axis=1)
            m_old = m_ref[g]
            m_new = jnp.maximum(m_old, jnp.max(s, axis=0, keepdims=True))
            pb = jnp.exp2(s - m_new).astype(BF16)
            vc = jnp.concatenate([vT_ref[0, g, f * fb + j] for j in range(fb)], axis=1)
            m_ref[g] = m_new
            acc_ref[g] = (jnp.exp2(m_old - m_new) * acc_ref[g]
                          + jnp.dot(vc, pb, preferred_element_type=F32))

    def pair_body(pf, carry):
        f0 = 2 * pf
        qk(f0 + 1, sb_ref)
        softmax_pv(f0, sa_ref)
        qk(jnp.minimum(f0 + 2, n_far - 1), sa_ref)
        softmax_pv(f0 + 1, sb_ref)
        return carry

    lax.fori_loop(0, n_far // 2, pair_body, 0)

    @pl.when(n_far % 2 == 1)
    def _():
        softmax_pv(n_far - 1, sa_ref)

    outs = []
    for g in range(N_KV_HEADS):
        og = acc_ref[g, :HEAD_DIM] / acc_ref[g, HEAD_DIM:HEAD_DIM + 1]
        outs.extend(og[:, r * TQ:(r + 1) * TQ] for r in range(REP))
    o_ref[0] = jnp.concatenate(outs, axis=0).T.astype(BF16)


def _attention(qT, qiT, wT, kh, vT, kin, bias_strips):
    b, nq = qT.shape[0], qT.shape[1]
    s = kin.shape[1]
    n_sel = min(IDX_TOPK_MAX, s // 4)
    nb = s // KB
    return pl.pallas_call(
        functools.partial(_attn_kernel, n_sel=n_sel),
        grid=(b, nq),
        in_specs=[pl.BlockSpec((1, 1, N_KV_HEADS, QK_DIM, REP * TQ), lambda bi, i: (bi, i, 0, 0, 0)),
                  pl.BlockSpec((1, 1, IDX_DIM, IDX_HEADS * TQ), lambda bi, i: (bi, i, 0, 0)),
                  pl.BlockSpec((1, 1, IDX_HEADS, TQ), lambda bi, i: (bi, i, 0, 0)),
                  pl.BlockSpec((1, N_KV_HEADS, s, QK_DIM), lambda bi, i: (bi, 0, 0, 0)),
                  pl.BlockSpec((1, N_KV_HEADS, nb, V_ROWS, KB), lambda bi, i: (bi, 0, 0, 0, 0)),
                  pl.BlockSpec((1, s, IDX_DIM), lambda bi, i: (bi, 0, 0)),
                  pl.BlockSpec((N_HEADS, 3 * TQ, TQ), lambda bi, i: (0, 0, 0))],
        out_specs=pl.BlockSpec((1, TQ, ATTN_WIDTH), lambda bi, i: (bi, i, 0)),
        out_shape=jax.ShapeDtypeStruct((b, s, ATTN_WIDTH), BF16),
        scratch_shapes=[pltpu.VMEM((nb, KB, TQ), I32),
                        pltpu.VMEM((nb, KB, TQ), F32),
                        pltpu.VMEM((nb, KB, TQ), F32),
                        pltpu.VMEM((8, TQ), I32),
                        pltpu.VMEM((N_KV_HEADS, 1, REP * TQ), F32),
                        pltpu.VMEM((N_KV_HEADS, V_ROWS, REP * TQ), F32),
                        pltpu.VMEM((N_KV_HEADS, FAR_KC, REP * TQ), F32),
                        pltpu.VMEM((N_KV_HEADS, FAR_KC, REP * TQ), F32)],
        compiler_params=_params("arbitrary", "arbitrary"),
        name="attn",
    )(qT, qiT, wT, kh, vT, kin, bias_strips)


HALO = 16


def _mix_kernel(cb_ref, cc_ref, cu_ref, ccp_ref, cup_ref, at_ref, ga_ref, gb_ref,
                cw_ref, wco_ref, wao_ref, o_ref):
    tm = cb_ref.shape[1]
    v = cc_ref[0].astype(F32) * cu_ref[0].astype(F32)
    first = pl.program_id(1) == 0
    hv = ccp_ref[0].astype(F32) * cup_ref[0].astype(F32)
    hv = jnp.where(first, 0.0, hv)
    row = lax.broadcasted_iota(I32, v.shape, 0)
    v1 = jnp.where(row == 0, hv[HALO - 1:HALO], pltpu.roll(v, 1, 0))
    v2 = pltpu.roll(v, 2, 0)
    v2 = jnp.where(row == 0, hv[HALO - 2:HALO - 1], jnp.where(row == 1, hv[HALO - 1:HALO], v2))
    y = cw_ref[0:1] * v2 + cw_ref[1:2] * v1 + cw_ref[2:3] * v
    yc = (cb_ref[0].astype(F32) * y).astype(BF16)
    y_conv = jnp.dot(yc, wco_ref[...], preferred_element_type=F32)
    y_attn = jnp.dot(at_ref[0], wao_ref[...], preferred_element_type=F32)
    mixed = _sigmoid(ga_ref[0].astype(F32)) * y_conv + _sigmoid(gb_ref[0].astype(F32)) * y_attn
    o_ref[0] = mixed.astype(BF16)


def _mix(proj, attn, conv_w, w_conv_out_b, w_attn_out_b):
    b, s, _ = proj.shape
    tm = MIX_TM
    hb = tm // HALO
    prev = lambda col: (lambda bi, i: (bi, jnp.maximum(i * hb - 1, 0), col))
    return pl.pallas_call(
        _mix_kernel,
        grid=(b, s // tm),
        in_specs=[pl.BlockSpec((1, tm, CONV_WIDTH), lambda bi, i: (bi, i, COL_CB)),
                  pl.BlockSpec((1, tm, CONV_WIDTH), lambda bi, i: (bi, i, COL_CC)),
                  pl.BlockSpec((1, tm, CONV_WIDTH), lambda bi, i: (bi, i, COL_CU)),
                  pl.BlockSpec((1, HALO, CONV_WIDTH), prev(COL_CC)),
                  pl.BlockSpec((1, HALO, CONV_WIDTH), prev(COL_CU)),
                  pl.BlockSpec((1, tm, ATTN_WIDTH), lambda bi, i: (bi, i, 0)),
                  pl.BlockSpec((1, tm, D_MODEL), lambda bi, i: (bi, i, COL_GA)),
                  pl.BlockSpec((1, tm, D_MODEL), lambda bi, i: (bi, i, COL_GB)),
                  pl.BlockSpec((8, CONV_WIDTH), lambda bi, i: (0, 0)),
                  pl.BlockSpec((CONV_WIDTH, D_MODEL), lambda bi, i: (0, 0)),
                  pl.BlockSpec((ATTN_WIDTH, D_MODEL), lambda bi, i: (0, 0))],
        out_specs=pl.BlockSpec((1, tm, D_MODEL), lambda bi, i: (bi, i, 0)),
        out_shape=jax.ShapeDtypeStruct((b, s, D_MODEL), BF16),
        compiler_params=_params("arbitrary", "arbitrary"),
        name="mix",
    )(proj, proj, proj, proj, proj, attn, proj, proj,
      jnp.pad(conv_w, ((0, 8 - CONV_K), (0, 0))), w_conv_out_b, w_attn_out_b)


def _post_kernel(x_ref, mx_ref, g1_ref, nw_ref, sc_ref, sh_ref, g2_ref, wo_ref, wrT_ref,
                 ws1_ref, ws3_ref, ws2_ref, base_ref, h2_ref, lg_ref):
    x1 = x_ref[0] + g1_ref[0] * jnp.dot(mx_ref[0], wo_ref[...], preferred_element_type=F32)
    ms = jnp.mean(x1 * x1, axis=-1, keepdims=True)
    h2 = x1 * lax.rsqrt(ms + EPS) * nw_ref[...] * (1.0 + sc_ref[0]) + sh_ref[0]
    _store_row_tiles(h2_ref, _pack_bf16_pairs(h2))
    lg_ref[...] = lax.dot_general(wrT_ref[...], h2, (((1,), (1,)), ((), ())),
                                  precision=lax.Precision.HIGHEST, preferred_element_type=F32)
    hb = h2.astype(BF16)
    a = jnp.dot(hb, ws1_ref[...], preferred_element_type=F32)
    u = jnp.dot(hb, ws3_ref[...], preferred_element_type=F32)
    shared = jnp.dot((a * _sigmoid(a) * u).astype(BF16), ws2_ref[...], preferred_element_type=F32)
    base_ref[0] = x1 + g2_ref[0] * shared


def _post(x, mixed, g1, norm_w, sc, sh, g2, w_o_b, w_router_t, ws1_b, ws3_b, ws2_b):
    b, s, d = x.shape
    tm = POST_TM
    nt = s // tm
    vec = pl.BlockSpec((1, 1, d), lambda bi, i: (bi, 0, 0))
    const = lambda shape: pl.BlockSpec(shape, lambda bi, i: (0,) * len(shape))
    return pl.pallas_call(
        _post_kernel,
        grid=(b, nt),
        in_specs=[pl.BlockSpec((1, tm, d), lambda bi, i: (bi, i, 0)),
                  pl.BlockSpec((1, tm, d), lambda bi, i: (bi, i, 0)),
                  vec, const((1, d)), vec, vec, vec,
                  const((d, d)), const((N_EXPERTS, d)),
                  const((d, D_EXPERT)), const((d, D_EXPERT)), const((D_EXPERT, d))],
        out_specs=[pl.BlockSpec((1, tm, d), lambda bi, i: (bi, i, 0)),
                   pl.BlockSpec((tm * ROW_SUB, LANES), lambda bi, i: (bi * nt + i, 0)),
                   pl.BlockSpec((N_EXPERTS, tm), lambda bi, i: (0, bi * nt + i))],
        out_shape=[jax.ShapeDtypeStruct((b, s, d), F32),
                   jax.ShapeDtypeStruct((b * s * ROW_SUB, LANES), PACKED),
                   jax.ShapeDtypeStruct((N_EXPERTS, b * s), F32)],
        compiler_params=_params("arbitrary", "arbitrary"),
        name="post",
    )(x, mixed, g1, norm_w.reshape(1, d), sc, sh, g2, w_o_b, w_router_t, ws1_b, ws3_b, ws2_b)


def _first_max(cur, ids, sentinel):
    m = jnp.max(cur, axis=0, keepdims=True)
    first = jnp.min(jnp.where(cur == m, ids, sentinel), axis=0, keepdims=True)
    return m, first


def _route_kernel(lg_ref, rb_ref, idx_ref, w_ref):
    tn = lg_ref.shape[1]
    gsz = N_EXPERTS // N_GROUPS
    scores = _sigmoid(lg_ref[...])
    sel = scores + rb_ref[...]
    sub = lax.broadcasted_iota(I32, (gsz, tn), 0).astype(F32)

    gs = []
    for g in range(N_GROUPS):
        v = sel[g * gsz:(g + 1) * gsz]
        m1, first = _first_max(v, sub, float(gsz))
        m2 = jnp.max(jnp.where(sub == first, -jnp.inf, v), axis=0, keepdims=True)
        gs.append(m1 + m2)
    cur = jnp.concatenate(gs, axis=0)
    gid = lax.broadcasted_iota(I32, (N_GROUPS, tn), 0).astype(F32)
    keep = jnp.zeros((N_GROUPS, tn), F32)
    for _ in range(TOPK_GROUPS):
        _, first = _first_max(cur, gid, float(N_GROUPS))
        hit = gid == first
        keep = jnp.where(hit, 1.0, keep)
        cur = jnp.where(hit, -jnp.inf, cur)

    cur = jnp.concatenate(
        [jnp.where(keep[g:g + 1] > 0.0, sel[g * gsz:(g + 1) * gsz], NEG) for g in range(N_GROUPS)],
        axis=0)
    eid = lax.broadcasted_iota(I32, (N_EXPERTS, tn), 0).astype(F32)
    ids, ws = [], []
    for _ in range(TOP_K):
        _, first = _first_max(cur, eid, float(N_EXPERTS))
        hit = eid == first
        ids.append(first)
        ws.append(jnp.sum(jnp.where(hit, scores, 0.0), axis=0, keepdims=True))
        cur = jnp.where(hit, -jnp.inf, cur)
    w = jnp.concatenate(ws, axis=0)
    idx_ref[...] = jnp.concatenate(ids, axis=0).astype(I32)
    w_ref[...] = w / jnp.sum(w, axis=0, keepdims=True) * ROUTED_SCALE


def _route(logits_t, router_bias):
    e, n = logits_t.shape
    tn = ROUTE_TN
    return pl.pallas_call(
        _route_kernel,
        grid=(n // tn,),
        in_specs=[pl.BlockSpec((e, tn), lambda j: (0, j)),
                  pl.BlockSpec((e, 1), lambda j: (0, 0))],
        out_specs=[pl.BlockSpec((TOP_K, tn), lambda j: (0, j)),
                   pl.BlockSpec((TOP_K, tn), lambda j: (0, j))],
        out_shape=[jax.ShapeDtypeStruct((TOP_K, n), I32),
                   jax.ShapeDtypeStruct((TOP_K, n), F32)],
        compiler_params=_params("arbitrary"),
        name="route",
    )(logits_t, router_bias.reshape(e, 1))


def _tile_major(a_t, n_tiles, tm):
    return a_t.reshape(TOP_K, n_tiles, tm).transpose(1, 0, 2).reshape(n_tiles, 1, TOP_K * tm)


def _dispatch_kernel(zs_ref, pos_ref, x_ref, xs_hbm, zbuf, sem):
    step = pl.program_id(0)

    @pl.when(step == 0)
    def _():
        zbuf[...] = jnp.zeros(zbuf.shape, zbuf.dtype)

        def zero_copy(t):
            start = pl.multiple_of(t * MOE_TM, MOE_TM)
            return pltpu.make_async_copy(zbuf, xs_hbm.at[pl.ds(start, MOE_TM)], sem.at[1])

        def start_body(t, carry):
            @pl.when(zs_ref[t] != 0)
            def _():
                zero_copy(t).start()
            return carry

        def wait_body(t, carry):
            @pl.when(zs_ref[t] != 0)
            def _():
                zero_copy(t).wait()
            return carry

        lax.fori_loop(0, zs_ref.shape[0], start_body, 0)
        lax.fori_loop(0, zs_ref.shape[0], wait_body, 0)

    def row_copy(k, r):
        return pltpu.make_async_copy(x_ref.at[r], xs_hbm.at[pos_ref[0, 0, k * DISP_TM + r]], sem.at[0])

    def body(i, carry):
        for k in range(TOP_K):
            for u in range(2):
                row_copy(k, i * 2 + u).start(priority=u)
        return carry

    lax.fori_loop(0, DISP_TM // 2, body, 0)
    for _ in range(TOP_K):
        pltpu.make_async_copy(x_ref, x_ref, sem.at[0]).wait()


def _dispatch(h2, pos_t, zero_start, n_rows):
    n = h2.shape[0]
    tm = DISP_TM
    n_tiles = n // tm
    grid_spec = pltpu.PrefetchScalarGridSpec(
        num_scalar_prefetch=1,
        grid=(n_tiles,),
        in_specs=[pl.BlockSpec((1, 1, tm * TOP_K), lambda t, zs: (t, 0, 0), memory_space=pltpu.SMEM),
                  pl.BlockSpec((tm, ROW_SUB, LANES), lambda t, zs: (t, 0, 0))],
        out_specs=pl.BlockSpec(memory_space=pl.ANY),
        scratch_shapes=[pltpu.VMEM((MOE_TM, ROW_SUB, LANES), h2.dtype),
                        pltpu.SemaphoreType.DMA((2,))],
    )
    return pl.pallas_call(
        _dispatch_kernel,
        grid_spec=grid_spec,
        out_shape=jax.ShapeDtypeStruct((n_rows, ROW_SUB, LANES), h2.dtype),
        compiler_params=_params("arbitrary"),
        name="dispatch",
    )(zero_start, _tile_major(pos_t, n_tiles, tm), h2)


def _experts_kernel(te_ref, nu_ref, nxt_ref, par_ref, x_ref, w1_hbm, w3_hbm, w2_hbm, y_ref,
                    w1f, w3f, w2f, wsem, w1b, w3b, w2b):
    j = pl.program_id(0)
    n_used = nu_ref[0]

    def weight_copies(e, slot):
        return [pltpu.make_async_copy(src.at[e], dst.at[slot], wsem.at[slot])
                for src, dst in ((w1_hbm, w1f), (w3_hbm, w3f), (w2_hbm, w2f))]

    @pl.when(j < n_used)
    def _():
        @pl.when((j == 0) | (te_ref[j] != te_ref[jnp.maximum(j - 1, 0)]))
        def _():
            for parity in (0, 1):
                @pl.when(par_ref[j] == parity)
                def _(parity=parity):
                    if parity == 0:
                        @pl.when(j == 0)
                        def _():
                            for c in weight_copies(te_ref[0], 0):
                                c.start()
                    for c in weight_copies(te_ref[j], parity):
                        c.wait()
                    w1b[...] = w1f[parity].astype(BF16)
                    w3b[...] = w3f[parity].astype(BF16)
                    w2b[...] = w2f[parity].astype(BF16)

                    @pl.when(nxt_ref[j] >= 0)
                    def _():
                        for c in weight_copies(nxt_ref[j], 1 - parity):
                            c.start()

        lo, hi = _unpack_bf16_pairs(_load_row_tiles(x_ref, MOE_TM))
        x = jnp.concatenate([lo.astype(BF16), hi.astype(BF16)], axis=1)
        a = jnp.dot(x, w1b[...], preferred_element_type=F32)
        u = jnp.dot(x, w3b[...], preferred_element_type=F32)
        y = jnp.dot((a * _sigmoid(a) * u).astype(BF16), w2b[...], preferred_element_type=F32)
        _store_row_tiles(y_ref, _pack_bf16_pairs(y))

    @pl.when(j >= n_used)
    def _():
        y_ref[...] = jnp.zeros(y_ref.shape, y_ref.dtype)


def _experts(xs, tile_expert, n_used, next_expert, slot_parity, w1, w3, w2):
    n_rows = xs.shape[0]
    nt = n_rows // MOE_TM
    d, f = w1.shape[1], w1.shape[2]
    blk = (MOE_TM * ROW_SUB, LANES)
    grid_spec = pltpu.PrefetchScalarGridSpec(
        num_scalar_prefetch=4,
        grid=(nt,),
        in_specs=[pl.BlockSpec(blk, lambda j, te, nu, nx, pa: (jnp.minimum(j, nu[0] - 1), 0)),
                  pl.BlockSpec(memory_space=pl.ANY),
                  pl.BlockSpec(memory_space=pl.ANY),
                  pl.BlockSpec(memory_space=pl.ANY)],
        out_specs=pl.BlockSpec(blk, lambda j, te, nu, nx, pa: (j, 0)),
        scratch_shapes=[pltpu.VMEM((2, d, f), F32),
                        pltpu.VMEM((2, d, f), F32),
                        pltpu.VMEM((2, f, d), F32),
                        pltpu.SemaphoreType.DMA((2,)),
                        pltpu.VMEM((d, f), BF16),
                        pltpu.VMEM((d, f), BF16),
                        pltpu.VMEM((f, d), BF16)],
    )
    ys = pl.pallas_call(
        _experts_kernel,
        grid_spec=grid_spec,
        out_shape=jax.ShapeDtypeStruct((n_rows * ROW_SUB, LANES), PACKED),
        compiler_params=_params("arbitrary"),
        name="experts",
    )(tile_expert, n_used, next_expert, slot_parity, xs.reshape(n_rows * ROW_SUB, LANES), w1, w3, w2)
    return ys.reshape(n_rows, ROW_SUB, LANES)


def _combine_kernel(pos_cur_ref, pos_nxt_ref, ys_hbm, base_ref, g2_ref, w_ref, o_ref, buf, sem):
    bi, i = pl.program_id(0), pl.program_id(1)
    step = bi * pl.num_programs(1) + i
    n_steps = pl.num_programs(0) * pl.num_programs(1)
    slot = lax.rem(step, 2)

    def issue(pos_ref, dst_slot):
        def body(i, carry):
            for k in range(TOP_K):
                for u in range(2):
                    r = i * 2 + u
                    pltpu.make_async_copy(ys_hbm.at[pos_ref[0, 0, k * COMB_TM + r]],
                                          buf.at[dst_slot, k, pl.ds(r * ROW_SUB, ROW_SUB), :],
                                          sem.at[dst_slot]).start(priority=u)
            return carry
        lax.fori_loop(0, COMB_TM // 2, body, 0)

    @pl.when(step == 0)
    def _():
        issue(pos_cur_ref, 0)

    for parity in (0, 1):
        @pl.when((step + 1 < n_steps) & (slot == parity))
        def _(parity=parity):
            issue(pos_nxt_ref, 1 - parity)

    pltpu.make_async_copy(buf.at[slot], buf.at[slot], sem.at[slot]).wait()

    w = w_ref[...]
    half = o_ref.shape[2] // 2
    for parity in (0, 1):
        @pl.when(slot == parity)
        def _(parity=parity):
            acc_lo = jnp.zeros((COMB_TM, half), F32)
            acc_hi = jnp.zeros((COMB_TM, half), F32)
            for k in range(TOP_K):
                lo, hi = _unpack_bf16_pairs(_load_row_tiles(buf.at[parity, k], COMB_TM))
                acc_lo = acc_lo + w[:, k:k + 1] * lo
                acc_hi = acc_hi + w[:, k:k + 1] * hi
            o_ref[0] = base_ref[0] + g2_ref[0] * jnp.concatenate([acc_lo, acc_hi], axis=1)


def _combine(ys, pos_t, w_sel, base, g2):
    b, s, d = base.shape
    tm = COMB_TM
    nt = s // tm
    n_tiles = b * nt
    pos_t = _tile_major(pos_t, n_tiles, tm)
    return pl.pallas_call(
        _combine_kernel,
        grid=(b, nt),
        in_specs=[pl.BlockSpec((1, 1, tm * TOP_K), lambda bi, i: (bi * nt + i, 0, 0),
                               memory_space=pltpu.SMEM),
                  pl.BlockSpec((1, 1, tm * TOP_K),
                               lambda bi, i: (jnp.minimum(bi * nt + i + 1, n_tiles - 1), 0, 0),
                               memory_space=pltpu.SMEM),
                  pl.BlockSpec(memory_space=pl.ANY),
                  pl.BlockSpec((1, tm, d), lambda bi, i: (bi, i, 0)),
                  pl.BlockSpec((1, 1, d), lambda bi, i: (bi, 0, 0)),
                  pl.BlockSpec((tm, TOP_K), lambda bi, i: (bi * nt + i, 0))],
        out_specs=pl.BlockSpec((1, tm, d), lambda bi, i: (bi, i, 0)),
        out_shape=jax.ShapeDtypeStruct((b, s, d), F32),
        scratch_shapes=[pltpu.VMEM((2, TOP_K, tm * ROW_SUB, LANES), ys.dtype),
                        pltpu.SemaphoreType.DMA((2,))],
        compiler_params=_params("arbitrary", "arbitrary"),
        name="combine",
    )(pos_t, pos_t, ys, base, g2, w_sel)


def _plan_kernel(te_ref, tri_ref, low_ref, pos_ref, cnt_ref, run_ref, start_ref):
    phase, j = pl.program_id(0), pl.program_id(1)
    tn = te_ref.shape[1]
    te = te_ref[...]
    eid = lax.broadcasted_iota(I32, (N_EXPERTS, tn), 0)
    hot = jnp.zeros((N_EXPERTS, tn), F32)
    for k in range(TOP_K):
        hot = hot + jnp.where(te[k:k + 1, :] == eid, 1.0, 0.0)
    tile_count = jnp.sum(hot, axis=1, keepdims=True)

    @pl.when((phase == 0) & (j == 0))
    def _():
        run_ref[...] = jnp.zeros(run_ref.shape, F32)

    @pl.when((phase == 1) & (j == 0))
    def _():
        counts = run_ref[...]
        cnt_ref[...] = counts
        tiles = jnp.floor((counts + (MOE_TM - 1)) * (1.0 / MOE_TM))
        start_ref[...] = jnp.dot(low_ref[...], tiles.astype(BF16), preferred_element_type=F32) * MOE_TM
        run_ref[...] = jnp.zeros(run_ref.shape, F32)

    @pl.when(phase == 1)
    def _():
        before = jnp.dot(hot.astype(BF16), tri_ref[...], preferred_element_type=F32)
        val = before + (run_ref[:, 0:1] + start_ref[:, 0:1])
        rows = [jnp.sum(jnp.where(te[k:k + 1, :] == eid, val, 0.0), axis=0, keepdims=True)
                for k in range(TOP_K)]
        pos_ref[...] = jnp.concatenate(rows, axis=0).astype(I32)

    run_ref[...] = run_ref[...] + tile_count


def _dispatch_plan(top_e_t):
    n = top_e_t.shape[1]
    tn = PLAN_TN
    n_tiles = n * TOP_K // MOE_TM + N_EXPERTS
    tri = jnp.asarray(np.triu(np.ones((tn, tn), np.float32), 1), BF16)
    low = jnp.asarray(np.tril(np.ones((N_EXPERTS, N_EXPERTS), np.float32), -1), BF16)
    pos_t, cnt = pl.pallas_call(
        _plan_kernel,
        grid=(2, n // tn),
        in_specs=[pl.BlockSpec((TOP_K, tn), lambda ph, j: (0, j)),
                  pl.BlockSpec((tn, tn), lambda ph, j: (0, 0)),
                  pl.BlockSpec((N_EXPERTS, N_EXPERTS), lambda ph, j: (0, 0))],
        out_specs=[pl.BlockSpec((TOP_K, tn), lambda ph, j: (0, j * ph)),
                   pl.BlockSpec((N_EXPERTS, LANES), lambda ph, j: (0, 0))],
        out_shape=[jax.ShapeDtypeStruct((TOP_K, n), I32),
                   jax.ShapeDtypeStruct((N_EXPERTS, LANES), F32)],
        scratch_shapes=[pltpu.VMEM((N_EXPERTS, LANES), F32),
                        pltpu.VMEM((N_EXPERTS, LANES), F32)],
        compiler_params=_params("arbitrary", "arbitrary"),
        name="plan",
    )(top_e_t, tri, low)
    counts = cnt[:, 0].astype(I32)
    tile_end = jnp.cumsum((counts + MOE_TM - 1) // MOE_TM)
    tile_expert = jnp.minimum(
        jnp.sum((tile_end[None, :] <= jnp.arange(n_tiles, dtype=I32)[:, None]).astype(I32), axis=1),
        N_EXPERTS - 1)
    n_used = tile_end[-1:].astype(I32)
    t_ids = jnp.arange(n_tiles, dtype=I32)
    is_last = jnp.any((tile_end[None, :] - 1 == t_ids[:, None]) & (counts[None, :] > 0), axis=1)
    zero_tile = (is_last | (t_ids >= n_used[0])).astype(I32)
    e_ids = jnp.arange(N_EXPERTS, dtype=I32)
    nonempty = counts > 0
    later = lax.cummin(jnp.where(nonempty, e_ids, N_EXPERTS), axis=0, reverse=True)
    next_e = jnp.concatenate([later[1:], jnp.full((1,), N_EXPERTS, I32)])
    next_e = jnp.where(next_e < N_EXPERTS, next_e, -1)
    rank = jnp.cumsum(nonempty.astype(I32)) - nonempty.astype(I32)
    tile_expert = tile_expert.astype(I32)
    hot = tile_expert[:, None] == e_ids[None, :]
    next_tile = jnp.sum(jnp.where(hot, next_e[None, :], 0), axis=1).astype(I32)
    parity_tile = jnp.sum(jnp.where(hot, rank[None, :] % 2, 0), axis=1).astype(I32)
    return pos_t, zero_tile, tile_expert, n_used, next_tile, parity_tile, n_tiles * MOE_TM


def _layer(x, c, rel_bias, norm1_w, norm2_w, w_ada, b_ada, w_in, conv_w, w_conv_out, q_norm_w,
           k_norm_w, idx_k_norm_w, idx_k_norm_b, w_attn_out, w_o, w_router, router_bias,
           w1, w3, w2, ws1, ws3, ws2):
    b, s, d = x.shape
    mod = _mod(c, w_ada, b_ada).reshape(b, 6, 1, d)
    sh1, sc1, g1, sh2, sc2, g2 = [mod[:, m] for m in range(6)]

    proj = _proj(x, norm1_w, sc1, sh1, _relayout_w_in(w_in))
    qT, qiT, kh, vT, kin, wT = _prep(proj, q_norm_w, k_norm_w, idx_k_norm_w, idx_k_norm_b,
                                     rel_bias[REL_BUCKETS - 1])
    attn = _attention(qT, qiT, wT, kh, vT, kin, _bias_strips(rel_bias))
    mixed = _mix(proj, attn, conv_w, w_conv_out.astype(BF16), w_attn_out.astype(BF16))
    base, h2, logits_t = _post(x, mixed, g1, norm2_w, sc2, sh2, g2, w_o.astype(BF16), w_router.T,
                               ws1.astype(BF16), ws3.astype(BF16), ws2.astype(BF16))
    top_e_t, w_sel_t = _route(logits_t, router_bias)
    pos_t, zero_start, tile_expert, n_used, next_expert, slot_parity, n_rows = _dispatch_plan(top_e_t)
    xs = _dispatch(h2.reshape(b * s, ROW_SUB, LANES), pos_t, zero_start, n_rows)
    ys = _experts(xs, tile_expert, n_used, next_expert, slot_parity, w1, w3, w2)
    return _combine(ys, pos_t, w_sel_t.T, base, g2)


def kernel(x, c, rel_bias, norm1_w, norm2_w, w_ada, b_ada, w_in, conv_w, w_conv_out, q_norm_w,
           k_norm_w, idx_k_norm_w, idx_k_norm_b, w_attn_out, w_o, w_router, router_bias,
           w1, w3, w2, ws1, ws3, ws2):
    assert x.shape[1] % PROJ_TM == 0 and x.shape[2] == D_MODEL and w_ada.shape[0] == 1
    return _layer(x, c, rel_bias, norm1_w[0], norm2_w[0], w_ada[0], b_ada[0], w_in, conv_w[0],
                  w_conv_out[0], q_norm_w[0], k_norm_w[0], idx_k_norm_w[0], idx_k_norm_b[0],
                  w_attn_out[0], w_o[0], w_router[0], router_bias[0], w1[0], w3[0], w2[0],
                  ws1[0], ws3[0], ws2[0])
```

```python
import functools
import math

import numpy as np
import jax
import jax.numpy as jnp
from jax import lax
from jax.experimental import pallas as pl
from jax.experimental.pallas import tpu as pltpu

F32 = jnp.float32
BF16 = jnp.bfloat16
I32 = jnp.int32
PACKED = jnp.int32

D_MODEL = 2048
CONV_WIDTH = D_MODEL // 2
CONV_K = 3
N_HEADS = 16
N_KV_HEADS = 4
HEAD_DIM = 64
ATTN_WIDTH = N_HEADS * HEAD_DIM
KV_WIDTH = N_KV_HEADS * HEAD_DIM
IDX_HEADS = 16
IDX_DIM = 64
IDX_TOPK_MAX = 256
REL_BUCKETS = 32
REL_MAX_DIST = 128
N_EXPERTS = 64
N_GROUPS = 8
TOPK_GROUPS = 4
TOP_K = 8
D_EXPERT = 512
ROUTED_SCALE = 2.5
EPS = 1e-6
NEG = -1e30

REP = N_HEADS // N_KV_HEADS

LANES = 128
VMEM_LIMIT = 56 * 1024 * 1024

TQ = 128
KB = 128
KC = 4 * KB
FAR_KC = 4 * KB
PROJ_TM = 1024
PROJ_TN = 768
PREP_TM = 512
MIX_TM = 512
POST_TM = 512
ROUTE_TN = 512
MOE_TM = 512
COMB_TM = 128
DISP_TM = 512
PLAN_TN = 512

ROW_SUB = D_MODEL // 2 // LANES
QK_DIM = 128
V_ROWS = HEAD_DIM + 16
LOG2E = math.log2(math.e)

_SEG = dict(cb=(0, 1024), cc=(1024, 2048), cu=(2048, 3072), q=(3072, 4096), k=(4096, 4352),
            v=(4352, 4608), qi=(4608, 5632), ki=(5632, 5696), wi=(5696, 5712),
            ga=(5712, 7760), gb=(7760, 9808))
_ORDER = ["ga", "gb", "cb", "cc", "cu", "q", "qi", "k", "v", "ki", "wi"]
PROJ_W = 9984
COL_GA, COL_GB = 0, 1
COL_CB, COL_CC, COL_CU, COL_Q, COL_QI = 4, 5, 6, 7, 8
COL_K, COL_V = 36, 37
COL_KW = 76

INT_MIN = -(2 ** 31)
INT_MAX = 2 ** 31 - 1


def _sortable_key_of(x):
    bits = int(np.float32(x).view(np.int32))
    return bits ^ 0x7FFFFFFF if bits < 0 else bits


NEG_KEY = _sortable_key_of(NEG)


def _sigmoid(x):
    return 1.0 / (1.0 + jnp.exp(-x))


def _pack_bf16_pairs(x):
    half = x.shape[1] // 2
    lo = lax.bitcast_convert_type(x[:, :half].astype(BF16).astype(F32), PACKED)
    hi = lax.bitcast_convert_type(x[:, half:].astype(BF16).astype(F32), PACKED)
    return lax.shift_right_logical(lo, jnp.full_like(lo, 16)) | (hi & jnp.int32(-65536))


def _unpack_bf16_pairs(w):
    lo = lax.bitcast_convert_type(w << 16, F32)
    hi = lax.bitcast_convert_type(w & jnp.int32(-65536), F32)
    return lo, hi


def _store_row_tiles(ref, words):
    m = words.shape[0]
    for sl in range(ROW_SUB):
        ref[pl.ds(sl, m, stride=ROW_SUB), :] = words[:, sl * LANES:(sl + 1) * LANES]


def _load_row_tiles(ref, m):
    return jnp.concatenate([ref[pl.ds(sl, m, stride=ROW_SUB), :] for sl in range(ROW_SUB)], axis=1)


def _params(*sem):
    return pltpu.CompilerParams(dimension_semantics=sem, vmem_limit_bytes=VMEM_LIMIT)


def _mod_kernel(c_ref, w_ref, b_ref, o_ref):
    c = c_ref[...]
    s = (c * _sigmoid(c)).astype(BF16)
    o_ref[...] = jnp.dot(s, w_ref[...].astype(BF16), preferred_element_type=F32) + b_ref[...]


def _mod(c, w_ada, b_ada):
    b = c.shape[0]
    rows = 8
    cp = jnp.pad(c, ((0, rows - b), (0, 0)))
    n = w_ada.shape[1]
    tn = 1024
    out = pl.pallas_call(
        _mod_kernel,
        grid=(n // tn,),
        in_specs=[pl.BlockSpec((rows, D_MODEL), lambda j: (0, 0)),
                  pl.BlockSpec((D_MODEL, tn), lambda j: (0, j)),
                  pl.BlockSpec((1, tn), lambda j: (0, j))],
        out_specs=pl.BlockSpec((rows, tn), lambda j: (0, j)),
        out_shape=jax.ShapeDtypeStruct((rows, n), F32),
        compiler_params=_params("arbitrary"),
        name="mod",
    )(cp, w_ada, b_ada.reshape(1, n))
    return out[:b]


def _column_spans():
    spans, dst = [], 0
    for name in _ORDER:
        lo, hi = _SEG[name]
        if spans and spans[-1][1] == lo:
            spans[-1][1] = hi
        else:
            spans.append([lo, hi, dst])
        dst += hi - lo
    return spans, dst


def _wprep_kernel(w_ref, o_ref):
    spans, used = _column_spans()
    for lo, hi, dst in spans:
        o_ref[:, dst:dst + hi - lo] = w_ref[0, :, lo:hi].astype(BF16)
    o_ref[:, used:] = jnp.zeros((o_ref.shape[0], o_ref.shape[1] - used), BF16)


def _relayout_w_in(w_in):
    w_in = w_in.astype(BF16)
    _, d, n_in = w_in.shape
    tr = 256
    return pl.pallas_call(
        _wprep_kernel,
        grid=(d // tr,),
        in_specs=[pl.BlockSpec((1, tr, n_in), lambda i: (0, i, 0))],
        out_specs=pl.BlockSpec((tr, PROJ_W), lambda i: (i, 0)),
        out_shape=jax.ShapeDtypeStruct((d, PROJ_W), BF16),
        compiler_params=_params("arbitrary"),
        name="wprep",
    )(w_in)


def _proj_kernel(x_ref, nw_ref, sc_ref, sh_ref, w_ref, o_ref, h_ref):
    @pl.when(pl.program_id(2) == 0)
    def _():
        x = x_ref[0]
        ms = jnp.mean(x * x, axis=-1, keepdims=True)
        y = x * lax.rsqrt(ms + EPS) * nw_ref[...]
        h_ref[...] = (y * (1.0 + sc_ref[0]) + sh_ref[0]).astype(BF16)

    o_ref[0] = jnp.dot(h_ref[...], w_ref[...], preferred_element_type=F32).astype(BF16)


def _proj(x, norm_w, sc, sh, w_in_p):
    b, s, d = x.shape
    tm, tn = PROJ_TM, PROJ_TN
    return pl.pallas_call(
        _proj_kernel,
        grid=(b, s // tm, PROJ_W // tn),
        in_specs=[pl.BlockSpec((1, tm, d), lambda bi, i, j: (bi, i, 0)),
                  pl.BlockSpec((1, d), lambda bi, i, j: (0, 0)),
                  pl.BlockSpec((1, 1, d), lambda bi, i, j: (bi, 0, 0)),
                  pl.BlockSpec((1, 1, d), lambda bi, i, j: (bi, 0, 0)),
                  pl.BlockSpec((d, tn), lambda bi, i, j: (0, j))],
        out_specs=pl.BlockSpec((1, tm, tn), lambda bi, i, j: (bi, i, j)),
        out_shape=jax.ShapeDtypeStruct((b, s, PROJ_W), BF16),
        scratch_shapes=[pltpu.VMEM((tm, d), BF16)],
        compiler_params=_params("arbitrary", "arbitrary", "arbitrary"),
        name="proj",
    )(x, norm_w.reshape(1, d), sc, sh, w_in_p)


def _prep_kernel(q_ref, qi_ref, k_ref, v_ref, kw_ref, qnw_ref, knw_ref, inw_ref, inb_ref, qtail_ref,
                 qT_ref, qiT_ref, kh_ref, vT_ref, kin_ref, wT_ref):
    tm = q_ref.shape[1]
    nqb = tm // TQ

    q3 = q_ref[0].astype(F32).T.reshape(N_HEADS, HEAD_DIM, tm)
    ms = jnp.mean(q3 * q3, axis=1, keepdims=True)
    qn = q3 * lax.rsqrt(ms + EPS) * (qnw_ref[...] * (HEAD_DIM ** -0.5 * LOG2E))
    qi3 = qi_ref[0].astype(F32).T.reshape(IDX_HEADS, IDX_DIM, tm)
    for jb in range(nqb):
        for h in range(N_HEADS):
            g, r = divmod(h, REP)
            qT_ref[0, jb, g, :HEAD_DIM, r * TQ:(r + 1) * TQ] = qn[h, :, jb * TQ:(jb + 1) * TQ].astype(BF16)
        for g in range(N_KV_HEADS):
            qT_ref[0, jb, g, HEAD_DIM:, :] = qtail_ref[g]
        for h in range(IDX_HEADS):
            qiT_ref[0, jb, :, h * TQ:(h + 1) * TQ] = qi3[h, :, jb * TQ:(jb + 1) * TQ].astype(BF16)

    k = k_ref[0].astype(F32)
    ones_cols = jnp.where(lax.broadcasted_iota(I32, (tm, QK_DIM - HEAD_DIM), 1) < 2, 1.0, 0.0)
    for g in range(N_KV_HEADS):
        kg = k[:, g * HEAD_DIM:(g + 1) * HEAD_DIM]
        msk = jnp.mean(kg * kg, axis=-1, keepdims=True)
        kn = kg * lax.rsqrt(msk + EPS) * knw_ref[...]
        kh_ref[0, g] = jnp.concatenate([kn, ones_cols], axis=1).astype(BF16)

    v3 = v_ref[0].astype(F32).T.reshape(N_KV_HEADS, HEAD_DIM, tm)
    ones_rows = jnp.where(lax.broadcasted_iota(I32, (V_ROWS - HEAD_DIM, KB), 0) == 0, 1.0, 0.0)
    for g in range(N_KV_HEADS):
        for jb in range(tm // KB):
            vT_ref[0, g, jb] = jnp.concatenate(
                [v3[g, :, jb * KB:(jb + 1) * KB], ones_rows], axis=0).astype(BF16)

    kw = kw_ref[0].astype(F32)
    ki = kw[:, :IDX_DIM]
    mu = jnp.mean(ki, axis=-1, keepdims=True)
    var = jnp.mean(jnp.square(ki - mu), axis=-1, keepdims=True)
    kin_ref[0] = ((ki - mu) * lax.rsqrt(var + EPS) * inw_ref[...] + inb_ref[...]).astype(BF16)
    wiT = kw.T[IDX_DIM:IDX_DIM + IDX_HEADS] * (IDX_HEADS ** -0.5 * IDX_DIM ** -0.5)
    for jb in range(nqb):
        wT_ref[0, jb] = wiT[:, jb * TQ:(jb + 1) * TQ]


def _prep(proj, q_norm_w, k_norm_w, idx_k_norm_w, idx_k_norm_b, far_bias):
    b, s, _ = proj.shape
    tm = PREP_TM
    nqb = tm // TQ
    nq = s // TQ
    fb2 = (far_bias * LOG2E).reshape(N_KV_HEADS, REP)
    hi = fb2.astype(BF16)
    lo = (fb2 - hi.astype(F32)).astype(BF16)
    tail = jnp.stack([hi, lo], axis=1)
    tail = jnp.broadcast_to(tail[..., None], (N_KV_HEADS, 2, REP, TQ)).reshape(N_KV_HEADS, 2, REP * TQ)
    qtail = jnp.pad(tail, ((0, 0), (0, QK_DIM - HEAD_DIM - 2), (0, 0)))
    return pl.pallas_call(
        _prep_kernel,
        grid=(b, s // tm),
        in_specs=[pl.BlockSpec((1, tm, ATTN_WIDTH), lambda bi, i: (bi, i, COL_Q)),
                  pl.BlockSpec((1, tm, IDX_HEADS * IDX_DIM), lambda bi, i: (bi, i, COL_QI)),
                  pl.BlockSpec((1, tm, KV_WIDTH), lambda bi, i: (bi, i, COL_K)),
                  pl.BlockSpec((1, tm, KV_WIDTH), lambda bi, i: (bi, i, COL_V)),
                  pl.BlockSpec((1, tm, LANES), lambda bi, i: (bi, i, COL_KW)),
                  pl.BlockSpec((1, HEAD_DIM, 1), lambda bi, i: (0, 0, 0)),
                  pl.BlockSpec((1, HEAD_DIM), lambda bi, i: (0, 0)),
                  pl.BlockSpec((1, IDX_DIM), lambda bi, i: (0, 0)),
                  pl.BlockSpec((1, IDX_DIM), lambda bi, i: (0, 0)),
                  pl.BlockSpec((N_KV_HEADS, QK_DIM - HEAD_DIM, REP * TQ), lambda bi, i: (0, 0, 0))],
        out_specs=[pl.BlockSpec((1, nqb, N_KV_HEADS, QK_DIM, REP * TQ), lambda bi, i: (bi, i, 0, 0, 0)),
                   pl.BlockSpec((1, nqb, IDX_DIM, IDX_HEADS * TQ), lambda bi, i: (bi, i, 0, 0)),
                   pl.BlockSpec((1, N_KV_HEADS, tm, QK_DIM), lambda bi, i: (bi, 0, i, 0)),
                   pl.BlockSpec((1, N_KV_HEADS, tm // KB, V_ROWS, KB), lambda bi, i: (bi, 0, i, 0, 0)),
                   pl.BlockSpec((1, tm, IDX_DIM), lambda bi, i: (bi, i, 0)),
                   pl.BlockSpec((1, nqb, IDX_HEADS, TQ), lambda bi, i: (bi, i, 0, 0))],
        out_shape=[jax.ShapeDtypeStruct((b, nq, N_KV_HEADS, QK_DIM, REP * TQ), BF16),
                   jax.ShapeDtypeStruct((b, nq, IDX_DIM, IDX_HEADS * TQ), BF16),
                   jax.ShapeDtypeStruct((b, N_KV_HEADS, s, QK_DIM), BF16),
                   jax.ShapeDtypeStruct((b, N_KV_HEADS, s // KB, V_ROWS, KB), BF16),
                   jax.ShapeDtypeStruct((b, s, IDX_DIM), BF16),
                   jax.ShapeDtypeStruct((b, nq, IDX_HEADS, TQ), F32)],
        compiler_params=_params("arbitrary", "arbitrary"),
        name="prep",
    )(proj, proj, proj, proj, proj,
      q_norm_w.reshape(1, HEAD_DIM, 1), k_norm_w.reshape(1, HEAD_DIM),
      idx_k_norm_w.reshape(1, IDX_DIM), idx_k_norm_b.reshape(1, IDX_DIM), qtail)


def _t5_bucket_np(n):
    n = np.maximum(n, 0)
    max_exact = REL_BUCKETS // 2
    nf = np.maximum(n, 1).astype(np.float64)
    large = max_exact + np.floor(np.log(nf / max_exact) / math.log(REL_MAX_DIST / max_exact)
                                 * (REL_BUCKETS - max_exact)).astype(np.int64)
    large = np.minimum(large, REL_BUCKETS - 1)
    return np.where(n < max_exact, n, large).astype(np.int32)


def _bias_kernel(rb_ref, bucket_ref, o_ref):
    h = pl.program_id(0)
    bucket = bucket_ref[...]
    acc = jnp.zeros(bucket.shape, F32)
    for bkt in range(REL_BUCKETS):
        acc = jnp.where(bucket == bkt, rb_ref[bkt, h], acc)
    o_ref[0] = (acc - rb_ref[REL_BUCKETS - 1, h]) * LOG2E


def _bias_strips(rel_bias):
    kk = np.arange(3 * TQ)[:, None]
    qq = np.arange(TQ)[None, :]
    bucket = jnp.asarray(_t5_bucket_np(qq + TQ - kk))
    return pl.pallas_call(
        _bias_kernel,
        grid=(N_HEADS,),
        in_specs=[pl.BlockSpec(memory_space=pltpu.SMEM),
                  pl.BlockSpec((3 * TQ, TQ), lambda h: (0, 0))],
        out_specs=pl.BlockSpec((1, 3 * TQ, TQ), lambda h: (h, 0, 0)),
        out_shape=jax.ShapeDtypeStruct((N_HEADS, 3 * TQ, TQ), F32),
        compiler_params=_params("arbitrary"),
        name="bias",
    )(rel_bias, bucket)


def _attn_kernel(qT_ref, qiT_ref, wT_ref, kh_ref, vT_ref, kin_ref, biasT_ref, o_ref,
                 keys_ref, am_ref, amf_ref, p_ref, m_ref, acc_ref, sa_ref, sb_ref, *, n_sel):
    i = pl.program_id(1)
    seq = kin_ref.shape[1]
    t0 = i * TQ
    n_chunks = lax.shift_right_logical(i + 4, 2)
    q_pos = t0 + lax.broadcasted_iota(I32, (KB, TQ), 1)
    k_off = lax.broadcasted_iota(I32, (KB, TQ), 0)

    qiT = qiT_ref[0, 0]
    wT = wT_ref[0, 0]

    def score_chunk(c, carry):
        k0 = pl.multiple_of(c * KC, KC)
        d = jnp.dot(kin_ref[0, pl.ds(k0, KC), :], qiT, preferred_element_type=F32)
        acc = jnp.zeros((KC, TQ), F32)
        for h in range(IDX_HEADS):
            acc = acc + wT[h:h + 1, :] * jnp.maximum(d[:, h * TQ:(h + 1) * TQ], 0.0)
        for j in range(KC // KB):
            blk = c * (KC // KB) + j
            sc = jnp.where(blk * KB + k_off <= q_pos, acc[j * KB:(j + 1) * KB], NEG)
            bits = lax.bitcast_convert_type(sc, I32)
            keys_ref[blk] = jnp.where(bits < 0, bits ^ 0x7FFFFFFF, bits)
        return carry

    lax.fori_loop(0, n_chunks, score_chunk, 0)

    n_virtual = (seq - n_chunks * KC).astype(F32)

    def count(pred):
        def body(c, acc):
            for j in range(KC // KB):
                blk = c * (KC // KB) + j
                hit = jnp.where(pred(keys_ref[blk], blk), 1.0, 0.0)
                acc = acc + jnp.sum(hit.reshape(KB // 8, 8, TQ), axis=0)
            return acc
        acc = lax.fori_loop(0, n_chunks, body, jnp.zeros((8, TQ), F32))
        return jnp.sum(acc, axis=0, keepdims=True)

    def bit_body(it, carry):
        thr, cnt_ge = carry
        cand = thr + lax.shift_left(jnp.int32(1), 31 - it)
        cnt = count(lambda kb, blk: kb >= cand) + jnp.where(NEG_KEY >= cand, n_virtual, 0.0)
        accept = cnt >= n_sel
        return jnp.where(accept, cand, thr), jnp.where(accept, cnt, cnt_ge)

    thr, cnt_ge = lax.fori_loop(
        0, 32, bit_body, (jnp.full((1, TQ), INT_MIN, I32), jnp.full((1, TQ), float(seq), F32)))

    cnt_gt = count(lambda kb, blk: kb > thr) + jnp.where(NEG_KEY > thr, n_virtual, 0.0)
    cnt_eq = cnt_ge - cnt_gt
    need = n_sel - cnt_gt
    p_ref[...] = jnp.full((8, TQ), INT_MAX, I32)
    has_tie = jnp.max(jnp.where(cnt_eq > need, 1.0, 0.0)) > 0.0

    @pl.when(has_tie)
    def _():
        idx_bits = int(seq).bit_length()

        def p_body(it, p):
            cand = p | lax.shift_left(jnp.int32(1), idx_bits - 1 - it)
            below = count(lambda kb, blk: (kb == thr) & (blk * KB + k_off < cand))
            return jnp.where(below < need, cand, p)

        p = lax.fori_loop(0, idx_bits, p_body, jnp.zeros((1, TQ), I32))
        p_ref[...] = jnp.broadcast_to(p, (8, TQ))

    p_last = p_ref[0:1, :]

    bw = jnp.maximum(i - 1, 0)
    ws = pl.multiple_of(bw * KB, KB)

    def mask_chunk(c, carry):
        for j in range(KC // KB):
            blk = c * (KC // KB) + j
            kb = keys_ref[blk]
            k_pos = blk * KB + k_off
            sel = (kb > thr) | ((kb == thr) & (k_pos <= p_last))
            v = jnp.where(sel & (k_pos <= q_pos), 0.0, NEG)
            am_ref[blk] = v
            amf_ref[blk] = jnp.where(k_pos < ws, v, NEG)
        return carry

    lax.fori_loop(0, n_chunks, mask_chunk, 0)

    off = pl.multiple_of(TQ - (t0 - ws), TQ)
    fb = FAR_KC // KB
    n_far = (bw + fb - 1) // fb

    def qk(f, dst):
        k0 = pl.multiple_of(f * FAR_KC, FAR_KC)
        for g in range(N_KV_HEADS):
            dst[g] = jnp.dot(kh_ref[0, g, pl.ds(k0, FAR_KC), :], qT_ref[0, 0, g],
                             preferred_element_type=F32)

    for g in range(N_KV_HEADS):
        sb_ref[g, :2 * KB] = jnp.dot(kh_ref[0, g, pl.ds(ws, 2 * KB), :], qT_ref[0, 0, g],
                                     preferred_element_type=F32)
    qk(0, sa_ref)
    am_near = jnp.concatenate([am_ref[bw], am_ref[bw + 1]], axis=0)
    for g in range(N_KV_HEADS):
        s = sb_ref[g, :2 * KB]
        s = jnp.concatenate(
            [s[:, r * TQ:(r + 1) * TQ] + (biasT_ref[REP * g + r, pl.ds(off, 2 * KB), :] + am_near)
             for r in range(REP)], axis=1)
        m = jnp.max(s, axis=0, keepdims=True)
        pb = jnp.exp2(s - m).astype(BF16)
        m_ref[g] = m
        acc_ref[g] = (jnp.dot(vT_ref[0, g, bw], pb[:KB], preferred_element_type=F32)
                      + jnp.dot(vT_ref[0, g, bw + 1], pb[KB:], preferred_element_type=F32))

    def softmax_pv(f, src):
        amf = jnp.concatenate([amf_ref[f * fb + j] for j in range(fb)], axis=0)
        for g in range(N_KV_HEADS):
            s = src[g]
            s = jnp.concatenate([s[:, r * TQ:(r + 1) * TQ] + amf for r in range(REP)], axis=1)
            m_old = m_ref[g]
            m_new = jnp.maximum(m_old, jnp.max(s, axis=0, keepdims=True))
            pb = jnp.exp2(s - m_new).astype(BF16)
            vc = jnp.concatenate([vT_ref[0, g, f * fb + j] for j in range(fb)], axis=1)
            m_ref[g] = m_new
            acc_ref[g] = (jnp.exp2(m_old - m_new) * acc_ref[g]
                          + jnp.dot(vc, pb, preferred_element_type=F32))

    def pair_body(pf, carry):
        f0 = 2 * pf
        qk(f0 + 1, sb_ref)
        softmax_pv(f0, sa_ref)
        qk(jnp.minimum(f0 + 2, n_far - 1), sa_ref)
        softmax_pv(f0 + 1, sb_ref)
        return carry

    lax.fori_loop(0, n_far // 2, pair_body, 0)

    @pl.when(n_far % 2 == 1)
    def _():
        softmax_pv(n_far - 1, sa_ref)

    outs = []
    for g in range(N_KV_HEADS):
        og = acc_ref[g, :HEAD_DIM] / acc_ref[g, HEAD_DIM:HEAD_DIM + 1]
        outs.extend(og[:, r * TQ:(r + 1) * TQ] for r in range(REP))
    o_ref[0] = jnp.concatenate(outs, axis=0).T.astype(BF16)


def _attention(qT, qiT, wT, kh, vT, kin, bias_strips):
    b, nq = qT.shape[0], qT.shape[1]
    s = kin.shape[1]
    n_sel = min(IDX_TOPK_MAX, s // 4)
    nb = s // KB
    return pl.pallas_call(
        functools.partial(_attn_kernel, n_sel=n_sel),
        grid=(b, nq),
        in_specs=[pl.BlockSpec((1, 1, N_KV_HEADS, QK_DIM, REP * TQ), lambda bi, i: (bi, i, 0, 0, 0)),
                  pl.BlockSpec((1, 1, IDX_DIM, IDX_HEADS * TQ), lambda bi, i: (bi, i, 0, 0)),
                  pl.BlockSpec((1, 1, IDX_HEADS, TQ), lambda bi, i: (bi, i, 0, 0)),
                  pl.BlockSpec((1, N_KV_HEADS, s, QK_DIM), lambda bi, i: (bi, 0, 0, 0)),
                  pl.BlockSpec((1, N_KV_HEADS, nb, V_ROWS, KB), lambda bi, i: (bi, 0, 0, 0, 0)),
                  pl.BlockSpec((1, s, IDX_DIM), lambda bi, i: (bi, 0, 0)),
                  pl.BlockSpec((N_HEADS, 3 * TQ, TQ), lambda bi, i: (0, 0, 0))],
        out_specs=pl.BlockSpec((1, TQ, ATTN_WIDTH), lambda bi, i: (bi, i, 0)),
        out_shape=jax.ShapeDtypeStruct((b, s, ATTN_WIDTH), BF16),
        scratch_shapes=[pltpu.VMEM((nb, KB, TQ), I32),
                        pltpu.VMEM((nb, KB, TQ), F32),
                        pltpu.VMEM((nb, KB, TQ), F32),
                        pltpu.VMEM((8, TQ), I32),
                        pltpu.VMEM((N_KV_HEADS, 1, REP * TQ), F32),
                        pltpu.VMEM((N_KV_HEADS, V_ROWS, REP * TQ), F32),
                        pltpu.VMEM((N_KV_HEADS, FAR_KC, REP * TQ), F32),
                        pltpu.VMEM((N_KV_HEADS, FAR_KC, REP * TQ), F32)],
        compiler_params=_params("arbitrary", "arbitrary"),
        name="attn",
    )(qT, qiT, wT, kh, vT, kin, bias_strips)


HALO = 16


def _mix_kernel(cb_ref, cc_ref, cu_ref, ccp_ref, cup_ref, at_ref, ga_ref, gb_ref,
                cw_ref, wco_ref, wao_ref, o_ref):
    tm = cb_ref.shape[1]
    v = cc_ref[0].astype(F32) * cu_ref[0].astype(F32)
    first = pl.program_id(1) == 0
    hv = ccp_ref[0].astype(F32) * cup_ref[0].astype(F32)
    hv = jnp.where(first, 0.0, hv)
    row = lax.broadcasted_iota(I32, v.shape, 0)
    v1 = jnp.where(row == 0, hv[HALO - 1:HALO], pltpu.roll(v, 1, 0))
    v2 = pltpu.roll(v, 2, 0)
    v2 = jnp.where(row == 0, hv[HALO - 2:HALO - 1], jnp.where(row == 1, hv[HALO - 1:HALO], v2))
    y = cw_ref[0:1] * v2 + cw_ref[1:2] * v1 + cw_ref[2:3] * v
    yc = (cb_ref[0].astype(F32) * y).astype(BF16)
    y_conv = jnp.dot(yc, wco_ref[...], preferred_element_type=F32)
    y_attn = jnp.dot(at_ref[0], wao_ref[...], preferred_element_type=F32)
    mixed = _sigmoid(ga_ref[0].astype(F32)) * y_conv + _sigmoid(gb_ref[0].astype(F32)) * y_attn
    o_ref[0] = mixed.astype(BF16)


def _mix(proj, attn, conv_w, w_conv_out_b, w_attn_out_b):
    b, s, _ = proj.shape
    tm = MIX_TM
    hb = tm // HALO
    prev = lambda col: (lambda bi, i: (bi, jnp.maximum(i * hb - 1, 0), col))
    return pl.pallas_call(
        _mix_kernel,
        grid=(b, s // tm),
        in_specs=[pl.BlockSpec((1, tm, CONV_WIDTH), lambda bi, i: (bi, i, COL_CB)),
                  pl.BlockSpec((1, tm, CONV_WIDTH), lambda bi, i: (bi, i, COL_CC)),
                  pl.BlockSpec((1, tm, CONV_WIDTH), lambda bi, i: (bi, i, COL_CU)),
                  pl.BlockSpec((1, HALO, CONV_WIDTH), prev(COL_CC)),
                  pl.BlockSpec((1, HALO, CONV_WIDTH), prev(COL_CU)),
                  pl.BlockSpec((1, tm, ATTN_WIDTH), lambda bi, i: (bi, i, 0)),
                  pl.BlockSpec((1, tm, D_MODEL), lambda bi, i: (bi, i, COL_GA)),
                  pl.BlockSpec((1, tm, D_MODEL), lambda bi, i: (bi, i, COL_GB)),
                  pl.BlockSpec((8, CONV_WIDTH), lambda bi, i: (0, 0)),
                  pl.BlockSpec((CONV_WIDTH, D_MODEL), lambda bi, i: (0, 0)),
                  pl.BlockSpec((ATTN_WIDTH, D_MODEL), lambda bi, i: (0, 0))],
        out_specs=pl.BlockSpec((1, tm, D_MODEL), lambda bi, i: (bi, i, 0)),
        out_shape=jax.ShapeDtypeStruct((b, s, D_MODEL), BF16),
        compiler_params=_params("arbitrary", "arbitrary"),
        name="mix",
    )(proj, proj, proj, proj, proj, attn, proj, proj,
      jnp.pad(conv_w, ((0, 8 - CONV_K), (0, 0))), w_conv_out_b, w_attn_out_b)


def _post_kernel(x_ref, mx_ref, g1_ref, nw_ref, sc_ref, sh_ref, g2_ref, wo_ref, wrT_ref,
                 ws1_ref, ws3_ref, ws2_ref, base_ref, h2_ref, lg_ref):
    x1 = x_ref[0] + g1_ref[0] * jnp.dot(mx_ref[0], wo_ref[...], preferred_element_type=F32)
    ms = jnp.mean(x1 * x1, axis=-1, keepdims=True)
    h2 = x1 * lax.rsqrt(ms + EPS) * nw_ref[...] * (1.0 + sc_ref[0]) + sh_ref[0]
    _store_row_tiles(h2_ref, _pack_bf16_pairs(h2))
    lg_ref[...] = lax.dot_general(wrT_ref[...], h2, (((1,), (1,)), ((), ())),
                                  precision=lax.Precision.HIGHEST, preferred_element_type=F32)
    hb = h2.astype(BF16)
    a = jnp.dot(hb, ws1_ref[...], preferred_element_type=F32)
    u = jnp.dot(hb, ws3_ref[...], preferred_element_type=F32)
    shared = jnp.dot((a * _sigmoid(a) * u).astype(BF16), ws2_ref[...], preferred_element_type=F32)
    base_ref[0] = x1 + g2_ref[0] * shared


def _post(x, mixed, g1, norm_w, sc, sh, g2, w_o_b, w_router_t, ws1_b, ws3_b, ws2_b):
    b, s, d = x.shape
    tm = POST_TM
    nt = s // tm
    vec = pl.BlockSpec((1, 1, d), lambda bi, i: (bi, 0, 0))
    const = lambda shape: pl.BlockSpec(shape, lambda bi, i: (0,) * len(shape))
    return pl.pallas_call(
        _post_kernel,
        grid=(b, nt),
        in_specs=[pl.BlockSpec((1, tm, d), lambda bi, i: (bi, i, 0)),
                  pl.BlockSpec((1, tm, d), lambda bi, i: (bi, i, 0)),
                  vec, const((1, d)), vec, vec, vec,
                  const((d, d)), const((N_EXPERTS, d)),
                  const((d, D_EXPERT)), const((d, D_EXPERT)), const((D_EXPERT, d))],
        out_specs=[pl.BlockSpec((1, tm, d), lambda bi, i: (bi, i, 0)),
                   pl.BlockSpec((tm * ROW_SUB, LANES), lambda bi, i: (bi * nt + i, 0)),
                   pl.BlockSpec((N_EXPERTS, tm), lambda bi, i: (0, bi * nt + i))],
        out_shape=[jax.ShapeDtypeStruct((b, s, d), F32),
                   jax.ShapeDtypeStruct((b * s * ROW_SUB, LANES), PACKED),
                   jax.ShapeDtypeStruct((N_EXPERTS, b * s), F32)],
        compiler_params=_params("arbitrary", "arbitrary"),
        name="post",
    )(x, mixed, g1, norm_w.reshape(1, d), sc, sh, g2, w_o_b, w_router_t, ws1_b, ws3_b, ws2_b)


def _first_max(cur, ids, sentinel):
    m = jnp.max(cur, axis=0, keepdims=True)
    first = jnp.min(jnp.where(cur == m, ids, sentinel), axis=0, keepdims=True)
    return m, first


def _route_kernel(lg_ref, rb_ref, idx_ref, w_ref):
    tn = lg_ref.shape[1]
    gsz = N_EXPERTS // N_GROUPS
    scores = _sigmoid(lg_ref[...])
    sel = scores + rb_ref[...]
    sub = lax.broadcasted_iota(I32, (gsz, tn), 0).astype(F32)

    gs = []
    for g in range(N_GROUPS):
        v = sel[g * gsz:(g + 1) * gsz]
        m1, first = _first_max(v, sub, float(gsz))
        m2 = jnp.max(jnp.where(sub == first, -jnp.inf, v), axis=0, keepdims=True)
        gs.append(m1 + m2)
    cur = jnp.concatenate(gs, axis=0)
    gid = lax.broadcasted_iota(I32, (N_GROUPS, tn), 0).astype(F32)
    keep = jnp.zeros((N_GROUPS, tn), F32)
    for _ in range(TOPK_GROUPS):
        _, first = _first_max(cur, gid, float(N_GROUPS))
        hit = gid == first
        keep = jnp.where(hit, 1.0, keep)
        cur = jnp.where(hit, -jnp.inf, cur)

    cur = jnp.concatenate(
        [jnp.where(keep[g:g + 1] > 0.0, sel[g * gsz:(g + 1) * gsz], NEG) for g in range(N_GROUPS)],
        axis=0)
    eid = lax.broadcasted_iota(I32, (N_EXPERTS, tn), 0).astype(F32)
    ids, ws = [], []
    for _ in range(TOP_K):
        _, first = _first_max(cur, eid, float(N_EXPERTS))
        hit = eid == first
        ids.append(first)
        ws.append(jnp.sum(jnp.where(hit, scores, 0.0), axis=0, keepdims=True))
        cur = jnp.where(hit, -jnp.inf, cur)
    w = jnp.concatenate(ws, axis=0)
    idx_ref[...] = jnp.concatenate(ids, axis=0).astype(I32)
    w_ref[...] = w / jnp.sum(w, axis=0, keepdims=True) * ROUTED_SCALE


def _route(logits_t, router_bias):
    e, n = logits_t.shape
    tn = ROUTE_TN
    return pl.pallas_call(
        _route_kernel,
        grid=(n // tn,),
        in_specs=[pl.BlockSpec((e, tn), lambda j: (0, j)),
                  pl.BlockSpec((e, 1), lambda j: (0, 0))],
        out_specs=[pl.BlockSpec((TOP_K, tn), lambda j: (0, j)),
                   pl.BlockSpec((TOP_K, tn), lambda j: (0, j))],
        out_shape=[jax.ShapeDtypeStruct((TOP_K, n), I32),
                   jax.ShapeDtypeStruct((TOP_K, n), F32)],
        compiler_params=_params("arbitrary"),
        name="route",
    )(logits_t, router_bias.reshape(e, 1))


def _tile_major(a_t, n_tiles, tm):
    return a_t.reshape(TOP_K, n_tiles, tm).transpose(1, 0, 2).reshape(n_tiles, 1, TOP_K * tm)


def _dispatch_kernel(zs_ref, pos_ref, x_ref, xs_hbm, zbuf, sem):
    step = pl.program_id(0)

    @pl.when(step == 0)
    def _():
        zbuf[...] = jnp.zeros(zbuf.shape, zbuf.dtype)

        def zero_copy(t):
            start = pl.multiple_of(t * MOE_TM, MOE_TM)
            return pltpu.make_async_copy(zbuf, xs_hbm.at[pl.ds(start, MOE_TM)], sem.at[1])

        def start_body(t, carry):
            @pl.when(zs_ref[t] != 0)
            def _():
                zero_copy(t).start()
            return carry

        def wait_body(t, carry):
            @pl.when(zs_ref[t] != 0)
            def _():
                zero_copy(t).wait()
            return carry

        lax.fori_loop(0, zs_ref.shape[0], start_body, 0)
        lax.fori_loop(0, zs_ref.shape[0], wait_body, 0)

    def row_copy(k, r):
        return pltpu.make_async_copy(x_ref.at[r], xs_hbm.at[pos_ref[0, 0, k * DISP_TM + r]], sem.at[0])

    def body(i, carry):
        for k in range(TOP_K):
            for u in range(2):
                row_copy(k, i * 2 + u).start(priority=u)
        return carry

    lax.fori_loop(0, DISP_TM // 2, body, 0)
    for _ in range(TOP_K):
        pltpu.make_async_copy(x_ref, x_ref, sem.at[0]).wait()


def _dispatch(h2, pos_t, zero_start, n_rows):
    n = h2.shape[0]
    tm = DISP_TM
    n_tiles = n // tm
    grid_spec = pltpu.PrefetchScalarGridSpec(
        num_scalar_prefetch=1,
        grid=(n_tiles,),
        in_specs=[pl.BlockSpec((1, 1, tm * TOP_K), lambda t, zs: (t, 0, 0), memory_space=pltpu.SMEM),
                  pl.BlockSpec((tm, ROW_SUB, LANES), lambda t, zs: (t, 0, 0))],
        out_specs=pl.BlockSpec(memory_space=pl.ANY),
        scratch_shapes=[pltpu.VMEM((MOE_TM, ROW_SUB, LANES), h2.dtype),
                        pltpu.SemaphoreType.DMA((2,))],
    )
    return pl.pallas_call(
        _dispatch_kernel,
        grid_spec=grid_spec,
        out_shape=jax.ShapeDtypeStruct((n_rows, ROW_SUB, LANES), h2.dtype),
        compiler_params=_params("arbitrary"),
        name="dispatch",
    )(zero_start, _tile_major(pos_t, n_tiles, tm), h2)


def _experts_kernel(te_ref, nu_ref, nxt_ref, par_ref, x_ref, w1_hbm, w3_hbm, w2_hbm, y_ref,
                    w1f, w3f, w2f, wsem, w1b, w3b, w2b):
    j = pl.program_id(0)
    n_used = nu_ref[0]

    def weight_copies(e, slot):
        return [pltpu.make_async_copy(src.at[e], dst.at[slot], wsem.at[slot])
                for src, dst in ((w1_hbm, w1f), (w3_hbm, w3f), (w2_hbm, w2f))]

    @pl.when(j < n_used)
    def _():
        @pl.when((j == 0) | (te_ref[j] != te_ref[jnp.maximum(j - 1, 0)]))
        def _():
            for parity in (0, 1):
                @pl.when(par_ref[j] == parity)
                def _(parity=parity):
                    if parity == 0:
                        @pl.when(j == 0)
                        def _():
                            for c in weight_copies(te_ref[0], 0):
                                c.start()
                    for c in weight_copies(te_ref[j], parity):
                        c.wait()
                    w1b[...] = w1f[parity].astype(BF16)
                    w3b[...] = w3f[parity].astype(BF16)
                    w2b[...] = w2f[parity].astype(BF16)

                    @pl.when(nxt_ref[j] >= 0)
                    def _():
                        for c in weight_copies(nxt_ref[j], 1 - parity):
                            c.start()

        lo, hi = _unpack_bf16_pairs(_load_row_tiles(x_ref, MOE_TM))
        x = jnp.concatenate([lo.astype(BF16), hi.astype(BF16)], axis=1)
        a = jnp.dot(x, w1b[...], preferred_element_type=F32)
        u = jnp.dot(x, w3b[...], preferred_element_type=F32)
        y = jnp.dot((a * _sigmoid(a) * u).astype(BF16), w2b[...], preferred_element_type=F32)
        _store_row_tiles(y_ref, _pack_bf16_pairs(y))

    @pl.when(j >= n_used)
    def _():
        y_ref[...] = jnp.zeros(y_ref.shape, y_ref.dtype)


def _experts(xs, tile_expert, n_used, next_expert, slot_parity, w1, w3, w2):
    n_rows = xs.shape[0]
    nt = n_rows // MOE_TM
    d, f = w1.shape[1], w1.shape[2]
    blk = (MOE_TM * ROW_SUB, LANES)
    grid_spec = pltpu.PrefetchScalarGridSpec(
        num_scalar_prefetch=4,
        grid=(nt,),
        in_specs=[pl.BlockSpec(blk, lambda j, te, nu, nx, pa: (jnp.minimum(j, nu[0] - 1), 0)),
                  pl.BlockSpec(memory_space=pl.ANY),
                  pl.BlockSpec(memory_space=pl.ANY),
                  pl.BlockSpec(memory_space=pl.ANY)],
        out_specs=pl.BlockSpec(blk, lambda j, te, nu, nx, pa: (j, 0)),
        scratch_shapes=[pltpu.VMEM((2, d, f), F32),
                        pltpu.VMEM((2, d, f), F32),
                        pltpu.VMEM((2, f, d), F32),
                        pltpu.SemaphoreType.DMA((2,)),
                        pltpu.VMEM((d, f), BF16),
                        pltpu.VMEM((d, f), BF16),
                        pltpu.VMEM((f, d), BF16)],
    )
    ys = pl.pallas_call(
        _experts_kernel,
        grid_spec=grid_spec,
        out_shape=jax.ShapeDtypeStruct((n_rows * ROW_SUB, LANES), PACKED),
        compiler_params=_params("arbitrary"),
        name="experts",
    )(tile_expert, n_used, next_expert, slot_parity, xs.reshape(n_rows * ROW_SUB, LANES), w1, w3, w2)
    return ys.reshape(n_rows, ROW_SUB, LANES)


def _combine_kernel(pos_cur_ref, pos_nxt_ref, ys_hbm, base_ref, g2_ref, w_ref, o_ref, buf, sem):
    bi, i = pl.program_id(0), pl.program_id(1)
    step = bi * pl.num_programs(1) + i
    n_steps = pl.num_programs(0) * pl.num_programs(1)
    slot = lax.rem(step, 2)

    def issue(pos_ref, dst_slot):
        def body(i, carry):
            for k in range(TOP_K):
                for u in range(2):
                    r = i * 2 + u
                    pltpu.make_async_copy(ys_hbm.at[pos_ref[0, 0, k * COMB_TM + r]],
                                          buf.at[dst_slot, k, pl.ds(r * ROW_SUB, ROW_SUB), :],
                                          sem.at[dst_slot]).start(priority=u)
            return carry
        lax.fori_loop(0, COMB_TM // 2, body, 0)

    @pl.when(step == 0)
    def _():
        issue(pos_cur_ref, 0)

    for parity in (0, 1):
        @pl.when((step + 1 < n_steps) & (slot == parity))
        def _(parity=parity):
            issue(pos_nxt_ref, 1 - parity)

    pltpu.make_async_copy(buf.at[slot], buf.at[slot], sem.at[slot]).wait()

    w = w_ref[...]
    half = o_ref.shape[2] // 2
    for parity in (0, 1):
        @pl.when(slot == parity)
        def _(parity=parity):
            acc_lo = jnp.zeros((COMB_TM, half), F32)
            acc_hi = jnp.zeros((COMB_TM, half), F32)
            for k in range(TOP_K):
                lo, hi = _unpack_bf16_pairs(_load_row_tiles(buf.at[parity, k], COMB_TM))
                acc_lo = acc_lo + w[:, k:k + 1] * lo
                acc_hi = acc_hi + w[:, k:k + 1] * hi
            o_ref[0] = base_ref[0] + g2_ref[0] * jnp.concatenate([acc_lo, acc_hi], axis=1)


def _combine(ys, pos_t, w_sel, base, g2):
    b, s, d = base.shape
    tm = COMB_TM
    nt = s // tm
    n_tiles = b * nt
    pos_t = _tile_major(pos_t, n_tiles, tm)
    return pl.pallas_call(
        _combine_kernel,
        grid=(b, nt),
        in_specs=[pl.BlockSpec((1, 1, tm * TOP_K), lambda bi, i: (bi * nt + i, 0, 0),
                               memory_space=pltpu.SMEM),
                  pl.BlockSpec((1, 1, tm * TOP_K),
                               lambda bi, i: (jnp.minimum(bi * nt + i + 1, n_tiles - 1), 0, 0),
                               memory_space=pltpu.SMEM),
                  pl.BlockSpec(memory_space=pl.ANY),
                  pl.BlockSpec((1, tm, d), lambda bi, i: (bi, i, 0)),
                  pl.BlockSpec((1, 1, d), lambda bi, i: (bi, 0, 0)),
                  pl.BlockSpec((tm, TOP_K), lambda bi, i: (bi * nt + i, 0))],
        out_specs=pl.BlockSpec((1, tm, d), lambda bi, i: (bi, i, 0)),
        out_shape=jax.ShapeDtypeStruct((b, s, d), F32),
        scratch_shapes=[pltpu.VMEM((2, TOP_K, tm * ROW_SUB, LANES), ys.dtype),
                        pltpu.SemaphoreType.DMA((2,))],
        compiler_params=_params("arbitrary", "arbitrary"),
        name="combine",
    )(pos_t, pos_t, ys, base, g2, w_sel)


def _plan_kernel(te_ref, tri_ref, low_ref, pos_ref, cnt_ref, run_ref, start_ref):
    phase, j = pl.program_id(0), pl.program_id(1)
    tn = te_ref.shape[1]
    te = te_ref[...]
    eid = lax.broadcasted_iota(I32, (N_EXPERTS, tn), 0)
    hot = jnp.zeros((N_EXPERTS, tn), F32)
    for k in range(TOP_K):
        hot = hot + jnp.where(te[k:k + 1, :] == eid, 1.0, 0.0)
    tile_count = jnp.sum(hot, axis=1, keepdims=True)

    @pl.when((phase == 0) & (j == 0))
    def _():
        run_ref[...] = jnp.zeros(run_ref.shape, F32)

    @pl.when((phase == 1) & (j == 0))
    def _():
        counts = run_ref[...]
        cnt_ref[...] = counts
        tiles = jnp.floor((counts + (MOE_TM - 1)) * (1.0 / MOE_TM))
        start_ref[...] = jnp.dot(low_ref[...], tiles.astype(BF16), preferred_element_type=F32) * MOE_TM
        run_ref[...] = jnp.zeros(run_ref.shape, F32)

    @pl.when(phase == 1)
    def _():
        before = jnp.dot(hot.astype(BF16), tri_ref[...], preferred_element_type=F32)
        val = before + (run_ref[:, 0:1] + start_ref[:, 0:1])
        rows = [jnp.sum(jnp.where(te[k:k + 1, :] == eid, val, 0.0), axis=0, keepdims=True)
                for k in range(TOP_K)]
        pos_ref[...] = jnp.concatenate(rows, axis=0).astype(I32)

    run_ref[...] = run_ref[...] + tile_count


def _dispatch_plan(top_e_t):
    n = top_e_t.shape[1]
    tn = PLAN_TN
    n_tiles = n * TOP_K // MOE_TM + N_EXPERTS
    tri = jnp.asarray(np.triu(np.ones((tn, tn), np.float32), 1), BF16)
    low = jnp.asarray(np.tril(np.ones((N_EXPERTS, N_EXPERTS), np.float32), -1), BF16)
    pos_t, cnt = pl.pallas_call(
        _plan_kernel,
        grid=(2, n // tn),
        in_specs=[pl.BlockSpec((TOP_K, tn), lambda ph, j: (0, j)),
                  pl.BlockSpec((tn, tn), lambda ph, j: (0, 0)),
                  pl.BlockSpec((N_EXPERTS, N_EXPERTS), lambda ph, j: (0, 0))],
        out_specs=[pl.BlockSpec((TOP_K, tn), lambda ph, j: (0, j * ph)),
                   pl.BlockSpec((N_EXPERTS, LANES), lambda ph, j: (0, 0))],
        out_shape=[jax.ShapeDtypeStruct((TOP_K, n), I32),
                   jax.ShapeDtypeStruct((N_EXPERTS, LANES), F32)],
        scratch_shapes=[pltpu.VMEM((N_EXPERTS, LANES), F32),
                        pltpu.VMEM((N_EXPERTS, LANES), F32)],
        compiler_params=_params("arbitrary", "arbitrary"),
        name="plan",
    )(top_e_t, tri, low)
    counts = cnt[:, 0].astype(I32)
    tile_end = jnp.cumsum((counts + MOE_TM - 1) // MOE_TM)
    tile_expert = jnp.minimum(
        jnp.sum((tile_end[None, :] <= jnp.arange(n_tiles, dtype=I32)[:, None]).astype(I32), axis=1),
        N_EXPERTS - 1)
    n_used = tile_end[-1:].astype(I32)
    t_ids = jnp.arange(n_tiles, dtype=I32)
    is_last = jnp.any((tile_end[None, :] - 1 == t_ids[:, None]) & (counts[None, :] > 0), axis=1)
    zero_tile = (is_last | (t_ids >= n_used[0])).astype(I32)
    e_ids = jnp.arange(N_EXPERTS, dtype=I32)
    nonempty = counts > 0
    later = lax.cummin(jnp.where(nonempty, e_ids, N_EXPERTS), axis=0, reverse=True)
    next_e = jnp.concatenate([later[1:], jnp.full((1,), N_EXPERTS, I32)])
    next_e = jnp.where(next_e < N_EXPERTS, next_e, -1)
    rank = jnp.cumsum(nonempty.astype(I32)) - nonempty.astype(I32)
    tile_expert = tile_expert.astype(I32)
    hot = tile_expert[:, None] == e_ids[None, :]
    next_tile = jnp.sum(jnp.where(hot, next_e[None, :], 0), axis=1).astype(I32)
    parity_tile = jnp.sum(jnp.where(hot, rank[None, :] % 2, 0), axis=1).astype(I32)
    return pos_t, zero_tile, tile_expert, n_used, next_tile, parity_tile, n_tiles * MOE_TM


def _layer(x, c, rel_bias, norm1_w, norm2_w, w_ada, b_ada, w_in, conv_w, w_conv_out, q_norm_w,
           k_norm_w, idx_k_norm_w, idx_k_norm_b, w_attn_out, w_o, w_router, router_bias,
           w1, w3, w2, ws1, ws3, ws2):
    b, s, d = x.shape
    mod = _mod(c, w_ada, b_ada).reshape(b, 6, 1, d)
    sh1, sc1, g1, sh2, sc2, g2 = [mod[:, m] for m in range(6)]

    proj = _proj(x, norm1_w, sc1, sh1, _relayout_w_in(w_in))
    qT, qiT, kh, vT, kin, wT = _prep(proj, q_norm_w, k_norm_w, idx_k_norm_w, idx_k_norm_b,
                                     rel_bias[REL_BUCKETS - 1])
    attn = _attention(qT, qiT, wT, kh, vT, kin, _bias_strips(rel_bias))
    mixed = _mix(proj, attn, conv_w, w_conv_out.astype(BF16), w_attn_out.astype(BF16))
    base, h2, logits_t = _post(x, mixed, g1, norm2_w, sc2, sh2, g2, w_o.astype(BF16), w_router.T,
                               ws1.astype(BF16), ws3.astype(BF16), ws2.astype(BF16))
    top_e_t, w_sel_t = _route(logits_t, router_bias)
    pos_t, zero_start, tile_expert, n_used, next_expert, slot_parity, n_rows = _dispatch_plan(top_e_t)
    xs = _dispatch(h2.reshape(b * s, ROW_SUB, LANES), pos_t, zero_start, n_rows)
    ys = _experts(xs, tile_expert, n_used, next_expert, slot_parity, w1, w3, w2)
    return _combine(ys, pos_t, w_sel_t.T, base, g2)


def kernel(x, c, rel_bias, norm1_w, norm2_w, w_ada, b_ada, w_in, conv_w, w_conv_out, q_norm_w,
           k_norm_w, idx_k_norm_w, idx_k_norm_b, w_attn_out, w_o, w_router, router_bias,
           w1, w3, w2, ws1, ws3, ws2):
    assert x.shape[1] % PROJ_TM == 0 and x.shape[2] == D_MODEL and w_ada.shape[0] == 1
    return _layer(x, c, rel_bias, norm1_w[0], norm2_w[0], w_ada[0], b_ada[0], w_in, conv_w[0],
                  w_conv_out[0], q_norm_w[0], k_norm_w[0], idx_k_norm_w[0], idx_k_norm_b[0],
                  w_attn_out[0], w_o[0], w_router[0], router_bias[0], w1[0], w3[0], w2[0],
                  ws1[0], ws3[0], ws2[0])
```

```python
import functools
import math

import numpy as np
import jax
import jax.numpy as jnp
from jax import lax
from jax.experimental import pallas as pl
from jax.experimental.pallas import tpu as pltpu

F32 = jnp.float32
BF16 = jnp.bfloat16
I32 = jnp.int32
PACKED = jnp.int32

D_MODEL = 2048
CONV_WIDTH = D_MODEL // 2
CONV_K = 3
N_HEADS = 16
N_KV_HEADS = 4
HEAD_DIM = 64
ATTN_WIDTH = N_HEADS * HEAD_DIM
KV_WIDTH = N_KV_HEADS * HEAD_DIM
IDX_HEADS = 16
IDX_DIM = 64
IDX_TOPK_MAX = 256
REL_BUCKETS = 32
REL_MAX_DIST = 128
N_EXPERTS = 64
N_GROUPS = 8
TOPK_GROUPS = 4
TOP_K = 8
D_EXPERT = 512
ROUTED_SCALE = 2.5
EPS = 1e-6
NEG = -1e30

REP = N_HEADS // N_KV_HEADS

LANES = 128
VMEM_LIMIT = 56 * 1024 * 1024

TQ = 128
KB = 128
KC = 4 * KB
FAR_KC = 4 * KB
PROJ_TM = 1024
PROJ_TN = 768
PREP_TM = 512
MIX_TM = 512
POST_TM = 512
ROUTE_TN = 512
MOE_TM = 512
COMB_TM = 128
DISP_TM = 1024
PLAN_TN = 512

ROW_SUB = D_MODEL // 2 // LANES
QK_DIM = 128
V_ROWS = HEAD_DIM + 16
LOG2E = math.log2(math.e)

_SEG = dict(cb=(0, 1024), cc=(1024, 2048), cu=(2048, 3072), q=(3072, 4096), k=(4096, 4352),
            v=(4352, 4608), qi=(4608, 5632), ki=(5632, 5696), wi=(5696, 5712),
            ga=(5712, 7760), gb=(7760, 9808))
_ORDER = ["ga", "gb", "cb", "cc", "cu", "q", "qi", "k", "v", "ki", "wi"]
PROJ_W = 9984
COL_GA, COL_GB = 0, 1
COL_CB, COL_CC, COL_CU, COL_Q, COL_QI = 4, 5, 6, 7, 8
COL_K, COL_V = 36, 37
COL_KW = 76

INT_MIN = -(2 ** 31)
INT_MAX = 2 ** 31 - 1


def _sortable_key_of(x):
    bits = int(np.float32(x).view(np.int32))
    return bits ^ 0x7FFFFFFF if bits < 0 else bits


NEG_KEY = _sortable_key_of(NEG)


def _sigmoid(x):
    return 1.0 / (1.0 + jnp.exp(-x))


def _pack_bf16_pairs(x):
    half = x.shape[1] // 2
    lo = lax.bitcast_convert_type(x[:, :half].astype(BF16).astype(F32), PACKED)
    hi = lax.bitcast_convert_type(x[:, half:].astype(BF16).astype(F32), PACKED)
    return lax.shift_right_logical(lo, jnp.full_like(lo, 16)) | (hi & jnp.int32(-65536))


def _unpack_bf16_pairs(w):
    lo = lax.bitcast_convert_type(w << 16, F32)
    hi = lax.bitcast_convert_type(w & jnp.int32(-65536), F32)
    return lo, hi


def _store_row_tiles(ref, words):
    m = words.shape[0]
    for sl in range(ROW_SUB):
        ref[pl.ds(sl, m, stride=ROW_SUB), :] = words[:, sl * LANES:(sl + 1) * LANES]


def _load_row_tiles(ref, m):
    return jnp.concatenate([ref[pl.ds(sl, m, stride=ROW_SUB), :] for sl in range(ROW_SUB)], axis=1)


def _params(*sem):
    return pltpu.CompilerParams(dimension_semantics=sem, vmem_limit_bytes=VMEM_LIMIT)


def _mod_kernel(c_ref, w_ref, b_ref, o_ref):
    c = c_ref[...]
    s = (c * _sigmoid(c)).astype(BF16)
    o_ref[...] = jnp.dot(s, w_ref[...].astype(BF16), preferred_element_type=F32) + b_ref[...]


def _mod(c, w_ada, b_ada):
    b = c.shape[0]
    rows = 8
    cp = jnp.pad(c, ((0, rows - b), (0, 0)))
    n = w_ada.shape[1]
    tn = 1024
    out = pl.pallas_call(
        _mod_kernel,
        grid=(n // tn,),
        in_specs=[pl.BlockSpec((rows, D_MODEL), lambda j: (0, 0)),
                  pl.BlockSpec((D_MODEL, tn), lambda j: (0, j)),
                  pl.BlockSpec((1, tn), lambda j: (0, j))],
        out_specs=pl.BlockSpec((rows, tn), lambda j: (0, j)),
        out_shape=jax.ShapeDtypeStruct((rows, n), F32),
        compiler_params=_params("arbitrary"),
        name="mod",
    )(cp, w_ada, b_ada.reshape(1, n))
    return out[:b]


def _column_spans():
    spans, dst = [], 0
    for name in _ORDER:
        lo, hi = _SEG[name]
        if spans and spans[-1][1] == lo:
            spans[-1][1] = hi
        else:
            spans.append([lo, hi, dst])
        dst += hi - lo
    return spans, dst


def _wprep_kernel(w_ref, o_ref):
    spans, used = _column_spans()
    for lo, hi, dst in spans:
        o_ref[:, dst:dst + hi - lo] = w_ref[:, lo:hi].astype(BF16)
    o_ref[:, used:] = jnp.zeros((o_ref.shape[0], o_ref.shape[1] - used), BF16)


def _relayout_w_in(w_in):
    w_in = w_in[0].astype(BF16)
    d, n_in = w_in.shape
    tr = 256
    return pl.pallas_call(
        _wprep_kernel,
        grid=(d // tr,),
        in_specs=[pl.BlockSpec((tr, n_in), lambda i: (i, 0))],
        out_specs=pl.BlockSpec((tr, PROJ_W), lambda i: (i, 0)),
        out_shape=jax.ShapeDtypeStruct((d, PROJ_W), BF16),
        compiler_params=_params("arbitrary"),
        name="wprep",
    )(w_in)


def _proj_kernel(x_ref, nw_ref, sc_ref, sh_ref, w_ref, o_ref, h_ref):
    @pl.when(pl.program_id(2) == 0)
    def _():
        x = x_ref[0]
        ms = jnp.mean(x * x, axis=-1, keepdims=True)
        y = x * lax.rsqrt(ms + EPS) * nw_ref[...]
        h_ref[...] = (y * (1.0 + sc_ref[0]) + sh_ref[0]).astype(BF16)

    o_ref[0] = jnp.dot(h_ref[...], w_ref[...], preferred_element_type=F32).astype(BF16)


def _proj(x, norm_w, sc, sh, w_in_p):
    b, s, d = x.shape
    tm, tn = PROJ_TM, PROJ_TN
    return pl.pallas_call(
        _proj_kernel,
        grid=(b, s // tm, PROJ_W // tn),
        in_specs=[pl.BlockSpec((1, tm, d), lambda bi, i, j: (bi, i, 0)),
                  pl.BlockSpec((1, d), lambda bi, i, j: (0, 0)),
                  pl.BlockSpec((1, 1, d), lambda bi, i, j: (bi, 0, 0)),
                  pl.BlockSpec((1, 1, d), lambda bi, i, j: (bi, 0, 0)),
                  pl.BlockSpec((d, tn), lambda bi, i, j: (0, j))],
        out_specs=pl.BlockSpec((1, tm, tn), lambda bi, i, j: (bi, i, j)),
        out_shape=jax.ShapeDtypeStruct((b, s, PROJ_W), BF16),
        scratch_shapes=[pltpu.VMEM((tm, d), BF16)],
        compiler_params=_params("arbitrary", "arbitrary", "arbitrary"),
        name="proj",
    )(x, norm_w.reshape(1, d), sc, sh, w_in_p)


def _prep_kernel(q_ref, qi_ref, k_ref, v_ref, kw_ref, qnw_ref, knw_ref, inw_ref, inb_ref, qtail_ref,
                 qT_ref, qiT_ref, kh_ref, vT_ref, kin_ref, wT_ref):
    tm = q_ref.shape[1]
    nqb = tm // TQ

    q3 = q_ref[0].astype(F32).T.reshape(N_HEADS, HEAD_DIM, tm)
    ms = jnp.mean(q3 * q3, axis=1, keepdims=True)
    qn = q3 * lax.rsqrt(ms + EPS) * (qnw_ref[...] * (HEAD_DIM ** -0.5 * LOG2E))
    qi3 = qi_ref[0].astype(F32).T.reshape(IDX_HEADS, IDX_DIM, tm)
    for jb in range(nqb):
        for h in range(N_HEADS):
            g, r = divmod(h, REP)
            qT_ref[0, jb, g, :HEAD_DIM, r * TQ:(r + 1) * TQ] = qn[h, :, jb * TQ:(jb + 1) * TQ].astype(BF16)
        for g in range(N_KV_HEADS):
            qT_ref[0, jb, g, HEAD_DIM:, :] = qtail_ref[g]
        for h in range(IDX_HEADS):
            qiT_ref[0, jb, :, h * TQ:(h + 1) * TQ] = qi3[h, :, jb * TQ:(jb + 1) * TQ].astype(BF16)

    k = k_ref[0].astype(F32)
    ones_cols = jnp.where(lax.broadcasted_iota(I32, (tm, QK_DIM - HEAD_DIM), 1) < 2, 1.0, 0.0)
    for g in range(N_KV_HEADS):
        kg = k[:, g * HEAD_DIM:(g + 1) * HEAD_DIM]
        msk = jnp.mean(kg * kg, axis=-1, keepdims=True)
        kn = kg * lax.rsqrt(msk + EPS) * knw_ref[...]
        kh_ref[0, g] = jnp.concatenate([kn, ones_cols], axis=1).astype(BF16)

    v3 = v_ref[0].astype(F32).T.reshape(N_KV_HEADS, HEAD_DIM, tm)
    ones_rows = jnp.where(lax.broadcasted_iota(I32, (V_ROWS - HEAD_DIM, KB), 0) == 0, 1.0, 0.0)
    for g in range(N_KV_HEADS):
        for jb in range(tm // KB):
            vT_ref[0, g, jb] = jnp.concatenate(
                [v3[g, :, jb * KB:(jb + 1) * KB], ones_rows], axis=0).astype(BF16)

    kw = kw_ref[0].astype(F32)
    ki = kw[:, :IDX_DIM]
    mu = jnp.mean(ki, axis=-1, keepdims=True)
    var = jnp.mean(jnp.square(ki - mu), axis=-1, keepdims=True)
    kin_ref[0] = ((ki - mu) * lax.rsqrt(var + EPS) * inw_ref[...] + inb_ref[...]).astype(BF16)
    wiT = kw.T[IDX_DIM:IDX_DIM + IDX_HEADS] * (IDX_HEADS ** -0.5 * IDX_DIM ** -0.5)
    for jb in range(nqb):
        wT_ref[0, jb] = wiT[:, jb * TQ:(jb + 1) * TQ]


def _prep(proj, q_norm_w, k_norm_w, idx_k_norm_w, idx_k_norm_b, far_bias):
    b, s, _ = proj.shape
    tm = PREP_TM
    nqb = tm // TQ
    nq = s // TQ
    fb2 = (far_bias * LOG2E).reshape(N_KV_HEADS, REP)
    hi = fb2.astype(BF16)
    lo = (fb2 - hi.astype(F32)).astype(BF16)
    tail = jnp.stack([hi, lo], axis=1)
    tail = jnp.broadcast_to(tail[..., None], (N_KV_HEADS, 2, REP, TQ)).reshape(N_KV_HEADS, 2, REP * TQ)
    qtail = jnp.pad(tail, ((0, 0), (0, QK_DIM - HEAD_DIM - 2), (0, 0)))
    return pl.pallas_call(
        _prep_kernel,
        grid=(b, s // tm),
        in_specs=[pl.BlockSpec((1, tm, ATTN_WIDTH), lambda bi, i: (bi, i, COL_Q)),
                  pl.BlockSpec((1, tm, IDX_HEADS * IDX_DIM), lambda bi, i: (bi, i, COL_QI)),
                  pl.BlockSpec((1, tm, KV_WIDTH), lambda bi, i: (bi, i, COL_K)),
                  pl.BlockSpec((1, tm, KV_WIDTH), lambda bi, i: (bi, i, COL_V)),
                  pl.BlockSpec((1, tm, LANES), lambda bi, i: (bi, i, COL_KW)),
                  pl.BlockSpec((1, HEAD_DIM, 1), lambda bi, i: (0, 0, 0)),
                  pl.BlockSpec((1, HEAD_DIM), lambda bi, i: (0, 0)),
                  pl.BlockSpec((1, IDX_DIM), lambda bi, i: (0, 0)),
                  pl.BlockSpec((1, IDX_DIM), lambda bi, i: (0, 0)),
                  pl.BlockSpec((N_KV_HEADS, QK_DIM - HEAD_DIM, REP * TQ), lambda bi, i: (0, 0, 0))],
        out_specs=[pl.BlockSpec((1, nqb, N_KV_HEADS, QK_DIM, REP * TQ), lambda bi, i: (bi, i, 0, 0, 0)),
                   pl.BlockSpec((1, nqb, IDX_DIM, IDX_HEADS * TQ), lambda bi, i: (bi, i, 0, 0)),
                   pl.BlockSpec((1, N_KV_HEADS, tm, QK_DIM), lambda bi, i: (bi, 0, i, 0)),
                   pl.BlockSpec((1, N_KV_HEADS, tm // KB, V_ROWS, KB), lambda bi, i: (bi, 0, i, 0, 0)),
                   pl.BlockSpec((1, tm, IDX_DIM), lambda bi, i: (bi, i, 0)),
                   pl.BlockSpec((1, nqb, IDX_HEADS, TQ), lambda bi, i: (bi, i, 0, 0))],
        out_shape=[jax.ShapeDtypeStruct((b, nq, N_KV_HEADS, QK_DIM, REP * TQ), BF16),
                   jax.ShapeDtypeStruct((b, nq, IDX_DIM, IDX_HEADS * TQ), BF16),
                   jax.ShapeDtypeStruct((b, N_KV_HEADS, s, QK_DIM), BF16),
                   jax.ShapeDtypeStruct((b, N_KV_HEADS, s // KB, V_ROWS, KB), BF16),
                   jax.ShapeDtypeStruct((b, s, IDX_DIM), BF16),
                   jax.ShapeDtypeStruct((b, nq, IDX_HEADS, TQ), F32)],
        compiler_params=_params("arbitrary", "arbitrary"),
        name="prep",
    )(proj, proj, proj, proj, proj,
      q_norm_w.reshape(1, HEAD_DIM, 1), k_norm_w.reshape(1, HEAD_DIM),
      idx_k_norm_w.reshape(1, IDX_DIM), idx_k_norm_b.reshape(1, IDX_DIM), qtail)


def _t5_bucket_np(n):
    n = np.maximum(n, 0)
    max_exact = REL_BUCKETS // 2
    nf = np.maximum(n, 1).astype(np.float64)
    large = max_exact + np.floor(np.log(nf / max_exact) / math.log(REL_MAX_DIST / max_exact)
                                 * (REL_BUCKETS - max_exact)).astype(np.int64)
    large = np.minimum(large, REL_BUCKETS - 1)
    return np.where(n < max_exact, n, large).astype(np.int32)


def _bias_kernel(rb_ref, bucket_ref, o_ref):
    h = pl.program_id(0)
    bucket = bucket_ref[...]
    acc = jnp.zeros(bucket.shape, F32)
    for bkt in range(REL_BUCKETS):
        acc = jnp.where(bucket == bkt, rb_ref[bkt, h], acc)
    o_ref[0] = (acc - rb_ref[REL_BUCKETS - 1, h]) * LOG2E


def _bias_strips(rel_bias):
    kk = np.arange(3 * TQ)[:, None]
    qq = np.arange(TQ)[None, :]
    bucket = jnp.asarray(_t5_bucket_np(qq + TQ - kk))
    return pl.pallas_call(
        _bias_kernel,
        grid=(N_HEADS,),
        in_specs=[pl.BlockSpec(memory_space=pltpu.SMEM),
                  pl.BlockSpec((3 * TQ, TQ), lambda h: (0, 0))],
        out_specs=pl.BlockSpec((1, 3 * TQ, TQ), lambda h: (h, 0, 0)),
        out_shape=jax.ShapeDtypeStruct((N_HEADS, 3 * TQ, TQ), F32),
        compiler_params=_params("arbitrary"),
        name="bias",
    )(rel_bias, bucket)


def _attn_kernel(qT_ref, qiT_ref, wT_ref, kh_ref, vT_ref, kin_ref, biasT_ref, o_ref,
                 keys_ref, am_ref, amf_ref, p_ref, m_ref, acc_ref, sa_ref, sb_ref, *, n_sel):
    i = pl.program_id(1)
    seq = kin_ref.shape[1]
    t0 = i * TQ
    n_chunks = lax.shift_right_logical(i + 4, 2)
    q_pos = t0 + lax.broadcasted_iota(I32, (KB, TQ), 1)
    k_off = lax.broadcasted_iota(I32, (KB, TQ), 0)

    qiT = qiT_ref[0, 0]
    wT = wT_ref[0, 0]

    def score_chunk(c, carry):
        k0 = pl.multiple_of(c * KC, KC)
        d = jnp.dot(kin_ref[0, pl.ds(k0, KC), :], qiT, preferred_element_type=F32)
        acc = jnp.zeros((KC, TQ), F32)
        for h in range(IDX_HEADS):
            acc = acc + wT[h:h + 1, :] * jnp.maximum(d[:, h * TQ:(h + 1) * TQ], 0.0)
        for j in range(KC // KB):
            blk = c * (KC // KB) + j
            sc = jnp.where(blk * KB + k_off <= q_pos, acc[j * KB:(j + 1) * KB], NEG)
            bits = lax.bitcast_convert_type(sc, I32)
            keys_ref[blk] = jnp.where(bits < 0, bits ^ 0x7FFFFFFF, bits)
        return carry

    lax.fori_loop(0, n_chunks, score_chunk, 0)

    n_virtual = (seq - n_chunks * KC).astype(F32)

    def count(pred):
        def body(c, acc):
            for j in range(KC // KB):
                blk = c * (KC // KB) + j
                hit = jnp.where(pred(keys_ref[blk], blk), 1.0, 0.0)
                acc = acc + jnp.sum(hit.reshape(KB // 8, 8, TQ), axis=0)
            return acc
        acc = lax.fori_loop(0, n_chunks, body, jnp.zeros((8, TQ), F32))
        return jnp.sum(acc, axis=0, keepdims=True)

    def bit_body(it, carry):
        thr, cnt_ge = carry
        cand = thr + lax.shift_left(jnp.int32(1), 31 - it)
        cnt = count(lambda kb, blk: kb >= cand) + jnp.where(NEG_KEY >= cand, n_virtual, 0.0)
        accept = cnt >= n_sel
        return jnp.where(accept, cand, thr), jnp.where(accept, cnt, cnt_ge)

    thr, cnt_ge = lax.fori_loop(
        0, 32, bit_body, (jnp.full((1, TQ), INT_MIN, I32), jnp.full((1, TQ), float(seq), F32)))

    cnt_gt = count(lambda kb, blk: kb > thr) + jnp.where(NEG_KEY > thr, n_virtual, 0.0)
    cnt_eq = cnt_ge - cnt_gt
    need = n_sel - cnt_gt
    p_ref[...] = jnp.full((8, TQ), INT_MAX, I32)
    has_tie = jnp.max(jnp.where(cnt_eq > need, 1.0, 0.0)) > 0.0

    @pl.when(has_tie)
    def _():
        idx_bits = int(seq).bit_length()

        def p_body(it, p):
            cand = p | lax.shift_left(jnp.int32(1), idx_bits - 1 - it)
            below = count(lambda kb, blk: (kb == thr) & (blk * KB + k_off < cand))
            return jnp.where(below < need, cand, p)

        p = lax.fori_loop(0, idx_bits, p_body, jnp.zeros((1, TQ), I32))
        p_ref[...] = jnp.broadcast_to(p, (8, TQ))

    p_last = p_ref[0:1, :]

    bw = jnp.maximum(i - 1, 0)
    ws = pl.multiple_of(bw * KB, KB)

    def mask_chunk(c, carry):
        for j in range(KC // KB):
            blk = c * (KC // KB) + j
            kb = keys_ref[blk]
            k_pos = blk * KB + k_off
            sel = (kb > thr) | ((kb == thr) & (k_pos <= p_last))
            v = jnp.where(sel & (k_pos <= q_pos), 0.0, NEG)
            am_ref[blk] = v
            amf_ref[blk] = jnp.where(k_pos < ws, v, NEG)
        return carry

    lax.fori_loop(0, n_chunks, mask_chunk, 0)

    off = pl.multiple_of(TQ - (t0 - ws), TQ)
    fb = FAR_KC // KB
    n_far = (bw + fb - 1) // fb

    def qk(f, dst):
        k0 = pl.multiple_of(f * FAR_KC, FAR_KC)
        for g in range(N_KV_HEADS):
            dst[g] = jnp.dot(kh_ref[0, g, pl.ds(k0, FAR_KC), :], qT_ref[0, 0, g],
                             preferred_element_type=F32)

    for g in range(N_KV_HEADS):
        sb_ref[g, :2 * KB] = jnp.dot(kh_ref[0, g, pl.ds(ws, 2 * KB), :], qT_ref[0, 0, g],
                                     preferred_element_type=F32)
    qk(0, sa_ref)
    am_near = jnp.concatenate([am_ref[bw], am_ref[bw + 1]], axis=0)
    for g in range(N_KV_HEADS):
        s = sb_ref[g, :2 * KB]
        s = jnp.concatenate(
            [s[:, r * TQ:(r + 1) * TQ] + (biasT_ref[REP * g + r, pl.ds(off, 2 * KB), :] + am_near)
             for r in range(REP)], axis=1)
        m = jnp.max(s, axis=0, keepdims=True)
        pb = jnp.exp2(s - m).astype(BF16)
        m_ref[g] = m
        acc_ref[g] = (jnp.dot(vT_ref[0, g, bw], pb[:KB], preferred_element_type=F32)
                      + jnp.dot(vT_ref[0, g, bw + 1], pb[KB:], preferred_element_type=F32))

    def softmax_pv(f, src):
        amf = jnp.concatenate([amf_ref[f * fb + j] for j in range(fb)], axis=0)
        for g in range(N_KV_HEADS):
            s = src[g]
            s = jnp.concatenate([s[:, r * TQ:(r + 1) * TQ] + amf for r in range(REP)], axis=1)
            m_old = m_ref[g]
            m_new = jnp.maximum(m_old, jnp.max(s, axis=0, keepdims=True))
            pb = jnp.exp2(s - m_new).astype(BF16)
            vc = jnp.concatenate([vT_ref[0, g, f * fb + j] for j in range(fb)], axis=1)
            m_ref[g] = m_new
            acc_ref[g] = (jnp.exp2(m_old - m_new) * acc_ref[g]
                          + jnp.dot(vc, pb, preferred_element_type=F32))

    def pair_body(pf, carry):
        f0 = 2 * pf
        qk(f0 + 1, sb_ref)
        softmax_pv(f0, sa_ref)
        qk(jnp.minimum(f0 + 2, n_far - 1), sa_ref)
        softmax_pv(f0 + 1, sb_ref)
        return carry

    lax.fori_loop(0, n_far // 2, pair_body, 0)

    @pl.when(n_far % 2 == 1)
    def _():
        softmax_pv(n_far - 1, sa_ref)

    outs = []
    for g in range(N_KV_HEADS):
        og = acc_ref[g, :HEAD_DIM] / acc_ref[g, HEAD_DIM:HEAD_DIM + 1]
        outs.extend(og[:, r * TQ:(r + 1) * TQ] for r in range(REP))
    o_ref[0] = jnp.concatenate(outs, axis=0).T.astype(BF16)


def _attention(qT, qiT, wT, kh, vT, kin, bias_strips):
    b, nq = qT.shape[0], qT.shape[1]
    s = kin.shape[1]
    n_sel = min(IDX_TOPK_MAX, s // 4)
    nb = s // KB
    return pl.pallas_call(
        functools.partial(_attn_kernel, n_sel=n_sel),
        grid=(b, nq),
        in_specs=[pl.BlockSpec((1, 1, N_KV_HEADS, QK_DIM, REP * TQ), lambda bi, i: (bi, i, 0, 0, 0)),
                  pl.BlockSpec((1, 1, IDX_DIM, IDX_HEADS * TQ), lambda bi, i: (bi, i, 0, 0)),
                  pl.BlockSpec((1, 1, IDX_HEADS, TQ), lambda bi, i: (bi, i, 0, 0)),
                  pl.BlockSpec((1, N_KV_HEADS, s, QK_DIM), lambda bi, i: (bi, 0, 0, 0)),
                  pl.BlockSpec((1, N_KV_HEADS, nb, V_ROWS, KB), lambda bi, i: (bi, 0, 0, 0, 0)),
                  pl.BlockSpec((1, s, IDX_DIM), lambda bi, i: (bi, 0, 0)),
                  pl.BlockSpec((N_HEADS, 3 * TQ, TQ), lambda bi, i: (0, 0, 0))],
        out_specs=pl.BlockSpec((1, TQ, ATTN_WIDTH), lambda bi, i: (bi, i, 0)),
        out_shape=jax.ShapeDtypeStruct((b, s, ATTN_WIDTH), BF16),
        scratch_shapes=[pltpu.VMEM((nb, KB, TQ), I32),
                        pltpu.VMEM((nb, KB, TQ), F32),
                        pltpu.VMEM((nb, KB, TQ), F32),
                        pltpu.VMEM((8, TQ), I32),
                        pltpu.VMEM((N_KV_HEADS, 1, REP * TQ), F32),
                        pltpu.VMEM((N_KV_HEADS, V_ROWS, REP * TQ), F32),
                        pltpu.VMEM((N_KV_HEADS, FAR_KC, REP * TQ), F32),
                        pltpu.VMEM((N_KV_HEADS, FAR_KC, REP * TQ), F32)],
        compiler_params=_params("arbitrary", "arbitrary"),
        name="attn",
    )(qT, qiT, wT, kh, vT, kin, bias_strips)


HALO = 16


def _mix_kernel(cb_ref, cc_ref, cu_ref, ccp_ref, cup_ref, at_ref, ga_ref, gb_ref,
                cw_ref, wco_ref, wao_ref, o_ref):
    tm = cb_ref.shape[1]
    v = cc_ref[0].astype(F32) * cu_ref[0].astype(F32)
    first = pl.program_id(1) == 0
    hv = ccp_ref[0].astype(F32) * cup_ref[0].astype(F32)
    hv = jnp.where(first, 0.0, hv)
    row = lax.broadcasted_iota(I32, v.shape, 0)
    v1 = jnp.where(row == 0, hv[HALO - 1:HALO], pltpu.roll(v, 1, 0))
    v2 = pltpu.roll(v, 2, 0)
    v2 = jnp.where(row == 0, hv[HALO - 2:HALO - 1], jnp.where(row == 1, hv[HALO - 1:HALO], v2))
    y = cw_ref[0:1] * v2 + cw_ref[1:2] * v1 + cw_ref[2:3] * v
    yc = (cb_ref[0].astype(F32) * y).astype(BF16)
    y_conv = jnp.dot(yc, wco_ref[...], preferred_element_type=F32)
    y_attn = jnp.dot(at_ref[0], wao_ref[...], preferred_element_type=F32)
    mixed = _sigmoid(ga_ref[0].astype(F32)) * y_conv + _sigmoid(gb_ref[0].astype(F32)) * y_attn
    o_ref[0] = mixed.astype(BF16)


def _mix(proj, attn, conv_w, w_conv_out_b, w_attn_out_b):
    b, s, _ = proj.shape
    tm = MIX_TM
    hb = tm // HALO
    prev = lambda col: (lambda bi, i: (bi, jnp.maximum(i * hb - 1, 0), col))
    return pl.pallas_call(
        _mix_kernel,
        grid=(b, s // tm),
        in_specs=[pl.BlockSpec((1, tm, CONV_WIDTH), lambda bi, i: (bi, i, COL_CB)),
                  pl.BlockSpec((1, tm, CONV_WIDTH), lambda bi, i: (bi, i, COL_CC)),
                  pl.BlockSpec((1, tm, CONV_WIDTH), lambda bi, i: (bi, i, COL_CU)),
                  pl.BlockSpec((1, HALO, CONV_WIDTH), prev(COL_CC)),
                  pl.BlockSpec((1, HALO, CONV_WIDTH), prev(COL_CU)),
                  pl.BlockSpec((1, tm, ATTN_WIDTH), lambda bi, i: (bi, i, 0)),
                  pl.BlockSpec((1, tm, D_MODEL), lambda bi, i: (bi, i, COL_GA)),
                  pl.BlockSpec((1, tm, D_MODEL), lambda bi, i: (bi, i, COL_GB)),
                  pl.BlockSpec((8, CONV_WIDTH), lambda bi, i: (0, 0)),
                  pl.BlockSpec((CONV_WIDTH, D_MODEL), lambda bi, i: (0, 0)),
                  pl.BlockSpec((ATTN_WIDTH, D_MODEL), lambda bi, i: (0, 0))],
        out_specs=pl.BlockSpec((1, tm, D_MODEL), lambda bi, i: (bi, i, 0)),
        out_shape=jax.ShapeDtypeStruct((b, s, D_MODEL), BF16),
        compiler_params=_params("arbitrary", "arbitrary"),
        name="mix",
    )(proj, proj, proj, proj, proj, attn, proj, proj,
      jnp.pad(conv_w, ((0, 8 - CONV_K), (0, 0))), w_conv_out_b, w_attn_out_b)


def _post_kernel(x_ref, mx_ref, g1_ref, nw_ref, sc_ref, sh_ref, g2_ref, wo_ref, wrT_ref,
                 ws1_ref, ws3_ref, ws2_ref, base_ref, h2_ref, lg_ref):
    x1 = x_ref[0] + g1_ref[0] * jnp.dot(mx_ref[0], wo_ref[...], preferred_element_type=F32)
    ms = jnp.mean(x1 * x1, axis=-1, keepdims=True)
    h2 = x1 * lax.rsqrt(ms + EPS) * nw_ref[...] * (1.0 + sc_ref[0]) + sh_ref[0]
    _store_row_tiles(h2_ref, _pack_bf16_pairs(h2))
    lg_ref[...] = lax.dot_general(wrT_ref[...], h2, (((1,), (1,)), ((), ())),
                                  precision=lax.Precision.HIGHEST, preferred_element_type=F32)
    hb = h2.astype(BF16)
    a = jnp.dot(hb, ws1_ref[...], preferred_element_type=F32)
    u = jnp.dot(hb, ws3_ref[...], preferred_element_type=F32)
    shared = jnp.dot((a * _sigmoid(a) * u).astype(BF16), ws2_ref[...], preferred_element_type=F32)
    base_ref[0] = x1 + g2_ref[0] * shared


def _post(x, mixed, g1, norm_w, sc, sh, g2, w_o_b, w_router_t, ws1_b, ws3_b, ws2_b):
    b, s, d = x.shape
    tm = POST_TM
    nt = s // tm
    vec = pl.BlockSpec((1, 1, d), lambda bi, i: (bi, 0, 0))
    const = lambda shape: pl.BlockSpec(shape, lambda bi, i: (0,) * len(shape))
    return pl.pallas_call(
        _post_kernel,
        grid=(b, nt),
        in_specs=[pl.BlockSpec((1, tm, d), lambda bi, i: (bi, i, 0)),
                  pl.BlockSpec((1, tm, d), lambda bi, i: (bi, i, 0)),
                  vec, const((1, d)), vec, vec, vec,
                  const((d, d)), const((N_EXPERTS, d)),
                  const((d, D_EXPERT)), const((d, D_EXPERT)), const((D_EXPERT, d))],
        out_specs=[pl.BlockSpec((1, tm, d), lambda bi, i: (bi, i, 0)),
                   pl.BlockSpec((tm * ROW_SUB, LANES), lambda bi, i: (bi * nt + i, 0)),
                   pl.BlockSpec((N_EXPERTS, tm), lambda bi, i: (0, bi * nt + i))],
        out_shape=[jax.ShapeDtypeStruct((b, s, d), F32),
                   jax.ShapeDtypeStruct((b * s * ROW_SUB, LANES), PACKED),
                   jax.ShapeDtypeStruct((N_EXPERTS, b * s), F32)],
        compiler_params=_params("arbitrary", "arbitrary"),
        name="post",
    )(x, mixed, g1, norm_w.reshape(1, d), sc, sh, g2, w_o_b, w_router_t, ws1_b, ws3_b, ws2_b)


def _first_max(cur, ids, sentinel):
    m = jnp.max(cur, axis=0, keepdims=True)
    first = jnp.min(jnp.where(cur == m, ids, sentinel), axis=0, keepdims=True)
    return m, first


def _route_kernel(lg_ref, rb_ref, idx_ref, w_ref):
    tn = lg_ref.shape[1]
    gsz = N_EXPERTS // N_GROUPS
    scores = _sigmoid(lg_ref[...])
    sel = scores + rb_ref[...]
    sub = lax.broadcasted_iota(I32, (gsz, tn), 0).astype(F32)

    gs = []
    for g in range(N_GROUPS):
        v = sel[g * gsz:(g + 1) * gsz]
        m1, first = _first_max(v, sub, float(gsz))
        m2 = jnp.max(jnp.where(sub == first, -jnp.inf, v), axis=0, keepdims=True)
        gs.append(m1 + m2)
    cur = jnp.concatenate(gs, axis=0)
    gid = lax.broadcasted_iota(I32, (N_GROUPS, tn), 0).astype(F32)
    keep = jnp.zeros((N_GROUPS, tn), F32)
    for _ in range(TOPK_GROUPS):
        _, first = _first_max(cur, gid, float(N_GROUPS))
        hit = gid == first
        keep = jnp.where(hit, 1.0, keep)
        cur = jnp.where(hit, -jnp.inf, cur)

    cur = jnp.concatenate(
        [jnp.where(keep[g:g + 1] > 0.0, sel[g * gsz:(g + 1) * gsz], NEG) for g in range(N_GROUPS)],
        axis=0)
    eid = lax.broadcasted_iota(I32, (N_EXPERTS, tn), 0).astype(F32)
    ids, ws = [], []
    for _ in range(TOP_K):
        _, first = _first_max(cur, eid, float(N_EXPERTS))
        hit = eid == first
        ids.append(first)
        ws.append(jnp.sum(jnp.where(hit, scores, 0.0), axis=0, keepdims=True))
        cur = jnp.where(hit, -jnp.inf, cur)
    w = jnp.concatenate(ws, axis=0)
    idx_ref[...] = jnp.concatenate(ids, axis=0).astype(I32)
    w_ref[...] = w / jnp.sum(w, axis=0, keepdims=True) * ROUTED_SCALE


def _route(logits_t, router_bias):
    e, n = logits_t.shape
    tn = ROUTE_TN
    return pl.pallas_call(
        _route_kernel,
        grid=(n // tn,),
        in_specs=[pl.BlockSpec((e, tn), lambda j: (0, j)),
                  pl.BlockSpec((e, 1), lambda j: (0, 0))],
        out_specs=[pl.BlockSpec((TOP_K, tn), lambda j: (0, j)),
                   pl.BlockSpec((TOP_K, tn), lambda j: (0, j))],
        out_shape=[jax.ShapeDtypeStruct((TOP_K, n), I32),
                   jax.ShapeDtypeStruct((TOP_K, n), F32)],
        compiler_params=_params("arbitrary"),
        name="route",
    )(logits_t, router_bias.reshape(e, 1))


def _tile_major(a_t, n_tiles, tm):
    return a_t.reshape(TOP_K, n_tiles, tm).transpose(1, 0, 2).reshape(n_tiles, 1, TOP_K * tm)


def _dispatch_kernel(zs_ref, pos_ref, x_ref, xs_hbm, zbuf, sem):
    step = pl.program_id(0)

    @pl.when(step == 0)
    def _():
        zbuf[...] = jnp.zeros(zbuf.shape, zbuf.dtype)

        def zero_copy(t):
            start = pl.multiple_of(t * MOE_TM, MOE_TM)
            return pltpu.make_async_copy(zbuf, xs_hbm.at[pl.ds(start, MOE_TM)], sem.at[1])

        def start_body(t, carry):
            @pl.when(zs_ref[t] != 0)
            def _():
                zero_copy(t).start()
            return carry

        def wait_body(t, carry):
            @pl.when(zs_ref[t] != 0)
            def _():
                zero_copy(t).wait()
            return carry

        lax.fori_loop(0, zs_ref.shape[0], start_body, 0)
        lax.fori_loop(0, zs_ref.shape[0], wait_body, 0)

    def row_copy(k, r):
        return pltpu.make_async_copy(x_ref.at[r], xs_hbm.at[pos_ref[0, 0, k * DISP_TM + r]], sem.at[0])

    def body(i, carry):
        for k in range(TOP_K):
            for u in range(2):
                row_copy(k, i * 2 + u).start(priority=u)
        return carry

    lax.fori_loop(0, DISP_TM // 2, body, 0)
    for _ in range(TOP_K):
        pltpu.make_async_copy(x_ref, x_ref, sem.at[0]).wait()


def _dispatch(h2, pos_t, zero_start, n_rows):
    n = h2.shape[0]
    tm = DISP_TM
    n_tiles = n // tm
    grid_spec = pltpu.PrefetchScalarGridSpec(
        num_scalar_prefetch=1,
        grid=(n_tiles,),
        in_specs=[pl.BlockSpec((1, 1, tm * TOP_K), lambda t, zs: (t, 0, 0), memory_space=pltpu.SMEM),
                  pl.BlockSpec((tm, ROW_SUB, LANES), lambda t, zs: (t, 0, 0))],
        out_specs=pl.BlockSpec(memory_space=pl.ANY),
        scratch_shapes=[pltpu.VMEM((MOE_TM, ROW_SUB, LANES), h2.dtype),
                        pltpu.SemaphoreType.DMA((2,))],
    )
    return pl.pallas_call(
        _dispatch_kernel,
        grid_spec=grid_spec,
        out_shape=jax.ShapeDtypeStruct((n_rows, ROW_SUB, LANES), h2.dtype),
        compiler_params=_params("arbitrary"),
        name="dispatch",
    )(zero_start, _tile_major(pos_t, n_tiles, tm), h2)


def _experts_kernel(te_ref, nu_ref, nxt_ref, par_ref, x_ref, w1_hbm, w3_hbm, w2_hbm, y_ref,
                    w1f, w3f, w2f, wsem, w1b, w3b, w2b):
    j = pl.program_id(0)
    n_used = nu_ref[0]

    def weight_copies(e, slot):
        return [pltpu.make_async_copy(src.at[e], dst.at[slot], wsem.at[slot])
                for src, dst in ((w1_hbm, w1f), (w3_hbm, w3f), (w2_hbm, w2f))]

    @pl.when(j < n_used)
    def _():
        @pl.when((j == 0) | (te_ref[j] != te_ref[jnp.maximum(j - 1, 0)]))
        def _():
            for parity in (0, 1):
                @pl.when(par_ref[j] == parity)
                def _(parity=parity):
                    if parity == 0:
                        @pl.when(j == 0)
                        def _():
                            for c in weight_copies(te_ref[0], 0):
                                c.start()
                    for c in weight_copies(te_ref[j], parity):
                        c.wait()
                    w1b[...] = w1f[parity].astype(BF16)
                    w3b[...] = w3f[parity].astype(BF16)
                    w2b[...] = w2f[parity].astype(BF16)

                    @pl.when(nxt_ref[j] >= 0)
                    def _():
                        for c in weight_copies(nxt_ref[j], 1 - parity):
                            c.start()

        lo, hi = _unpack_bf16_pairs(_load_row_tiles(x_ref, MOE_TM))
        x = jnp.concatenate([lo.astype(BF16), hi.astype(BF16)], axis=1)
        a = jnp.dot(x, w1b[...], preferred_element_type=F32)
        u = jnp.dot(x, w3b[...], preferred_element_type=F32)
        y = jnp.dot((a * _sigmoid(a) * u).astype(BF16), w2b[...], preferred_element_type=F32)
        _store_row_tiles(y_ref, _pack_bf16_pairs(y))

    @pl.when(j >= n_used)
    def _():
        y_ref[...] = jnp.zeros(y_ref.shape, y_ref.dtype)


def _experts(xs, tile_expert, n_used, next_expert, slot_parity, w1, w3, w2):
    n_rows = xs.shape[0]
    nt = n_rows // MOE_TM
    d, f = w1.shape[1], w1.shape[2]
    blk = (MOE_TM * ROW_SUB, LANES)
    grid_spec = pltpu.PrefetchScalarGridSpec(
        num_scalar_prefetch=4,
        grid=(nt,),
        in_specs=[pl.BlockSpec(blk, lambda j, te, nu, nx, pa: (jnp.minimum(j, nu[0] - 1), 0)),
                  pl.BlockSpec(memory_space=pl.ANY),
                  pl.BlockSpec(memory_space=pl.ANY),
                  pl.BlockSpec(memory_space=pl.ANY)],
        out_specs=pl.BlockSpec(blk, lambda j, te, nu, nx, pa: (j, 0)),
        scratch_shapes=[pltpu.VMEM((2, d, f), F32),
                        pltpu.VMEM((2, d, f), F32),
                        pltpu.VMEM((2, f, d), F32),
                        pltpu.SemaphoreType.DMA((2,)),
                        pltpu.VMEM((d, f), BF16),
                        pltpu.VMEM((d, f), BF16),
                        pltpu.VMEM((f, d), BF16)],
    )
    ys = pl.pallas_call(
        _experts_kernel,
        grid_spec=grid_spec,
        out_shape=jax.ShapeDtypeStruct((n_rows * ROW_SUB, LANES), PACKED),
        compiler_params=_params("arbitrary"),
        name="experts",
    )(tile_expert, n_used, next_expert, slot_parity, xs.reshape(n_rows * ROW_SUB, LANES), w1, w3, w2)
    return ys.reshape(n_rows, ROW_SUB, LANES)


def _combine_kernel(pos_cur_ref, pos_nxt_ref, ys_hbm, base_ref, g2_ref, w_ref, o_ref, buf, sem):
    bi, i = pl.program_id(0), pl.program_id(1)
    step = bi * pl.num_programs(1) + i
    n_steps = pl.num_programs(0) * pl.num_programs(1)
    slot = lax.rem(step, 2)

    def issue(pos_ref, dst_slot):
        def body(i, carry):
            for k in range(TOP_K):
                for u in range(2):
                    r = i * 2 + u
                    pltpu.make_async_copy(ys_hbm.at[pos_ref[0, 0, k * COMB_TM + r]],
                                          buf.at[dst_slot, k, pl.ds(r * ROW_SUB, ROW_SUB), :],
                                          sem.at[dst_slot]).start(priority=u)
            return carry
        lax.fori_loop(0, COMB_TM // 2, body, 0)

    @pl.when(step == 0)
    def _():
        issue(pos_cur_ref, 0)

    for parity in (0, 1):
        @pl.when((step + 1 < n_steps) & (slot == parity))
        def _(parity=parity):
            issue(pos_nxt_ref, 1 - parity)

    pltpu.make_async_copy(buf.at[slot], buf.at[slot], sem.at[slot]).wait()

    w = w_ref[...]
    half = o_ref.shape[2] // 2
    for parity in (0, 1):
        @pl.when(slot == parity)
        def _(parity=parity):
            acc_lo = jnp.zeros((COMB_TM, half), F32)
            acc_hi = jnp.zeros((COMB_TM, half), F32)
            for k in range(TOP_K):
                lo, hi = _unpack_bf16_pairs(_load_row_tiles(buf.at[parity, k], COMB_TM))
                acc_lo = acc_lo + w[:, k:k + 1] * lo
                acc_hi = acc_hi + w[:, k:k + 1] * hi
            o_ref[0] = base_ref[0] + g2_ref[0] * jnp.concatenate([acc_lo, acc_hi], axis=1)


def _combine(ys, pos_t, w_sel, base, g2):
    b, s, d = base.shape
    tm = COMB_TM
    nt = s // tm
    n_tiles = b * nt
    pos_t = _tile_major(pos_t, n_tiles, tm)
    return pl.pallas_call(
        _combine_kernel,
        grid=(b, nt),
        in_specs=[pl.BlockSpec((1, 1, tm * TOP_K), lambda bi, i: (bi * nt + i, 0, 0),
                               memory_space=pltpu.SMEM),
                  pl.BlockSpec((1, 1, tm * TOP_K),
                               lambda bi, i: (jnp.minimum(bi * nt + i + 1, n_tiles - 1), 0, 0),
                               memory_space=pltpu.SMEM),
                  pl.BlockSpec(memory_space=pl.ANY),
                  pl.BlockSpec((1, tm, d), lambda bi, i: (bi, i, 0)),
                  pl.BlockSpec((1, 1, d), lambda bi, i: (bi, 0, 0)),
                  pl.BlockSpec((tm, TOP_K), lambda bi, i: (bi * nt + i, 0))],
        out_specs=pl.BlockSpec((1, tm, d), lambda bi, i: (bi, i, 0)),
        out_shape=jax.ShapeDtypeStruct((b, s, d), F32),
        scratch_shapes=[pltpu.VMEM((2, TOP_K, tm * ROW_SUB, LANES), ys.dtype),
                        pltpu.SemaphoreType.DMA((2,))],
        compiler_params=_params("arbitrary", "arbitrary"),
        name="combine",
    )(pos_t, pos_t, ys, base, g2, w_sel)


def _plan_kernel(te_ref, tri_ref, low_ref, pos_ref, cnt_ref, run_ref, start_ref):
    phase, j = pl.program_id(0), pl.program_id(1)
    tn = te_ref.shape[1]
    te = te_ref[...]
    eid = lax.broadcasted_iota(I32, (N_EXPERTS, tn), 0)
    hot = jnp.zeros((N_EXPERTS, tn), F32)
    for k in range(TOP_K):
        hot = hot + jnp.where(te[k:k + 1, :] == eid, 1.0, 0.0)
    tile_count = jnp.sum(hot, axis=1, keepdims=True)

    @pl.when((phase == 0) & (j == 0))
    def _():
        run_ref[...] = jnp.zeros(run_ref.shape, F32)

    @pl.when((phase == 1) & (j == 0))
    def _():
        counts = run_ref[...]
        cnt_ref[...] = counts
        tiles = jnp.floor((counts + (MOE_TM - 1)) * (1.0 / MOE_TM))
        start_ref[...] = jnp.dot(low_ref[...], tiles.astype(BF16), preferred_element_type=F32) * MOE_TM
        run_ref[...] = jnp.zeros(run_ref.shape, F32)

    @pl.when(phase == 1)
    def _():
        before = jnp.dot(hot.astype(BF16), tri_ref[...], preferred_element_type=F32)
        val = before + (run_ref[:, 0:1] + start_ref[:, 0:1])
        rows = [jnp.sum(jnp.where(te[k:k + 1, :] == eid, val, 0.0), axis=0, keepdims=True)
                for k in range(TOP_K)]
        pos_ref[...] = jnp.concatenate(rows, axis=0).astype(I32)

    run_ref[...] = run_ref[...] + tile_count


def _dispatch_plan(top_e_t):
    n = top_e_t.shape[1]
    tn = PLAN_TN
    n_tiles = n * TOP_K // MOE_TM + N_EXPERTS
    tri = jnp.asarray(np.triu(np.ones((tn, tn), np.float32), 1), BF16)
    low = jnp.asarray(np.tril(np.ones((N_EXPERTS, N_EXPERTS), np.float32), -1), BF16)
    pos_t, cnt = pl.pallas_call(
        _plan_kernel,
        grid=(2, n // tn),
        in_specs=[pl.BlockSpec((TOP_K, tn), lambda ph, j: (0, j)),
                  pl.BlockSpec((tn, tn), lambda ph, j: (0, 0)),
                  pl.BlockSpec((N_EXPERTS, N_EXPERTS), lambda ph, j: (0, 0))],
        out_specs=[pl.BlockSpec((TOP_K, tn), lambda ph, j: (0, j * ph)),
                   pl.BlockSpec((N_EXPERTS, LANES), lambda ph, j: (0, 0))],
        out_shape=[jax.ShapeDtypeStruct((TOP_K, n), I32),
                   jax.ShapeDtypeStruct((N_EXPERTS, LANES), F32)],
        scratch_shapes=[pltpu.VMEM((N_EXPERTS, LANES), F32),
                        pltpu.VMEM((N_EXPERTS, LANES), F32)],
        compiler_params=_params("arbitrary", "arbitrary"),
        name="plan",
    )(top_e_t, tri, low)
    counts = cnt[:, 0].astype(I32)
    tile_end = jnp.cumsum((counts + MOE_TM - 1) // MOE_TM)
    tile_expert = jnp.minimum(
        jnp.sum((tile_end[None, :] <= jnp.arange(n_tiles, dtype=I32)[:, None]).astype(I32), axis=1),
        N_EXPERTS - 1)
    n_used = tile_end[-1:].astype(I32)
    t_ids = jnp.arange(n_tiles, dtype=I32)
    is_last = jnp.any((tile_end[None, :] - 1 == t_ids[:, None]) & (counts[None, :] > 0), axis=1)
    zero_tile = (is_last | (t_ids >= n_used[0])).astype(I32)
    e_ids = jnp.arange(N_EXPERTS, dtype=I32)
    nonempty = counts > 0
    later = lax.cummin(jnp.where(nonempty, e_ids, N_EXPERTS), axis=0, reverse=True)
    next_e = jnp.concatenate([later[1:], jnp.full((1,), N_EXPERTS, I32)])
    next_e = jnp.where(next_e < N_EXPERTS, next_e, -1)
    rank = jnp.cumsum(nonempty.astype(I32)) - nonempty.astype(I32)
    tile_expert = tile_expert.astype(I32)
    hot = tile_expert[:, None] == e_ids[None, :]
    next_tile = jnp.sum(jnp.where(hot, next_e[None, :], 0), axis=1).astype(I32)
    parity_tile = jnp.sum(jnp.where(hot, rank[None, :] % 2, 0), axis=1).astype(I32)
    return pos_t, zero_tile, tile_expert, n_used, next_tile, parity_tile, n_tiles * MOE_TM


def _layer(x, c, rel_bias, norm1_w, norm2_w, w_ada, b_ada, w_in, conv_w, w_conv_out, q_norm_w,
           k_norm_w, idx_k_norm_w, idx_k_norm_b, w_attn_out, w_o, w_router, router_bias,
           w1, w3, w2, ws1, ws3, ws2):
    b, s, d = x.shape
    mod = _mod(c, w_ada, b_ada).reshape(b, 6, 1, d)
    sh1, sc1, g1, sh2, sc2, g2 = [mod[:, m] for m in range(6)]

    proj = _proj(x, norm1_w, sc1, sh1, _relayout_w_in(w_in))
    qT, qiT, kh, vT, kin, wT = _prep(proj, q_norm_w, k_norm_w, idx_k_norm_w, idx_k_norm_b,
                                     rel_bias[REL_BUCKETS - 1])
    attn = _attention(qT, qiT, wT, kh, vT, kin, _bias_strips(rel_bias))
    mixed = _mix(proj, attn, conv_w, w_conv_out.astype(BF16), w_attn_out.astype(BF16))
    base, h2, logits_t = _post(x, mixed, g1, norm2_w, sc2, sh2, g2, w_o.astype(BF16), w_router.T,
                               ws1.astype(BF16), ws3.astype(BF16), ws2.astype(BF16))
    top_e_t, w_sel_t = _route(logits_t, router_bias)
    pos_t, zero_start, tile_expert, n_used, next_expert, slot_parity, n_rows = _dispatch_plan(top_e_t)
    xs = _dispatch(h2.reshape(b * s, ROW_SUB, LANES), pos_t, zero_start, n_rows)
    ys = _experts(xs, tile_expert, n_used, next_expert, slot_parity, w1, w3, w2)
    return _combine(ys, pos_t, w_sel_t.T, base, g2)


def kernel(x, c, rel_bias, norm1_w, norm2_w, w_ada, b_ada, w_in, conv_w, w_conv_out, q_norm_w,
           k_norm_w, idx_k_norm_w, idx_k_norm_b, w_attn_out, w_o, w_router, router_bias,
           w1, w3, w2, ws1, ws3, ws2):
    assert x.shape[1] % PROJ_TM == 0 and x.shape[2] == D_MODEL and w_ada.shape[0] == 1
    return _layer(x, c, rel_bias, norm1_w[0], norm2_w[0], w_ada[0], b_ada[0], w_in, conv_w[0],
                  w_conv_out[0], q_norm_w[0], k_norm_w[0], idx_k_norm_w[0], idx_k_norm_b[0],
                  w_attn_out[0], w_o[0], w_router[0], router_bias[0], w1[0], w3[0], w2[0],
                  ws1[0], ws3[0], ws2[0])
```
